```python
import math
import jax, jax.numpy as jnp
from jax import lax
import numpy as np

D_MODEL = 1024
BATCH = 32
SEQ = 256
DEPTH = 2
DEC_BATCH = 2
DEC_SEQ = 4096
PAST_LEN = 256

GRID_W = 64
RET_HEADS = 4
RET_DK = 128
RET_DV = 128
RET_CHUNK = 128
RET_W = RET_HEADS * RET_DV
ATT_HEADS = 8
ATT_KV_HEADS = 2
ATT_HEAD_DIM = 64
ATT_BLOCK = 128
ATT_W = ATT_HEADS * ATT_HEAD_DIM
ATT_KV_W = ATT_KV_HEADS * ATT_HEAD_DIM
ROPE_THETA = 10000.0
SSM_CH = 512
SSM_GROUP = 16
SSM_GROUPS = SSM_CH // SSM_GROUP
SSM_STATE = 64
N_EXPERTS = 16
N_EXPERT_GROUPS = 4
EXPERTS_PER_GROUP = N_EXPERTS // N_EXPERT_GROUPS
TOP_K = 2
D_FF = 512
N_MOD = 6
EPS = 1e-6

IN_SPLITS = (RET_HEADS * RET_DK, RET_HEADS * RET_DK, RET_W, RET_W,
             ATT_W, ATT_KV_W, ATT_KV_W, SSM_CH, D_MODEL, D_MODEL, D_MODEL)
IN_COLS = sum(IN_SPLITS)
IN_OFFSETS = tuple(sum(IN_SPLITS[:i + 1]) for i in range(len(IN_SPLITS) - 1))

kernel_name = "hybrid_prefix_ctx_diffusion_step"


def _rms(x, g):
    xf = x.astype(jnp.float32)
    y = xf * lax.rsqrt(jnp.mean(xf * xf, axis=-1, keepdims=True) + EPS)
    return (y * g.astype(jnp.float32)).astype(x.dtype)


def _axial_rope_tables(n_tokens):
    rows = n_tokens // GRID_W
    row = jnp.repeat(jnp.arange(rows, dtype=jnp.float32), GRID_W)
    col = jnp.tile(jnp.arange(GRID_W, dtype=jnp.float32), rows)
    n_freq = ATT_HEAD_DIM // 4
    inv = ROPE_THETA ** (-jnp.arange(n_freq, dtype=jnp.float32) / n_freq)
    ang = jnp.concatenate([row[:, None] * inv, col[:, None] * inv], axis=-1)
    return jnp.cos(ang), jnp.sin(ang)


def _apply_rope(x, cos, sin):
    b, n, h, d = x.shape
    xr = x.reshape(b, n, h, d // 2, 2)
    x0, x1 = xr[..., 0], xr[..., 1]
    c = cos[None, :, None, :]
    s = sin[None, :, None, :]
    out = jnp.stack([x0 * c - x1 * s, x0 * s + x1 * c], axis=-1)
    return out.reshape(b, n, h, d).astype(x.dtype)


def _block_attention(q, k, v):
    b, n, h, d = q.shape
    n_blk = n // ATT_BLOCK
    grp = h // ATT_KV_HEADS
    qb = q.reshape(b, n_blk, ATT_BLOCK, ATT_KV_HEADS, grp, d).transpose(1, 0, 2, 3, 4, 5)
    scale = d ** -0.5

    def one_block(q_blk):
        s = jnp.einsum('bqkgd,bskd->bkgqs', q_blk, k).astype(jnp.float32) * scale
        p = jax.nn.softmax(s, axis=-1).astype(v.dtype)
        return jnp.einsum('bkgqs,bskd->bqkgd', p, v)

    out = lax.map(one_block, qb)
    return out.transpose(1, 0, 2, 3, 4, 5).reshape(b, n, h * d)


def _retention_direction(q, k, v, log_gamma, s0):
    b, n_tok, h, dk = q.shape
    dv = v.shape[-1]
    n_chunk = n_tok // RET_CHUNK
    dt = q.dtype
    lg = log_gamma.astype(jnp.float32)
    pos = jnp.arange(RET_CHUNK, dtype=jnp.float32)
    diff = pos[:, None] - pos[None, :]
    decay_mat = jnp.where(diff[None] >= 0, jnp.exp(lg[:, None, None] * jnp.maximum(diff, 0.0)[None]), 0.0).astype(dt)
    zeta = jnp.exp(lg[:, None] * (RET_CHUNK - 1.0 - pos)[None]).astype(dt)
    xi = jnp.exp(lg[:, None] * (pos + 1.0)[None]).astype(dt)
    chunk_decay = jnp.exp(lg * RET_CHUNK).astype(dt)[None, :, None, None]
    qc = q.reshape(b, n_chunk, RET_CHUNK, h, dk)
    kc = k.reshape(b, n_chunk, RET_CHUNK, h, dk)
    vc = v.reshape(b, n_chunk, RET_CHUNK, h, dv)
    scores = jnp.einsum('bnthd,bnshd->bnhts', qc, kc) * decay_mat
    o_inner = jnp.einsum('bnhts,bnshe->bnthe', scores, vc)
    kv = jnp.einsum('bnshd,bnshe,hs->nbhde', kc, vc, zeta)

    def step(state, kv_n):
        return chunk_decay * state + kv_n, state

    s_final, s_before = lax.scan(step, s0.astype(dt), kv)
    o_cross = jnp.einsum('bnthd,nbhde,ht->bnthe', qc, s_before, xi)
    return (o_inner + o_cross).reshape(b, n_tok, h, dv), s_final


def _linear_combine(x, y):
    return (y[0] * x[0], y[0] * x[1] + y[1])


def _s5_direction(u, a_re, a_im, log_dt, b_c, c_c, s0, reverse):
    lam = lax.complex(a_re.astype(jnp.float32), a_im.astype(jnp.float32))
    step = jnp.exp(log_dt.astype(jnp.float32))[:, None]
    a_bar = jnp.exp(lam * step)
    b_bar = ((a_bar - 1.0) / lam)[:, :, None] * b_c
    bu = jnp.einsum('gpi,blgi->blgp', b_bar, u.astype(jnp.float32))
    edge = -1 if reverse else 0
    bu = bu.at[:, edge].add(a_bar[None] * s0)
    a_seq = jnp.broadcast_to(a_bar, bu.shape)
    _, xs = lax.associative_scan(_linear_combine, (a_seq, bu), axis=1, reverse=reverse)
    y = jnp.einsum('gop,blgp->blgo', c_c, xs).real
    final = xs[:, 0] if reverse else xs[:, -1]
    return y, final


def _moe(h, w_router, b_router, w_gate, w_up, w_down):
    b, n, _ = h.shape
    logits = (h @ w_router).astype(jnp.float32) + b_router.astype(jnp.float32)
    probs = jax.nn.softmax(logits, axis=-1)
    grouped = probs.reshape(b, n, N_EXPERT_GROUPS, EXPERTS_PER_GROUP)
    group_score = jnp.sum(lax.top_k(grouped, TOP_K)[0], axis=-1)
    best = jnp.argmax(group_score, axis=-1)
    in_group = jnp.arange(N_EXPERT_GROUPS) == best[..., None]
    masked = jnp.where(in_group[..., None], grouped, -1.0).reshape(b, n, N_EXPERTS)
    top_v, top_i = lax.top_k(masked, TOP_K)
    wts = top_v / jnp.sum(top_v, axis=-1, keepdims=True)
    combine = jnp.sum(jax.nn.one_hot(top_i, N_EXPERTS, dtype=jnp.float32) * wts[..., None], axis=-2).astype(h.dtype)
    hid = jax.nn.silu(jnp.einsum('bld,edf->blef', h, w_gate)) * jnp.einsum('bld,edf->blef', h, w_up)
    return jnp.einsum('blef,efd->bld', hid * combine[..., None], w_down)


def _layer(x, mod, rope, ctx, p):
    b, n, _ = x.shape
    shift1, scale1, gate1, shift2, scale2, gate2 = jnp.split(mod[:, None, :], N_MOD, axis=-1)
    h = _rms(x, p['norm1_g']) * (1.0 + scale1) + shift1
    rq, rk, rv, rg, aq, ak, av, su, g_ret, g_att, g_ssm = jnp.split(h @ p['w_in'], IN_OFFSETS, axis=-1)

    rq = rq.reshape(b, n, RET_HEADS, RET_DK) * (RET_DK ** -0.5)
    rk = rk.reshape(b, n, RET_HEADS, RET_DK)
    rv = rv.reshape(b, n, RET_HEADS, RET_DV)
    if ctx is None:
        ret_s0 = jnp.zeros((b, 2, RET_HEADS, RET_DK, RET_DV), x.dtype)
        ssm_s0 = jnp.zeros((b, 2, SSM_GROUPS, SSM_STATE), jnp.complex64)
    else:
        ret_s0, ssm_s0 = ctx[2], ctx[3]
    o_f, ret_f = _retention_direction(rq, rk, rv, p['ret_lg_f'], ret_s0[:, 0])
    o_b, ret_b = _retention_direction(jnp.flip(rq, 1), jnp.flip(rk, 1), jnp.flip(rv, 1), p['ret_lg_b'], ret_s0[:, 1])
    ro = _rms(o_f + jnp.flip(o_b, 1), p['ret_norm_g'].reshape(RET_HEADS, RET_DV)).reshape(b, n, RET_W)
    ret_branch = (ro * jax.nn.silu(rg)) @ p['w_ret_out']

    aq = _rms(aq.reshape(b, n, ATT_HEADS, ATT_HEAD_DIM), p['att_q_norm_g'])
    ak = _rms(ak.reshape(b, n, ATT_KV_HEADS, ATT_HEAD_DIM), p['att_k_norm_g'])
    av = av.reshape(b, n, ATT_KV_HEADS, ATT_HEAD_DIM)
    if rope is None:
        keys, vals = ak, av
    else:
        cos, sin = rope
        aq = _apply_rope(aq, cos, sin)
        keys = jnp.concatenate([_apply_rope(ak, cos, sin), ctx[0].astype(ak.dtype)], axis=1)
        vals = jnp.concatenate([av, ctx[1].astype(av.dtype)], axis=1)
    att_branch = _block_attention(aq, keys, vals) @ p['w_att_out']

    u = su.reshape(b, n, SSM_GROUPS, SSM_GROUP)
    b_c = lax.complex(p['ssm_b_re'].astype(jnp.float32), p['ssm_b_im'].astype(jnp.float32))
    c_c = lax.complex(p['ssm_c_re'].astype(jnp.float32), p['ssm_c_im'].astype(jnp.float32))
    y_f, ssm_f = _s5_direction(u, p['a_re_f'], p['a_im_f'], p['log_dt_f'], b_c, c_c, ssm_s0[:, 0], False)
    y_b, ssm_b = _s5_direction(u, p['a_re_b'], p['a_im_b'], p['log_dt_b'], b_c, c_c, ssm_s0[:, 1], True)
    y = (y_f + y_b).astype(x.dtype) + p['ssm_d'].reshape(SSM_GROUPS, SSM_GROUP) * u
    y = jax.nn.gelu(y.reshape(b, n, SSM_CH))
    glu_a, glu_b = jnp.split(y @ p['w_glu'], 2, axis=-1)
    ssm_branch = glu_a * jax.nn.sigmoid(glu_b)

    merged = (jax.nn.sigmoid(g_ret) * ret_branch + jax.nn.sigmoid(g_att) * att_branch
              + jax.nn.sigmoid(g_ssm) * ssm_branch)
    x = x + gate1 * (merged @ p['w_out'])

    h2 = _rms(x, p['norm2_g']) * (1.0 + scale2) + shift2
    x = x + gate2 * _moe(h2, p['w_router'], p['b_router'], p['w_gate'], p['w_up'], p['w_down'])

    if ctx is None:
        return x, (ak, av, jnp.stack([ret_f, ret_b], axis=1), jnp.stack([ssm_f, ssm_b], axis=1))
    return x, None


def setup_inputs(seed: int = 0) -> dict:
    key = jax.random.key(seed)
    ks = iter(jax.random.split(key, 48))
    f32 = jnp.float32
    D = D_MODEL
    G, P, I = SSM_GROUPS, SSM_STATE, SSM_GROUP

    def nrm(shape, scale=1.0):
        return jax.random.normal(next(ks), shape, f32) * scale

    ret_base = jnp.log(1.0 - 2.0 ** (-5.0 - jnp.arange(RET_HEADS, dtype=f32)))
    a_im_base = jnp.pi * jnp.arange(P, dtype=f32)
    inp = {}
    inp['x_prompt'] = nrm((BATCH, SEQ, D))
    inp['x_sample'] = nrm((DEC_BATCH, DEC_SEQ, D))
    inp['c'] = nrm((DEC_BATCH, D))
    inp['cache_attn_k'] = nrm((DEC_BATCH, DEPTH, PAST_LEN, ATT_KV_HEADS, ATT_HEAD_DIM))
    inp['cache_attn_v'] = nrm((DEC_BATCH, DEPTH, PAST_LEN, ATT_KV_HEADS, ATT_HEAD_DIM))
    inp['state_ret'] = nrm((DEC_BATCH, DEPTH, 2, RET_HEADS, RET_DK, RET_DV), 4.0)
    inp['state_ssm_re'] = nrm((DEC_BATCH, DEPTH, 2, G, P), 0.1)
    inp['state_ssm_im'] = nrm((DEC_BATCH, DEPTH, 2, G, P), 0.1)
    inp['c_ctx'] = nrm((D,))
    inp['w_ada'] = nrm((DEPTH, D, N_MOD * D), 0.5 * D ** -0.5)
    inp['b_ada'] = nrm((DEPTH, N_MOD * D), 0.02)
    inp['norm1_g'] = 1.0 + nrm((DEPTH, D), 0.02)
    inp['norm2_g'] = 1.0 + nrm((DEPTH, D), 0.02)
    inp['w_in'] = nrm((DEPTH, D, IN_COLS), D ** -0.5)
    inp['ret_lg_f'] = ret_base[None] * jnp.exp(nrm((DEPTH, RET_HEADS), 0.05))
    inp['ret_lg_b'] = ret_base[None] * jnp.exp(nrm((DEPTH, RET_HEADS), 0.05))
    inp['ret_norm_g'] = 1.0 + nrm((DEPTH, RET_W), 0.02)
    inp['w_ret_out'] = nrm((DEPTH, RET_W, D), RET_W ** -0.5)
    inp['att_q_norm_g'] = 1.0 + nrm((DEPTH, ATT_HEAD_DIM), 0.02)
    inp['att_k_norm_g'] = 1.0 + nrm((DEPTH, ATT_HEAD_DIM), 0.02)
    inp['w_att_out'] = nrm((DEPTH, ATT_W, D), ATT_W ** -0.5)
    inp['a_re_f'] = -0.5 * jnp.exp(nrm((DEPTH, G, P), 0.05))
    inp['a_im_f'] = a_im_base + nrm((DEPTH, G, P), 0.01)
    inp['a_re_b'] = -0.5 * jnp.exp(nrm((DEPTH, G, P), 0.05))
    inp['a_im_b'] = a_im_base + nrm((DEPTH, G, P), 0.01)
    inp['log_dt_f'] = jax.random.uniform(next(ks), (DEPTH, G), f32, math.log(0.001), math.log(0.1))
    inp['log_dt_b'] = jax.random.uniform(next(ks), (DEPTH, G), f32, math.log(0.001), math.log(0.1))
    inp['ssm_b_re'] = nrm((DEPTH, G, P, I), (2 * I) ** -0.5)
    inp['ssm_b_im'] = nrm((DEPTH, G, P, I), (2 * I) ** -0.5)
    inp['ssm_c_re'] = nrm((DEPTH, G, I, P), (2 * P) ** -0.5)
    inp['ssm_c_im'] = nrm((DEPTH, G, I, P), (2 * P) ** -0.5)
    inp['ssm_d'] = nrm((DEPTH, SSM_CH))
    inp['w_glu'] = nrm((DEPTH, SSM_CH, 2 * D), SSM_CH ** -0.5)
    inp['w_out'] = nrm((DEPTH, D, D), D ** -0.5)
    inp['w_router'] = nrm((D, N_EXPERTS), D ** -0.5)
    inp['b_router'] = nrm((N_EXPERTS,), 0.01)
    inp['w_gate'] = nrm((DEPTH, N_EXPERTS, D, D_FF), D ** -0.5)
    inp['w_up'] = nrm((DEPTH, N_EXPERTS, D, D_FF), D ** -0.5)
    inp['w_down'] = nrm((DEPTH, N_EXPERTS, D_FF, D), D_FF ** -0.5)
    inp['final_norm_g'] = 1.0 + nrm((D,), 0.02)
    return inp


def reference(x_prompt, x_sample, c, cache_attn_k, cache_attn_v, state_ret, state_ssm_re, state_ssm_im, c_ctx,
              w_ada, b_ada, norm1_g, norm2_g, w_in, ret_lg_f, ret_lg_b, ret_norm_g, w_ret_out,
              att_q_norm_g, att_k_norm_g, w_att_out, a_re_f, a_im_f, a_re_b, a_im_b, log_dt_f, log_dt_b,
              ssm_b_re, ssm_b_im, ssm_c_re, ssm_c_im, ssm_d, w_glu, w_out, w_router, b_router,
              w_gate, w_up, w_down, final_norm_g):
    def layer_params(l):
        return {'norm1_g': norm1_g[l], 'norm2_g': norm2_g[l], 'w_in': w_in[l],
                'ret_lg_f': ret_lg_f[l], 'ret_lg_b': ret_lg_b[l], 'ret_norm_g': ret_norm_g[l],
                'w_ret_out': w_ret_out[l], 'att_q_norm_g': att_q_norm_g[l], 'att_k_norm_g': att_k_norm_g[l],
                'w_att_out': w_att_out[l], 'a_re_f': a_re_f[l], 'a_im_f': a_im_f[l], 'a_re_b': a_re_b[l],
                'a_im_b': a_im_b[l], 'log_dt_f': log_dt_f[l], 'log_dt_b': log_dt_b[l],
                'ssm_b_re': ssm_b_re[l], 'ssm_b_im': ssm_b_im[l], 'ssm_c_re': ssm_c_re[l], 'ssm_c_im': ssm_c_im[l],
                'ssm_d': ssm_d[l], 'w_glu': w_glu[l], 'w_out': w_out[l],
                'w_router': w_router, 'b_router': b_router,
                'w_gate': w_gate[l], 'w_up': w_up[l], 'w_down': w_down[l]}

    xp = x_prompt
    ks_, vs_, rets_, ssms_ = [], [], [], []
    for l in range(DEPTH):
        mod_ctx = jax.nn.silu(c_ctx)[None] @ w_ada[l] + b_ada[l]
        xp, (k_c, v_c, st_r, st_s) = _layer(xp, mod_ctx, None, None, layer_params(l))
        ks_.append(k_c)
        vs_.append(v_c)
        rets_.append(st_r)
        ssms_.append(st_s)
    y_prompt = _rms(xp, final_norm_g)
    new_cache_attn_k = jnp.stack(ks_, axis=1)
    new_cache_attn_v = jnp.stack(vs_, axis=1)
    new_state_ret = jnp.stack(rets_, axis=1)
    ssm_all = jnp.stack(ssms_, axis=1)
    new_state_ssm_re = jnp.real(ssm_all)
    new_state_ssm_im = jnp.imag(ssm_all)

    rope = _axial_rope_tables(x_sample.shape[1])
    xs = x_sample
    for l in range(DEPTH):
        mod_lat = jax.nn.silu(c) @ w_ada[l] + b_ada[l]
        ssm_ctx = lax.complex(state_ssm_re[:, l].astype(jnp.float32), state_ssm_im[:, l].astype(jnp.float32))
        ctx = (cache_attn_k[:, l], cache_attn_v[:, l], state_ret[:, l], ssm_ctx)
        xs, _ = _layer(xs, mod_lat, rope, ctx, layer_params(l))
    y_sample = _rms(xs, final_norm_g)

    return (y_prompt, y_sample, new_cache_attn_k, new_cache_attn_v, new_state_ret, new_state_ssm_re, new_state_ssm_im)
```

```python
import functools
import math

import jax
import jax.numpy as jnp
from jax import lax
from jax.experimental import pallas as pl
from jax.experimental.pallas import tpu as pltpu

F32 = jnp.float32
BF16 = jnp.bfloat16

D_MODEL = 1024
DEPTH = 2
GRID_W = 64
RET_HEADS = 4
RET_DK = 128
RET_DV = 128
RET_CHUNK = 128
RET_W = RET_HEADS * RET_DV
ATT_HEADS = 8
ATT_KV_HEADS = 2
ATT_GROUP = ATT_HEADS // ATT_KV_HEADS
ATT_HEAD_DIM = 64
ATT_W = ATT_HEADS * ATT_HEAD_DIM
ATT_KV_W = ATT_KV_HEADS * ATT_HEAD_DIM
ROPE_THETA = 10000.0
SSM_CH = 512
SSM_GROUP = 16
SSM_GROUPS = SSM_CH // SSM_GROUP
SSM_STATE = 64
SSM_N = SSM_GROUPS * SSM_STATE
N_EXPERTS = 16
N_EXPERT_GROUPS = 4
EXPERTS_PER_GROUP = N_EXPERTS // N_EXPERT_GROUPS
D_FF = 512
N_MOD = 6
EPS = 1e-6

C_RQ, C_RK, C_RV, C_RG = 0, 512, 1024, 1536
C_AQ, C_AK, C_AV, C_SU, C_G, C_END = 2048, 2560, 2688, 2816, 3328, 6400

VMEM_LIMIT = 52 * 1024 * 1024

TM_PROJ = 512
TM_MOE = 1024
S5_ROWS = 32
S5_SEG = 256
S5_TCH = 8
ADA_ROWS = 8
ADA_TN = 1536


def _dot(a, b):
    return jnp.dot(a, b, preferred_element_type=F32)


def _dot_nt(a, b):
    return lax.dot_general(a, b, (((1,), (1,)), ((), ())), preferred_element_type=F32)


def _dot_tn(a, b):
    return lax.dot_general(a, b, (((0,), (0,)), ((), ())), preferred_element_type=F32)


def _const_spec(shape):
    n = len(shape)
    return pl.BlockSpec(shape, lambda *_: (0,) * n, pipeline_mode=pl.Buffered(1))


def _params(sem, vmem=VMEM_LIMIT):
    return pltpu.CompilerParams(dimension_semantics=sem, vmem_limit_bytes=vmem)


def _mod_row_fn(mod_rows, tm):
    base, per_row = mod_rows
    return lambda i: base + (i * tm) // per_row


def _ada_kernel(c_ref, w_ref, b_ref, o_ref):
    c = c_ref[...]
    a = (c * jax.nn.sigmoid(c)).astype(BF16)
    o_ref[0] = _dot(a, w_ref[0].astype(BF16)) + b_ref[0]


def _ada(c_rows, w_ada, b_ada):
    n_col = N_MOD * D_MODEL
    return pl.pallas_call(
        _ada_kernel,
        out_shape=jax.ShapeDtypeStruct((DEPTH, ADA_ROWS, n_col), F32),
        grid=(DEPTH, n_col // ADA_TN),
        in_specs=[
            pl.BlockSpec((ADA_ROWS, D_MODEL), lambda l, j: (0, 0)),
            pl.BlockSpec((1, D_MODEL, ADA_TN), lambda l, j: (l, 0, j)),
            pl.BlockSpec((1, 1, ADA_TN), lambda l, j: (l, 0, j)),
        ],
        out_specs=pl.BlockSpec((1, ADA_ROWS, ADA_TN), lambda l, j: (l, 0, j)),
        compiler_params=_params(("arbitrary", "arbitrary")),
    )(c_rows, w_ada, b_ada.reshape(DEPTH, 1, n_col))


def _head_mean_sq(z, ones):
    z2 = z * z
    hi = z2.astype(BF16)
    lo = (z2 - hi.astype(F32)).astype(BF16)
    return (_dot(hi, ones) + _dot(lo, ones)) * (1.0 / ATT_HEAD_DIM)


def _inproj_kernel(*refs, rope):
    if rope:
        (x_ref, mod_ref, g1_ref, w_ref, ones_ref, gq_ref, gk_ref, wsw_ref, gqs_ref, gks_ref, cos_ref, sin_ref,
         rqkv_ref, rg_ref, aq_ref, ak_ref, av_ref, su_ref, gates_ref) = refs
    else:
        (x_ref, mod_ref, g1_ref, w_ref, ones_ref, gq_ref, gk_ref,
         rqkv_ref, rg_ref, aq_ref, ak_ref, av_ref, su_ref, gates_ref) = refs
    x = x_ref[...]
    xn = x * lax.rsqrt(jnp.mean(x * x, axis=-1, keepdims=True) + EPS) * g1_ref[...]
    h = (xn * (1.0 + mod_ref[0, 1:2, :]) + mod_ref[0, 0:1, :]).astype(BF16)

    def seg(a, b):
        return _dot(h, w_ref[:, a:b])

    rqkv_ref[:, 0:RET_W] = (seg(C_RQ, C_RK) * (RET_DK ** -0.5)).astype(BF16)
    rqkv_ref[:, RET_W:3 * RET_W] = seg(C_RK, C_RG).astype(BF16)
    rg_ref[...] = seg(C_RG, C_AQ).astype(BF16)
    su_ref[...] = seg(C_SU, C_G).astype(BF16)
    for j in range(3):
        gates_ref[:, j * D_MODEL:(j + 1) * D_MODEL] = seg(C_G + j * D_MODEL, C_G + (j + 1) * D_MODEL).astype(BF16)
    av_ref[...] = seg(C_AV, C_SU)

    zq = seg(C_AQ, C_AK)
    inv_q = lax.rsqrt(_head_mean_sq(zq, ones_ref[...]) + EPS)
    yq = zq * inv_q * gq_ref[...]
    zk = seg(C_AK, C_AV)
    inv_k = lax.rsqrt(_head_mean_sq(zk, ones_ref[0:ATT_KV_W, 0:ATT_KV_W]) + EPS)
    yk = zk * inv_k * gk_ref[...]
    if rope:
        cos = cos_ref[...]
        sin = sin_ref[...]
        cos_q = jnp.concatenate([cos] * (ATT_W // ATT_KV_W), axis=1)
        sin_q = jnp.concatenate([sin] * (ATT_W // ATT_KV_W), axis=1)
        yq_sw = _dot(h, wsw_ref[:, 0:ATT_W]) * inv_q * gqs_ref[...]
        yk_sw = _dot(h, wsw_ref[:, ATT_W:ATT_W + ATT_KV_W]) * inv_k * gks_ref[...]
        yq = yq * cos_q + yq_sw * sin_q
        yk = yk * cos + yk_sw * sin
    aq_ref[...] = yq.astype(BF16)
    ak_ref[...] = yk


def _inproj(x, mod, mod_rows, g1, w_in, ones, gq, gk, rope_args):
    n = x.shape[0]
    tm = TM_PROJ
    mod_row = _mod_row_fn(mod_rows, tm)
    rope = rope_args is not None
    row = lambda i: (i, 0)
    in_specs = [
        pl.BlockSpec((tm, D_MODEL), row),
        pl.BlockSpec((1, N_MOD, D_MODEL), lambda i: (mod_row(i), 0, 0)),
        _const_spec((1, D_MODEL)),
        _const_spec((D_MODEL, C_END)),
        _const_spec((ATT_W, ATT_W)),
        _const_spec((1, ATT_W)),
        _const_spec((1, ATT_KV_W)),
    ]
    args = [x, mod, g1, w_in, ones, gq, gk]
    if rope:
        w_sw, gqs, gks, cos, sin = rope_args
        n_pos = cos.shape[0] // tm
        in_specs += [
            _const_spec((D_MODEL, ATT_W + ATT_KV_W)),
            _const_spec((1, ATT_W)),
            _const_spec((1, ATT_KV_W)),
            pl.BlockSpec((tm, ATT_KV_W), lambda i: (i % n_pos, 0)),
            pl.BlockSpec((tm, ATT_KV_W), lambda i: (i % n_pos, 0)),
        ]
        args += [w_sw, gqs, gks, cos, sin]
    out_shape = (
        jax.ShapeDtypeStruct((n, 3 * RET_W), BF16),
        jax.ShapeDtypeStruct((n, RET_W), BF16),
        jax.ShapeDtypeStruct((n, ATT_W), BF16),
        jax.ShapeDtypeStruct((n, ATT_KV_W), F32),
        jax.ShapeDtypeStruct((n, ATT_KV_W), F32),
        jax.ShapeDtypeStruct((n, SSM_CH), BF16),
        jax.ShapeDtypeStruct((n, 3 * D_MODEL), BF16),
    )
    out_specs = tuple(pl.BlockSpec((tm, s.shape[1]), row) for s in out_shape)
    return pl.pallas_call(
        functools.partial(_inproj_kernel, rope=rope),
        out_shape=out_shape,
        grid=(n // tm,),
        in_specs=in_specs,
        out_specs=out_specs,
        compiler_params=_params(("parallel",)),
    )(*args)


T_DEC_F, T_DEC_B, T_XI_F, T_ZETA_F, T_XI_B, T_ZETA_B, T_CD_F, T_CD_B, N_TAB = range(9)


def _ret_kernel(*refs, has_s0, bb, seq):
    if has_s0:
        lg_ref, q_ref, k_ref, v_ref, rg_ref, gn_ref, s0_ref, o_ref, st_ref, tab_ref, acc_ref = refs
    else:
        lg_ref, q_ref, k_ref, v_ref, rg_ref, gn_ref, o_ref, st_ref, tab_ref, acc_ref = refs
    c = RET_CHUNK
    head = pl.program_id(1)
    lgf = lg_ref[0, head]
    lgb = lg_ref[1, head]
    t = lax.broadcasted_iota(jnp.int32, (c, c), 0).astype(F32)
    s = lax.broadcasted_iota(jnp.int32, (c, c), 1).astype(F32)
    tab_ref[T_DEC_F] = jnp.where(t >= s, jnp.exp(lgf * jnp.maximum(t - s, 0.0)), 0.0)
    tab_ref[T_DEC_B] = jnp.where(s >= t, jnp.exp(lgb * jnp.maximum(s - t, 0.0)), 0.0)
    tab_ref[T_XI_F] = jnp.exp(lgf * (t + 1.0))
    tab_ref[T_ZETA_F] = jnp.exp(lgf * (c - 1.0 - t))
    tab_ref[T_XI_B] = jnp.exp(lgb * (c - t))
    tab_ref[T_ZETA_B] = jnp.exp(lgb * t)
    tab_ref[T_CD_F] = jnp.exp(lgf * (c + 0.0 * t))
    tab_ref[T_CD_B] = jnp.exp(lgb * (c + 0.0 * t))
    if has_s0:
        st_ref[...] = s0_ref[...]
    else:
        st_ref[...] = jnp.zeros_like(st_ref)
    acc_ref[...] = jnp.zeros_like(acc_ref)
    n_chunk = seq // c

    def one_direction(rb, d, off, t_dec, t_xi, t_zeta, t_cd):
        rows = pl.ds(off, c)
        q = q_ref[rb, rows, :]
        k = k_ref[rb, rows, :]
        v = v_ref[rb, rows, :]
        p = (_dot_nt(q, k) * tab_ref[t_dec]).astype(BF16)
        st = st_ref[rb, d, 0]
        o = _dot(p, v) + _dot(q, st.astype(BF16)) * tab_ref[t_xi]
        acc_ref[rb, rows, :] += o
        kz = (k.astype(F32) * tab_ref[t_zeta]).astype(BF16)
        st_ref[rb, d, 0] = tab_ref[t_cd] * st + _dot_tn(kz, v)

    def body(i, carry):
        off_f = pl.multiple_of(i * c, c)
        off_b = pl.multiple_of((n_chunk - 1 - i) * c, c)
        for rb in range(bb):
            one_direction(rb, 0, off_f, T_DEC_F, T_XI_F, T_ZETA_F, T_CD_F)
            one_direction(rb, 1, off_b, T_DEC_B, T_XI_B, T_ZETA_B, T_CD_B)
        return carry

    lax.fori_loop(0, n_chunk, body, 0)

    def finish(i, carry):
        rows = pl.ds(pl.multiple_of(i * c, c), c)
        for rb in range(bb):
            o = acc_ref[rb, rows, :]
            ro = o * lax.rsqrt(jnp.mean(o * o, axis=-1, keepdims=True) + EPS) * gn_ref[...]
            g = rg_ref[rb, rows, :].astype(F32)
            o_ref[rb, rows, :] = (ro * (g * jax.nn.sigmoid(g))).astype(BF16)
        return carry

    lax.fori_loop(0, n_chunk, finish, 0)


def _retention(rqkv, rg, lg, gn, s0, bb):
    b, seq, _ = rqkv.shape
    has_s0 = s0 is not None
    blk = (bb, seq, RET_DK)
    st_spec = pl.BlockSpec((bb, 2, 1, RET_DK, RET_DV), lambda i, h: (i, 0, h, 0, 0))
    in_specs = [
        pl.BlockSpec(memory_space=pltpu.SMEM),
        pl.BlockSpec(blk, lambda i, h: (i, 0, h)),
        pl.BlockSpec(blk, lambda i, h: (i, 0, RET_HEADS + h)),
        pl.BlockSpec(blk, lambda i, h: (i, 0, 2 * RET_HEADS + h)),
        pl.BlockSpec(blk, lambda i, h: (i, 0, h)),
        pl.BlockSpec((1, RET_DV), lambda i, h: (0, h)),
    ]
    args = [lg, rqkv, rqkv, rqkv, rg, gn]
    if has_s0:
        in_specs.append(st_spec)
        args.append(s0)
    return pl.pallas_call(
        functools.partial(_ret_kernel, has_s0=has_s0, bb=bb, seq=seq),
        out_shape=(
            jax.ShapeDtypeStruct((b, seq, RET_W), BF16),
            jax.ShapeDtypeStruct((b, 2, RET_HEADS, RET_DK, RET_DV), F32),
        ),
        grid=(b // bb, RET_HEADS),
        in_specs=in_specs,
        out_specs=(pl.BlockSpec(blk, lambda i, h: (i, 0, h)), st_spec),
        scratch_shapes=[
            pltpu.VMEM((N_TAB, RET_CHUNK, RET_CHUNK), F32),
            pltpu.VMEM((bb, seq, RET_DV), F32),
        ],
        compiler_params=_params(("parallel", "parallel")),
    )(*args)


def _attn_kernel(q_ref, k_ref, v_ref, o_ref, *, qb, n_keys, kc):
    m_rows = ATT_GROUP * qb
    q = q_ref[0].reshape(m_rows, ATT_HEAD_DIM)
    scale = ATT_HEAD_DIM ** -0.5

    def fold(carry, rows):
        m, l, acc = carry
        s = _dot_nt(q, k_ref[0, 0, rows, :]) * scale
        m_new = jnp.maximum(m, jnp.max(s, axis=1, keepdims=True))
        alpha = jnp.exp(m - m_new)
        p = jnp.exp(s - m_new)
        l = alpha * l + jnp.sum(p, axis=1, keepdims=True)
        acc = alpha * acc + _dot(p.astype(BF16), v_ref[0, 0, rows, :])
        return m_new, l, acc

    carry = (jnp.full((m_rows, 1), -jnp.inf, F32), jnp.zeros((m_rows, 1), F32),
             jnp.zeros((m_rows, ATT_HEAD_DIM), F32))
    n_full = n_keys // kc
    carry = lax.fori_loop(0, n_full, lambda j, cr: fold(cr, pl.ds(pl.multiple_of(j * kc, kc), kc)), carry)
    if n_keys > n_full * kc:
        carry = fold(carry, pl.ds(n_full * kc, n_keys - n_full * kc))
    _, l, acc = carry
    o_ref[0] = (acc / l).reshape(ATT_GROUP, qb, ATT_HEAD_DIM).astype(BF16)


def _attention(q, k, v, qb, kc):
    b, _, seq, _ = q.shape
    n_keys = k.shape[2]
    q_spec = pl.BlockSpec((1, ATT_GROUP, qb, ATT_HEAD_DIM), lambda i, g, j: (i, g, j, 0))
    kv_spec = pl.BlockSpec((1, 1, n_keys, ATT_HEAD_DIM), lambda i, g, j: (i, g, 0, 0))
    return pl.pallas_call(
        functools.partial(_attn_kernel, qb=qb, n_keys=n_keys, kc=kc),
        out_shape=jax.ShapeDtypeStruct(q.shape, BF16),
        grid=(b, ATT_KV_HEADS, seq // qb),
        in_specs=[q_spec, kv_spec, kv_spec],
        out_specs=q_spec,
        compiler_params=_params(("parallel", "parallel", "parallel")),
    )(q, k, v)


def _s5_kernel(*refs, two_pass):
    if two_pass:
        u_ref, b_ref, c_ref, a_ref, aseg_ref, s0_ref, y_ref, st_ref, bu_ref = refs
    else:
        u_ref, b_ref, c_ref, a_ref, y_ref, fin_ref, st_ref, bu_ref = refs
    n = SSM_N
    d = pl.program_id(0)
    p = pl.program_id(1)
    i = pl.program_id(2)
    n_i = pl.num_programs(2)
    n_pass = 2 if two_pass else 1

    @pl.when((p == 0) & (i == 0))
    def _():
        st_ref[...] = jnp.zeros_like(st_ref)

    u2 = u_ref[...].reshape(S5_TCH * S5_ROWS, SSM_CH)
    bu_ref[...] = _dot(u2, b_ref[0])
    ar = a_ref[0, 0:1, :]
    ai = a_ref[0, 1:2, :]

    def step(t, carry):
        te = t + d * (S5_TCH - 1 - 2 * t)
        rows = pl.ds(pl.multiple_of(te * S5_ROWS, S5_ROWS), S5_ROWS)
        xr = st_ref[:, 0:n]
        xi = st_ref[:, n:2 * n]
        nr = ar * xr - ai * xi + bu_ref[rows, 0:n]
        ni = ar * xi + ai * xr + bu_ref[rows, n:2 * n]
        st_ref[:, 0:n] = nr
        st_ref[:, n:2 * n] = ni
        bu_ref[rows, 0:n] = nr
        bu_ref[rows, n:2 * n] = ni
        return carry

    lax.fori_loop(0, S5_TCH, step, 0)

    @pl.when(p == n_pass - 1)
    def _():
        y = _dot(bu_ref[...].astype(BF16), c_ref[0])
        y_ref[0] = y.reshape(S5_TCH, S5_ROWS, SSM_CH)

    if two_pass:
        n_seg = S5_ROWS // s0_ref.shape[1]
        gr = aseg_ref[0, 0:1, :]
        gi = aseg_ref[0, 1:2, :]

        def chain(order):
            for b in range(s0_ref.shape[1]):
                cr = s0_ref[0, b:b + 1, 0:n]
                ci = s0_ref[0, b:b + 1, n:2 * n]
                for sgm in order:
                    r = b * n_seg + sgm
                    fr = st_ref[r:r + 1, 0:n]
                    fi = st_ref[r:r + 1, n:2 * n]
                    st_ref[r:r + 1, 0:n] = cr
                    st_ref[r:r + 1, n:2 * n] = ci
                    cr, ci = gr * cr - gi * ci + fr, gr * ci + gi * cr + fi

        @pl.when((p == 0) & (i == n_i - 1) & (d == 0))
        def _():
            chain(range(n_seg))

        @pl.when((p == 0) & (i == n_i - 1) & (d == 1))
        def _():
            chain(range(n_seg - 1, -1, -1))
    else:
        @pl.when(i == n_i - 1)
        def _():
            fin_ref[0] = st_ref[...]


def _s5(u_tm, bmat, cmat, a_bar, a_seg, s0):
    two_pass = s0 is not None
    n_pass = 2 if two_pass else 1
    n_i = S5_SEG // S5_TCH

    def chunk(d, p, i):
        return i + d * (n_i - 1 - 2 * i)

    in_specs = [
        pl.BlockSpec((S5_TCH, S5_ROWS, SSM_CH), lambda d, p, i: (chunk(d, p, i), 0, 0)),
        pl.BlockSpec((1, SSM_CH, 2 * SSM_N), lambda d, p, i: (d, 0, 0)),
        pl.BlockSpec((1, 2 * SSM_N, SSM_CH), lambda d, p, i: (d, 0, 0)),
        pl.BlockSpec((1, 2, SSM_N), lambda d, p, i: (d, 0, 0)),
    ]
    args = [u_tm, bmat, cmat, a_bar]
    y_shape = jax.ShapeDtypeStruct((2, S5_SEG, S5_ROWS, SSM_CH), F32)
    if two_pass:
        in_specs += [
            pl.BlockSpec((1, 2, SSM_N), lambda d, p, i: (d, 0, 0)),
            pl.BlockSpec((1, s0.shape[1], 2 * SSM_N), lambda d, p, i: (d, 0, 0)),
        ]
        args += [a_seg, s0]
        y_spec = pl.BlockSpec((1, S5_TCH, S5_ROWS, SSM_CH), lambda d, p, i: (d, chunk(d, p, i * p), 0, 0))
        out_shape, out_specs = y_shape, y_spec
    else:
        y_spec = pl.BlockSpec((1, S5_TCH, S5_ROWS, SSM_CH), lambda d, p, i: (d, chunk(d, p, i), 0, 0))
        out_shape = (y_shape, jax.ShapeDtypeStruct((2, S5_ROWS, 2 * SSM_N), F32))
        out_specs = (y_spec, pl.BlockSpec((1, S5_ROWS, 2 * SSM_N), lambda d, p, i: (d, 0, 0)))
    return pl.pallas_call(
        functools.partial(_s5_kernel, two_pass=two_pass),
        out_shape=out_shape,
        grid=(2, n_pass, n_i),
        in_specs=in_specs,
        out_specs=out_specs,
        scratch_shapes=[
            pltpu.VMEM((S5_ROWS, 2 * SSM_N), F32),
            pltpu.VMEM((S5_TCH * S5_ROWS, 2 * SSM_N), F32),
        ],
        compiler_params=_params(("arbitrary", "arbitrary", "arbitrary")),
    )(*args)


def _route(probs):
    epg = EXPERTS_PER_GROUP
    groups = [probs[g * epg:(g + 1) * epg] for g in range(N_EXPERT_GROUPS)]
    scores = []
    for grp in groups:
        best = None
        for a in range(epg):
            for b in range(a + 1, epg):
                pair = grp[a] + grp[b]
                best = pair if best is None else jnp.maximum(best, pair)
        scores.append(best)
    top_score = scores[0]
    top_group = jnp.zeros_like(scores[0], dtype=jnp.int32)
    for g in range(1, N_EXPERT_GROUPS):
        better = scores[g] > top_score
        top_score = jnp.where(better, scores[g], top_score)
        top_group = jnp.where(better, g, top_group)
    sel = []
    for j in range(epg):
        v = groups[0][j]
        for g in range(1, N_EXPERT_GROUPS):
            v = jnp.where(top_group == g, groups[g][j], v)
        sel.append(v)
    v1 = sel[0]
    i1 = jnp.zeros_like(top_group)
    for j in range(1, epg):
        better = sel[j] > v1
        v1 = jnp.where(better, sel[j], v1)
        i1 = jnp.where(better, j, i1)
    v2 = jnp.full_like(v1, -1.0)
    i2 = jnp.zeros_like(top_group)
    for j in range(epg):
        better = (sel[j] > v2) & (i1 != j)
        v2 = jnp.where(better, sel[j], v2)
        i2 = jnp.where(better, j, i2)
    total = v1 + v2
    w1 = v1 / total
    w2 = v2 / total
    out = []
    for g in range(N_EXPERT_GROUPS):
        for j in range(epg):
            w = jnp.where(i1 == j, w1, jnp.where(i2 == j, w2, 0.0))
            out.append(jnp.where(top_group == g, w, 0.0))
    return out


def _merge_kernel(x_ref, ro_ref, att_ref, y_ref, su_ref, gates_ref, mod_ref, wro_ref, wao_ref, wglu_ref, wout_ref,
                  g2_ref, dskip_ref, wrt_ref, br_ref, x1_ref, h2_ref, comb_ref):
    ret_branch = _dot(ro_ref[...], wro_ref[...])
    att_branch = _dot(att_ref[...], wao_ref[...])
    y = y_ref[0] + y_ref[1] + dskip_ref[...] * su_ref[...].astype(F32)
    glu = _dot(jax.nn.gelu(y).astype(BF16), wglu_ref[...])
    ssm_branch = glu[:, 0:D_MODEL] * jax.nn.sigmoid(glu[:, D_MODEL:2 * D_MODEL])

    def gate(j):
        return jax.nn.sigmoid(gates_ref[:, j * D_MODEL:(j + 1) * D_MODEL].astype(F32))

    merged = gate(0) * ret_branch + gate(1) * att_branch + gate(2) * ssm_branch
    x1 = x_ref[...] + mod_ref[0, 2:3, :] * _dot(merged.astype(BF16), wout_ref[...])
    x1_ref[...] = x1
    xn = x1 * lax.rsqrt(jnp.mean(x1 * x1, axis=-1, keepdims=True) + EPS) * g2_ref[...]
    h2 = xn * (1.0 + mod_ref[0, 4:5, :]) + mod_ref[0, 3:4, :]
    h2_ref[...] = h2.astype(BF16)

    logits = lax.dot_general(wrt_ref[...], h2, (((1,), (1,)), ((), ())), precision=lax.Precision.HIGHEST,
                             preferred_element_type=F32) + br_ref[...]
    e = jnp.exp(logits - jnp.max(logits, axis=0, keepdims=True))
    probs = e / jnp.sum(e, axis=0, keepdims=True)
    rows = _route([probs[j:j + 1, :] for j in range(N_EXPERTS)])
    comb_ref[...] = jnp.concatenate(rows, axis=0)


def _merge(x, ro, att, y, su, gates, mod, mod_rows, wro, wao, wglu, wout, g2, dskip, wrt, br):
    n = x.shape[0]
    tm = TM_PROJ
    mod_row = _mod_row_fn(mod_rows, tm)
    row = lambda i: (i, 0)
    in_specs = [
        pl.BlockSpec((tm, D_MODEL), row),
        pl.BlockSpec((tm, RET_W), row),
        pl.BlockSpec((tm, ATT_W), row),
        pl.BlockSpec((2, tm, SSM_CH), lambda i: (0, i, 0)),
        pl.BlockSpec((tm, SSM_CH), row),
        pl.BlockSpec((tm, 3 * D_MODEL), row),
        pl.BlockSpec((1, N_MOD, D_MODEL), lambda i: (mod_row(i), 0, 0)),
        _const_spec((RET_W, D_MODEL)),
        _const_spec((ATT_W, D_MODEL)),
        _const_spec((SSM_CH, 2 * D_MODEL)),
        _const_spec((D_MODEL, D_MODEL)),
        _const_spec((1, D_MODEL)),
        _const_spec((1, SSM_CH)),
        _const_spec((N_EXPERTS, D_MODEL)),
        _const_spec((N_EXPERTS, 1)),
    ]
    return pl.pallas_call(
        _merge_kernel,
        out_shape=(
            jax.ShapeDtypeStruct((n, D_MODEL), F32),
            jax.ShapeDtypeStruct((n, D_MODEL), BF16),
            jax.ShapeDtypeStruct((N_EXPERTS, n), F32),
        ),
        grid=(n // tm,),
        in_specs=in_specs,
        out_specs=(
            pl.BlockSpec((tm, D_MODEL), row),
            pl.BlockSpec((tm, D_MODEL), row),
            pl.BlockSpec((N_EXPERTS, tm), lambda i: (0, i)),
        ),
        compiler_params=_params(("parallel",)),
    )(x, ro, att, y, su, gates, mod, wro, wao, wglu, wout, g2, dskip, wrt, br)


def _moe_kernel(*refs, final):
    if final:
        h2_ref, comb_ref, x1_ref, mod_ref, wg_ref, wu_ref, wd_ref, fg_ref, o_ref, acc_ref = refs
    else:
        h2_ref, comb_ref, x1_ref, mod_ref, wg_ref, wu_ref, wd_ref, o_ref, acc_ref = refs
    e = pl.program_id(1)

    @pl.when(e == 0)
    def _():
        acc_ref[...] = jnp.zeros_like(acc_ref)

    h = h2_ref[...]
    g = _dot(h, wg_ref[0])
    u = _dot(h, wu_ref[0])
    comb = comb_ref[...]
    lane = lax.broadcasted_iota(jnp.int32, comb.shape, 1)
    w = jnp.sum(jnp.where(lane == e, comb, 0.0), axis=1, keepdims=True)
    hid = (g * jax.nn.sigmoid(g) * u * w).astype(BF16)
    acc_ref[...] += _dot(hid, wd_ref[0])

    @pl.when(e == N_EXPERTS - 1)
    def _():
        x2 = x1_ref[...] + mod_ref[0, 5:6, :] * acc_ref[...]
        if final:
            x2 = x2 * lax.rsqrt(jnp.mean(x2 * x2, axis=-1, keepdims=True) + EPS) * fg_ref[...]
        o_ref[...] = x2


def _moe(h2, comb, x1, mod, mod_rows, wg, wu, wd, final_g):
    n = h2.shape[0]
    tm = TM_MOE
    mod_row = _mod_row_fn(mod_rows, tm)
    final = final_g is not None
    row = lambda i, e: (i, 0)
    in_specs = [
        pl.BlockSpec((tm, D_MODEL), row),
        pl.BlockSpec((tm, N_EXPERTS), row),
        pl.BlockSpec((tm, D_MODEL), row),
        pl.BlockSpec((1, N_MOD, D_MODEL), lambda i, e: (mod_row(i), 0, 0)),
        pl.BlockSpec((1, D_MODEL, D_FF), lambda i, e: (e, 0, 0)),
        pl.BlockSpec((1, D_MODEL, D_FF), lambda i, e: (e, 0, 0)),
        pl.BlockSpec((1, D_FF, D_MODEL), lambda i, e: (e, 0, 0)),
    ]
    args = [h2, comb, x1, mod, wg, wu, wd]
    if final:
        in_specs.append(pl.BlockSpec((1, D_MODEL), lambda i, e: (0, 0)))
        args.append(final_g)
    return pl.pallas_call(
        functools.partial(_moe_kernel, final=final),
        out_shape=jax.ShapeDtypeStruct((n, D_MODEL), F32),
        grid=(n // tm, N_EXPERTS),
        in_specs=in_specs,
        out_specs=pl.BlockSpec((tm, D_MODEL), row),
        scratch_shapes=[pltpu.VMEM((tm, D_MODEL), F32)],
        compiler_params=_params(("parallel", "arbitrary")),
    )(*args)


def _block_diag(m):
    g, r, c = m.shape
    eye = jnp.eye(g, dtype=m.dtype)
    return (eye[:, None, :, None] * m[:, :, None, :]).reshape(g * r, g * c)


def _s5_direction_params(a_re, a_im, log_dt, b_re, b_im, c_re, c_im):
    step = jnp.exp(log_dt)[:, None]
    mag = jnp.exp(a_re * step)
    ar = mag * jnp.cos(a_im * step)
    ai = mag * jnp.sin(a_im * step)
    den = a_re * a_re + a_im * a_im
    fr = ((ar - 1.0) * a_re + ai * a_im) / den
    fi = (ai * a_re - (ar - 1.0) * a_im) / den
    bbr = fr[:, :, None] * b_re - fi[:, :, None] * b_im
    bbi = fr[:, :, None] * b_im + fi[:, :, None] * b_re
    bmat = jnp.concatenate([_block_diag(bbr.transpose(0, 2, 1)), _block_diag(bbi.transpose(0, 2, 1))], axis=1)
    cmat = jnp.concatenate([_block_diag(c_re.transpose(0, 2, 1)), _block_diag(-c_im.transpose(0, 2, 1))], axis=0)
    a_bar = jnp.stack([ar.reshape(-1), ai.reshape(-1)])
    sr, si = ar, ai
    for _ in range(int(math.log2(S5_SEG))):
        sr, si = sr * sr - si * si, 2.0 * sr * si
    a_seg = jnp.stack([sr.reshape(-1), si.reshape(-1)])
    return bmat.astype(BF16), cmat.astype(BF16), a_bar, a_seg


def _rope_tables(n_tokens):
    rows = n_tokens // GRID_W
    row = jnp.repeat(jnp.arange(rows, dtype=F32), GRID_W)
    col = jnp.tile(jnp.arange(GRID_W, dtype=F32), rows)
    n_freq = ATT_HEAD_DIM // 4
    inv = ROPE_THETA ** (-jnp.arange(n_freq, dtype=F32) / n_freq)
    ang = jnp.concatenate([row[:, None] * inv, col[:, None] * inv], axis=-1)
    cos = jnp.repeat(jnp.cos(ang), 2, axis=-1)
    sin = jnp.repeat(jnp.sin(ang), 2, axis=-1) * jnp.tile(jnp.array([-1.0, 1.0], F32), ATT_HEAD_DIM // 2)
    return jnp.tile(cos, (1, ATT_KV_HEADS)), jnp.tile(sin, (1, ATT_KV_HEADS))


def _swap_pairs(a):
    idx = jnp.arange(a.shape[-1]) ^ 1
    return jnp.take(a, idx, axis=-1)


def _to_heads(a, batch, heads):
    return a.reshape(batch, -1, heads, ATT_HEAD_DIM).transpose(0, 2, 1, 3)


def _layer(x, batch, mod, mod_rows, p, ctx):
    n = x.shape[0]
    seq = n // batch
    latent = ctx is not None
    rope_args = (p['w_sw'], p['gqs'], p['gks'], p['cos'], p['sin']) if latent else None
    rqkv, rg, aq, ak, av, su, gates = _inproj(x, mod, mod_rows, p['g1'], p['w_in'], p['ones'], p['gq'], p['gk'],
                                              rope_args)

    ro, ret_state = _retention(rqkv.reshape(batch, seq, 3 * RET_W), rg.reshape(batch, seq, RET_W), p['ret_lg'],
                               p['ret_gn'], ctx[2] if latent else None, bb=1 if latent else 8)

    q_h = _to_heads(aq, batch, ATT_HEADS)
    k_all = ak.reshape(batch, seq, ATT_KV_W)
    v_all = av.reshape(batch, seq, ATT_KV_W)
    if latent:
        k_all = jnp.concatenate([k_all, ctx[0].reshape(batch, -1, ATT_KV_W)], axis=1)
        v_all = jnp.concatenate([v_all, ctx[1].reshape(batch, -1, ATT_KV_W)], axis=1)
    k_h = _to_heads(k_all.astype(BF16).reshape(-1, ATT_KV_W), batch, ATT_KV_HEADS)
    v_h = _to_heads(v_all.astype(BF16).reshape(-1, ATT_KV_W), batch, ATT_KV_HEADS)
    att = _attention(q_h, k_h, v_h, qb=128 if latent else seq, kc=min(512, k_h.shape[2]))
    att = att.transpose(0, 2, 1, 3).reshape(n, ATT_W)

    u_tm = su.reshape(S5_ROWS, S5_SEG, SSM_CH).transpose(1, 0, 2)
    if latent:
        s0 = jnp.concatenate([ctx[3].reshape(batch, 2, SSM_N), ctx[4].reshape(batch, 2, SSM_N)], axis=-1)
        y_tm = _s5(u_tm, p['s5_b'], p['s5_c'], p['s5_a'], p['s5_aseg'], s0.transpose(1, 0, 2))
        ssm_state = None
    else:
        y_tm, ssm_state = _s5(u_tm, p['s5_b'], p['s5_c'], p['s5_a'], None, None)
    y = y_tm.transpose(0, 2, 1, 3).reshape(2, n, SSM_CH)

    x1, h2, comb_t = _merge(x, ro.reshape(n, RET_W), att, y, su, gates, mod, mod_rows, p['w_ret_out'], p['w_att_out'],
                            p['w_glu'], p['w_out'], p['g2'], p['ssm_d'], p['w_router_t'], p['b_router'])
    x2 = _moe(h2, comb_t.T, x1, mod, mod_rows, p['w_gate'], p['w_up'], p['w_down'], p.get('final_g'))
    if latent:
        return x2, None
    return x2, (ak, av, ret_state, ssm_state)


def kernel(x_prompt, x_sample, c, cache_attn_k, cache_attn_v, state_ret, state_ssm_re, state_ssm_im, c_ctx, w_ada,
           b_ada, norm1_g, norm2_g, w_in, ret_lg_f, ret_lg_b, ret_norm_g, w_ret_out, att_q_norm_g, att_k_norm_g,
           w_att_out, a_re_f, a_im_f, a_re_b, a_im_b, log_dt_f, log_dt_b, ssm_b_re, ssm_b_im, ssm_c_re, ssm_c_im,
           ssm_d, w_glu, w_out, w_router, b_router, w_gate, w_up, w_down, final_norm_g):
    batch, seq, _ = x_prompt.shape
    dec_batch, dec_seq, _ = x_sample.shape
    assert batch * seq == S5_ROWS * S5_SEG and dec_batch * dec_seq == S5_ROWS * S5_SEG
    assert seq == S5_SEG and dec_seq % S5_SEG == 0

    c_rows = jnp.zeros((ADA_ROWS, D_MODEL), F32).at[0].set(c_ctx).at[1:1 + dec_batch].set(c)
    mod_all = _ada(c_rows, w_ada, b_ada).reshape(DEPTH, ADA_ROWS, N_MOD, D_MODEL)

    cos, sin = _rope_tables(dec_seq)
    ones = _block_diag(jnp.ones((ATT_HEADS, ATT_HEAD_DIM, ATT_HEAD_DIM), BF16))

    xp = x_prompt.reshape(batch * seq, D_MODEL)
    xs = x_sample.reshape(dec_batch * dec_seq, D_MODEL)
    ks_, vs_, rets_, ssms_ = [], [], [], []
    for l in range(DEPTH):
        gq = jnp.tile(att_q_norm_g[l], ATT_HEADS)[None]
        gk = jnp.tile(att_k_norm_g[l], ATT_KV_HEADS)[None]
        w_qk = w_in[l][:, C_AQ:C_AV]
        fwd = _s5_direction_params(a_re_f[l], a_im_f[l], log_dt_f[l], ssm_b_re[l], ssm_b_im[l], ssm_c_re[l],
                                   ssm_c_im[l])
        bwd = _s5_direction_params(a_re_b[l], a_im_b[l], log_dt_b[l], ssm_b_re[l], ssm_b_im[l], ssm_c_re[l],
                                   ssm_c_im[l])
        p = {
            'g1': norm1_g[l][None], 'g2': norm2_g[l][None], 'w_in': w_in[l].astype(BF16), 'ones': ones,
            'gq': gq, 'gk': gk, 'w_sw': _swap_pairs(w_qk).astype(BF16), 'gqs': _swap_pairs(gq),
            'gks': _swap_pairs(gk), 'cos': cos, 'sin': sin,
            'ret_lg': jnp.stack([ret_lg_f[l], ret_lg_b[l]]), 'ret_gn': ret_norm_g[l][None],
            'w_ret_out': w_ret_out[l].astype(BF16), 'w_att_out': w_att_out[l].astype(BF16),
            's5_b': jnp.stack([fwd[0], bwd[0]]), 's5_c': jnp.stack([fwd[1], bwd[1]]),
            's5_a': jnp.stack([fwd[2], bwd[2]]), 's5_aseg': jnp.stack([fwd[3], bwd[3]]),
            'ssm_d': ssm_d[l][None], 'w_glu': w_glu[l].astype(BF16), 'w_out': w_out[l].astype(BF16),
            'w_router_t': w_router.T, 'b_router': b_router[:, None],
            'w_gate': w_gate[l].astype(BF16), 'w_up': w_up[l].astype(BF16), 'w_down': w_down[l].astype(BF16),
        }
        if l == DEPTH - 1:
            p['final_g'] = final_norm_g[None]
        mod = mod_all[l]

        xp, (k_c, v_c, st_r, st_s) = _layer(xp, batch, mod, (0, batch * seq), p, None)
        ks_.append(k_c.reshape(batch, seq, ATT_KV_HEADS, ATT_HEAD_DIM))
        vs_.append(v_c.reshape(batch, seq, ATT_KV_HEADS, ATT_HEAD_DIM))
        rets_.append(st_r)
        ssms_.append(st_s)

        ctx = (cache_attn_k[:, l], cache_attn_v[:, l], state_ret[:, l], state_ssm_re[:, l], state_ssm_im[:, l])
        xs, _ = _layer(xs, dec_batch, mod, (1, dec_seq), p, ctx)

    ssm_all = jnp.stack(ssms_, axis=1)
    ssm_all = ssm_all.transpose(2, 1, 0, 3)
    new_re = ssm_all[..., 0:SSM_N].reshape(batch, DEPTH, 2, SSM_GROUPS, SSM_STATE)
    new_im = ssm_all[..., SSM_N:].reshape(batch, DEPTH, 2, SSM_GROUPS, SSM_STATE)
    return (xp.reshape(batch, seq, D_MODEL), xs.reshape(dec_batch, dec_seq, D_MODEL),
            jnp.stack(ks_, axis=1), jnp.stack(vs_, axis=1), jnp.stack(rets_, axis=1), new_re, new_im)
```

```python
import functools
import math

import jax
import jax.numpy as jnp
from jax import lax
from jax.experimental import pallas as pl
from jax.experimental.pallas import tpu as pltpu

F32 = jnp.float32
BF16 = jnp.bfloat16

D_MODEL = 1024
DEPTH = 2
GRID_W = 64
RET_HEADS = 4
RET_DK = 128
RET_DV = 128
RET_CHUNK = 128
RET_W = RET_HEADS * RET_DV
ATT_HEADS = 8
ATT_KV_HEADS = 2
ATT_GROUP = ATT_HEADS // ATT_KV_HEADS
ATT_HEAD_DIM = 64
ATT_W = ATT_HEADS * ATT_HEAD_DIM
ATT_KV_W = ATT_KV_HEADS * ATT_HEAD_DIM
ROPE_THETA = 10000.0
SSM_CH = 512
SSM_GROUP = 16
SSM_GROUPS = SSM_CH // SSM_GROUP
SSM_STATE = 64
SSM_N = SSM_GROUPS * SSM_STATE
N_EXPERTS = 16
N_EXPERT_GROUPS = 4
EXPERTS_PER_GROUP = N_EXPERTS // N_EXPERT_GROUPS
D_FF = 512
N_MOD = 6
EPS = 1e-6

C_RQ, C_RK, C_RV, C_RG = 0, 512, 1024, 1536
C_AQ, C_AK, C_AV, C_SU, C_G, C_END = 2048, 2560, 2688, 2816, 3328, 6400

VMEM_LIMIT = 52 * 1024 * 1024

ATT_Q_SCALE = ATT_HEAD_DIM ** -0.5 * math.log2(math.e)
ATT_QB = 256
ATT_KC = 256

TM_PROJ = 512
TM_MOE = 1024
S5_ROWS = 32
S5_SEG = 256
S5_TCH = 8
ADA_ROWS = 8
ADA_TN = 1536


def _dot(a, b):
    return jnp.dot(a, b, preferred_element_type=F32)


def _dot_nt(a, b):
    return lax.dot_general(a, b, (((1,), (1,)), ((), ())), preferred_element_type=F32)


def _dot_tn(a, b):
    return lax.dot_general(a, b, (((0,), (0,)), ((), ())), preferred_element_type=F32)


def _const_spec(shape):
    n = len(shape)
    return pl.BlockSpec(shape, lambda *_: (0,) * n, pipeline_mode=pl.Buffered(1))


def _params(sem, vmem=VMEM_LIMIT):
    return pltpu.CompilerParams(dimension_semantics=sem, vmem_limit_bytes=vmem)


def _mod_row_fn(mod_rows, tm):
    base, per_row = mod_rows
    return lambda i: base + (i * tm) // per_row


def _ada_kernel(c_ref, w_ref, b_ref, o_ref):
    c = c_ref[...]
    a = (c * jax.nn.sigmoid(c)).astype(BF16)
    o_ref[0] = _dot(a, w_ref[0].astype(BF16)) + b_ref[0]


def _ada(c_rows, w_ada, b_ada):
    n_col = N_MOD * D_MODEL
    return pl.pallas_call(
        _ada_kernel,
        out_shape=jax.ShapeDtypeStruct((DEPTH, ADA_ROWS, n_col), F32),
        grid=(DEPTH, n_col // ADA_TN),
        in_specs=[
            pl.BlockSpec((ADA_ROWS, D_MODEL), lambda l, j: (0, 0)),
            pl.BlockSpec((1, D_MODEL, ADA_TN), lambda l, j: (l, 0, j)),
            pl.BlockSpec((1, 1, ADA_TN), lambda l, j: (l, 0, j)),
        ],
        out_specs=pl.BlockSpec((1, ADA_ROWS, ADA_TN), lambda l, j: (l, 0, j)),
        compiler_params=_params(("arbitrary", "arbitrary")),
    )(c_rows, w_ada, b_ada.reshape(DEPTH, 1, n_col))


def _head_mean_sq(z, ones):
    z2 = z * z
    hi = z2.astype(BF16)
    lo = (z2 - hi.astype(F32)).astype(BF16)
    return (_dot(hi, ones) + _dot(lo, ones)) * (1.0 / ATT_HEAD_DIM)


def _inproj_kernel(*refs, rope):
    if rope:
        (x_ref, mod_ref, g1_ref, w_ref, ones_ref, gq_ref, gk_ref, wsw_ref, gqs_ref, gks_ref, cos_ref, sin_ref,
         rqkv_ref, rg_ref, aq_ref, ak_ref, av_ref, su_ref, gates_ref) = refs
    else:
        (x_ref, mod_ref, g1_ref, w_ref, ones_ref, gq_ref, gk_ref,
         rqkv_ref, rg_ref, aq_ref, ak_ref, av_ref, su_ref, gates_ref) = refs
    x = x_ref[...]
    xn = x * lax.rsqrt(jnp.mean(x * x, axis=-1, keepdims=True) + EPS) * g1_ref[...]
    h = (xn * (1.0 + mod_ref[0, 1:2, :]) + mod_ref[0, 0:1, :]).astype(BF16)

    def seg(a, b):
        return _dot(h, w_ref[:, a:b])

    rqkv_ref[:, 0:RET_W] = (seg(C_RQ, C_RK) * (RET_DK ** -0.5)).astype(BF16)
    rqkv_ref[:, RET_W:3 * RET_W] = seg(C_RK, C_RG).astype(BF16)
    rg_ref[...] = seg(C_RG, C_AQ).astype(BF16)
    su_ref[...] = seg(C_SU, C_G).astype(BF16)
    for j in range(3):
        gates_ref[:, j * D_MODEL:(j + 1) * D_MODEL] = seg(C_G + j * D_MODEL, C_G + (j + 1) * D_MODEL).astype(BF16)
    av_ref[...] = seg(C_AV, C_SU)

    zq = seg(C_AQ, C_AK)
    inv_q = lax.rsqrt(_head_mean_sq(zq, ones_ref[...]) + EPS)
    yq = zq * inv_q * gq_ref[...]
    zk = seg(C_AK, C_AV)
    inv_k = lax.rsqrt(_head_mean_sq(zk, ones_ref[0:ATT_KV_W, 0:ATT_KV_W]) + EPS)
    yk = zk * inv_k * gk_ref[...]
    if rope:
        cos = cos_ref[...]
        sin = sin_ref[...]
        cos_q = jnp.concatenate([cos] * (ATT_W // ATT_KV_W), axis=1)
        sin_q = jnp.concatenate([sin] * (ATT_W // ATT_KV_W), axis=1)
        yq_sw = _dot(h, wsw_ref[:, 0:ATT_W]) * inv_q * gqs_ref[...]
        yk_sw = _dot(h, wsw_ref[:, ATT_W:ATT_W + ATT_KV_W]) * inv_k * gks_ref[...]
        yq = yq * cos_q + yq_sw * sin_q
        yk = yk * cos + yk_sw * sin
    aq_ref[...] = (yq * ATT_Q_SCALE).astype(BF16)
    ak_ref[...] = yk


def _inproj(x, mod, mod_rows, g1, w_in, ones, gq, gk, rope_args):
    n = x.shape[0]
    tm = TM_PROJ
    mod_row = _mod_row_fn(mod_rows, tm)
    rope = rope_args is not None
    row = lambda i: (i, 0)
    in_specs = [
        pl.BlockSpec((tm, D_MODEL), row),
        pl.BlockSpec((1, N_MOD, D_MODEL), lambda i: (mod_row(i), 0, 0)),
        _const_spec((1, D_MODEL)),
        _const_spec((D_MODEL, C_END)),
        _const_spec((ATT_W, ATT_W)),
        _const_spec((1, ATT_W)),
        _const_spec((1, ATT_KV_W)),
    ]
    args = [x, mod, g1, w_in, ones, gq, gk]
    if rope:
        w_sw, gqs, gks, cos, sin = rope_args
        n_pos = cos.shape[0] // tm
        in_specs += [
            _const_spec((D_MODEL, ATT_W + ATT_KV_W)),
            _const_spec((1, ATT_W)),
            _const_spec((1, ATT_KV_W)),
            pl.BlockSpec((tm, ATT_KV_W), lambda i: (i % n_pos, 0)),
            pl.BlockSpec((tm, ATT_KV_W), lambda i: (i % n_pos, 0)),
        ]
        args += [w_sw, gqs, gks, cos, sin]
    out_shape = (
        jax.ShapeDtypeStruct((n, 3 * RET_W), BF16),
        jax.ShapeDtypeStruct((n, RET_W), BF16),
        jax.ShapeDtypeStruct((n, ATT_W), BF16),
        jax.ShapeDtypeStruct((n, ATT_KV_W), F32),
        jax.ShapeDtypeStruct((n, ATT_KV_W), F32),
        jax.ShapeDtypeStruct((n, SSM_CH), BF16),
        jax.ShapeDtypeStruct((n, 3 * D_MODEL), BF16),
    )
    out_specs = tuple(pl.BlockSpec((tm, s.shape[1]), row) for s in out_shape)
    return pl.pallas_call(
        functools.partial(_inproj_kernel, rope=rope),
        out_shape=out_shape,
        grid=(n // tm,),
        in_specs=in_specs,
        out_specs=out_specs,
        compiler_params=_params(("parallel",)),
    )(*args)


T_DEC_F, T_DEC_B, T_XI_F, T_ZETA_F, T_XI_B, T_ZETA_B, T_CD_F, T_CD_B, N_TAB = range(9)


def _ret_kernel(*refs, has_s0, bb, seq):
    if has_s0:
        lg_ref, q_ref, k_ref, v_ref, rg_ref, gn_ref, s0_ref, o_ref, st_ref, tab_ref, acc_ref = refs
    else:
        lg_ref, q_ref, k_ref, v_ref, rg_ref, gn_ref, o_ref, st_ref, tab_ref, acc_ref = refs
    c = RET_CHUNK
    head = pl.program_id(1)
    lgf = lg_ref[0, head]
    lgb = lg_ref[1, head]
    t = lax.broadcasted_iota(jnp.int32, (c, c), 0).astype(F32)
    s = lax.broadcasted_iota(jnp.int32, (c, c), 1).astype(F32)
    tab_ref[T_DEC_F] = jnp.where(t >= s, jnp.exp(lgf * jnp.maximum(t - s, 0.0)), 0.0)
    tab_ref[T_DEC_B] = jnp.where(s >= t, jnp.exp(lgb * jnp.maximum(s - t, 0.0)), 0.0)
    tab_ref[T_XI_F] = jnp.exp(lgf * (t + 1.0))
    tab_ref[T_ZETA_F] = jnp.exp(lgf * (c - 1.0 - t))
    tab_ref[T_XI_B] = jnp.exp(lgb * (c - t))
    tab_ref[T_ZETA_B] = jnp.exp(lgb * t)
    tab_ref[T_CD_F] = jnp.exp(lgf * (c + 0.0 * t))
    tab_ref[T_CD_B] = jnp.exp(lgb * (c + 0.0 * t))
    if has_s0:
        st_ref[...] = s0_ref[...]
    else:
        st_ref[...] = jnp.zeros_like(st_ref)
    acc_ref[...] = jnp.zeros_like(acc_ref)
    n_chunk = seq // c

    def one_direction(rb, d, off, t_dec, t_xi, t_zeta, t_cd):
        rows = pl.ds(off, c)
        q = q_ref[rb, rows, :]
        k = k_ref[rb, rows, :]
        v = v_ref[rb, rows, :]
        p = (_dot_nt(q, k) * tab_ref[t_dec]).astype(BF16)
        st = st_ref[rb, d, 0]
        o = _dot(p, v) + _dot(q, st.astype(BF16)) * tab_ref[t_xi]
        acc_ref[rb, rows, :] += o
        kz = (k.astype(F32) * tab_ref[t_zeta]).astype(BF16)
        st_ref[rb, d, 0] = tab_ref[t_cd] * st + _dot_tn(kz, v)

    def body(i, carry):
        off_f = pl.multiple_of(i * c, c)
        off_b = pl.multiple_of((n_chunk - 1 - i) * c, c)
        for rb in range(bb):
            one_direction(rb, 0, off_f, T_DEC_F, T_XI_F, T_ZETA_F, T_CD_F)
            one_direction(rb, 1, off_b, T_DEC_B, T_XI_B, T_ZETA_B, T_CD_B)
        return carry

    lax.fori_loop(0, n_chunk, body, 0)

    def finish(i, carry):
        rows = pl.ds(pl.multiple_of(i * c, c), c)
        for rb in range(bb):
            o = acc_ref[rb, rows, :]
            ro = o * lax.rsqrt(jnp.mean(o * o, axis=-1, keepdims=True) + EPS) * gn_ref[...]
            g = rg_ref[rb, rows, :].astype(F32)
            o_ref[rb, rows, :] = (ro * (g * jax.nn.sigmoid(g))).astype(BF16)
        return carry

    lax.fori_loop(0, n_chunk, finish, 0)


def _retention(rqkv, rg, lg, gn, s0, bb):
    b, seq, _ = rqkv.shape
    has_s0 = s0 is not None
    blk = (bb, seq, RET_DK)
    st_spec = pl.BlockSpec((bb, 2, 1, RET_DK, RET_DV), lambda i, h: (i, 0, h, 0, 0))
    in_specs = [
        pl.BlockSpec(memory_space=pltpu.SMEM),
        pl.BlockSpec(blk, lambda i, h: (i, 0, h)),
        pl.BlockSpec(blk, lambda i, h: (i, 0, RET_HEADS + h)),
        pl.BlockSpec(blk, lambda i, h: (i, 0, 2 * RET_HEADS + h)),
        pl.BlockSpec(blk, lambda i, h: (i, 0, h)),
        pl.BlockSpec((1, RET_DV), lambda i, h: (0, h)),
    ]
    args = [lg, rqkv, rqkv, rqkv, rg, gn]
    if has_s0:
        in_specs.append(st_spec)
        args.append(s0)
    return pl.pallas_call(
        functools.partial(_ret_kernel, has_s0=has_s0, bb=bb, seq=seq),
        out_shape=(
            jax.ShapeDtypeStruct((b, seq, RET_W), BF16),
            jax.ShapeDtypeStruct((b, 2, RET_HEADS, RET_DK, RET_DV), F32),
        ),
        grid=(b // bb, RET_HEADS),
        in_specs=in_specs,
        out_specs=(pl.BlockSpec(blk, lambda i, h: (i, 0, h)), st_spec),
        scratch_shapes=[
            pltpu.VMEM((N_TAB, RET_CHUNK, RET_CHUNK), F32),
            pltpu.VMEM((bb, seq, RET_DV), F32),
        ],
        compiler_params=_params(("parallel", "parallel")),
    )(*args)


ST_M, ST_ALPHA, N_ST = range(3)
ATT_V_ROWS = ATT_HEAD_DIM + 16
ATT_RING = 4


def _attn_kernel(q_ref, k_ref, v_ref, o_ref, s_ref, p_ref, acc_ref, st_ref, mx_ref):
    n_chunk = k_ref.shape[2]
    qb = q_ref.shape[3]
    q_t = jnp.concatenate([q_ref[0, h] for h in range(ATT_GROUP)], axis=1)

    def scores(j):
        return _dot(k_ref[0, 0, j], q_t)

    def weighted_values(j, slot):
        return _dot(v_ref[0, 0, j], p_ref[slot])

    def phase(j, slot, first=False, last=False):
        if not last:
            put_scores(j + 1, (slot + 1) % ATT_RING)
        if not first:
            acc_ref[...] = st_ref[ST_ALPHA] * acc_ref[...] + weighted_values(j - 1, (slot - 1) % ATT_RING)
        m = st_ref[ST_M]
        m_new = jnp.maximum(m, mx_ref[slot])
        st_ref[ST_ALPHA] = jnp.exp2(m - m_new)
        st_ref[ST_M] = m_new
        p_ref[slot] = jnp.exp2((s_ref[slot] - m_new).astype(BF16))

    def put_scores(j, slot):
        s = scores(j)
        s_ref[slot] = s
        mx_ref[slot] = jnp.max(s, axis=0, keepdims=True)

    st_ref[ST_M] = jnp.full(st_ref.shape[1:], -jnp.inf, F32)
    acc_ref[...] = jnp.zeros_like(acc_ref)
    put_scores(0, 0)
    phase(0, 0, first=True, last=n_chunk == 1)
    n_loop = max(n_chunk - 2, 0) // ATT_RING

    def revolution(i, carry):
        for t in range(1, ATT_RING + 1):
            phase(ATT_RING * i + t, t % ATT_RING)
        return carry

    lax.fori_loop(0, n_loop, revolution, 0)
    for j in range(1 + ATT_RING * n_loop, n_chunk):
        phase(j, j % ATT_RING, last=j == n_chunk - 1)
    acc = st_ref[ST_ALPHA] * acc_ref[...] + weighted_values(n_chunk - 1, (n_chunk - 1) % ATT_RING)
    out = (acc[0:ATT_HEAD_DIM] / acc[ATT_HEAD_DIM:ATT_HEAD_DIM + 1]).astype(BF16)
    for h in range(ATT_GROUP):
        o_ref[0, h] = out[:, h * qb:(h + 1) * qb]


def _attention(q_t, k, v_t, qb):
    b, _, _, seq = q_t.shape
    _, _, n_chunk, kc, _ = k.shape
    q_spec = pl.BlockSpec((1, ATT_GROUP, ATT_HEAD_DIM, qb), lambda i, g, j: (i, g, 0, j))
    return pl.pallas_call(
        _attn_kernel,
        out_shape=jax.ShapeDtypeStruct(q_t.shape, BF16),
        grid=(b, ATT_KV_HEADS, seq // qb),
        in_specs=[
            q_spec,
            pl.BlockSpec((1, 1, n_chunk, kc, ATT_HEAD_DIM), lambda i, g, j: (i, g, 0, 0, 0)),
            pl.BlockSpec((1, 1, n_chunk, ATT_V_ROWS, kc), lambda i, g, j: (i, g, 0, 0, 0)),
        ],
        out_specs=q_spec,
        scratch_shapes=[
            pltpu.VMEM((ATT_RING, kc, ATT_GROUP * qb), F32),
            pltpu.VMEM((ATT_RING, kc, ATT_GROUP * qb), BF16),
            pltpu.VMEM((ATT_V_ROWS, ATT_GROUP * qb), F32),
            pltpu.VMEM((N_ST, 1, ATT_GROUP * qb), F32),
            pltpu.VMEM((ATT_RING, 1, ATT_GROUP * qb), F32),
        ],
        compiler_params=_params(("parallel", "parallel", "parallel")),
    )(q_t, k, v_t)


def _s5_kernel(*refs, two_pass):
    if two_pass:
        u_ref, b_ref, c_ref, a_ref, aseg_ref, s0_ref, y_ref, st_ref, bu_ref = refs
    else:
        u_ref, b_ref, c_ref, a_ref, y_ref, fin_ref, st_ref, bu_ref = refs
    n = SSM_N
    d = pl.program_id(0)
    p = pl.program_id(1)
    i = pl.program_id(2)
    n_i = pl.num_programs(2)
    n_pass = 2 if two_pass else 1

    @pl.when((p == 0) & (i == 0))
    def _():
        st_ref[...] = jnp.zeros_like(st_ref)

    u2 = u_ref[...].reshape(S5_TCH * S5_ROWS, SSM_CH)
    bu_ref[...] = _dot(u2, b_ref[0])
    ar = a_ref[0, 0:1, :]
    ai = a_ref[0, 1:2, :]

    def step(t, carry):
        te = t + d * (S5_TCH - 1 - 2 * t)
        rows = pl.ds(pl.multiple_of(te * S5_ROWS, S5_ROWS), S5_ROWS)
        xr = st_ref[:, 0:n]
        xi = st_ref[:, n:2 * n]
        nr = ar * xr - ai * xi + bu_ref[rows, 0:n]
        ni = ar * xi + ai * xr + bu_ref[rows, n:2 * n]
        st_ref[:, 0:n] = nr
        st_ref[:, n:2 * n] = ni
        bu_ref[rows, 0:n] = nr
        bu_ref[rows, n:2 * n] = ni
        return carry

    lax.fori_loop(0, S5_TCH, step, 0)

    @pl.when(p == n_pass - 1)
    def _():
        y = _dot(bu_ref[...].astype(BF16), c_ref[0])
        y_ref[0] = y.reshape(S5_TCH, S5_ROWS, SSM_CH)

    if two_pass:
        n_seg = S5_ROWS // s0_ref.shape[1]
        gr = aseg_ref[0, 0:1, :]
        gi = aseg_ref[0, 1:2, :]

        def chain(order):
            for b in range(s0_ref.shape[1]):
                cr = s0_ref[0, b:b + 1, 0:n]
                ci = s0_ref[0, b:b + 1, n:2 * n]
                for sgm in order:
                    r = b * n_seg + sgm
                    fr = st_ref[r:r + 1, 0:n]
                    fi = st_ref[r:r + 1, n:2 * n]
                    st_ref[r:r + 1, 0:n] = cr
                    st_ref[r:r + 1, n:2 * n] = ci
                    cr, ci = gr * cr - gi * ci + fr, gr * ci + gi * cr + fi

        @pl.when((p == 0) & (i == n_i - 1) & (d == 0))
        def _():
            chain(range(n_seg))

        @pl.when((p == 0) & (i == n_i - 1) & (d == 1))
        def _():
            chain(range(n_seg - 1, -1, -1))
    else:
        @pl.when(i == n_i - 1)
        def _():
            fin_ref[0] = st_ref[...]


def _s5(u_tm, bmat, cmat, a_bar, a_seg, s0):
    two_pass = s0 is not None
    n_pass = 2 if two_pass else 1
    n_i = S5_SEG // S5_TCH

    def chunk(d, p, i):
        return i + d * (n_i - 1 - 2 * i)

    in_specs = [
        pl.BlockSpec((S5_TCH, S5_ROWS, SSM_CH), lambda d, p, i: (chunk(d, p, i), 0, 0)),
        pl.BlockSpec((1, SSM_CH, 2 * SSM_N), lambda d, p, i: (d, 0, 0)),
        pl.BlockSpec((1, 2 * SSM_N, SSM_CH), lambda d, p, i: (d, 0, 0)),
        pl.BlockSpec((1, 2, SSM_N), lambda d, p, i: (d, 0, 0)),
    ]
    args = [u_tm, bmat, cmat, a_bar]
    y_shape = jax.ShapeDtypeStruct((2, S5_SEG, S5_ROWS, SSM_CH), F32)
    if two_pass:
        in_specs += [
            pl.BlockSpec((1, 2, SSM_N), lambda d, p, i: (d, 0, 0)),
            pl.BlockSpec((1, s0.shape[1], 2 * SSM_N), lambda d, p, i: (d, 0, 0)),
        ]
        args += [a_seg, s0]
        y_spec = pl.BlockSpec((1, S5_TCH, S5_ROWS, SSM_CH), lambda d, p, i: (d, chunk(d, p, i * p), 0, 0))
        out_shape, out_specs = y_shape, y_spec
    else:
        y_spec = pl.BlockSpec((1, S5_TCH, S5_ROWS, SSM_CH), lambda d, p, i: (d, chunk(d, p, i), 0, 0))
        out_shape = (y_shape, jax.ShapeDtypeStruct((2, S5_ROWS, 2 * SSM_N), F32))
        out_specs = (y_spec, pl.BlockSpec((1, S5_ROWS, 2 * SSM_N), lambda d, p, i: (d, 0, 0)))
    return pl.pallas_call(
        functools.partial(_s5_kernel, two_pass=two_pass),
        out_shape=out_shape,
        grid=(2, n_pass, n_i),
        in_specs=in_specs,
        out_specs=out_specs,
        scratch_shapes=[
            pltpu.VMEM((S5_ROWS, 2 * SSM_N), F32),
            pltpu.VMEM((S5_TCH * S5_ROWS, 2 * SSM_N), F32),
        ],
        compiler_params=_params(("arbitrary", "arbitrary", "arbitrary")),
    )(*args)


def _route(probs):
    epg = EXPERTS_PER_GROUP
    groups = [probs[g * epg:(g + 1) * epg] for g in range(N_EXPERT_GROUPS)]
    scores = []
    for grp in groups:
        best = None
        for a in range(epg):
            for b in range(a + 1, epg):
                pair = grp[a] + grp[b]
                best = pair if best is None else jnp.maximum(best, pair)
        scores.append(best)
    top_score = scores[0]
    top_group = jnp.zeros_like(scores[0], dtype=jnp.int32)
    for g in range(1, N_EXPERT_GROUPS):
        better = scores[g] > top_score
        top_score = jnp.where(better, scores[g], top_score)
        top_group = jnp.where(better, g, top_group)
    sel = []
    for j in range(epg):
        v = groups[0][j]
        for g in range(1, N_EXPERT_GROUPS):
            v = jnp.where(top_group == g, groups[g][j], v)
        sel.append(v)
    v1 = sel[0]
    i1 = jnp.zeros_like(top_group)
    for j in range(1, epg):
        better = sel[j] > v1
        v1 = jnp.where(better, sel[j], v1)
        i1 = jnp.where(better, j, i1)
    v2 = jnp.full_like(v1, -1.0)
    i2 = jnp.zeros_like(top_group)
    for j in range(epg):
        better = (sel[j] > v2) & (i1 != j)
        v2 = jnp.where(better, sel[j], v2)
        i2 = jnp.where(better, j, i2)
    total = v1 + v2
    w1 = v1 / total
    w2 = v2 / total
    out = []
    for g in range(N_EXPERT_GROUPS):
        for j in range(epg):
            w = jnp.where(i1 == j, w1, jnp.where(i2 == j, w2, 0.0))
            out.append(jnp.where(top_group == g, w, 0.0))
    return out


def _merge_kernel(x_ref, ro_ref, att_ref, y_ref, su_ref, gates_ref, mod_ref, wro_ref, wao_ref, wglu_ref, wout_ref,
                  g2_ref, dskip_ref, wrt_ref, br_ref, x1_ref, h2_ref, comb_ref):
    ret_branch = _dot(ro_ref[...], wro_ref[...])
    att_branch = _dot(att_ref[...], wao_ref[...])
    y = y_ref[0] + y_ref[1] + dskip_ref[...] * su_ref[...].astype(F32)
    glu = _dot(jax.nn.gelu(y).astype(BF16), wglu_ref[...])
    ssm_branch = glu[:, 0:D_MODEL] * jax.nn.sigmoid(glu[:, D_MODEL:2 * D_MODEL])

    def gate(j):
        return jax.nn.sigmoid(gates_ref[:, j * D_MODEL:(j + 1) * D_MODEL].astype(F32))

    merged = gate(0) * ret_branch + gate(1) * att_branch + gate(2) * ssm_branch
    x1 = x_ref[...] + mod_ref[0, 2:3, :] * _dot(merged.astype(BF16), wout_ref[...])
    x1_ref[...] = x1
    xn = x1 * lax.rsqrt(jnp.mean(x1 * x1, axis=-1, keepdims=True) + EPS) * g2_ref[...]
    h2 = xn * (1.0 + mod_ref[0, 4:5, :]) + mod_ref[0, 3:4, :]
    h2_ref[...] = h2.astype(BF16)

    logits = lax.dot_general(wrt_ref[...], h2, (((1,), (1,)), ((), ())), precision=lax.Precision.HIGHEST,
                             preferred_element_type=F32) + br_ref[...]
    e = jnp.exp(logits - jnp.max(logits, axis=0, keepdims=True))
    probs = e / jnp.sum(e, axis=0, keepdims=True)
    rows = _route([probs[j:j + 1, :] for j in range(N_EXPERTS)])
    comb_ref[...] = jnp.concatenate(rows, axis=0)


def _merge(x, ro, att, y, su, gates, mod, mod_rows, wro, wao, wglu, wout, g2, dskip, wrt, br):
    n = x.shape[0]
    tm = TM_PROJ
    mod_row = _mod_row_fn(mod_rows, tm)
    row = lambda i: (i, 0)
    in_specs = [
        pl.BlockSpec((tm, D_MODEL), row),
        pl.BlockSpec((tm, RET_W), row),
        pl.BlockSpec((tm, ATT_W), row),
        pl.BlockSpec((2, tm, SSM_CH), lambda i: (0, i, 0)),
        pl.BlockSpec((tm, SSM_CH), row),
        pl.BlockSpec((tm, 3 * D_MODEL), row),
        pl.BlockSpec((1, N_MOD, D_MODEL), lambda i: (mod_row(i), 0, 0)),
        _const_spec((RET_W, D_MODEL)),
        _const_spec((ATT_W, D_MODEL)),
        _const_spec((SSM_CH, 2 * D_MODEL)),
        _const_spec((D_MODEL, D_MODEL)),
        _const_spec((1, D_MODEL)),
        _const_spec((1, SSM_CH)),
        _const_spec((N_EXPERTS, D_MODEL)),
        _const_spec((N_EXPERTS, 1)),
    ]
    return pl.pallas_call(
        _merge_kernel,
        out_shape=(
            jax.ShapeDtypeStruct((n, D_MODEL), F32),
            jax.ShapeDtypeStruct((n, D_MODEL), BF16),
            jax.ShapeDtypeStruct((N_EXPERTS, n), F32),
        ),
        grid=(n // tm,),
        in_specs=in_specs,
        out_specs=(
            pl.BlockSpec((tm, D_MODEL), row),
            pl.BlockSpec((tm, D_MODEL), row),
            pl.BlockSpec((N_EXPERTS, tm), lambda i: (0, i)),
        ),
        compiler_params=_params(("parallel",)),
    )(x, ro, att, y, su, gates, mod, wro, wao, wglu, wout, g2, dskip, wrt, br)


def _moe_kernel(*refs, final):
    if final:
        h2_ref, comb_ref, x1_ref, mod_ref, wg_ref, wu_ref, wd_ref, fg_ref, o_ref, acc_ref = refs
    else:
        h2_ref, comb_ref, x1_ref, mod_ref, wg_ref, wu_ref, wd_ref, o_ref, acc_ref = refs
    e = pl.program_id(1)

    @pl.when(e == 0)
    def _():
        acc_ref[...] = jnp.zeros_like(acc_ref)

    h = h2_ref[...]
    g = _dot(h, wg_ref[0])
    u = _dot(h, wu_ref[0])
    comb = comb_ref[...]
    lane = lax.broadcasted_iota(jnp.int32, comb.shape, 1)
    w = jnp.sum(jnp.where(lane == e, comb, 0.0), axis=1, keepdims=True)
    hid = (g * jax.nn.sigmoid(g) * u * w).astype(BF16)
    acc_ref[...] += _dot(hid, wd_ref[0])

    @pl.when(e == N_EXPERTS - 1)
    def _():
        x2 = x1_ref[...] + mod_ref[0, 5:6, :] * acc_ref[...]
        if final:
            x2 = x2 * lax.rsqrt(jnp.mean(x2 * x2, axis=-1, keepdims=True) + EPS) * fg_ref[...]
        o_ref[...] = x2


def _moe(h2, comb, x1, mod, mod_rows, wg, wu, wd, final_g):
    n = h2.shape[0]
    tm = TM_MOE
    mod_row = _mod_row_fn(mod_rows, tm)
    final = final_g is not None
    row = lambda i, e: (i, 0)
    in_specs = [
        pl.BlockSpec((tm, D_MODEL), row),
        pl.BlockSpec((tm, N_EXPERTS), row),
        pl.BlockSpec((tm, D_MODEL), row),
        pl.BlockSpec((1, N_MOD, D_MODEL), lambda i, e: (mod_row(i), 0, 0)),
        pl.BlockSpec((1, D_MODEL, D_FF), lambda i, e: (e, 0, 0)),
        pl.BlockSpec((1, D_MODEL, D_FF), lambda i, e: (e, 0, 0)),
        pl.BlockSpec((1, D_FF, D_MODEL), lambda i, e: (e, 0, 0)),
    ]
    args = [h2, comb, x1, mod, wg, wu, wd]
    if final:
        in_specs.append(pl.BlockSpec((1, D_MODEL), lambda i, e: (0, 0)))
        args.append(final_g)
    return pl.pallas_call(
        functools.partial(_moe_kernel, final=final),
        out_shape=jax.ShapeDtypeStruct((n, D_MODEL), F32),
        grid=(n // tm, N_EXPERTS),
        in_specs=in_specs,
        out_specs=pl.BlockSpec((tm, D_MODEL), row),
        scratch_shapes=[pltpu.VMEM((tm, D_MODEL), F32)],
        compiler_params=_params(("parallel", "arbitrary")),
    )(*args)


def _block_diag(m):
    g, r, c = m.shape
    eye = jnp.eye(g, dtype=m.dtype)
    return (eye[:, None, :, None] * m[:, :, None, :]).reshape(g * r, g * c)


def _s5_direction_params(a_re, a_im, log_dt, b_re, b_im, c_re, c_im):
    step = jnp.exp(log_dt)[:, None]
    mag = jnp.exp(a_re * step)
    ar = mag * jnp.cos(a_im * step)
    ai = mag * jnp.sin(a_im * step)
    den = a_re * a_re + a_im * a_im
    fr = ((ar - 1.0) * a_re + ai * a_im) / den
    fi = (ai * a_re - (ar - 1.0) * a_im) / den
    bbr = fr[:, :, None] * b_re - fi[:, :, None] * b_im
    bbi = fr[:, :, None] * b_im + fi[:, :, None] * b_re
    bmat = jnp.concatenate([_block_diag(bbr.transpose(0, 2, 1)), _block_diag(bbi.transpose(0, 2, 1))], axis=1)
    cmat = jnp.concatenate([_block_diag(c_re.transpose(0, 2, 1)), _block_diag(-c_im.transpose(0, 2, 1))], axis=0)
    a_bar = jnp.stack([ar.reshape(-1), ai.reshape(-1)])
    sr, si = ar, ai
    for _ in range(int(math.log2(S5_SEG))):
        sr, si = sr * sr - si * si, 2.0 * sr * si
    a_seg = jnp.stack([sr.reshape(-1), si.reshape(-1)])
    return bmat.astype(BF16), cmat.astype(BF16), a_bar, a_seg


def _rope_tables(n_tokens):
    rows = n_tokens // GRID_W
    row = jnp.repeat(jnp.arange(rows, dtype=F32), GRID_W)
    col = jnp.tile(jnp.arange(GRID_W, dtype=F32), rows)
    n_freq = ATT_HEAD_DIM // 4
    inv = ROPE_THETA ** (-jnp.arange(n_freq, dtype=F32) / n_freq)
    ang = jnp.concatenate([row[:, None] * inv, col[:, None] * inv], axis=-1)
    cos = jnp.repeat(jnp.cos(ang), 2, axis=-1)
    sin = jnp.repeat(jnp.sin(ang), 2, axis=-1) * jnp.tile(jnp.array([-1.0, 1.0], F32), ATT_HEAD_DIM // 2)
    return jnp.tile(cos, (1, ATT_KV_HEADS)), jnp.tile(sin, (1, ATT_KV_HEADS))


def _swap_pairs(a):
    idx = jnp.arange(a.shape[-1]) ^ 1
    return jnp.take(a, idx, axis=-1)


def _layer(x, batch, mod, mod_rows, p, ctx):
    n = x.shape[0]
    seq = n // batch
    latent = ctx is not None
    rope_args = (p['w_sw'], p['gqs'], p['gks'], p['cos'], p['sin']) if latent else None
    rqkv, rg, aq, ak, av, su, gates = _inproj(x, mod, mod_rows, p['g1'], p['w_in'], p['ones'], p['gq'], p['gk'],
                                              rope_args)

    ro, ret_state = _retention(rqkv.reshape(batch, seq, 3 * RET_W), rg.reshape(batch, seq, RET_W), p['ret_lg'],
                               p['ret_gn'], ctx[2] if latent else None, bb=1 if latent else 8)

    q_t = aq.reshape(batch, seq, ATT_HEADS, ATT_HEAD_DIM).transpose(0, 2, 3, 1)
    k_all = ak.reshape(batch, seq, ATT_KV_W)
    v_all = av.reshape(batch, seq, ATT_KV_W)
    if latent:
        k_all = jnp.concatenate([k_all, ctx[0].reshape(batch, -1, ATT_KV_W)], axis=1)
        v_all = jnp.concatenate([v_all, ctx[1].reshape(batch, -1, ATT_KV_W)], axis=1)
    chunked = (batch, -1, ATT_KC, ATT_KV_HEADS, ATT_HEAD_DIM)
    k_c = k_all.astype(BF16).reshape(chunked).transpose(0, 3, 1, 2, 4)
    v_t = v_all.astype(BF16).reshape(chunked).transpose(0, 3, 1, 4, 2)
    ones_row = jnp.zeros(v_t.shape[:3] + (ATT_V_ROWS - ATT_HEAD_DIM, ATT_KC), BF16).at[..., 0, :].set(1.0)
    v_t = jnp.concatenate([v_t, ones_row], axis=3)
    att_t = _attention(q_t, k_c, v_t, qb=ATT_QB)
    att = att_t.transpose(0, 3, 1, 2).reshape(n, ATT_W)

    u_tm = su.reshape(S5_ROWS, S5_SEG, SSM_CH).transpose(1, 0, 2)
    if latent:
        s0 = jnp.concatenate([ctx[3].reshape(batch, 2, SSM_N), ctx[4].reshape(batch, 2, SSM_N)], axis=-1)
        y_tm = _s5(u_tm, p['s5_b'], p['s5_c'], p['s5_a'], p['s5_aseg'], s0.transpose(1, 0, 2))
        ssm_state = None
    else:
        y_tm, ssm_state = _s5(u_tm, p['s5_b'], p['s5_c'], p['s5_a'], None, None)
    y = y_tm.transpose(0, 2, 1, 3).reshape(2, n, SSM_CH)

    x1, h2, comb_t = _merge(x, ro.reshape(n, RET_W), att, y, su, gates, mod, mod_rows, p['w_ret_out'], p['w_att_out'],
                            p['w_glu'], p['w_out'], p['g2'], p['ssm_d'], p['w_router_t'], p['b_router'])
    x2 = _moe(h2, comb_t.T, x1, mod, mod_rows, p['w_gate'], p['w_up'], p['w_down'], p.get('final_g'))
    if latent:
        return x2, None
    return x2, (ak, av, ret_state, ssm_state)


def kernel(x_prompt, x_sample, c, cache_attn_k, cache_attn_v, state_ret, state_ssm_re, state_ssm_im, c_ctx, w_ada,
           b_ada, norm1_g, norm2_g, w_in, ret_lg_f, ret_lg_b, ret_norm_g, w_ret_out, att_q_norm_g, att_k_norm_g,
           w_att_out, a_re_f, a_im_f, a_re_b, a_im_b, log_dt_f, log_dt_b, ssm_b_re, ssm_b_im, ssm_c_re, ssm_c_im,
           ssm_d, w_glu, w_out, w_router, b_router, w_gate, w_up, w_down, final_norm_g):
    batch, seq, _ = x_prompt.shape
    dec_batch, dec_seq, _ = x_sample.shape
    assert batch * seq == S5_ROWS * S5_SEG and dec_batch * dec_seq == S5_ROWS * S5_SEG
    assert seq == S5_SEG and dec_seq % S5_SEG == 0

    c_rows = jnp.zeros((ADA_ROWS, D_MODEL), F32).at[0].set(c_ctx).at[1:1 + dec_batch].set(c)
    mod_all = _ada(c_rows, w_ada, b_ada).reshape(DEPTH, ADA_ROWS, N_MOD, D_MODEL)

    cos, sin = _rope_tables(dec_seq)
    ones = _block_diag(jnp.ones((ATT_HEADS, ATT_HEAD_DIM, ATT_HEAD_DIM), BF16))

    xp = x_prompt.reshape(batch * seq, D_MODEL)
    xs = x_sample.reshape(dec_batch * dec_seq, D_MODEL)
    ks_, vs_, rets_, ssms_ = [], [], [], []
    for l in range(DEPTH):
        gq = jnp.tile(att_q_norm_g[l], ATT_HEADS)[None]
        gk = jnp.tile(att_k_norm_g[l], ATT_KV_HEADS)[None]
        w_qk = w_in[l][:, C_AQ:C_AV]
        fwd = _s5_direction_params(a_re_f[l], a_im_f[l], log_dt_f[l], ssm_b_re[l], ssm_b_im[l], ssm_c_re[l],
                                   ssm_c_im[l])
        bwd = _s5_direction_params(a_re_b[l], a_im_b[l], log_dt_b[l], ssm_b_re[l], ssm_b_im[l], ssm_c_re[l],
                                   ssm_c_im[l])
        p = {
            'g1': norm1_g[l][None], 'g2': norm2_g[l][None], 'w_in': w_in[l].astype(BF16), 'ones': ones,
            'gq': gq, 'gk': gk, 'w_sw': _swap_pairs(w_qk).astype(BF16), 'gqs': _swap_pairs(gq),
            'gks': _swap_pairs(gk), 'cos': cos, 'sin': sin,
            'ret_lg': jnp.stack([ret_lg_f[l], ret_lg_b[l]]), 'ret_gn': ret_norm_g[l][None],
            'w_ret_out': w_ret_out[l].astype(BF16), 'w_att_out': w_att_out[l].astype(BF16),
            's5_b': jnp.stack([fwd[0], bwd[0]]), 's5_c': jnp.stack([fwd[1], bwd[1]]),
            's5_a': jnp.stack([fwd[2], bwd[2]]), 's5_aseg': jnp.stack([fwd[3], bwd[3]]),
            'ssm_d': ssm_d[l][None], 'w_glu': w_glu[l].astype(BF16), 'w_out': w_out[l].astype(BF16),
            'w_router_t': w_router.T, 'b_router': b_router[:, None],
            'w_gate': w_gate[l].astype(BF16), 'w_up': w_up[l].astype(BF16), 'w_down': w_down[l].astype(BF16),
        }
        if l == DEPTH - 1:
            p['final_g'] = final_norm_g[None]
        mod = mod_all[l]

        xp, (k_c, v_c, st_r, st_s) = _layer(xp, batch, mod, (0, batch * seq), p, None)
        ks_.append(k_c.reshape(batch, seq, ATT_KV_HEADS, ATT_HEAD_DIM))
        vs_.append(v_c.reshape(batch, seq, ATT_KV_HEADS, ATT_HEAD_DIM))
        rets_.append(st_r)
        ssms_.append(st_s)

        ctx = (cache_attn_k[:, l], cache_attn_v[:, l], state_ret[:, l], state_ssm_re[:, l], state_ssm_im[:, l])
        xs, _ = _layer(xs, dec_batch, mod, (1, dec_seq), p, ctx)

    ssm_all = jnp.stack(ssms_, axis=1)
    ssm_all = ssm_all.transpose(2, 1, 0, 3)
    new_re = ssm_all[..., 0:SSM_N].reshape(batch, DEPTH, 2, SSM_GROUPS, SSM_STATE)
    new_im = ssm_all[..., SSM_N:].reshape(batch, DEPTH, 2, SSM_GROUPS, SSM_STATE)
    return (xp.reshape(batch, seq, D_MODEL), xs.reshape(dec_batch, dec_seq, D_MODEL),
            jnp.stack(ks_, axis=1), jnp.stack(vs_, axis=1), jnp.stack(rets_, axis=1), new_re, new_im)
```

```python
import functools
import math

import jax
import jax.numpy as jnp
from jax import lax
from jax.experimental import pallas as pl
from jax.experimental.pallas import tpu as pltpu

F32 = jnp.float32
BF16 = jnp.bfloat16

D_MODEL = 1024
DEPTH = 2
GRID_W = 64
RET_HEADS = 4
RET_DK = 128
RET_DV = 128
RET_CHUNK = 128
RET_W = RET_HEADS * RET_DV
ATT_HEADS = 8
ATT_KV_HEADS = 2
ATT_GROUP = ATT_HEADS // ATT_KV_HEADS
ATT_HEAD_DIM = 64
ATT_W = ATT_HEADS * ATT_HEAD_DIM
ATT_KV_W = ATT_KV_HEADS * ATT_HEAD_DIM
ROPE_THETA = 10000.0
SSM_CH = 512
SSM_GROUP = 16
SSM_GROUPS = SSM_CH // SSM_GROUP
SSM_STATE = 64
SSM_N = SSM_GROUPS * SSM_STATE
N_EXPERTS = 16
N_EXPERT_GROUPS = 4
EXPERTS_PER_GROUP = N_EXPERTS // N_EXPERT_GROUPS
D_FF = 512
N_MOD = 6
EPS = 1e-6

C_RQ, C_RK, C_RV, C_RG = 0, 512, 1024, 1536
C_AQ, C_AK, C_AV, C_SU, C_G, C_END = 2048, 2560, 2688, 2816, 3328, 6400

VMEM_LIMIT = 52 * 1024 * 1024

ATT_Q_SCALE = ATT_HEAD_DIM ** -0.5 * math.log2(math.e)
ATT_QB = 256
ATT_KC = 256

TM_PROJ = 512
TM_MOE = 1024
S5_ROWS = 32
S5_SEG = 256
S5_T = 16
S5_GL = 2 * SSM_STATE
S5_GB = 8
ADA_ROWS = 8
ADA_TN = 1536


def _dot(a, b):
    return jnp.dot(a, b, preferred_element_type=F32)


def _dot_nt(a, b):
    return lax.dot_general(a, b, (((1,), (1,)), ((), ())), preferred_element_type=F32)


def _dot_tn(a, b):
    return lax.dot_general(a, b, (((0,), (0,)), ((), ())), preferred_element_type=F32)


def _const_spec(shape):
    n = len(shape)
    return pl.BlockSpec(shape, lambda *_: (0,) * n, pipeline_mode=pl.Buffered(1))


def _params(sem, vmem=VMEM_LIMIT):
    return pltpu.CompilerParams(dimension_semantics=sem, vmem_limit_bytes=vmem)


def _mod_row_fn(mod_rows, tm):
    base, per_row = mod_rows
    return lambda i: base + (i * tm) // per_row


def _ada_kernel(c_ref, w_ref, b_ref, o_ref):
    c = c_ref[...]
    a = (c * jax.nn.sigmoid(c)).astype(BF16)
    o_ref[0] = _dot(a, w_ref[0].astype(BF16)) + b_ref[0]


def _ada(c_rows, w_ada, b_ada):
    n_col = N_MOD * D_MODEL
    return pl.pallas_call(
        _ada_kernel,
        out_shape=jax.ShapeDtypeStruct((DEPTH, ADA_ROWS, n_col), F32),
        grid=(DEPTH, n_col // ADA_TN),
        in_specs=[
            pl.BlockSpec((ADA_ROWS, D_MODEL), lambda l, j: (0, 0)),
            pl.BlockSpec((1, D_MODEL, ADA_TN), lambda l, j: (l, 0, j)),
            pl.BlockSpec((1, 1, ADA_TN), lambda l, j: (l, 0, j)),
        ],
        out_specs=pl.BlockSpec((1, ADA_ROWS, ADA_TN), lambda l, j: (l, 0, j)),
        compiler_params=_params(("arbitrary", "arbitrary")),
    )(c_rows, w_ada, b_ada.reshape(DEPTH, 1, n_col))


def _head_mean_sq(z, ones):
    z2 = z * z
    hi = z2.astype(BF16)
    lo = (z2 - hi.astype(F32)).astype(BF16)
    return (_dot(hi, ones) + _dot(lo, ones)) * (1.0 / ATT_HEAD_DIM)


def _inproj_kernel(*refs, rope):
    if rope:
        (x_ref, mod_ref, g1_ref, w_ref, ones_ref, gq_ref, gk_ref, wsw_ref, gqs_ref, gks_ref, cos_ref, sin_ref,
         rqkv_ref, rg_ref, aq_ref, ak_ref, av_ref, su_ref, gates_ref) = refs
    else:
        (x_ref, mod_ref, g1_ref, w_ref, ones_ref, gq_ref, gk_ref,
         rqkv_ref, rg_ref, aq_ref, ak_ref, av_ref, su_ref, gates_ref) = refs
    x = x_ref[...]
    xn = x * lax.rsqrt(jnp.mean(x * x, axis=-1, keepdims=True) + EPS) * g1_ref[...]
    h = (xn * (1.0 + mod_ref[0, 1:2, :]) + mod_ref[0, 0:1, :]).astype(BF16)

    def seg(a, b):
        return _dot(h, w_ref[:, a:b])

    rqkv_ref[:, 0:RET_W] = (seg(C_RQ, C_RK) * (RET_DK ** -0.5)).astype(BF16)
    rqkv_ref[:, RET_W:3 * RET_W] = seg(C_RK, C_RG).astype(BF16)
    rg_ref[...] = seg(C_RG, C_AQ).astype(BF16)
    su_ref[...] = seg(C_SU, C_G).astype(BF16)
    for j in range(3):
        gates_ref[:, j * D_MODEL:(j + 1) * D_MODEL] = seg(C_G + j * D_MODEL, C_G + (j + 1) * D_MODEL).astype(BF16)
    av_ref[...] = seg(C_AV, C_SU)

    zq = seg(C_AQ, C_AK)
    inv_q = lax.rsqrt(_head_mean_sq(zq, ones_ref[...]) + EPS)
    yq = zq * inv_q * gq_ref[...]
    zk = seg(C_AK, C_AV)
    inv_k = lax.rsqrt(_head_mean_sq(zk, ones_ref[0:ATT_KV_W, 0:ATT_KV_W]) + EPS)
    yk = zk * inv_k * gk_ref[...]
    if rope:
        cos = cos_ref[...]
        sin = sin_ref[...]
        cos_q = jnp.concatenate([cos] * (ATT_W // ATT_KV_W), axis=1)
        sin_q = jnp.concatenate([sin] * (ATT_W // ATT_KV_W), axis=1)
        yq_sw = _dot(h, wsw_ref[:, 0:ATT_W]) * inv_q * gqs_ref[...]
        yk_sw = _dot(h, wsw_ref[:, ATT_W:ATT_W + ATT_KV_W]) * inv_k * gks_ref[...]
        yq = yq * cos_q + yq_sw * sin_q
        yk = yk * cos + yk_sw * sin
    aq_ref[...] = (yq * ATT_Q_SCALE).astype(BF16)
    ak_ref[...] = yk


def _inproj(x, mod, mod_rows, g1, w_in, ones, gq, gk, rope_args):
    n = x.shape[0]
    tm = TM_PROJ
    mod_row = _mod_row_fn(mod_rows, tm)
    rope = rope_args is not None
    row = lambda i: (i, 0)
    in_specs = [
        pl.BlockSpec((tm, D_MODEL), row),
        pl.BlockSpec((1, N_MOD, D_MODEL), lambda i: (mod_row(i), 0, 0)),
        _const_spec((1, D_MODEL)),
        _const_spec((D_MODEL, C_END)),
        _const_spec((ATT_W, ATT_W)),
        _const_spec((1, ATT_W)),
        _const_spec((1, ATT_KV_W)),
    ]
    args = [x, mod, g1, w_in, ones, gq, gk]
    if rope:
        w_sw, gqs, gks, cos, sin = rope_args
        n_pos = cos.shape[0] // tm
        in_specs += [
            _const_spec((D_MODEL, ATT_W + ATT_KV_W)),
            _const_spec((1, ATT_W)),
            _const_spec((1, ATT_KV_W)),
            pl.BlockSpec((tm, ATT_KV_W), lambda i: (i % n_pos, 0)),
            pl.BlockSpec((tm, ATT_KV_W), lambda i: (i % n_pos, 0)),
        ]
        args += [w_sw, gqs, gks, cos, sin]
    out_shape = (
        jax.ShapeDtypeStruct((n, 3 * RET_W), BF16),
        jax.ShapeDtypeStruct((n, RET_W), BF16),
        jax.ShapeDtypeStruct((n, ATT_W), BF16),
        jax.ShapeDtypeStruct((n, ATT_KV_W), F32),
        jax.ShapeDtypeStruct((n, ATT_KV_W), F32),
        jax.ShapeDtypeStruct((n, SSM_CH), BF16),
        jax.ShapeDtypeStruct((n, 3 * D_MODEL), BF16),
    )
    out_specs = tuple(pl.BlockSpec((tm, s.shape[1]), row) for s in out_shape)
    return pl.pallas_call(
        functools.partial(_inproj_kernel, rope=rope),
        out_shape=out_shape,
        grid=(n // tm,),
        in_specs=in_specs,
        out_specs=out_specs,
        compiler_params=_params(("parallel",)),
    )(*args)


T_DEC_F, T_DEC_B, T_XI_F, T_ZETA_F, T_XI_B, T_ZETA_B, T_CD_F, T_CD_B, N_TAB = range(9)


def _ret_kernel(*refs, has_s0, bb, seq):
    if has_s0:
        lg_ref, q_ref, k_ref, v_ref, rg_ref, gn_ref, s0_ref, o_ref, st_ref, tab_ref, acc_ref = refs
    else:
        lg_ref, q_ref, k_ref, v_ref, rg_ref, gn_ref, o_ref, st_ref, tab_ref, acc_ref = refs
    c = RET_CHUNK
    head = pl.program_id(1)
    lgf = lg_ref[0, head]
    lgb = lg_ref[1, head]
    t = lax.broadcasted_iota(jnp.int32, (c, c), 0).astype(F32)
    s = lax.broadcasted_iota(jnp.int32, (c, c), 1).astype(F32)
    tab_ref[T_DEC_F] = jnp.where(t >= s, jnp.exp(lgf * jnp.maximum(t - s, 0.0)), 0.0)
    tab_ref[T_DEC_B] = jnp.where(s >= t, jnp.exp(lgb * jnp.maximum(s - t, 0.0)), 0.0)
    tab_ref[T_XI_F] = jnp.exp(lgf * (t + 1.0))
    tab_ref[T_ZETA_F] = jnp.exp(lgf * (c - 1.0 - t))
    tab_ref[T_XI_B] = jnp.exp(lgb * (c - t))
    tab_ref[T_ZETA_B] = jnp.exp(lgb * t)
    tab_ref[T_CD_F] = jnp.exp(lgf * (c + 0.0 * t))
    tab_ref[T_CD_B] = jnp.exp(lgb * (c + 0.0 * t))
    if has_s0:
        st_ref[...] = s0_ref[...]
    else:
        st_ref[...] = jnp.zeros_like(st_ref)
    acc_ref[...] = jnp.zeros_like(acc_ref)
    n_chunk = seq // c

    def one_direction(rb, d, off, t_dec, t_xi, t_zeta, t_cd):
        rows = pl.ds(off, c)
        q = q_ref[rb, rows, :]
        k = k_ref[rb, rows, :]
        v = v_ref[rb, rows, :]
        p = (_dot_nt(q, k) * tab_ref[t_dec]).astype(BF16)
        st = st_ref[rb, d, 0]
        o = _dot(p, v) + _dot(q, st.astype(BF16)) * tab_ref[t_xi]
        acc_ref[rb, rows, :] += o
        kz = (k.astype(F32) * tab_ref[t_zeta]).astype(BF16)
        st_ref[rb, d, 0] = tab_ref[t_cd] * st + _dot_tn(kz, v)

    def body(i, carry):
        off_f = pl.multiple_of(i * c, c)
        off_b = pl.multiple_of((n_chunk - 1 - i) * c, c)
        for rb in range(bb):
            one_direction(rb, 0, off_f, T_DEC_F, T_XI_F, T_ZETA_F, T_CD_F)
            one_direction(rb, 1, off_b, T_DEC_B, T_XI_B, T_ZETA_B, T_CD_B)
        return carry

    lax.fori_loop(0, n_chunk, body, 0)

    def finish(i, carry):
        rows = pl.ds(pl.multiple_of(i * c, c), c)
        for rb in range(bb):
            o = acc_ref[rb, rows, :]
            ro = o * lax.rsqrt(jnp.mean(o * o, axis=-1, keepdims=True) + EPS) * gn_ref[...]
            g = rg_ref[rb, rows, :].astype(F32)
            o_ref[rb, rows, :] = (ro * (g * jax.nn.sigmoid(g))).astype(BF16)
        return carry

    lax.fori_loop(0, n_chunk, finish, 0)


def _retention(rqkv, rg, lg, gn, s0, bb):
    b, seq, _ = rqkv.shape
    has_s0 = s0 is not None
    blk = (bb, seq, RET_DK)
    st_spec = pl.BlockSpec((bb, 2, 1, RET_DK, RET_DV), lambda i, h: (i, 0, h, 0, 0))
    in_specs = [
        pl.BlockSpec(memory_space=pltpu.SMEM),
        pl.BlockSpec(blk, lambda i, h: (i, 0, h)),
        pl.BlockSpec(blk, lambda i, h: (i, 0, RET_HEADS + h)),
        pl.BlockSpec(blk, lambda i, h: (i, 0, 2 * RET_HEADS + h)),
        pl.BlockSpec(blk, lambda i, h: (i, 0, h)),
        pl.BlockSpec((1, RET_DV), lambda i, h: (0, h)),
    ]
    args = [lg, rqkv, rqkv, rqkv, rg, gn]
    if has_s0:
        in_specs.append(st_spec)
        args.append(s0)
    return pl.pallas_call(
        functools.partial(_ret_kernel, has_s0=has_s0, bb=bb, seq=seq),
        out_shape=(
            jax.ShapeDtypeStruct((b, seq, RET_W), BF16),
            jax.ShapeDtypeStruct((b, 2, RET_HEADS, RET_DK, RET_DV), F32),
        ),
        grid=(b // bb, RET_HEADS),
        in_specs=in_specs,
        out_specs=(pl.BlockSpec(blk, lambda i, h: (i, 0, h)), st_spec),
        scratch_shapes=[
            pltpu.VMEM((N_TAB, RET_CHUNK, RET_CHUNK), F32),
            pltpu.VMEM((bb, seq, RET_DV), F32),
        ],
        compiler_params=_params(("parallel", "parallel")),
    )(*args)


ST_M, ST_ALPHA, N_ST = range(3)
ATT_V_ROWS = ATT_HEAD_DIM + 16
ATT_RING = 4


def _attn_kernel(q_ref, k_ref, v_ref, o_ref, s_ref, p_ref, acc_ref, st_ref, mx_ref):
    n_chunk = k_ref.shape[2]
    qb = q_ref.shape[3]
    q_t = jnp.concatenate([q_ref[0, h] for h in range(ATT_GROUP)], axis=1)

    def scores(j):
        return _dot(k_ref[0, 0, j], q_t)

    def weighted_values(j, slot):
        return _dot(v_ref[0, 0, j], p_ref[slot])

    def phase(j, slot, first=False, last=False):
        if not last:
            put_scores(j + 1, (slot + 1) % ATT_RING)
        if not first:
            acc_ref[...] = st_ref[ST_ALPHA] * acc_ref[...] + weighted_values(j - 1, (slot - 1) % ATT_RING)
        m = st_ref[ST_M]
        m_new = jnp.maximum(m, mx_ref[slot])
        st_ref[ST_ALPHA] = jnp.exp2(m - m_new)
        st_ref[ST_M] = m_new
        p_ref[slot] = jnp.exp2((s_ref[slot] - m_new).astype(BF16))

    def put_scores(j, slot):
        s = scores(j)
        s_ref[slot] = s
        mx_ref[slot] = jnp.max(s, axis=0, keepdims=True)

    st_ref[ST_M] = jnp.full(st_ref.shape[1:], -jnp.inf, F32)
    acc_ref[...] = jnp.zeros_like(acc_ref)
    put_scores(0, 0)
    phase(0, 0, first=True, last=n_chunk == 1)
    n_loop = max(n_chunk - 2, 0) // ATT_RING

    def revolution(i, carry):
        for t in range(1, ATT_RING + 1):
            phase(ATT_RING * i + t, t % ATT_RING)
        return carry

    lax.fori_loop(0, n_loop, revolution, 0)
    for j in range(1 + ATT_RING * n_loop, n_chunk):
        phase(j, j % ATT_RING, last=j == n_chunk - 1)
    acc = st_ref[ST_ALPHA] * acc_ref[...] + weighted_values(n_chunk - 1, (n_chunk - 1) % ATT_RING)
    out = (acc[0:ATT_HEAD_DIM] / acc[ATT_HEAD_DIM:ATT_HEAD_DIM + 1]).astype(BF16)
    for h in range(ATT_GROUP):
        o_ref[0, h] = out[:, h * qb:(h + 1) * qb]


def _attention(q_t, k, v_t, qb):
    b, _, _, seq = q_t.shape
    _, _, n_chunk, kc, _ = k.shape
    q_spec = pl.BlockSpec((1, ATT_GROUP, ATT_HEAD_DIM, qb), lambda i, g, j: (i, g, 0, j))
    return pl.pallas_call(
        _attn_kernel,
        out_shape=jax.ShapeDtypeStruct(q_t.shape, BF16),
        grid=(b, ATT_KV_HEADS, seq // qb),
        in_specs=[
            q_spec,
            pl.BlockSpec((1, 1, n_chunk, kc, ATT_HEAD_DIM), lambda i, g, j: (i, g, 0, 0, 0)),
            pl.BlockSpec((1, 1, n_chunk, ATT_V_ROWS, kc), lambda i, g, j: (i, g, 0, 0, 0)),
        ],
        out_specs=q_spec,
        scratch_shapes=[
            pltpu.VMEM((ATT_RING, kc, ATT_GROUP * qb), F32),
            pltpu.VMEM((ATT_RING, kc, ATT_GROUP * qb), BF16),
            pltpu.VMEM((ATT_V_ROWS, ATT_GROUP * qb), F32),
            pltpu.VMEM((N_ST, 1, ATT_GROUP * qb), F32),
            pltpu.VMEM((ATT_RING, 1, ATT_GROUP * qb), F32),
        ],
        compiler_params=_params(("parallel", "parallel", "parallel")),
    )(q_t, k, v_t)


def _s5_kernel(*refs, two_pass):
    if two_pass:
        u_ref, tw_ref, wy_ref, a_ref, aseg_ref, s0_ref, y_ref, v_ref, init_ref = refs
    else:
        u_ref, tw_ref, wy_ref, a_ref, y_ref, fin_ref, v_ref = refs
    gb = u_ref.shape[0]
    n_chunk = u_ref.shape[1] // S5_ROWS
    out_w = S5_T * SSM_GROUP
    for d in range(2):
        for j in range(gb):
            r = _dot(u_ref[j], tw_ref[d, j])
            if d == 0:
                y_ref[j] = r[:, 0:out_w]
            else:
                y_ref[j] += r[:, 0:out_w]
            v_ref[d, :, j * S5_GL:(j + 1) * S5_GL] = r[:, out_w:out_w + S5_GL]

    def times(coef_ref, d, x):
        swapped = jnp.concatenate(
            [pltpu.roll(x[:, j * S5_GL:(j + 1) * S5_GL], SSM_STATE, axis=1) for j in range(gb)], axis=1)
        return coef_ref[d, 0:1, :] * x + coef_ref[d, 1:2, :] * swapped

    def scan(init, keep_states):
        def step(s, xs):
            nxt = []
            for d in range(2):
                c = s + d * (n_chunk - 1 - 2 * s)
                rows = pl.ds(pl.multiple_of(c * S5_ROWS, S5_ROWS), S5_ROWS)
                v = v_ref[d, rows, :]
                if keep_states:
                    v_ref[d, rows, :] = xs[d]
                nxt.append(times(a_ref, d, xs[d]) + v)
            return tuple(nxt)

        return lax.fori_loop(0, n_chunk, step, init)

    zero = jnp.zeros((S5_ROWS, gb * S5_GL), F32)
    if two_pass:
        local = scan((zero, zero), keep_states=False)
        n_batch = s0_ref.shape[1]
        n_seg = S5_ROWS // n_batch
        for d in range(2):
            order = range(n_seg) if d == 0 else range(n_seg - 1, -1, -1)
            for b in range(n_batch):
                cur = s0_ref[d, b:b + 1, :]
                for sgm in order:
                    r = b * n_seg + sgm
                    init_ref[d, r:r + 1, :] = cur
                    cur = times(aseg_ref, d, cur) + local[d][r:r + 1, :]
        final = scan((init_ref[0], init_ref[1]), keep_states=True)
    else:
        final = scan((zero, zero), keep_states=True)
        fin_ref[0] = final[0]
        fin_ref[1] = final[1]
    for d in range(2):
        for j in range(gb):
            y_ref[j] += _dot(v_ref[d, :, j * S5_GL:(j + 1) * S5_GL].astype(BF16), wy_ref[d, j])


def _s5(u_g, tw, wy, a_t, a_seg, s0):
    two_pass = s0 is not None
    n_g, n_rows, in_w = u_g.shape
    gb = S5_GB
    in_specs = [
        pl.BlockSpec((gb, n_rows, in_w), lambda i: (i, 0, 0)),
        pl.BlockSpec((2, gb) + tw.shape[2:], lambda i: (0, i, 0, 0)),
        pl.BlockSpec((2, gb) + wy.shape[2:], lambda i: (0, i, 0, 0)),
        pl.BlockSpec((2, 2, gb * S5_GL), lambda i: (0, 0, i)),
    ]
    args = [u_g, tw, wy, a_t]
    y_shape = jax.ShapeDtypeStruct(u_g.shape, F32)
    y_spec = pl.BlockSpec((gb, n_rows, in_w), lambda i: (i, 0, 0))
    scratch = [pltpu.VMEM((2, n_rows, gb * S5_GL), F32)]
    if two_pass:
        in_specs += [
            pl.BlockSpec((2, 2, gb * S5_GL), lambda i: (0, 0, i)),
            pl.BlockSpec((2, s0.shape[1], gb * S5_GL), lambda i: (0, 0, i)),
        ]
        args += [a_seg, s0]
        out_shape, out_specs = y_shape, y_spec
        scratch.append(pltpu.VMEM((2, S5_ROWS, gb * S5_GL), F32))
    else:
        out_shape = (y_shape, jax.ShapeDtypeStruct((2, S5_ROWS, n_g * S5_GL), F32))
        out_specs = (y_spec, pl.BlockSpec((2, S5_ROWS, gb * S5_GL), lambda i: (0, 0, i)))
    return pl.pallas_call(
        functools.partial(_s5_kernel, two_pass=two_pass),
        out_shape=out_shape,
        grid=(n_g // gb,),
        in_specs=in_specs,
        out_specs=out_specs,
        scratch_shapes=scratch,
        compiler_params=_params(("parallel",)),
    )(*args)


def _route(probs):
    epg = EXPERTS_PER_GROUP
    groups = [probs[g * epg:(g + 1) * epg] for g in range(N_EXPERT_GROUPS)]
    scores = []
    for grp in groups:
        best = None
        for a in range(epg):
            for b in range(a + 1, epg):
                pair = grp[a] + grp[b]
                best = pair if best is None else jnp.maximum(best, pair)
        scores.append(best)
    top_score = scores[0]
    top_group = jnp.zeros_like(scores[0], dtype=jnp.int32)
    for g in range(1, N_EXPERT_GROUPS):
        better = scores[g] > top_score
        top_score = jnp.where(better, scores[g], top_score)
        top_group = jnp.where(better, g, top_group)
    sel = []
    for j in range(epg):
        v = groups[0][j]
        for g in range(1, N_EXPERT_GROUPS):
            v = jnp.where(top_group == g, groups[g][j], v)
        sel.append(v)
    v1 = sel[0]
    i1 = jnp.zeros_like(top_group)
    for j in range(1, epg):
        better = sel[j] > v1
        v1 = jnp.where(better, sel[j], v1)
        i1 = jnp.where(better, j, i1)
    v2 = jnp.full_like(v1, -1.0)
    i2 = jnp.zeros_like(top_group)
    for j in range(epg):
        better = (sel[j] > v2) & (i1 != j)
        v2 = jnp.where(better, sel[j], v2)
        i2 = jnp.where(better, j, i2)
    total = v1 + v2
    w1 = v1 / total
    w2 = v2 / total
    out = []
    for g in range(N_EXPERT_GROUPS):
        for j in range(epg):
            w = jnp.where(i1 == j, w1, jnp.where(i2 == j, w2, 0.0))
            out.append(jnp.where(top_group == g, w, 0.0))
    return out


def _merge_kernel(x_ref, ro_ref, att_ref, y_ref, su_ref, gates_ref, mod_ref, wro_ref, wao_ref, wglu_ref, wout_ref,
                  g2_ref, dskip_ref, wrt_ref, br_ref, x1_ref, h2_ref, comb_ref):
    ret_branch = _dot(ro_ref[...], wro_ref[...])
    att_branch = _dot(att_ref[...], wao_ref[...])
    y = y_ref[...] + dskip_ref[...] * su_ref[...].astype(F32)
    glu = _dot(jax.nn.gelu(y).astype(BF16), wglu_ref[...])
    ssm_branch = glu[:, 0:D_MODEL] * jax.nn.sigmoid(glu[:, D_MODEL:2 * D_MODEL])

    def gate(j):
        return jax.nn.sigmoid(gates_ref[:, j * D_MODEL:(j + 1) * D_MODEL].astype(F32))

    merged = gate(0) * ret_branch + gate(1) * att_branch + gate(2) * ssm_branch
    x1 = x_ref[...] + mod_ref[0, 2:3, :] * _dot(merged.astype(BF16), wout_ref[...])
    x1_ref[...] = x1
    xn = x1 * lax.rsqrt(jnp.mean(x1 * x1, axis=-1, keepdims=True) + EPS) * g2_ref[...]
    h2 = xn * (1.0 + mod_ref[0, 4:5, :]) + mod_ref[0, 3:4, :]
    h2_ref[...] = h2.astype(BF16)

    logits = lax.dot_general(wrt_ref[...], h2, (((1,), (1,)), ((), ())), precision=lax.Precision.HIGHEST,
                             preferred_element_type=F32) + br_ref[...]
    e = jnp.exp(logits - jnp.max(logits, axis=0, keepdims=True))
    probs = e / jnp.sum(e, axis=0, keepdims=True)
    rows = _route([probs[j:j + 1, :] for j in range(N_EXPERTS)])
    comb_ref[...] = jnp.concatenate(rows, axis=0)


def _merge(x, ro, att, y, su, gates, mod, mod_rows, wro, wao, wglu, wout, g2, dskip, wrt, br):
    n = x.shape[0]
    tm = TM_PROJ
    mod_row = _mod_row_fn(mod_rows, tm)
    row = lambda i: (i, 0)
    in_specs = [
        pl.BlockSpec((tm, D_MODEL), row),
        pl.BlockSpec((tm, RET_W), row),
        pl.BlockSpec((tm, ATT_W), row),
        pl.BlockSpec((tm, SSM_CH), row),
        pl.BlockSpec((tm, SSM_CH), row),
        pl.BlockSpec((tm, 3 * D_MODEL), row),
        pl.BlockSpec((1, N_MOD, D_MODEL), lambda i: (mod_row(i), 0, 0)),
        _const_spec((RET_W, D_MODEL)),
        _const_spec((ATT_W, D_MODEL)),
        _const_spec((SSM_CH, 2 * D_MODEL)),
        _const_spec((D_MODEL, D_MODEL)),
        _const_spec((1, D_MODEL)),
        _const_spec((1, SSM_CH)),
        _const_spec((N_EXPERTS, D_MODEL)),
        _const_spec((N_EXPERTS, 1)),
    ]
    return pl.pallas_call(
        _merge_kernel,
        out_shape=(
            jax.ShapeDtypeStruct((n, D_MODEL), F32),
            jax.ShapeDtypeStruct((n, D_MODEL), BF16),
            jax.ShapeDtypeStruct((N_EXPERTS, n), F32),
        ),
        grid=(n // tm,),
        in_specs=in_specs,
        out_specs=(
            pl.BlockSpec((tm, D_MODEL), row),
            pl.BlockSpec((tm, D_MODEL), row),
            pl.BlockSpec((N_EXPERTS, tm), lambda i: (0, i)),
        ),
        compiler_params=_params(("parallel",)),
    )(x, ro, att, y, su, gates, mod, wro, wao, wglu, wout, g2, dskip, wrt, br)


def _moe_kernel(*refs, final):
    if final:
        h2_ref, comb_ref, x1_ref, mod_ref, wg_ref, wu_ref, wd_ref, fg_ref, o_ref, acc_ref = refs
    else:
        h2_ref, comb_ref, x1_ref, mod_ref, wg_ref, wu_ref, wd_ref, o_ref, acc_ref = refs
    e = pl.program_id(1)

    @pl.when(e == 0)
    def _():
        acc_ref[...] = jnp.zeros_like(acc_ref)

    h = h2_ref[...]
    g = _dot(h, wg_ref[0])
    u = _dot(h, wu_ref[0])
    comb = comb_ref[...]
    lane = lax.broadcasted_iota(jnp.int32, comb.shape, 1)
    w = jnp.sum(jnp.where(lane == e, comb, 0.0), axis=1, keepdims=True)
    hid = (g * jax.nn.sigmoid(g) * u * w).astype(BF16)
    acc_ref[...] += _dot(hid, wd_ref[0])

    @pl.when(e == N_EXPERTS - 1)
    def _():
        x2 = x1_ref[...] + mod_ref[0, 5:6, :] * acc_ref[...]
        if final:
            x2 = x2 * lax.rsqrt(jnp.mean(x2 * x2, axis=-1, keepdims=True) + EPS) * fg_ref[...]
        o_ref[...] = x2


def _moe(h2, comb, x1, mod, mod_rows, wg, wu, wd, final_g):
    n = h2.shape[0]
    tm = TM_MOE
    mod_row = _mod_row_fn(mod_rows, tm)
    final = final_g is not None
    row = lambda i, e: (i, 0)
    in_specs = [
        pl.BlockSpec((tm, D_MODEL), row),
        pl.BlockSpec((tm, N_EXPERTS), row),
        pl.BlockSpec((tm, D_MODEL), row),
        pl.BlockSpec((1, N_MOD, D_MODEL), lambda i, e: (mod_row(i), 0, 0)),
        pl.BlockSpec((1, D_MODEL, D_FF), lambda i, e: (e, 0, 0)),
        pl.BlockSpec((1, D_MODEL, D_FF), lambda i, e: (e, 0, 0)),
        pl.BlockSpec((1, D_FF, D_MODEL), lambda i, e: (e, 0, 0)),
    ]
    args = [h2, comb, x1, mod, wg, wu, wd]
    if final:
        in_specs.append(pl.BlockSpec((1, D_MODEL), lambda i, e: (0, 0)))
        args.append(final_g)
    return pl.pallas_call(
        functools.partial(_moe_kernel, final=final),
        out_shape=jax.ShapeDtypeStruct((n, D_MODEL), F32),
        grid=(n // tm, N_EXPERTS),
        in_specs=in_specs,
        out_specs=pl.BlockSpec((tm, D_MODEL), row),
        scratch_shapes=[pltpu.VMEM((tm, D_MODEL), F32)],
        compiler_params=_params(("parallel", "arbitrary")),
    )(*args)


def _block_diag(m):
    g, r, c = m.shape
    eye = jnp.eye(g, dtype=m.dtype)
    return (eye[:, None, :, None] * m[:, :, None, :]).reshape(g * r, g * c)


def _s5_direction_params(a_re, a_im, log_dt, b_re, b_im, c_re, c_im, reverse):
    n_g, n_p = a_re.shape
    hp = lax.Precision.HIGHEST
    step = jnp.exp(log_dt)[:, None]
    mag = jnp.exp(a_re * step)
    ar = mag * jnp.cos(a_im * step)
    ai = mag * jnp.sin(a_im * step)
    den = a_re * a_re + a_im * a_im
    fr = ((ar - 1.0) * a_re + ai * a_im) / den
    fi = (ai * a_re - (ar - 1.0) * a_im) / den
    bbr = fr[:, :, None] * b_re - fi[:, :, None] * b_im
    bbi = fr[:, :, None] * b_im + fi[:, :, None] * b_re
    pr, pi = [jnp.ones_like(ar)], [jnp.zeros_like(ar)]
    for _ in range(S5_T):
        r, i = pr[-1], pi[-1]
        pr.append(r * ar - i * ai)
        pi.append(r * ai + i * ar)
    pr, pi = jnp.stack(pr), jnp.stack(pi)
    mr = pr[..., None] * bbr - pi[..., None] * bbi
    mi = pr[..., None] * bbi + pi[..., None] * bbr
    cr = c_re * pr[:, :, None, :] - c_im * pi[:, :, None, :]
    ci = c_re * pi[:, :, None, :] + c_im * pr[:, :, None, :]
    kern = (jnp.einsum('gop,kgpi->kgoi', c_re, mr[:S5_T], precision=hp)
            - jnp.einsum('gop,kgpi->kgoi', c_im, mi[:S5_T], precision=hp))
    t = jnp.arange(S5_T)
    lag = (t[:, None] - t[None, :]) if reverse else (t[None, :] - t[:, None])
    toep = jnp.where((lag >= 0)[:, :, None, None, None], kern[jnp.clip(lag, 0, S5_T - 1)], 0.0)
    toep = toep.transpose(2, 0, 4, 1, 3).reshape(n_g, S5_T * SSM_GROUP, S5_T * SSM_GROUP)
    e_x = t if reverse else S5_T - 1 - t
    wx = jnp.concatenate([mr[e_x], mi[e_x]], axis=2)
    wx = wx.transpose(1, 0, 3, 2).reshape(n_g, S5_T * SSM_GROUP, S5_GL)
    e_y = S5_T - t if reverse else t + 1
    wy = jnp.concatenate([cr[e_y], -ci[e_y]], axis=3)
    wy = wy.transpose(1, 3, 0, 2).reshape(n_g, S5_GL, S5_T * SSM_GROUP)

    def coef_rows(r, i):
        return jnp.stack([jnp.concatenate([r, r], axis=1).reshape(-1), jnp.concatenate([-i, i], axis=1).reshape(-1)])

    sr, si = pr[S5_T], pi[S5_T]
    a_t = coef_rows(sr, si)
    for _ in range(int(math.log2(S5_SEG // S5_T))):
        sr, si = sr * sr - si * si, 2.0 * sr * si
    return jnp.concatenate([toep, wx], axis=2).astype(BF16), wy.astype(BF16), a_t, coef_rows(sr, si)


def _rope_tables(n_tokens):
    rows = n_tokens // GRID_W
    row = jnp.repeat(jnp.arange(rows, dtype=F32), GRID_W)
    col = jnp.tile(jnp.arange(GRID_W, dtype=F32), rows)
    n_freq = ATT_HEAD_DIM // 4
    inv = ROPE_THETA ** (-jnp.arange(n_freq, dtype=F32) / n_freq)
    ang = jnp.concatenate([row[:, None] * inv, col[:, None] * inv], axis=-1)
    cos = jnp.repeat(jnp.cos(ang), 2, axis=-1)
    sin = jnp.repeat(jnp.sin(ang), 2, axis=-1) * jnp.tile(jnp.array([-1.0, 1.0], F32), ATT_HEAD_DIM // 2)
    return jnp.tile(cos, (1, ATT_KV_HEADS)), jnp.tile(sin, (1, ATT_KV_HEADS))


def _swap_pairs(a):
    idx = jnp.arange(a.shape[-1]) ^ 1
    return jnp.take(a, idx, axis=-1)


def _layer(x, batch, mod, mod_rows, p, ctx):
    n = x.shape[0]
    seq = n // batch
    latent = ctx is not None
    rope_args = (p['w_sw'], p['gqs'], p['gks'], p['cos'], p['sin']) if latent else None
    rqkv, rg, aq, ak, av, su, gates = _inproj(x, mod, mod_rows, p['g1'], p['w_in'], p['ones'], p['gq'], p['gk'],
                                              rope_args)

    ro, ret_state = _retention(rqkv.reshape(batch, seq, 3 * RET_W), rg.reshape(batch, seq, RET_W), p['ret_lg'],
                               p['ret_gn'], ctx[2] if latent else None, bb=1 if latent else 8)

    q_t = aq.reshape(batch, seq, ATT_HEADS, ATT_HEAD_DIM).transpose(0, 2, 3, 1)
    k_all = ak.reshape(batch, seq, ATT_KV_W)
    v_all = av.reshape(batch, seq, ATT_KV_W)
    if latent:
        k_all = jnp.concatenate([k_all, ctx[0].reshape(batch, -1, ATT_KV_W)], axis=1)
        v_all = jnp.concatenate([v_all, ctx[1].reshape(batch, -1, ATT_KV_W)], axis=1)
    chunked = (batch, -1, ATT_KC, ATT_KV_HEADS, ATT_HEAD_DIM)
    k_c = k_all.astype(BF16).reshape(chunked).transpose(0, 3, 1, 2, 4)
    v_t = v_all.astype(BF16).reshape(chunked).transpose(0, 3, 1, 4, 2)
    ones_row = jnp.zeros(v_t.shape[:3] + (ATT_V_ROWS - ATT_HEAD_DIM, ATT_KC), BF16).at[..., 0, :].set(1.0)
    v_t = jnp.concatenate([v_t, ones_row], axis=3)
    att_t = _attention(q_t, k_c, v_t, qb=ATT_QB)
    att = att_t.transpose(0, 3, 1, 2).reshape(n, ATT_W)

    n_chunk = S5_SEG // S5_T
    u_g = su.reshape(S5_ROWS, n_chunk, S5_T, SSM_GROUPS, SSM_GROUP).transpose(3, 1, 0, 2, 4)
    u_g = u_g.reshape(SSM_GROUPS, n_chunk * S5_ROWS, S5_T * SSM_GROUP)
    if latent:
        s0 = jnp.stack([ctx[3], ctx[4]], axis=3).transpose(1, 0, 2, 3, 4).reshape(2, batch, SSM_GROUPS * S5_GL)
        y_g = _s5(u_g, p['s5_tw'], p['s5_wy'], p['s5_a'], p['s5_aseg'], s0)
        ssm_state = None
    else:
        y_g, ssm_state = _s5(u_g, p['s5_tw'], p['s5_wy'], p['s5_a'], None, None)
        ssm_state = ssm_state.reshape(2, batch, SSM_GROUPS, 2, SSM_STATE)
    y = y_g.reshape(SSM_GROUPS, n_chunk, S5_ROWS, S5_T, SSM_GROUP).transpose(2, 1, 3, 0, 4).reshape(n, SSM_CH)

    x1, h2, comb_t = _merge(x, ro.reshape(n, RET_W), att, y, su, gates, mod, mod_rows, p['w_ret_out'], p['w_att_out'],
                            p['w_glu'], p['w_out'], p['g2'], p['ssm_d'], p['w_router_t'], p['b_router'])
    x2 = _moe(h2, comb_t.T, x1, mod, mod_rows, p['w_gate'], p['w_up'], p['w_down'], p.get('final_g'))
    if latent:
        return x2, None
    return x2, (ak, av, ret_state, ssm_state)


def kernel(x_prompt, x_sample, c, cache_attn_k, cache_attn_v, state_ret, state_ssm_re, state_ssm_im, c_ctx, w_ada,
           b_ada, norm1_g, norm2_g, w_in, ret_lg_f, ret_lg_b, ret_norm_g, w_ret_out, att_q_norm_g, att_k_norm_g,
           w_att_out, a_re_f, a_im_f, a_re_b, a_im_b, log_dt_f, log_dt_b, ssm_b_re, ssm_b_im, ssm_c_re, ssm_c_im,
           ssm_d, w_glu, w_out, w_router, b_router, w_gate, w_up, w_down, final_norm_g):
    batch, seq, _ = x_prompt.shape
    dec_batch, dec_seq, _ = x_sample.shape
    assert batch * seq == S5_ROWS * S5_SEG and dec_batch * dec_seq == S5_ROWS * S5_SEG
    assert seq == S5_SEG and dec_seq % S5_SEG == 0

    c_rows = jnp.zeros((ADA_ROWS, D_MODEL), F32).at[0].set(c_ctx).at[1:1 + dec_batch].set(c)
    mod_all = _ada(c_rows, w_ada, b_ada).reshape(DEPTH, ADA_ROWS, N_MOD, D_MODEL)

    cos, sin = _rope_tables(dec_seq)
    ones = _block_diag(jnp.ones((ATT_HEADS, ATT_HEAD_DIM, ATT_HEAD_DIM), BF16))

    xp = x_prompt.reshape(batch * seq, D_MODEL)
    xs = x_sample.reshape(dec_batch * dec_seq, D_MODEL)
    ks_, vs_, rets_, ssms_ = [], [], [], []
    for l in range(DEPTH):
        gq = jnp.tile(att_q_norm_g[l], ATT_HEADS)[None]
        gk = jnp.tile(att_k_norm_g[l], ATT_KV_HEADS)[None]
        w_qk = w_in[l][:, C_AQ:C_AV]
        fwd = _s5_direction_params(a_re_f[l], a_im_f[l], log_dt_f[l], ssm_b_re[l], ssm_b_im[l], ssm_c_re[l],
                                   ssm_c_im[l], reverse=False)
        bwd = _s5_direction_params(a_re_b[l], a_im_b[l], log_dt_b[l], ssm_b_re[l], ssm_b_im[l], ssm_c_re[l],
                                   ssm_c_im[l], reverse=True)
        p = {
            'g1': norm1_g[l][None], 'g2': norm2_g[l][None], 'w_in': w_in[l].astype(BF16), 'ones': ones,
            'gq': gq, 'gk': gk, 'w_sw': _swap_pairs(w_qk).astype(BF16), 'gqs': _swap_pairs(gq),
            'gks': _swap_pairs(gk), 'cos': cos, 'sin': sin,
            'ret_lg': jnp.stack([ret_lg_f[l], ret_lg_b[l]]), 'ret_gn': ret_norm_g[l][None],
            'w_ret_out': w_ret_out[l].astype(BF16), 'w_att_out': w_att_out[l].astype(BF16),
            's5_tw': jnp.stack([fwd[0], bwd[0]]), 's5_wy': jnp.stack([fwd[1], bwd[1]]),
            's5_a': jnp.stack([fwd[2], bwd[2]]), 's5_aseg': jnp.stack([fwd[3], bwd[3]]),
            'ssm_d': ssm_d[l][None], 'w_glu': w_glu[l].astype(BF16), 'w_out': w_out[l].astype(BF16),
            'w_router_t': w_router.T, 'b_router': b_router[:, None],
            'w_gate': w_gate[l].astype(BF16), 'w_up': w_up[l].astype(BF16), 'w_down': w_down[l].astype(BF16),
        }
        if l == DEPTH - 1:
            p['final_g'] = final_norm_g[None]
        mod = mod_all[l]

        xp, (k_c, v_c, st_r, st_s) = _layer(xp, batch, mod, (0, batch * seq), p, None)
        ks_.append(k_c.reshape(batch, seq, ATT_KV_HEADS, ATT_HEAD_DIM))
        vs_.append(v_c.reshape(batch, seq, ATT_KV_HEADS, ATT_HEAD_DIM))
        rets_.append(st_r)
        ssms_.append(st_s)

        ctx = (cache_attn_k[:, l], cache_attn_v[:, l], state_ret[:, l], state_ssm_re[:, l], state_ssm_im[:, l])
        xs, _ = _layer(xs, dec_batch, mod, (1, dec_seq), p, ctx)

    ssm_all = jnp.stack(ssms_, axis=1)
    ssm_all = ssm_all.transpose(2, 1, 0, 3, 4, 5)
    new_re = ssm_all[..., 0, :]
    new_im = ssm_all[..., 1, :]
    return (xp.reshape(batch, seq, D_MODEL), xs.reshape(dec_batch, dec_seq, D_MODEL),
            jnp.stack(ks_, axis=1), jnp.stack(vs_, axis=1), jnp.stack(rets_, axis=1), new_re, new_im)
```

```python
import functools
import math

import jax
import jax.numpy as jnp
from jax import lax
from jax.experimental import pallas as pl
from jax.experimental.pallas import tpu as pltpu

F32 = jnp.float32
BF16 = jnp.bfloat16

D_MODEL = 1024
DEPTH = 2
GRID_W = 64
RET_HEADS = 4
RET_DK = 128
RET_DV = 128
RET_CHUNK = 128
RET_W = RET_HEADS * RET_DV
ATT_HEADS = 8
ATT_KV_HEADS = 2
ATT_GROUP = ATT_HEADS // ATT_KV_HEADS
ATT_HEAD_DIM = 64
ATT_W = ATT_HEADS * ATT_HEAD_DIM
ATT_KV_W = ATT_KV_HEADS * ATT_HEAD_DIM
ROPE_THETA = 10000.0
SSM_CH = 512
SSM_GROUP = 16
SSM_GROUPS = SSM_CH // SSM_GROUP
SSM_STATE = 64
N_EXPERTS = 16
N_EXPERT_GROUPS = 4
EXPERTS_PER_GROUP = N_EXPERTS // N_EXPERT_GROUPS
D_FF = 512
N_MOD = 6
EPS = 1e-6

C_RQ, C_RK, C_RV, C_RG = 0, 512, 1024, 1536
C_AQ, C_AK, C_AV, C_SU, C_G, C_END = 2048, 2560, 2688, 2816, 3328, 6400

VMEM_LIMIT = 52 * 1024 * 1024

ATT_Q_SCALE = ATT_HEAD_DIM ** -0.5 * math.log2(math.e)
ATT_QB = 256
ATT_KC = 256
ATT_V_ROWS = ATT_HEAD_DIM + 16
ATT_RING = 4

TM_PROJ = 512
TM_MOE = 1024
S5_ROWS = 32
S5_SEG = 256
S5_T = 16
S5_GL = 2 * SSM_STATE
S5_GB = 8
S5_TM = 2048
LANES = 128
STAGE_ROWS = 256
ADA_ROWS = 8
ADA_TN = 1536


def _dot(a, b):
    return jnp.dot(a, b, preferred_element_type=F32)


def _dot_nt(a, b):
    return lax.dot_general(a, b, (((1,), (1,)), ((), ())), preferred_element_type=F32)


def _dot_tn(a, b):
    return lax.dot_general(a, b, (((0,), (0,)), ((), ())), preferred_element_type=F32)


def _const_spec(shape):
    n = len(shape)
    return pl.BlockSpec(shape, lambda *_: (0,) * n, pipeline_mode=pl.Buffered(1))


def _params(sem, vmem=VMEM_LIMIT):
    return pltpu.CompilerParams(dimension_semantics=sem, vmem_limit_bytes=vmem)


def _mod_row_fn(mod_rows, tm):
    base, per_row = mod_rows
    return lambda i: base + (i * tm) // per_row


def _modulated_norm(x, g, mod_ref):
    xn = x * lax.rsqrt(jnp.mean(x * x, axis=-1, keepdims=True) + EPS) * g
    return xn * (1.0 + mod_ref[0, 1:2, :]) + mod_ref[0, 0:1, :]


def _ada_kernel(c_ref, w_ref, b_ref, o_ref):
    c = c_ref[...]
    a = (c * jax.nn.sigmoid(c)).astype(BF16)
    o_ref[0] = _dot(a, w_ref[0].astype(BF16)) + b_ref[0]


def _ada(c_rows, w_ada, b_ada):
    n_col = N_MOD * D_MODEL
    return pl.pallas_call(
        _ada_kernel,
        out_shape=jax.ShapeDtypeStruct((DEPTH, ADA_ROWS, n_col), F32),
        grid=(DEPTH, n_col // ADA_TN),
        in_specs=[
            pl.BlockSpec((ADA_ROWS, D_MODEL), lambda l, j: (0, 0)),
            pl.BlockSpec((1, D_MODEL, ADA_TN), lambda l, j: (l, 0, j)),
            pl.BlockSpec((1, 1, ADA_TN), lambda l, j: (l, 0, j)),
        ],
        out_specs=pl.BlockSpec((1, ADA_ROWS, ADA_TN), lambda l, j: (l, 0, j)),
        compiler_params=_params(("arbitrary", "arbitrary")),
    )(c_rows, w_ada, b_ada.reshape(DEPTH, 1, n_col))


def _head_mean_sq(z, ones):
    z2 = z * z
    hi = z2.astype(BF16)
    lo = (z2 - hi.astype(F32)).astype(BF16)
    return (_dot(hi, ones) + _dot(lo, ones)) * (1.0 / ATT_HEAD_DIM)


def _inproj_kernel(*refs, latent):
    if latent:
        (x_ref, mod_ref, g1_ref, w_ref, wqt_ref, wvt_ref, ones_ref, gq_ref, gk_ref,
         wqst_ref, wks_ref, gqs_ref, gks_ref, cos_ref, sin_ref, cost_ref, sint_ref,
         rqkv_ref, rg_ref, gates_ref, qt_ref, k16_ref, vt_ref) = refs
    else:
        (x_ref, mod_ref, g1_ref, w_ref, wqt_ref, wvt_ref, ones_ref, gq_ref, gk_ref,
         rqkv_ref, rg_ref, gates_ref, qt_ref, k16_ref, vt_ref, ak_ref, av_ref) = refs
    tm = x_ref.shape[0]
    h = _modulated_norm(x_ref[...], g1_ref[...], mod_ref).astype(BF16)

    def seg(a, b):
        return _dot(h, w_ref[:, a:b])

    rqkv_ref[:, 0:RET_W] = (seg(C_RQ, C_RK) * (RET_DK ** -0.5)).astype(BF16)
    rqkv_ref[:, RET_W:3 * RET_W] = seg(C_RK, C_RG).astype(BF16)
    rg_ref[...] = seg(C_RG, C_AQ).astype(BF16)
    for j in range(3):
        gates_ref[:, j * D_MODEL:(j + 1) * D_MODEL] = seg(C_G + j * D_MODEL, C_G + (j + 1) * D_MODEL).astype(BF16)

    zk = seg(C_AK, C_AV)
    inv_k = lax.rsqrt(_head_mean_sq(zk, ones_ref[...]) + EPS)
    yk = zk * inv_k * gk_ref[...]
    zq = _dot_nt(wqt_ref[...], h).reshape(ATT_HEADS, ATT_HEAD_DIM, tm)
    inv_q = lax.rsqrt(jnp.mean(zq * zq, axis=1, keepdims=True) + EPS)
    yq = zq * inv_q * gq_ref[...]
    if latent:
        yk_sw = _dot(h, wks_ref[...]) * inv_k * gks_ref[...]
        yk = yk * cos_ref[...] + yk_sw * sin_ref[...]
        zq_sw = _dot_nt(wqst_ref[...], h).reshape(ATT_HEADS, ATT_HEAD_DIM, tm)
        yq = yq * cost_ref[...] + (zq_sw * inv_q * gqs_ref[...]) * sint_ref[...]
    qt_ref[...] = (yq * ATT_Q_SCALE).reshape(ATT_W, tm).astype(BF16)
    k16_ref[...] = yk.astype(BF16)
    vt = _dot_nt(wvt_ref[...], h).astype(BF16)
    for c in range(tm // ATT_KC):
        vt_ref[c] = vt[:, c * ATT_KC:(c + 1) * ATT_KC]
    if not latent:
        ak_ref[...] = yk
        av_ref[...] = seg(C_AV, C_SU)


def _inproj(x, mod, mod_rows, p, latent):
    n = x.shape[0]
    tm = TM_PROJ
    mod_row = _mod_row_fn(mod_rows, tm)
    row = lambda i: (i, 0)
    in_specs = [
        pl.BlockSpec((tm, D_MODEL), row),
        pl.BlockSpec((1, N_MOD, D_MODEL), lambda i: (mod_row(i), 0, 0)),
        _const_spec((1, D_MODEL)),
        _const_spec((D_MODEL, C_END)),
        _const_spec((ATT_W, D_MODEL)),
        _const_spec((ATT_KV_W, D_MODEL)),
        _const_spec((ATT_KV_W, ATT_KV_W)),
        _const_spec((ATT_HEAD_DIM, 1)),
        _const_spec((1, ATT_KV_W)),
    ]
    args = [x, mod, p['g1'], p['w_in'], p['wq_t'], p['wv_t'], p['ones'], p['gq'], p['gk']]
    if latent:
        n_pos = p['cos'].shape[0] // tm
        in_specs += [
            _const_spec((ATT_W, D_MODEL)),
            _const_spec((D_MODEL, ATT_KV_W)),
            _const_spec((ATT_HEAD_DIM, 1)),
            _const_spec((1, ATT_KV_W)),
            pl.BlockSpec((tm, ATT_KV_W), lambda i: (i % n_pos, 0)),
            pl.BlockSpec((tm, ATT_KV_W), lambda i: (i % n_pos, 0)),
            pl.BlockSpec((ATT_HEAD_DIM, tm), lambda i: (0, i % n_pos)),
            pl.BlockSpec((ATT_HEAD_DIM, tm), lambda i: (0, i % n_pos)),
        ]
        args += [p['wqs_t'], p['wk_sw'], p['gqs'], p['gks'], p['cos'], p['sin'], p['cos_t'], p['sin_t']]
    out_shape = [
        jax.ShapeDtypeStruct((n, 3 * RET_W), BF16),
        jax.ShapeDtypeStruct((n, RET_W), BF16),
        jax.ShapeDtypeStruct((n, 3 * D_MODEL), BF16),
        jax.ShapeDtypeStruct((ATT_W, n), BF16),
        jax.ShapeDtypeStruct((n, ATT_KV_W), BF16),
        jax.ShapeDtypeStruct((n // ATT_KC, ATT_KV_W, ATT_KC), BF16),
    ]
    out_specs = [
        pl.BlockSpec((tm, 3 * RET_W), row),
        pl.BlockSpec((tm, RET_W), row),
        pl.BlockSpec((tm, 3 * D_MODEL), row),
        pl.BlockSpec((ATT_W, tm), lambda i: (0, i)),
        pl.BlockSpec((tm, ATT_KV_W), row),
        pl.BlockSpec((tm // ATT_KC, ATT_KV_W, ATT_KC), lambda i: (i, 0, 0)),
    ]
    if not latent:
        out_shape += [jax.ShapeDtypeStruct((n, ATT_KV_W), F32)] * 2
        out_specs += [pl.BlockSpec((tm, ATT_KV_W), row)] * 2
    return pl.pallas_call(
        functools.partial(_inproj_kernel, latent=latent),
        out_shape=tuple(out_shape),
        grid=(n // tm,),
        in_specs=in_specs,
        out_specs=tuple(out_specs),
        compiler_params=_params(("parallel",)),
    )(*args)


T_DEC_F, T_DEC_B, T_XI_F, T_ZETA_F, T_XI_B, T_ZETA_B, T_CD_F, T_CD_B, N_TAB = range(9)


def _ret_kernel(*refs, has_s0, bb, seq):
    if has_s0:
        lg_ref, q_ref, k_ref, v_ref, rg_ref, gn_ref, s0_ref, o_ref, st_ref, tab_ref, acc_ref = refs
    else:
        lg_ref, q_ref, k_ref, v_ref, rg_ref, gn_ref, o_ref, st_ref, tab_ref, acc_ref = refs
    c = RET_CHUNK
    head = pl.program_id(1)
    lgf = lg_ref[0, head]
    lgb = lg_ref[1, head]
    t = lax.broadcasted_iota(jnp.int32, (c, c), 0).astype(F32)
    s = lax.broadcasted_iota(jnp.int32, (c, c), 1).astype(F32)
    tab_ref[T_DEC_F] = jnp.where(t >= s, jnp.exp(lgf * jnp.maximum(t - s, 0.0)), 0.0)
    tab_ref[T_DEC_B] = jnp.where(s >= t, jnp.exp(lgb * jnp.maximum(s - t, 0.0)), 0.0)
    tab_ref[T_XI_F] = jnp.exp(lgf * (t + 1.0))
    tab_ref[T_ZETA_F] = jnp.exp(lgf * (c - 1.0 - t))
    tab_ref[T_XI_B] = jnp.exp(lgb * (c - t))
    tab_ref[T_ZETA_B] = jnp.exp(lgb * t)
    tab_ref[T_CD_F] = jnp.exp(lgf * (c + 0.0 * t))
    tab_ref[T_CD_B] = jnp.exp(lgb * (c + 0.0 * t))
    if has_s0:
        st_ref[...] = s0_ref[...]
    else:
        st_ref[...] = jnp.zeros_like(st_ref)
    acc_ref[...] = jnp.zeros_like(acc_ref)
    n_chunk = seq // c

    def one_direction(rb, d, off, t_dec, t_xi, t_zeta, t_cd):
        rows = pl.ds(off, c)
        q = q_ref[rb, rows, :]
        k = k_ref[rb, rows, :]
        v = v_ref[rb, rows, :]
        p = (_dot_nt(q, k) * tab_ref[t_dec]).astype(BF16)
        st = st_ref[rb, d, 0]
        o = _dot(p, v) + _dot(q, st.astype(BF16)) * tab_ref[t_xi]
        acc_ref[rb, rows, :] += o
        kz = (k.astype(F32) * tab_ref[t_zeta]).astype(BF16)
        st_ref[rb, d, 0] = tab_ref[t_cd] * st + _dot_tn(kz, v)

    def body(i, carry):
        off_f = pl.multiple_of(i * c, c)
        off_b = pl.multiple_of((n_chunk - 1 - i) * c, c)
        for rb in range(bb):
            one_direction(rb, 0, off_f, T_DEC_F, T_XI_F, T_ZETA_F, T_CD_F)
            one_direction(rb, 1, off_b, T_DEC_B, T_XI_B, T_ZETA_B, T_CD_B)
        return carry

    lax.fori_loop(0, n_chunk, body, 0)

    def finish(i, carry):
        rows = pl.ds(pl.multiple_of(i * c, c), c)
        for rb in range(bb):
            o = acc_ref[rb, rows, :]
            ro = o * lax.rsqrt(jnp.mean(o * o, axis=-1, keepdims=True) + EPS) * gn_ref[...]
            g = rg_ref[rb, rows, :].astype(F32)
            o_ref[rb, rows, :] = (ro * (g * jax.nn.sigmoid(g))).astype(BF16)
        return carry

    lax.fori_loop(0, n_chunk, finish, 0)


def _retention(rqkv, rg, lg, gn, s0, bb):
    b, seq, _ = rqkv.shape
    has_s0 = s0 is not None
    blk = (bb, seq, RET_DK)
    st_spec = pl.BlockSpec((bb, 2, 1, RET_DK, RET_DV), lambda i, h: (i, 0, h, 0, 0))
    in_specs = [
        pl.BlockSpec(memory_space=pltpu.SMEM),
        pl.BlockSpec(blk, lambda i, h: (i, 0, h)),
        pl.BlockSpec(blk, lambda i, h: (i, 0, RET_HEADS + h)),
        pl.BlockSpec(blk, lambda i, h: (i, 0, 2 * RET_HEADS + h)),
        pl.BlockSpec(blk, lambda i, h: (i, 0, h)),
        pl.BlockSpec((1, RET_DV), lambda i, h: (0, h)),
    ]
    args = [lg, rqkv, rqkv, rqkv, rg, gn]
    if has_s0:
        in_specs.append(st_spec)
        args.append(s0)
    return pl.pallas_call(
        functools.partial(_ret_kernel, has_s0=has_s0, bb=bb, seq=seq),
        out_shape=(
            jax.ShapeDtypeStruct((b, seq, RET_W), BF16),
            jax.ShapeDtypeStruct((b, 2, RET_HEADS, RET_DK, RET_DV), F32),
        ),
        grid=(b // bb, RET_HEADS),
        in_specs=in_specs,
        out_specs=(pl.BlockSpec(blk, lambda i, h: (i, 0, h)), st_spec),
        scratch_shapes=[
            pltpu.VMEM((N_TAB, RET_CHUNK, RET_CHUNK), F32),
            pltpu.VMEM((bb, seq, RET_DV), F32),
        ],
        compiler_params=_params(("parallel", "parallel")),
    )(*args)


ST_M, ST_ALPHA, N_ST = range(3)


def _attn_kernel(*refs, has_ctx):
    if has_ctx:
        q_ref, k_ref, v_ref, kc_ref, vc_ref, o_ref, s_ref, p_ref, acc_ref, st_ref, mx_ref = refs
    else:
        q_ref, k_ref, v_ref, o_ref, s_ref, p_ref, acc_ref, st_ref, mx_ref = refs
    kv_head = pl.program_id(1)
    qb = q_ref.shape[1]
    kc = s_ref.shape[1]
    n_own = k_ref.shape[0] // kc
    n_chunk = n_own + (1 if has_ctx else 0)
    q_t = jnp.concatenate([q_ref[h * ATT_HEAD_DIM:(h + 1) * ATT_HEAD_DIM, :] for h in range(ATT_GROUP)], axis=1)
    q_both = jnp.concatenate([q_t] * ATT_KV_HEADS, axis=0)
    row = lax.broadcasted_iota(jnp.int32, q_both.shape, 0)
    q_pad = jnp.where(row // ATT_HEAD_DIM == kv_head, q_both, jnp.zeros_like(q_both))
    ones_rows = (lax.broadcasted_iota(jnp.int32, (ATT_V_ROWS - ATT_HEAD_DIM, kc), 0) == 0).astype(BF16)

    def keys(j):
        if has_ctx and isinstance(j, int) and j == n_own:
            return kc_ref[...]
        return k_ref[pl.ds(pl.multiple_of(j * kc, kc), kc), :]

    def values(j):
        v = vc_ref[...] if has_ctx and isinstance(j, int) and j == n_own else v_ref[j]
        return jnp.concatenate([v, ones_rows], axis=0)

    def put_scores(j, slot):
        s = _dot(keys(j), q_pad)
        s_ref[slot] = s
        mx_ref[slot] = jnp.max(s, axis=0, keepdims=True)

    def weighted_values(j, slot):
        return _dot(values(j), p_ref[slot])

    def phase(j, slot, first=False, last=False):
        if not last:
            put_scores(j + 1, (slot + 1) % ATT_RING)
        if not first:
            acc_ref[...] = st_ref[ST_ALPHA] * acc_ref[...] + weighted_values(j - 1, (slot - 1) % ATT_RING)
        m = st_ref[ST_M]
        m_new = jnp.maximum(m, mx_ref[slot])
        st_ref[ST_ALPHA] = jnp.exp2(m - m_new)
        st_ref[ST_M] = m_new
        p_ref[slot] = jnp.exp2((s_ref[slot] - m_new).astype(BF16))

    st_ref[ST_M] = jnp.full(st_ref.shape[1:], -jnp.inf, F32)
    acc_ref[...] = jnp.zeros_like(acc_ref)
    put_scores(0, 0)
    phase(0, 0, first=True, last=n_chunk == 1)
    n_loop = max(n_own - 2, 0) // ATT_RING

    def revolution(i, carry):
        for t in range(1, ATT_RING + 1):
            phase(ATT_RING * i + t, t % ATT_RING)
        return carry

    lax.fori_loop(0, n_loop, revolution, 0)
    for j in range(1 + ATT_RING * n_loop, n_chunk):
        phase(j, j % ATT_RING, last=j == n_chunk - 1)
    acc = st_ref[ST_ALPHA] * acc_ref[...] + weighted_values(n_chunk - 1, (n_chunk - 1) % ATT_RING)
    out = (acc[0:ATT_HEAD_DIM] / acc[ATT_HEAD_DIM:ATT_HEAD_DIM + 1]).astype(BF16)
    for h in range(ATT_GROUP):
        o_ref[h * ATT_HEAD_DIM:(h + 1) * ATT_HEAD_DIM, :] = out[:, h * qb:(h + 1) * qb]


def _attention(q_t, k, v_t, k_ctx, v_ctx_t, batch):
    n = q_t.shape[1]
    seq = n // batch
    qb, kc = ATT_QB, ATT_KC
    n_q = seq // qb
    has_ctx = k_ctx is not None
    q_spec = pl.BlockSpec((ATT_GROUP * ATT_HEAD_DIM, qb), lambda b, g, j: (g, b * n_q + j))
    in_specs = [
        q_spec,
        pl.BlockSpec((seq, ATT_KV_W), lambda b, g, j: (b, 0)),
        pl.BlockSpec((seq // kc, ATT_HEAD_DIM, kc), lambda b, g, j: (b, g, 0)),
    ]
    args = [q_t, k, v_t]
    if has_ctx:
        in_specs += [
            pl.BlockSpec((kc, ATT_KV_W), lambda b, g, j: (b, 0)),
            pl.BlockSpec((ATT_HEAD_DIM, kc), lambda b, g, j: (b * ATT_KV_HEADS + g, 0)),
        ]
        args += [k_ctx, v_ctx_t]
    width = ATT_GROUP * qb
    return pl.pallas_call(
        functools.partial(_attn_kernel, has_ctx=has_ctx),
        out_shape=jax.ShapeDtypeStruct(q_t.shape, BF16),
        grid=(batch, ATT_KV_HEADS, n_q),
        in_specs=in_specs,
        out_specs=q_spec,
        scratch_shapes=[
            pltpu.VMEM((ATT_RING, kc, width), F32),
            pltpu.VMEM((ATT_RING, kc, width), BF16),
            pltpu.VMEM((ATT_V_ROWS, width), F32),
            pltpu.VMEM((N_ST, 1, width), F32),
            pltpu.VMEM((ATT_RING, 1, width), F32),
        ],
        compiler_params=_params(("parallel", "parallel", "parallel")),
    )(*args)


def _s5_in_kernel(x_ref, mod_ref, g1_ref, w_ref, o_ref, h_ref):
    tm = x_ref.shape[0]
    n_chunk = o_ref.shape[2]
    n_col = D_MODEL // LANES

    def stage(i, carry):
        rows = pl.ds(pl.multiple_of(i * STAGE_ROWS, STAGE_ROWS), STAGE_ROWS)
        h = _modulated_norm(x_ref[rows, :], g1_ref[...], mod_ref)
        for j in range(n_col):
            h_ref[j, rows, :] = h[:, j * LANES:(j + 1) * LANES]
        return carry

    lax.fori_loop(0, tm // STAGE_ROWS, stage, 0)
    for s in range(S5_T):
        h = jnp.concatenate([h_ref[j, pl.ds(s, n_chunk, stride=S5_T), :] for j in range(n_col)], axis=1)
        o_ref[s] = _dot_nt(w_ref[...], h.astype(BF16)).astype(BF16)


def _s5_in(x, mod, mod_rows, g1, w_su_t):
    n = x.shape[0]
    tm = S5_TM
    mod_row = _mod_row_fn(mod_rows, tm)
    return pl.pallas_call(
        _s5_in_kernel,
        out_shape=jax.ShapeDtypeStruct((S5_T, SSM_CH, n // S5_T), BF16),
        grid=(n // tm,),
        in_specs=[
            pl.BlockSpec((tm, D_MODEL), lambda i: (i, 0)),
            pl.BlockSpec((1, N_MOD, D_MODEL), lambda i: (mod_row(i), 0, 0)),
            _const_spec((1, D_MODEL)),
            _const_spec((SSM_CH, D_MODEL)),
        ],
        out_specs=pl.BlockSpec((S5_T, SSM_CH, tm // S5_T), lambda i: (0, 0, i)),
        scratch_shapes=[pltpu.VMEM((D_MODEL // LANES, tm, LANES), F32)],
        compiler_params=_params(("parallel",)),
    )(x, mod, g1, w_su_t)


def _s5_kernel(*refs, two_pass):
    if two_pass:
        u_ref, tt_ref, wx_ref, wy_ref, a_ref, aseg_ref, s0_ref, y_ref, v_ref, init_ref = refs
    else:
        u_ref, tt_ref, wx_ref, wy_ref, a_ref, y_ref, fin_ref, v_ref = refs
    gb = tt_ref.shape[1]
    n_chunk = u_ref.shape[2] // S5_ROWS
    grp = SSM_GROUP

    def put_outputs(j, yt, accumulate):
        for t in range(S5_T):
            if accumulate:
                y_ref[t, j * grp:(j + 1) * grp, :] += yt[t * grp:(t + 1) * grp, :]
            else:
                y_ref[t, j * grp:(j + 1) * grp, :] = yt[t * grp:(t + 1) * grp, :]

    for j in range(gb):
        z = jnp.concatenate([u_ref[s, j * grp:(j + 1) * grp, :] for s in range(S5_T)], axis=0)
        put_outputs(j, _dot(tt_ref[0, j], z) + _dot(tt_ref[1, j], z), accumulate=False)
        for d in range(2):
            v_ref[d, j] = _dot_tn(z, wx_ref[d, j])

    def times(coef_ref, d, j, x):
        lanes = slice(j * S5_GL, (j + 1) * S5_GL)
        return coef_ref[d, 0:1, lanes] * x + coef_ref[d, 1:2, lanes] * pltpu.roll(x, SSM_STATE, axis=1)

    def scan(init, keep_states):
        xs = [list(init[0]), list(init[1])]
        for s in range(n_chunk):
            for d in range(2):
                c = s if d == 0 else n_chunk - 1 - s
                rows = pl.ds(c, S5_ROWS, stride=n_chunk)
                for j in range(gb):
                    v = v_ref[d, j, rows, :]
                    if keep_states:
                        v_ref[d, j, rows, :] = xs[d][j]
                    xs[d][j] = times(a_ref, d, j, xs[d][j]) + v
        return xs

    zero = [[jnp.zeros((S5_ROWS, S5_GL), F32)] * gb] * 2
    if two_pass:
        local = scan(zero, keep_states=False)
        n_batch = s0_ref.shape[1]
        n_seg = S5_ROWS // n_batch
        for d in range(2):
            order = range(n_seg) if d == 0 else range(n_seg - 1, -1, -1)
            for j in range(gb):
                for b in range(n_batch):
                    cur = s0_ref[d, b:b + 1, j * S5_GL:(j + 1) * S5_GL]
                    for sgm in order:
                        r = b * n_seg + sgm
                        init_ref[d, j, r:r + 1, :] = cur
                        cur = times(aseg_ref, d, j, cur) + local[d][j][r:r + 1, :]
        scan([[init_ref[d, j] for j in range(gb)] for d in range(2)], keep_states=True)
    else:
        final = scan(zero, keep_states=True)
        for d in range(2):
            for j in range(gb):
                fin_ref[d, :, j * S5_GL:(j + 1) * S5_GL] = final[d][j]
    for j in range(gb):
        put_outputs(j, _dot_nt(wy_ref[0, j], v_ref[0, j].astype(BF16))
                    + _dot_nt(wy_ref[1, j], v_ref[1, j].astype(BF16)), accumulate=True)


def _s5(u, tt, wx, wy, a_t, a_seg, s0):
    two_pass = s0 is not None
    n_rows = u.shape[2]
    gb = S5_GB
    io_spec = pl.BlockSpec((S5_T, gb * SSM_GROUP, n_rows), lambda i: (0, i, 0))
    coef_spec = pl.BlockSpec((2, 2, gb * S5_GL), lambda i: (0, 0, i))
    in_specs = [
        io_spec,
        pl.BlockSpec((2, gb) + tt.shape[2:], lambda i: (0, i, 0, 0)),
        pl.BlockSpec((2, gb) + wx.shape[2:], lambda i: (0, i, 0, 0)),
        pl.BlockSpec((2, gb) + wy.shape[2:], lambda i: (0, i, 0, 0)),
        coef_spec,
    ]
    args = [u, tt, wx, wy, a_t]
    y_shape = jax.ShapeDtypeStruct(u.shape, F32)
    scratch = [pltpu.VMEM((2, gb, n_rows, S5_GL), F32)]
    if two_pass:
        in_specs += [coef_spec, pl.BlockSpec((2, s0.shape[1], gb * S5_GL), lambda i: (0, 0, i))]
        args += [a_seg, s0]
        out_shape, out_specs = y_shape, io_spec
        scratch.append(pltpu.VMEM((2, gb, S5_ROWS, S5_GL), F32))
    else:
        out_shape = (y_shape, jax.ShapeDtypeStruct((2, S5_ROWS, SSM_GROUPS * S5_GL), F32))
        out_specs = (io_spec, pl.BlockSpec((2, S5_ROWS, gb * S5_GL), lambda i: (0, 0, i)))
    return pl.pallas_call(
        functools.partial(_s5_kernel, two_pass=two_pass),
        out_shape=out_shape,
        grid=(SSM_GROUPS // gb,),
        in_specs=in_specs,
        out_specs=out_specs,
        scratch_shapes=scratch,
        compiler_params=_params(("parallel",)),
    )(*args)


def _s5_out_kernel(y_ref, u_ref, d_ref, w_ref, o_ref, buf_ref):
    n_chunk = y_ref.shape[2]
    n_col = D_MODEL // LANES
    for s in range(S5_T):
        y = y_ref[s] + d_ref[...] * u_ref[s].astype(F32)
        glu = _dot_tn(jax.nn.gelu(y).astype(BF16), w_ref[...])
        out = glu[:, 0:D_MODEL] * jax.nn.sigmoid(glu[:, D_MODEL:2 * D_MODEL])
        for j in range(n_col):
            buf_ref[j, pl.ds(s, n_chunk, stride=S5_T), :] = out[:, j * LANES:(j + 1) * LANES]
    for j in range(n_col):
        o_ref[:, j * LANES:(j + 1) * LANES] = buf_ref[j].astype(BF16)


def _s5_out(y, u, d_skip, w_glu):
    n = y.shape[2] * S5_T
    tm = S5_TM
    io_spec = pl.BlockSpec((S5_T, SSM_CH, tm // S5_T), lambda i: (0, 0, i))
    return pl.pallas_call(
        _s5_out_kernel,
        out_shape=jax.ShapeDtypeStruct((n, D_MODEL), BF16),
        grid=(n // tm,),
        in_specs=[io_spec, io_spec, _const_spec((SSM_CH, 1)), _const_spec((SSM_CH, 2 * D_MODEL))],
        out_specs=pl.BlockSpec((tm, D_MODEL), lambda i: (i, 0)),
        scratch_shapes=[pltpu.VMEM((D_MODEL // LANES, tm, LANES), F32)],
        compiler_params=_params(("parallel",)),
    )(y, u, d_skip, w_glu)


def _route(probs):
    epg = EXPERTS_PER_GROUP
    groups = [probs[g * epg:(g + 1) * epg] for g in range(N_EXPERT_GROUPS)]
    scores = []
    for grp in groups:
        best = None
        for a in range(epg):
            for b in range(a + 1, epg):
                pair = grp[a] + grp[b]
                best = pair if best is None else jnp.maximum(best, pair)
        scores.append(best)
    top_score = scores[0]
    top_group = jnp.zeros_like(scores[0], dtype=jnp.int32)
    for g in range(1, N_EXPERT_GROUPS):
        better = scores[g] > top_score
        top_score = jnp.where(better, scores[g], top_score)
        top_group = jnp.where(better, g, top_group)
    sel = []
    for j in range(epg):
        v = groups[0][j]
        for g in range(1, N_EXPERT_GROUPS):
            v = jnp.where(top_group == g, groups[g][j], v)
        sel.append(v)
    v1 = sel[0]
    i1 = jnp.zeros_like(top_group)
    for j in range(1, epg):
        better = sel[j] > v1
        v1 = jnp.where(better, sel[j], v1)
        i1 = jnp.where(better, j, i1)
    v2 = jnp.full_like(v1, -1.0)
    i2 = jnp.zeros_like(top_group)
    for j in range(epg):
        better = (sel[j] > v2) & (i1 != j)
        v2 = jnp.where(better, sel[j], v2)
        i2 = jnp.where(better, j, i2)
    total = v1 + v2
    w1 = v1 / total
    w2 = v2 / total
    out = []
    for g in range(N_EXPERT_GROUPS):
        for j in range(epg):
            w = jnp.where(i1 == j, w1, jnp.where(i2 == j, w2, 0.0))
            out.append(jnp.where(top_group == g, w, 0.0))
    return out


def _merge_kernel(x_ref, ro_ref, att_ref, ssm_ref, gates_ref, mod_ref, wro_ref, wao_ref, wout_ref, g2_ref, wrt_ref,
                  br_ref, x1_ref, h2_ref, comb_ref):
    ret_branch = _dot(ro_ref[...], wro_ref[...])
    att_branch = _dot_tn(att_ref[...], wao_ref[...])

    def gate(j):
        return jax.nn.sigmoid(gates_ref[:, j * D_MODEL:(j + 1) * D_MODEL].astype(F32))

    merged = gate(0) * ret_branch + gate(1) * att_branch + gate(2) * ssm_ref[...].astype(F32)
    x1 = x_ref[...] + mod_ref[0, 2:3, :] * _dot(merged.astype(BF16), wout_ref[...])
    x1_ref[...] = x1
    xn = x1 * lax.rsqrt(jnp.mean(x1 * x1, axis=-1, keepdims=True) + EPS) * g2_ref[...]
    h2 = xn * (1.0 + mod_ref[0, 4:5, :]) + mod_ref[0, 3:4, :]
    h2_ref[...] = h2.astype(BF16)

    logits = lax.dot_general(wrt_ref[...], h2, (((1,), (1,)), ((), ())), precision=lax.Precision.HIGHEST,
                             preferred_element_type=F32) + br_ref[...]
    e = jnp.exp(logits - jnp.max(logits, axis=0, keepdims=True))
    probs = e / jnp.sum(e, axis=0, keepdims=True)
    rows = _route([probs[j:j + 1, :] for j in range(N_EXPERTS)])
    comb_ref[...] = jnp.concatenate(rows, axis=0)


def _merge(x, ro, att_t, ssm, gates, mod, mod_rows, wro, wao, wout, g2, wrt, br):
    n = x.shape[0]
    tm = TM_PROJ
    mod_row = _mod_row_fn(mod_rows, tm)
    row = lambda i: (i, 0)
    in_specs = [
        pl.BlockSpec((tm, D_MODEL), row),
        pl.BlockSpec((tm, RET_W), row),
        pl.BlockSpec((ATT_W, tm), lambda i: (0, i)),
        pl.BlockSpec((tm, D_MODEL), row),
        pl.BlockSpec((tm, 3 * D_MODEL), row),
        pl.BlockSpec((1, N_MOD, D_MODEL), lambda i: (mod_row(i), 0, 0)),
        _const_spec((RET_W, D_MODEL)),
        _const_spec((ATT_W, D_MODEL)),
        _const_spec((D_MODEL, D_MODEL)),
        _const_spec((1, D_MODEL)),
        _const_spec((N_EXPERTS, D_MODEL)),
        _const_spec((N_EXPERTS, 1)),
    ]
    return pl.pallas_call(
        _merge_kernel,
        out_shape=(
            jax.ShapeDtypeStruct((n, D_MODEL), F32),
            jax.ShapeDtypeStruct((n, D_MODEL), BF16),
            jax.ShapeDtypeStruct((N_EXPERTS, n), F32),
        ),
        grid=(n // tm,),
        in_specs=in_specs,
        out_specs=(
            pl.BlockSpec((tm, D_MODEL), row),
            pl.BlockSpec((tm, D_MODEL), row),
            pl.BlockSpec((N_EXPERTS, tm), lambda i: (0, i)),
        ),
        compiler_params=_params(("parallel",)),
    )(x, ro, att_t, ssm, gates, mod, wro, wao, wout, g2, wrt, br)


def _moe_kernel(*refs, final):
    if final:
        h2_ref, comb_ref, x1_ref, mod_ref, wg_ref, wu_ref, wd_ref, fg_ref, o_ref, acc_ref = refs
    else:
        h2_ref, comb_ref, x1_ref, mod_ref, wg_ref, wu_ref, wd_ref, o_ref, acc_ref = refs
    e = pl.program_id(1)

    @pl.when(e == 0)
    def _():
        acc_ref[...] = jnp.zeros_like(acc_ref)

    h = h2_ref[...]
    g = _dot(h, wg_ref[0])
    u = _dot(h, wu_ref[0])
    comb = comb_ref[...]
    lane = lax.broadcasted_iota(jnp.int32, comb.shape, 1)
    w = jnp.sum(jnp.where(lane == e, comb, 0.0), axis=1, keepdims=True)
    hid = (g * jax.nn.sigmoid(g) * u * w).astype(BF16)
    acc_ref[...] += _dot(hid, wd_ref[0])

    @pl.when(e == N_EXPERTS - 1)
    def _():
        x2 = x1_ref[...] + mod_ref[0, 5:6, :] * acc_ref[...]
        if final:
            x2 = x2 * lax.rsqrt(jnp.mean(x2 * x2, axis=-1, keepdims=True) + EPS) * fg_ref[...]
        o_ref[...] = x2


def _moe(h2, comb, x1, mod, mod_rows, wg, wu, wd, final_g):
    n = h2.shape[0]
    tm = TM_MOE
    mod_row = _mod_row_fn(mod_rows, tm)
    final = final_g is not None
    row = lambda i, e: (i, 0)
    in_specs = [
        pl.BlockSpec((tm, D_MODEL), row),
        pl.BlockSpec((tm, N_EXPERTS), row),
        pl.BlockSpec((tm, D_MODEL), row),
        pl.BlockSpec((1, N_MOD, D_MODEL), lambda i, e: (mod_row(i), 0, 0)),
        pl.BlockSpec((1, D_MODEL, D_FF), lambda i, e: (e, 0, 0)),
        pl.BlockSpec((1, D_MODEL, D_FF), lambda i, e: (e, 0, 0)),
        pl.BlockSpec((1, D_FF, D_MODEL), lambda i, e: (e, 0, 0)),
    ]
    args = [h2, comb, x1, mod, wg, wu, wd]
    if final:
        in_specs.append(pl.BlockSpec((1, D_MODEL), lambda i, e: (0, 0)))
        args.append(final_g)
    return pl.pallas_call(
        functools.partial(_moe_kernel, final=final),
        out_shape=jax.ShapeDtypeStruct((n, D_MODEL), F32),
        grid=(n // tm, N_EXPERTS),
        in_specs=in_specs,
        out_specs=pl.BlockSpec((tm, D_MODEL), row),
        scratch_shapes=[pltpu.VMEM((tm, D_MODEL), F32)],
        compiler_params=_params(("parallel", "arbitrary")),
    )(*args)


def _block_diag(m):
    g, r, c = m.shape
    eye = jnp.eye(g, dtype=m.dtype)
    return (eye[:, None, :, None] * m[:, :, None, :]).reshape(g * r, g * c)


def _s5_direction_params(a_re, a_im, log_dt, b_re, b_im, c_re, c_im, reverse):
    n_g = a_re.shape[0]
    width = S5_T * SSM_GROUP
    hp = lax.Precision.HIGHEST
    step = jnp.exp(log_dt)[:, None]
    mag = jnp.exp(a_re * step)
    ar = mag * jnp.cos(a_im * step)
    ai = mag * jnp.sin(a_im * step)
    den = a_re * a_re + a_im * a_im
    fr = ((ar - 1.0) * a_re + ai * a_im) / den
    fi = (ai * a_re - (ar - 1.0) * a_im) / den
    bbr = fr[:, :, None] * b_re - fi[:, :, None] * b_im
    bbi = fr[:, :, None] * b_im + fi[:, :, None] * b_re
    pr, pi = [jnp.ones_like(ar)], [jnp.zeros_like(ar)]
    for _ in range(S5_T):
        r, i = pr[-1], pi[-1]
        pr.append(r * ar - i * ai)
        pi.append(r * ai + i * ar)
    pr, pi = jnp.stack(pr), jnp.stack(pi)
    mr = pr[..., None] * bbr - pi[..., None] * bbi
    mi = pr[..., None] * bbi + pi[..., None] * bbr
    cr = c_re * pr[:, :, None, :] - c_im * pi[:, :, None, :]
    ci = c_re * pi[:, :, None, :] + c_im * pr[:, :, None, :]
    kern = (jnp.einsum('gop,kgpi->kgoi', c_re, mr[:S5_T], precision=hp)
            - jnp.einsum('gop,kgpi->kgoi', c_im, mi[:S5_T], precision=hp))
    t = jnp.arange(S5_T)
    lag = (t[:, None] - t[None, :]) if reverse else (t[None, :] - t[:, None])
    toep = jnp.where((lag >= 0)[:, :, None, None, None], kern[jnp.clip(lag, 0, S5_T - 1)], 0.0)
    tt = toep.transpose(2, 1, 3, 0, 4).reshape(n_g, width, width)
    e_x = t if reverse else S5_T - 1 - t
    wx = jnp.concatenate([mr[e_x], mi[e_x]], axis=2)
    wx = wx.transpose(1, 0, 3, 2).reshape(n_g, width, S5_GL)
    e_y = S5_T - t if reverse else t + 1
    wy = jnp.concatenate([cr[e_y], -ci[e_y]], axis=3)
    wy = wy.transpose(1, 0, 2, 3).reshape(n_g, width, S5_GL)

    def coef_rows(r, i):
        return jnp.stack([jnp.concatenate([r, r], axis=1).reshape(-1), jnp.concatenate([-i, i], axis=1).reshape(-1)])

    sr, si = pr[S5_T], pi[S5_T]
    a_t = coef_rows(sr, si)
    for _ in range(int(math.log2(S5_SEG // S5_T))):
        sr, si = sr * sr - si * si, 2.0 * sr * si
    return tt.astype(BF16), wx.astype(BF16), wy.astype(BF16), a_t, coef_rows(sr, si)


def _rope_tables(n_tokens):
    rows = n_tokens // GRID_W
    row = jnp.repeat(jnp.arange(rows, dtype=F32), GRID_W)
    col = jnp.tile(jnp.arange(GRID_W, dtype=F32), rows)
    n_freq = ATT_HEAD_DIM // 4
    inv = ROPE_THETA ** (-jnp.arange(n_freq, dtype=F32) / n_freq)
    ang = jnp.concatenate([row[:, None] * inv, col[:, None] * inv], axis=-1)
    cos = jnp.repeat(jnp.cos(ang), 2, axis=-1)
    sin = jnp.repeat(jnp.sin(ang), 2, axis=-1) * jnp.tile(jnp.array([-1.0, 1.0], F32), ATT_HEAD_DIM // 2)
    return cos, sin


def _swap_pairs(a, axis=-1):
    idx = jnp.arange(a.shape[axis]) ^ 1
    return jnp.take(a, idx, axis=axis)


def _layer(x, batch, mod, mod_rows, p, ctx):
    n = x.shape[0]
    seq = n // batch
    latent = ctx is not None
    outs = _inproj(x, mod, mod_rows, p, latent)
    rqkv, rg, gates, q_t, k16, v_t = outs[:6]

    ro, ret_state = _retention(rqkv.reshape(batch, seq, 3 * RET_W), rg.reshape(batch, seq, RET_W), p['ret_lg'],
                               p['ret_gn'], ctx[2] if latent else None, bb=1 if latent else 8)

    if latent:
        k_ctx = ctx[0].reshape(batch * ATT_KC, ATT_KV_W).astype(BF16)
        v_ctx_t = ctx[1].reshape(batch, ATT_KC, ATT_KV_W).transpose(0, 2, 1).reshape(batch * ATT_KV_W, ATT_KC)
        att_t = _attention(q_t, k16, v_t, k_ctx, v_ctx_t.astype(BF16), batch)
    else:
        att_t = _attention(q_t, k16, v_t, None, None, batch)

    u = _s5_in(x, mod, mod_rows, p['g1'], p['w_su_t'])
    if latent:
        s0 = jnp.stack([ctx[3], ctx[4]], axis=3).transpose(1, 0, 2, 3, 4).reshape(2, batch, SSM_GROUPS * S5_GL)
        y = _s5(u, p['s5_tt'], p['s5_wx'], p['s5_wy'], p['s5_a'], p['s5_aseg'], s0)
        ssm_state = None
    else:
        y, ssm_state = _s5(u, p['s5_tt'], p['s5_wx'], p['s5_wy'], p['s5_a'], None, None)
        ssm_state = ssm_state.reshape(2, batch, SSM_GROUPS, 2, SSM_STATE)
    ssm = _s5_out(y, u, p['ssm_d'], p['w_glu'])

    x1, h2, comb_t = _merge(x, ro.reshape(n, RET_W), att_t, ssm, gates, mod, mod_rows, p['w_ret_out'],
                            p['w_att_out'], p['w_out'], p['g2'], p['w_router_t'], p['b_router'])
    x2 = _moe(h2, comb_t.T, x1, mod, mod_rows, p['w_gate'], p['w_up'], p['w_down'], p.get('final_g'))
    if latent:
        return x2, None
    return x2, (outs[6], outs[7], ret_state, ssm_state)


def kernel(x_prompt, x_sample, c, cache_attn_k, cache_attn_v, state_ret, state_ssm_re, state_ssm_im, c_ctx, w_ada,
           b_ada, norm1_g, norm2_g, w_in, ret_lg_f, ret_lg_b, ret_norm_g, w_ret_out, att_q_norm_g, att_k_norm_g,
           w_att_out, a_re_f, a_im_f, a_re_b, a_im_b, log_dt_f, log_dt_b, ssm_b_re, ssm_b_im, ssm_c_re, ssm_c_im,
           ssm_d, w_glu, w_out, w_router, b_router, w_gate, w_up, w_down, final_norm_g):
    batch, seq, _ = x_prompt.shape
    dec_batch, dec_seq, _ = x_sample.shape
    assert batch * seq == S5_ROWS * S5_SEG and dec_batch * dec_seq == S5_ROWS * S5_SEG
    assert seq == S5_SEG and dec_seq % S5_SEG == 0
    assert cache_attn_k.shape[2] == ATT_KC and seq % ATT_QB == 0

    c_rows = jnp.zeros((ADA_ROWS, D_MODEL), F32).at[0].set(c_ctx).at[1:1 + dec_batch].set(c)
    mod_all = _ada(c_rows, w_ada, b_ada).reshape(DEPTH, ADA_ROWS, N_MOD, D_MODEL)

    cos, sin = _rope_tables(dec_seq)
    ones = _block_diag(jnp.ones((ATT_KV_HEADS, ATT_HEAD_DIM, ATT_HEAD_DIM), BF16))

    xp = x_prompt.reshape(batch * seq, D_MODEL)
    xs = x_sample.reshape(dec_batch * dec_seq, D_MODEL)
    ks_, vs_, rets_, ssms_ = [], [], [], []
    for l in range(DEPTH):
        w_l = w_in[l]
        gq = att_q_norm_g[l][:, None]
        gk = jnp.tile(att_k_norm_g[l], ATT_KV_HEADS)[None]
        wq_t = w_l[:, C_AQ:C_AK].T
        fwd = _s5_direction_params(a_re_f[l], a_im_f[l], log_dt_f[l], ssm_b_re[l], ssm_b_im[l], ssm_c_re[l],
                                   ssm_c_im[l], reverse=False)
        bwd = _s5_direction_params(a_re_b[l], a_im_b[l], log_dt_b[l], ssm_b_re[l], ssm_b_im[l], ssm_c_re[l],
                                   ssm_c_im[l], reverse=True)
        p = {
            'g1': norm1_g[l][None], 'g2': norm2_g[l][None], 'w_in': w_l.astype(BF16), 'ones': ones,
            'wq_t': wq_t.astype(BF16), 'wv_t': w_l[:, C_AV:C_SU].T.astype(BF16), 'gq': gq, 'gk': gk,
            'wqs_t': _swap_pairs(wq_t, axis=0).astype(BF16), 'wk_sw': _swap_pairs(w_l[:, C_AK:C_AV]).astype(BF16),
            'gqs': _swap_pairs(gq, axis=0), 'gks': _swap_pairs(gk),
            'cos': jnp.tile(cos, (1, ATT_KV_HEADS)), 'sin': jnp.tile(sin, (1, ATT_KV_HEADS)),
            'cos_t': cos.T, 'sin_t': sin.T,
            'ret_lg': jnp.stack([ret_lg_f[l], ret_lg_b[l]]), 'ret_gn': ret_norm_g[l][None],
            'w_ret_out': w_ret_out[l].astype(BF16), 'w_att_out': w_att_out[l].astype(BF16),
            'w_su_t': w_l[:, C_SU:C_G].T.astype(BF16),
            's5_tt': jnp.stack([fwd[0], bwd[0]]), 's5_wx': jnp.stack([fwd[1], bwd[1]]),
            's5_wy': jnp.stack([fwd[2], bwd[2]]), 's5_a': jnp.stack([fwd[3], bwd[3]]),
            's5_aseg': jnp.stack([fwd[4], bwd[4]]),
            'ssm_d': ssm_d[l][:, None], 'w_glu': w_glu[l].astype(BF16), 'w_out': w_out[l].astype(BF16),
            'w_router_t': w_router.T, 'b_router': b_router[:, None],
            'w_gate': w_gate[l].astype(BF16), 'w_up': w_up[l].astype(BF16), 'w_down': w_down[l].astype(BF16),
        }
        if l == DEPTH - 1:
            p['final_g'] = final_norm_g[None]
        mod = mod_all[l]

        xp, (k_c, v_c, st_r, st_s) = _layer(xp, batch, mod, (0, batch * seq), p, None)
        ks_.append(k_c.reshape(batch, seq, ATT_KV_HEADS, ATT_HEAD_DIM))
        vs_.append(v_c.reshape(batch, seq, ATT_KV_HEADS, ATT_HEAD_DIM))
        rets_.append(st_r)
        ssms_.append(st_s)

        ctx = (cache_attn_k[:, l], cache_attn_v[:, l], state_ret[:, l], state_ssm_re[:, l], state_ssm_im[:, l])
        xs, _ = _layer(xs, dec_batch, mod, (1, dec_seq), p, ctx)

    ssm_all = jnp.stack(ssms_, axis=1)
    ssm_all = ssm_all.transpose(2, 1, 0, 3, 4, 5)
    new_re = ssm_all[..., 0, :]
    new_im = ssm_all[..., 1, :]
    return (xp.reshape(batch, seq, D_MODEL), xs.reshape(dec_batch, dec_seq, D_MODEL),
            jnp.stack(ks_, axis=1), jnp.stack(vs_, axis=1), jnp.stack(rets_, axis=1), new_re, new_im)
```

```python
import functools
import math

import jax
import jax.numpy as jnp
from jax import lax
from jax.experimental import pallas as pl
from jax.experimental.pallas import tpu as pltpu

F32 = jnp.float32
BF16 = jnp.bfloat16

D_MODEL = 1024
DEPTH = 2
GRID_W = 64
RET_HEADS = 4
RET_DK = 128
RET_DV = 128
RET_CHUNK = 128
RET_W = RET_HEADS * RET_DV
ATT_HEADS = 8
ATT_KV_HEADS = 2
ATT_GROUP = ATT_HEADS // ATT_KV_HEADS
ATT_HEAD_DIM = 64
ATT_W = ATT_HEADS * ATT_HEAD_DIM
ATT_KV_W = ATT_KV_HEADS * ATT_HEAD_DIM
ROPE_THETA = 10000.0
SSM_CH = 512
SSM_GROUP = 16
SSM_GROUPS = SSM_CH // SSM_GROUP
SSM_STATE = 64
N_EXPERTS = 16
N_EXPERT_GROUPS = 4
EXPERTS_PER_GROUP = N_EXPERTS // N_EXPERT_GROUPS
D_FF = 512
N_MOD = 6
EPS = 1e-6

C_RQ, C_RK, C_RV, C_RG = 0, 512, 1024, 1536
C_AQ, C_AK, C_AV, C_SU, C_G, C_END = 2048, 2560, 2688, 2816, 3328, 6400

VMEM_LIMIT = 52 * 1024 * 1024

ATT_Q_SCALE = ATT_HEAD_DIM ** -0.5 * math.log2(math.e)
ATT_QB = 256
ATT_KC = 256
ATT_V_ROWS = ATT_HEAD_DIM + 16
ATT_RING = 4

TM_PROJ = 512
MOE_TILE = 512
MOE_PART = 4096
META_ROWS = 8
S5_ROWS = 32
S5_SEG = 256
S5_T = 16
S5_GL = 2 * SSM_STATE
S5_GB = 8
S5_TM = 2048
LANES = 128
STAGE_ROWS = 256
ADA_ROWS = 8
ADA_TN = 1536


def _dot(a, b):
    return jnp.dot(a, b, preferred_element_type=F32)


def _dot_nt(a, b):
    return lax.dot_general(a, b, (((1,), (1,)), ((), ())), preferred_element_type=F32)


def _dot_tn(a, b):
    return lax.dot_general(a, b, (((0,), (0,)), ((), ())), preferred_element_type=F32)


def _const_spec(shape):
    n = len(shape)
    return pl.BlockSpec(shape, lambda *_: (0,) * n, pipeline_mode=pl.Buffered(1))


def _params(sem, vmem=VMEM_LIMIT):
    return pltpu.CompilerParams(dimension_semantics=sem, vmem_limit_bytes=vmem)


def _mod_row_fn(mod_rows, tm):
    base, per_row = mod_rows
    return lambda i: base + (i * tm) // per_row


def _modulated_norm(x, g, mod_ref):
    xn = x * lax.rsqrt(jnp.mean(x * x, axis=-1, keepdims=True) + EPS) * g
    return xn * (1.0 + mod_ref[0, 1:2, :]) + mod_ref[0, 0:1, :]


def _ada_kernel(c_ref, w_ref, b_ref, o_ref):
    c = c_ref[...]
    a = (c * jax.nn.sigmoid(c)).astype(BF16)
    o_ref[0] = _dot(a, w_ref[0].astype(BF16)) + b_ref[0]


def _ada(c_rows, w_ada, b_ada):
    n_col = N_MOD * D_MODEL
    return pl.pallas_call(
        _ada_kernel,
        out_shape=jax.ShapeDtypeStruct((DEPTH, ADA_ROWS, n_col), F32),
        grid=(DEPTH, n_col // ADA_TN),
        in_specs=[
            pl.BlockSpec((ADA_ROWS, D_MODEL), lambda l, j: (0, 0)),
            pl.BlockSpec((1, D_MODEL, ADA_TN), lambda l, j: (l, 0, j)),
            pl.BlockSpec((1, 1, ADA_TN), lambda l, j: (l, 0, j)),
        ],
        out_specs=pl.BlockSpec((1, ADA_ROWS, ADA_TN), lambda l, j: (l, 0, j)),
        compiler_params=_params(("arbitrary", "arbitrary")),
    )(c_rows, w_ada, b_ada.reshape(DEPTH, 1, n_col))


def _head_mean_sq(z, ones):
    z2 = z * z
    hi = z2.astype(BF16)
    lo = (z2 - hi.astype(F32)).astype(BF16)
    return (_dot(hi, ones) + _dot(lo, ones)) * (1.0 / ATT_HEAD_DIM)


def _inproj_kernel(*refs, latent):
    if latent:
        (x_ref, mod_ref, g1_ref, w_ref, wqt_ref, wvt_ref, ones_ref, gq_ref, gk_ref,
         wqst_ref, wks_ref, gqs_ref, gks_ref, cos_ref, sin_ref, cost_ref, sint_ref,
         rqkv_ref, rg_ref, gates_ref, qt_ref, k16_ref, vt_ref) = refs
    else:
        (x_ref, mod_ref, g1_ref, w_ref, wqt_ref, wvt_ref, ones_ref, gq_ref, gk_ref,
         rqkv_ref, rg_ref, gates_ref, qt_ref, k16_ref, vt_ref, ak_ref, av_ref) = refs
    tm = x_ref.shape[0]
    h = _modulated_norm(x_ref[...], g1_ref[...], mod_ref).astype(BF16)

    def seg(a, b):
        return _dot(h, w_ref[:, a:b])

    rqkv_ref[:, 0:RET_W] = (seg(C_RQ, C_RK) * (RET_DK ** -0.5)).astype(BF16)
    rqkv_ref[:, RET_W:3 * RET_W] = seg(C_RK, C_RG).astype(BF16)
    rg_ref[...] = seg(C_RG, C_AQ).astype(BF16)
    for j in range(3):
        gates_ref[:, j * D_MODEL:(j + 1) * D_MODEL] = seg(C_G + j * D_MODEL, C_G + (j + 1) * D_MODEL).astype(BF16)

    zk = seg(C_AK, C_AV)
    inv_k = lax.rsqrt(_head_mean_sq(zk, ones_ref[...]) + EPS)
    yk = zk * inv_k * gk_ref[...]
    zq = _dot_nt(wqt_ref[...], h).reshape(ATT_HEADS, ATT_HEAD_DIM, tm)
    inv_q = lax.rsqrt(jnp.mean(zq * zq, axis=1, keepdims=True) + EPS)
    yq = zq * inv_q * gq_ref[...]
    if latent:
        yk_sw = _dot(h, wks_ref[...]) * inv_k * gks_ref[...]
        yk = yk * cos_ref[...] + yk_sw * sin_ref[...]
        zq_sw = _dot_nt(wqst_ref[...], h).reshape(ATT_HEADS, ATT_HEAD_DIM, tm)
        yq = yq * cost_ref[...] + (zq_sw * inv_q * gqs_ref[...]) * sint_ref[...]
    qt_ref[...] = (yq * ATT_Q_SCALE).reshape(ATT_W, tm).astype(BF16)
    k16_ref[...] = yk.astype(BF16)
    vt = _dot_nt(wvt_ref[...], h).astype(BF16)
    for c in range(tm // ATT_KC):
        vt_ref[c] = vt[:, c * ATT_KC:(c + 1) * ATT_KC]
    if not latent:
        ak_ref[...] = yk
        av_ref[...] = seg(C_AV, C_SU)


def _inproj(x, mod, mod_rows, p, latent):
    n = x.shape[0]
    tm = TM_PROJ
    mod_row = _mod_row_fn(mod_rows, tm)
    row = lambda i: (i, 0)
    in_specs = [
        pl.BlockSpec((tm, D_MODEL), row),
        pl.BlockSpec((1, N_MOD, D_MODEL), lambda i: (mod_row(i), 0, 0)),
        _const_spec((1, D_MODEL)),
        _const_spec((D_MODEL, C_END)),
        _const_spec((ATT_W, D_MODEL)),
        _const_spec((ATT_KV_W, D_MODEL)),
        _const_spec((ATT_KV_W, ATT_KV_W)),
        _const_spec((ATT_HEAD_DIM, 1)),
        _const_spec((1, ATT_KV_W)),
    ]
    args = [x, mod, p['g1'], p['w_in'], p['wq_t'], p['wv_t'], p['ones'], p['gq'], p['gk']]
    if latent:
        n_pos = p['cos'].shape[0] // tm
        in_specs += [
            _const_spec((ATT_W, D_MODEL)),
            _const_spec((D_MODEL, ATT_KV_W)),
            _const_spec((ATT_HEAD_DIM, 1)),
            _const_spec((1, ATT_KV_W)),
            pl.BlockSpec((tm, ATT_KV_W), lambda i: (i % n_pos, 0)),
            pl.BlockSpec((tm, ATT_KV_W), lambda i: (i % n_pos, 0)),
            pl.BlockSpec((ATT_HEAD_DIM, tm), lambda i: (0, i % n_pos)),
            pl.BlockSpec((ATT_HEAD_DIM, tm), lambda i: (0, i % n_pos)),
        ]
        args += [p['wqs_t'], p['wk_sw'], p['gqs'], p['gks'], p['cos'], p['sin'], p['cos_t'], p['sin_t']]
    out_shape = [
        jax.ShapeDtypeStruct((n, 3 * RET_W), BF16),
        jax.ShapeDtypeStruct((n, RET_W), BF16),
        jax.ShapeDtypeStruct((n, 3 * D_MODEL), BF16),
        jax.ShapeDtypeStruct((ATT_W, n), BF16),
        jax.ShapeDtypeStruct((n, ATT_KV_W), BF16),
        jax.ShapeDtypeStruct((n // ATT_KC, ATT_KV_W, ATT_KC), BF16),
    ]
    out_specs = [
        pl.BlockSpec((tm, 3 * RET_W), row),
        pl.BlockSpec((tm, RET_W), row),
        pl.BlockSpec((tm, 3 * D_MODEL), row),
        pl.BlockSpec((ATT_W, tm), lambda i: (0, i)),
        pl.BlockSpec((tm, ATT_KV_W), row),
        pl.BlockSpec((tm // ATT_KC, ATT_KV_W, ATT_KC), lambda i: (i, 0, 0)),
    ]
    if not latent:
        out_shape += [jax.ShapeDtypeStruct((n, ATT_KV_W), F32)] * 2
        out_specs += [pl.BlockSpec((tm, ATT_KV_W), row)] * 2
    return pl.pallas_call(
        functools.partial(_inproj_kernel, latent=latent),
        out_shape=tuple(out_shape),
        grid=(n // tm,),
        in_specs=in_specs,
        out_specs=tuple(out_specs),
        compiler_params=_params(("parallel",)),
    )(*args)


T_DEC_F, T_DEC_B, T_XI_F, T_ZETA_F, T_XI_B, T_ZETA_B, T_CD_F, T_CD_B, N_TAB = range(9)


def _ret_kernel(*refs, has_s0, bb, seq):
    if has_s0:
        lg_ref, q_ref, k_ref, v_ref, rg_ref, gn_ref, s0_ref, o_ref, st_ref, tab_ref, acc_ref = refs
    else:
        lg_ref, q_ref, k_ref, v_ref, rg_ref, gn_ref, o_ref, st_ref, tab_ref, acc_ref = refs
    c = RET_CHUNK
    head = pl.program_id(1)
    lgf = lg_ref[0, head]
    lgb = lg_ref[1, head]
    t = lax.broadcasted_iota(jnp.int32, (c, c), 0).astype(F32)
    s = lax.broadcasted_iota(jnp.int32, (c, c), 1).astype(F32)
    tab_ref[T_DEC_F] = jnp.where(t >= s, jnp.exp(lgf * jnp.maximum(t - s, 0.0)), 0.0)
    tab_ref[T_DEC_B] = jnp.where(s >= t, jnp.exp(lgb * jnp.maximum(s - t, 0.0)), 0.0)
    tab_ref[T_XI_F] = jnp.exp(lgf * (t + 1.0))
    tab_ref[T_ZETA_F] = jnp.exp(lgf * (c - 1.0 - t))
    tab_ref[T_XI_B] = jnp.exp(lgb * (c - t))
    tab_ref[T_ZETA_B] = jnp.exp(lgb * t)
    tab_ref[T_CD_F] = jnp.exp(lgf * (c + 0.0 * t))
    tab_ref[T_CD_B] = jnp.exp(lgb * (c + 0.0 * t))
    if has_s0:
        st_ref[...] = s0_ref[...]
    else:
        st_ref[...] = jnp.zeros_like(st_ref)
    acc_ref[...] = jnp.zeros_like(acc_ref)
    n_chunk = seq // c

    def one_direction(rb, d, off, t_dec, t_xi, t_zeta, t_cd):
        rows = pl.ds(off, c)
        q = q_ref[rb, rows, :]
        k = k_ref[rb, rows, :]
        v = v_ref[rb, rows, :]
        p = (_dot_nt(q, k) * tab_ref[t_dec]).astype(BF16)
        st = st_ref[rb, d, 0]
        o = _dot(p, v) + _dot(q, st.astype(BF16)) * tab_ref[t_xi]
        acc_ref[rb, rows, :] += o
        kz = (k.astype(F32) * tab_ref[t_zeta]).astype(BF16)
        st_ref[rb, d, 0] = tab_ref[t_cd] * st + _dot_tn(kz, v)

    def body(i, carry):
        off_f = pl.multiple_of(i * c, c)
        off_b = pl.multiple_of((n_chunk - 1 - i) * c, c)
        for rb in range(bb):
            one_direction(rb, 0, off_f, T_DEC_F, T_XI_F, T_ZETA_F, T_CD_F)
            one_direction(rb, 1, off_b, T_DEC_B, T_XI_B, T_ZETA_B, T_CD_B)
        return carry

    lax.fori_loop(0, n_chunk, body, 0)

    def finish(i, carry):
        rows = pl.ds(pl.multiple_of(i * c, c), c)
        for rb in range(bb):
            o = acc_ref[rb, rows, :]
            ro = o * lax.rsqrt(jnp.mean(o * o, axis=-1, keepdims=True) + EPS) * gn_ref[...]
            g = rg_ref[rb, rows, :].astype(F32)
            o_ref[rb, rows, :] = (ro * (g * jax.nn.sigmoid(g))).astype(BF16)
        return carry

    lax.fori_loop(0, n_chunk, finish, 0)


def _retention(rqkv, rg, lg, gn, s0, bb):
    b, seq, _ = rqkv.shape
    has_s0 = s0 is not None
    blk = (bb, seq, RET_DK)
    st_spec = pl.BlockSpec((bb, 2, 1, RET_DK, RET_DV), lambda i, h: (i, 0, h, 0, 0))
    in_specs = [
        pl.BlockSpec(memory_space=pltpu.SMEM),
        pl.BlockSpec(blk, lambda i, h: (i, 0, h)),
        pl.BlockSpec(blk, lambda i, h: (i, 0, RET_HEADS + h)),
        pl.BlockSpec(blk, lambda i, h: (i, 0, 2 * RET_HEADS + h)),
        pl.BlockSpec(blk, lambda i, h: (i, 0, h)),
        pl.BlockSpec((1, RET_DV), lambda i, h: (0, h)),
    ]
    args = [lg, rqkv, rqkv, rqkv, rg, gn]
    if has_s0:
        in_specs.append(st_spec)
        args.append(s0)
    return pl.pallas_call(
        functools.partial(_ret_kernel, has_s0=has_s0, bb=bb, seq=seq),
        out_shape=(
            jax.ShapeDtypeStruct((b, seq, RET_W), BF16),
            jax.ShapeDtypeStruct((b, 2, RET_HEADS, RET_DK, RET_DV), F32),
        ),
        grid=(b // bb, RET_HEADS),
        in_specs=in_specs,
        out_specs=(pl.BlockSpec(blk, lambda i, h: (i, 0, h)), st_spec),
        scratch_shapes=[
            pltpu.VMEM((N_TAB, RET_CHUNK, RET_CHUNK), F32),
            pltpu.VMEM((bb, seq, RET_DV), F32),
        ],
        compiler_params=_params(("parallel", "parallel")),
    )(*args)


ST_M, ST_ALPHA, N_ST = range(3)


def _attn_kernel(*refs, has_ctx):
    if has_ctx:
        q_ref, k_ref, v_ref, kc_ref, vc_ref, o_ref, s_ref, p_ref, acc_ref, st_ref, mx_ref = refs
    else:
        q_ref, k_ref, v_ref, o_ref, s_ref, p_ref, acc_ref, st_ref, mx_ref = refs
    kv_head = pl.program_id(1)
    qb = q_ref.shape[1]
    kc = s_ref.shape[1]
    n_own = k_ref.shape[0] // kc
    n_chunk = n_own + (1 if has_ctx else 0)
    q_t = jnp.concatenate([q_ref[h * ATT_HEAD_DIM:(h + 1) * ATT_HEAD_DIM, :] for h in range(ATT_GROUP)], axis=1)
    q_both = jnp.concatenate([q_t] * ATT_KV_HEADS, axis=0)
    row = lax.broadcasted_iota(jnp.int32, q_both.shape, 0)
    q_pad = jnp.where(row // ATT_HEAD_DIM == kv_head, q_both, jnp.zeros_like(q_both))
    ones_rows = (lax.broadcasted_iota(jnp.int32, (ATT_V_ROWS - ATT_HEAD_DIM, kc), 0) == 0).astype(BF16)

    def keys(j):
        if has_ctx and isinstance(j, int) and j == n_own:
            return kc_ref[...]
        return k_ref[pl.ds(pl.multiple_of(j * kc, kc), kc), :]

    def values(j):
        v = vc_ref[...] if has_ctx and isinstance(j, int) and j == n_own else v_ref[j]
        return jnp.concatenate([v, ones_rows], axis=0)

    def put_scores(j, slot):
        s = _dot(keys(j), q_pad)
        s_ref[slot] = s
        mx_ref[slot] = jnp.max(s, axis=0, keepdims=True)

    def weighted_values(j, slot):
        return _dot(values(j), p_ref[slot])

    def phase(j, slot, first=False, last=False):
        if not last:
            put_scores(j + 1, (slot + 1) % ATT_RING)
        if not first:
            acc_ref[...] = st_ref[ST_ALPHA] * acc_ref[...] + weighted_values(j - 1, (slot - 1) % ATT_RING)
        m = st_ref[ST_M]
        m_new = jnp.maximum(m, mx_ref[slot])
        st_ref[ST_ALPHA] = jnp.exp2(m - m_new)
        st_ref[ST_M] = m_new
        p_ref[slot] = jnp.exp2((s_ref[slot] - m_new).astype(BF16))

    st_ref[ST_M] = jnp.full(st_ref.shape[1:], -jnp.inf, F32)
    acc_ref[...] = jnp.zeros_like(acc_ref)
    put_scores(0, 0)
    phase(0, 0, first=True, last=n_chunk == 1)
    n_loop = max(n_own - 2, 0) // ATT_RING

    def revolution(i, carry):
        for t in range(1, ATT_RING + 1):
            phase(ATT_RING * i + t, t % ATT_RING)
        return carry

    lax.fori_loop(0, n_loop, revolution, 0)
    for j in range(1 + ATT_RING * n_loop, n_chunk):
        phase(j, j % ATT_RING, last=j == n_chunk - 1)
    acc = st_ref[ST_ALPHA] * acc_ref[...] + weighted_values(n_chunk - 1, (n_chunk - 1) % ATT_RING)
    out = (acc[0:ATT_HEAD_DIM] / acc[ATT_HEAD_DIM:ATT_HEAD_DIM + 1]).astype(BF16)
    for h in range(ATT_GROUP):
        o_ref[h * ATT_HEAD_DIM:(h + 1) * ATT_HEAD_DIM, :] = out[:, h * qb:(h + 1) * qb]


def _attention(q_t, k, v_t, k_ctx, v_ctx_t, batch):
    n = q_t.shape[1]
    seq = n // batch
    qb, kc = ATT_QB, ATT_KC
    n_q = seq // qb
    has_ctx = k_ctx is not None
    q_spec = pl.BlockSpec((ATT_GROUP * ATT_HEAD_DIM, qb), lambda b, g, j: (g, b * n_q + j))
    in_specs = [
        q_spec,
        pl.BlockSpec((seq, ATT_KV_W), lambda b, g, j: (b, 0)),
        pl.BlockSpec((seq // kc, ATT_HEAD_DIM, kc), lambda b, g, j: (b, g, 0)),
    ]
    args = [q_t, k, v_t]
    if has_ctx:
        in_specs += [
            pl.BlockSpec((kc, ATT_KV_W), lambda b, g, j: (b, 0)),
            pl.BlockSpec((ATT_HEAD_DIM, kc), lambda b, g, j: (b * ATT_KV_HEADS + g, 0)),
        ]
        args += [k_ctx, v_ctx_t]
    width = ATT_GROUP * qb
    return pl.pallas_call(
        functools.partial(_attn_kernel, has_ctx=has_ctx),
        out_shape=jax.ShapeDtypeStruct(q_t.shape, BF16),
        grid=(batch, ATT_KV_HEADS, n_q),
        in_specs=in_specs,
        out_specs=q_spec,
        scratch_shapes=[
            pltpu.VMEM((ATT_RING, kc, width), F32),
            pltpu.VMEM((ATT_RING, kc, width), BF16),
            pltpu.VMEM((ATT_V_ROWS, width), F32),
            pltpu.VMEM((N_ST, 1, width), F32),
            pltpu.VMEM((ATT_RING, 1, width), F32),
        ],
        compiler_params=_params(("parallel", "parallel", "parallel")),
    )(*args)


def _s5_in_kernel(x_ref, mod_ref, g1_ref, w_ref, o_ref, h_ref):
    tm = x_ref.shape[0]
    n_chunk = o_ref.shape[2]
    n_col = D_MODEL // LANES

    def stage(i, carry):
        rows = pl.ds(pl.multiple_of(i * STAGE_ROWS, STAGE_ROWS), STAGE_ROWS)
        h = _modulated_norm(x_ref[rows, :], g1_ref[...], mod_ref)
        for j in range(n_col):
            h_ref[j, rows, :] = h[:, j * LANES:(j + 1) * LANES]
        return carry

    lax.fori_loop(0, tm // STAGE_ROWS, stage, 0)
    for s in range(S5_T):
        h = jnp.concatenate([h_ref[j, pl.ds(s, n_chunk, stride=S5_T), :] for j in range(n_col)], axis=1)
        o_ref[s] = _dot_nt(w_ref[...], h.astype(BF16)).astype(BF16)


def _s5_in(x, mod, mod_rows, g1, w_su_t):
    n = x.shape[0]
    tm = S5_TM
    mod_row = _mod_row_fn(mod_rows, tm)
    return pl.pallas_call(
        _s5_in_kernel,
        out_shape=jax.ShapeDtypeStruct((S5_T, SSM_CH, n // S5_T), BF16),
        grid=(n // tm,),
        in_specs=[
            pl.BlockSpec((tm, D_MODEL), lambda i: (i, 0)),
            pl.BlockSpec((1, N_MOD, D_MODEL), lambda i: (mod_row(i), 0, 0)),
            _const_spec((1, D_MODEL)),
            _const_spec((SSM_CH, D_MODEL)),
        ],
        out_specs=pl.BlockSpec((S5_T, SSM_CH, tm // S5_T), lambda i: (0, 0, i)),
        scratch_shapes=[pltpu.VMEM((D_MODEL // LANES, tm, LANES), F32)],
        compiler_params=_params(("parallel",)),
    )(x, mod, g1, w_su_t)


def _s5_kernel(*refs, two_pass):
    if two_pass:
        u_ref, tt_ref, wx_ref, wy_ref, a_ref, aseg_ref, s0_ref, y_ref, v_ref, init_ref = refs
    else:
        u_ref, tt_ref, wx_ref, wy_ref, a_ref, y_ref, fin_ref, v_ref = refs
    gb = tt_ref.shape[1]
    n_chunk = u_ref.shape[2] // S5_ROWS
    grp = SSM_GROUP

    def put_outputs(j, yt, accumulate):
        for t in range(S5_T):
            if accumulate:
                y_ref[t, j * grp:(j + 1) * grp, :] += yt[t * grp:(t + 1) * grp, :]
            else:
                y_ref[t, j * grp:(j + 1) * grp, :] = yt[t * grp:(t + 1) * grp, :]

    for j in range(gb):
        z = jnp.concatenate([u_ref[s, j * grp:(j + 1) * grp, :] for s in range(S5_T)], axis=0)
        put_outputs(j, _dot(tt_ref[0, j], z) + _dot(tt_ref[1, j], z), accumulate=False)
        for d in range(2):
            v_ref[d, j] = _dot_tn(z, wx_ref[d, j])

    def times(coef_ref, d, j, x):
        lanes = slice(j * S5_GL, (j + 1) * S5_GL)
        return coef_ref[d, 0:1, lanes] * x + coef_ref[d, 1:2, lanes] * pltpu.roll(x, SSM_STATE, axis=1)

    def scan(init, keep_states):
        xs = [list(init[0]), list(init[1])]
        for s in range(n_chunk):
            for d in range(2):
                c = s if d == 0 else n_chunk - 1 - s
                rows = pl.ds(c, S5_ROWS, stride=n_chunk)
                for j in range(gb):
                    v = v_ref[d, j, rows, :]
                    if keep_states:
                        v_ref[d, j, rows, :] = xs[d][j]
                    xs[d][j] = times(a_ref, d, j, xs[d][j]) + v
        return xs

    zero = [[jnp.zeros((S5_ROWS, S5_GL), F32)] * gb] * 2
    if two_pass:
        local = scan(zero, keep_states=False)
        n_batch = s0_ref.shape[1]
        n_seg = S5_ROWS // n_batch
        for d in range(2):
            order = range(n_seg) if d == 0 else range(n_seg - 1, -1, -1)
            for j in range(gb):
                for b in range(n_batch):
                    cur = s0_ref[d, b:b + 1, j * S5_GL:(j + 1) * S5_GL]
                    for sgm in order:
                        r = b * n_seg + sgm
                        init_ref[d, j, r:r + 1, :] = cur
                        cur = times(aseg_ref, d, j, cur) + local[d][j][r:r + 1, :]
        scan([[init_ref[d, j] for j in range(gb)] for d in range(2)], keep_states=True)
    else:
        final = scan(zero, keep_states=True)
        for d in range(2):
            for j in range(gb):
                fin_ref[d, :, j * S5_GL:(j + 1) * S5_GL] = final[d][j]
    for j in range(gb):
        put_outputs(j, _dot_nt(wy_ref[0, j], v_ref[0, j].astype(BF16))
                    + _dot_nt(wy_ref[1, j], v_ref[1, j].astype(BF16)), accumulate=True)


def _s5(u, tt, wx, wy, a_t, a_seg, s0):
    two_pass = s0 is not None
    n_rows = u.shape[2]
    gb = S5_GB
    io_spec = pl.BlockSpec((S5_T, gb * SSM_GROUP, n_rows), lambda i: (0, i, 0))
    coef_spec = pl.BlockSpec((2, 2, gb * S5_GL), lambda i: (0, 0, i))
    in_specs = [
        io_spec,
        pl.BlockSpec((2, gb) + tt.shape[2:], lambda i: (0, i, 0, 0)),
        pl.BlockSpec((2, gb) + wx.shape[2:], lambda i: (0, i, 0, 0)),
        pl.BlockSpec((2, gb) + wy.shape[2:], lambda i: (0, i, 0, 0)),
        coef_spec,
    ]
    args = [u, tt, wx, wy, a_t]
    y_shape = jax.ShapeDtypeStruct(u.shape, F32)
    scratch = [pltpu.VMEM((2, gb, n_rows, S5_GL), F32)]
    if two_pass:
        in_specs += [coef_spec, pl.BlockSpec((2, s0.shape[1], gb * S5_GL), lambda i: (0, 0, i))]
        args += [a_seg, s0]
        out_shape, out_specs = y_shape, io_spec
        scratch.append(pltpu.VMEM((2, gb, S5_ROWS, S5_GL), F32))
    else:
        out_shape = (y_shape, jax.ShapeDtypeStruct((2, S5_ROWS, SSM_GROUPS * S5_GL), F32))
        out_specs = (io_spec, pl.BlockSpec((2, S5_ROWS, gb * S5_GL), lambda i: (0, 0, i)))
    return pl.pallas_call(
        functools.partial(_s5_kernel, two_pass=two_pass),
        out_shape=out_shape,
        grid=(SSM_GROUPS // gb,),
        in_specs=in_specs,
        out_specs=out_specs,
        scratch_shapes=scratch,
        compiler_params=_params(("parallel",)),
    )(*args)


def _s5_out_kernel(y_ref, u_ref, d_ref, w_ref, o_ref, buf_ref):
    n_chunk = y_ref.shape[2]
    n_col = D_MODEL // LANES
    for s in range(S5_T):
        y = y_ref[s] + d_ref[...] * u_ref[s].astype(F32)
        glu = _dot_tn(jax.nn.gelu(y).astype(BF16), w_ref[...])
        out = glu[:, 0:D_MODEL] * jax.nn.sigmoid(glu[:, D_MODEL:2 * D_MODEL])
        for j in range(n_col):
            buf_ref[j, pl.ds(s, n_chunk, stride=S5_T), :] = out[:, j * LANES:(j + 1) * LANES]
    for j in range(n_col):
        o_ref[:, j * LANES:(j + 1) * LANES] = buf_ref[j].astype(BF16)


def _s5_out(y, u, d_skip, w_glu):
    n = y.shape[2] * S5_T
    tm = S5_TM
    io_spec = pl.BlockSpec((S5_T, SSM_CH, tm // S5_T), lambda i: (0, 0, i))
    return pl.pallas_call(
        _s5_out_kernel,
        out_shape=jax.ShapeDtypeStruct((n, D_MODEL), BF16),
        grid=(n // tm,),
        in_specs=[io_spec, io_spec, _const_spec((SSM_CH, 1)), _const_spec((SSM_CH, 2 * D_MODEL))],
        out_specs=pl.BlockSpec((tm, D_MODEL), lambda i: (i, 0)),
        scratch_shapes=[pltpu.VMEM((D_MODEL // LANES, tm, LANES), F32)],
        compiler_params=_params(("parallel",)),
    )(y, u, d_skip, w_glu)


def _route(probs):
    epg = EXPERTS_PER_GROUP
    groups = [probs[g * epg:(g + 1) * epg] for g in range(N_EXPERT_GROUPS)]
    scores = []
    for grp in groups:
        best = None
        for a in range(epg):
            for b in range(a + 1, epg):
                pair = grp[a] + grp[b]
                best = pair if best is None else jnp.maximum(best, pair)
        scores.append(best)
    top_score = scores[0]
    top_group = jnp.zeros_like(scores[0], dtype=jnp.int32)
    for g in range(1, N_EXPERT_GROUPS):
        better = scores[g] > top_score
        top_score = jnp.where(better, scores[g], top_score)
        top_group = jnp.where(better, g, top_group)
    sel = []
    for j in range(epg):
        v = groups[0][j]
        for g in range(1, N_EXPERT_GROUPS):
            v = jnp.where(top_group == g, groups[g][j], v)
        sel.append(v)
    v1 = sel[0]
    i1 = jnp.zeros_like(top_group)
    for j in range(1, epg):
        better = sel[j] > v1
        v1 = jnp.where(better, sel[j], v1)
        i1 = jnp.where(better, j, i1)
    v2 = jnp.full_like(v1, -1.0)
    i2 = jnp.zeros_like(top_group)
    for j in range(epg):
        better = (sel[j] > v2) & (i1 != j)
        v2 = jnp.where(better, sel[j], v2)
        i2 = jnp.where(better, j, i2)
    total = v1 + v2
    w1 = v1 / total
    w2 = v2 / total
    return top_group, [jnp.where(i1 == j, w1, jnp.where(i2 == j, w2, 0.0)) for j in range(epg)]


def _merge_kernel(x_ref, ro_ref, att_ref, ssm_ref, gates_ref, mod_ref, wro_ref, wao_ref, wout_ref, g2_ref, wrt_ref,
                  br_ref, x1_ref, h2_ref, meta_ref, cnt_ref):
    ret_branch = _dot(ro_ref[...], wro_ref[...])
    att_branch = _dot_tn(att_ref[...], wao_ref[...])

    def gate(j):
        return jax.nn.sigmoid(gates_ref[:, j * D_MODEL:(j + 1) * D_MODEL].astype(F32))

    merged = gate(0) * ret_branch + gate(1) * att_branch + gate(2) * ssm_ref[...].astype(F32)
    x1 = x_ref[...] + mod_ref[0, 2:3, :] * _dot(merged.astype(BF16), wout_ref[...])
    x1_ref[...] = x1
    xn = x1 * lax.rsqrt(jnp.mean(x1 * x1, axis=-1, keepdims=True) + EPS) * g2_ref[...]
    h2 = xn * (1.0 + mod_ref[0, 4:5, :]) + mod_ref[0, 3:4, :]
    h2_ref[...] = h2.astype(BF16)

    logits = lax.dot_general(wrt_ref[...], h2, (((1,), (1,)), ((), ())), precision=lax.Precision.HIGHEST,
                             preferred_element_type=F32) + br_ref[...]
    e = jnp.exp(logits - jnp.max(logits, axis=0, keepdims=True))
    probs = e / jnp.sum(e, axis=0, keepdims=True)
    top_group, weights = _route([probs[j:j + 1, :] for j in range(N_EXPERTS)])

    tm = top_group.shape[1]

    @pl.when(pl.program_id(0) % (MOE_PART // tm) == 0)
    def _():
        cnt_ref[...] = jnp.zeros_like(cnt_ref)

    zero_row = jnp.zeros((1, tm), F32)
    onehot = jnp.concatenate([(top_group == g).astype(F32) for g in range(N_EXPERT_GROUPS)]
                             + [zero_row] * (META_ROWS - N_EXPERT_GROUPS), axis=0)
    lane = lax.broadcasted_iota(jnp.int32, onehot.shape, 1)
    incl = onehot
    shift = 1
    while shift < tm:
        incl = incl + jnp.where(lane >= shift, pltpu.roll(incl, shift, axis=1), 0.0)
        shift *= 2
    before = cnt_ref[0, :, 0:1]
    rank = jnp.sum(onehot * (incl - onehot + before), axis=0, keepdims=True)
    cnt_ref[0] = jnp.broadcast_to(before + jnp.sum(onehot, axis=1, keepdims=True), cnt_ref.shape[1:])
    meta_ref[...] = jnp.concatenate([top_group.astype(F32), rank] + weights
                                    + [zero_row] * (META_ROWS - 2 - EXPERTS_PER_GROUP), axis=0)


def _merge(x, ro, att_t, ssm, gates, mod, mod_rows, wro, wao, wout, g2, wrt, br):
    n = x.shape[0]
    tm = TM_PROJ
    mod_row = _mod_row_fn(mod_rows, tm)
    row = lambda i: (i, 0)
    in_specs = [
        pl.BlockSpec((tm, D_MODEL), row),
        pl.BlockSpec((tm, RET_W), row),
        pl.BlockSpec((ATT_W, tm), lambda i: (0, i)),
        pl.BlockSpec((tm, D_MODEL), row),
        pl.BlockSpec((tm, 3 * D_MODEL), row),
        pl.BlockSpec((1, N_MOD, D_MODEL), lambda i: (mod_row(i), 0, 0)),
        _const_spec((RET_W, D_MODEL)),
        _const_spec((ATT_W, D_MODEL)),
        _const_spec((D_MODEL, D_MODEL)),
        _const_spec((1, D_MODEL)),
        _const_spec((N_EXPERTS, D_MODEL)),
        _const_spec((N_EXPERTS, 1)),
    ]
    return pl.pallas_call(
        _merge_kernel,
        out_shape=(
            jax.ShapeDtypeStruct((n, D_MODEL), F32),
            jax.ShapeDtypeStruct((n, D_MODEL), BF16),
            jax.ShapeDtypeStruct((META_ROWS, n), F32),
            jax.ShapeDtypeStruct((n // MOE_PART, META_ROWS, LANES), F32),
        ),
        grid=(n // tm,),
        in_specs=in_specs,
        out_specs=(
            pl.BlockSpec((tm, D_MODEL), row),
            pl.BlockSpec((tm, D_MODEL), row),
            pl.BlockSpec((META_ROWS, tm), lambda i: (0, i)),
            pl.BlockSpec((1, META_ROWS, LANES), lambda i: (i // (MOE_PART // tm), 0, 0)),
        ),
        compiler_params=_params(("arbitrary",)),
    )(x, ro, att_t, ssm, gates, mod, wro, wao, wout, g2, wrt, br)


def _moe_kernel(dest_ref, tile_group_ref, n_valid_ref, *refs, final, n_scatter, n_expert, n_gather):
    if final:
        (h2_ref, wrow_ref, wg_ref, wu_ref, wd_ref, x1_ref, mod_ref, fg_ref, o_ref,
         sorted_ref, wsort_ref, stage_ref, acc_ref) = refs
    else:
        (h2_ref, wrow_ref, wg_ref, wu_ref, wd_ref, x1_ref, mod_ref, o_ref,
         sorted_ref, wsort_ref, stage_ref, acc_ref) = refs
    tm = h2_ref.shape[0]
    epg = EXPERTS_PER_GROUP
    per_part = n_scatter + n_expert + n_gather
    part = pl.program_id(0) // per_part
    i = pl.program_id(0) % per_part

    @pl.when(i == 0)
    def _():
        sorted_ref[...] = jnp.zeros_like(sorted_ref)
        wsort_ref[...] = jnp.zeros_like(wsort_ref)

    @pl.when(i < n_scatter)
    def _():
        stage_ref[...] = h2_ref[...].astype(F32)
        base = (part * n_scatter + i) * tm

        def move(t, carry):
            d = dest_ref[base + t]
            sorted_ref[pl.ds(d, 1), :] = stage_ref[pl.ds(t, 1), :]
            wsort_ref[pl.ds(d, 1), :] = wrow_ref[pl.ds(t, 1), :]
            return carry

        lax.fori_loop(0, tm, move, 0, unroll=8)

    step = i - n_scatter
    tile = step // epg
    ein = step % epg

    @pl.when((step >= 0) & (step < n_expert) & (tile < n_valid_ref[part]))
    def _():
        rows = pl.ds(pl.multiple_of(tile * MOE_TILE, MOE_TILE), MOE_TILE)
        x = sorted_ref[rows, :].astype(BF16)
        g = _dot(x, wg_ref[0])
        u = _dot(x, wu_ref[0])
        w = wsort_ref[rows, :]
        lane = lax.broadcasted_iota(jnp.int32, w.shape, 1)
        w_col = jnp.sum(jnp.where(lane == ein, w, 0.0), axis=1, keepdims=True)
        out = _dot((g * jax.nn.sigmoid(g) * u * w_col).astype(BF16), wd_ref[0])

        @pl.when(ein == 0)
        def _():
            acc_ref[...] = out

        @pl.when(ein > 0)
        def _():
            acc_ref[...] += out

        @pl.when(ein == epg - 1)
        def _():
            sorted_ref[rows, :] = acc_ref[...]

    @pl.when(step >= n_expert)
    def _():
        base = (part * n_gather + step - n_expert) * tm

        def move(t, carry):
            stage_ref[pl.ds(t, 1), :] = sorted_ref[pl.ds(dest_ref[base + t], 1), :]
            return carry

        lax.fori_loop(0, tm, move, 0, unroll=8)
        x2 = x1_ref[...] + mod_ref[0, 5:6, :] * stage_ref[...]
        if final:
            x2 = x2 * lax.rsqrt(jnp.mean(x2 * x2, axis=-1, keepdims=True) + EPS) * fg_ref[...]
        o_ref[...] = x2


def _moe_plan(meta, cnt):
    n = meta.shape[1]
    n_part = cnt.shape[0]
    n_tiles = n // n_part // MOE_TILE + N_EXPERT_GROUPS
    counts = cnt[:, 0:N_EXPERT_GROUPS, 0].astype(jnp.int32)
    padded = (counts + MOE_TILE - 1) // MOE_TILE * MOE_TILE
    ends = jnp.cumsum(padded, axis=1)
    starts = ends - padded
    group = meta[0].astype(jnp.int32).reshape(n_part, -1)
    dest = meta[1].astype(jnp.int32).reshape(n_part, -1)
    for g in range(N_EXPERT_GROUPS):
        dest = dest + jnp.where(group == g, starts[:, g:g + 1], 0)
    tile_start = jnp.arange(n_tiles, dtype=jnp.int32) * MOE_TILE
    tile_group = jnp.sum(tile_start[None, :, None] >= ends[:, None, :], axis=2).astype(jnp.int32)
    tile_group = jnp.minimum(tile_group, N_EXPERT_GROUPS - 1)
    n_valid = (ends[:, N_EXPERT_GROUPS - 1] // MOE_TILE).astype(jnp.int32)
    w_rows = jnp.pad(meta[2:2 + EXPERTS_PER_GROUP].T, ((0, 0), (0, LANES - EXPERTS_PER_GROUP)))
    return dest.reshape(-1), tile_group, n_valid, w_rows


def _moe(h2, meta, cnt, x1, mod, mod_rows, wg, wu, wd, final_g):
    n = h2.shape[0]
    tm = MOE_TILE
    n_part = cnt.shape[0]
    dest, tile_group, n_valid, w_rows = _moe_plan(meta, cnt)
    n_tiles = tile_group.shape[1]
    epg = EXPERTS_PER_GROUP
    n_scatter, n_expert, n_gather = n // n_part // tm, n_tiles * epg, n // n_part // tm
    per_part = n_scatter + n_expert + n_gather
    mod_row = _mod_row_fn(mod_rows, tm)
    final = final_g is not None

    def scatter_tile(i):
        return (i // per_part) * n_scatter + jnp.minimum(i % per_part, n_scatter - 1)

    def gather_tile(i):
        return (i // per_part) * n_gather + jnp.clip(i % per_part - n_scatter - n_expert, 0, n_gather - 1)

    def expert(i, tg, nv):
        part = i // per_part
        step = jnp.clip(i % per_part - n_scatter, 0, n_expert - 1)
        tile = step // epg
        in_use = tile < nv[part]
        group = tg[part * n_tiles + jnp.minimum(tile, nv[part] - 1)]
        return group * epg + jnp.where(in_use, step % epg, epg - 1)

    in_specs = [
        pl.BlockSpec((tm, D_MODEL), lambda i, d, tg, nv: (scatter_tile(i), 0)),
        pl.BlockSpec((tm, LANES), lambda i, d, tg, nv: (scatter_tile(i), 0)),
        pl.BlockSpec((1, D_MODEL, D_FF), lambda i, d, tg, nv: (expert(i, tg, nv), 0, 0)),
        pl.BlockSpec((1, D_MODEL, D_FF), lambda i, d, tg, nv: (expert(i, tg, nv), 0, 0)),
        pl.BlockSpec((1, D_FF, D_MODEL), lambda i, d, tg, nv: (expert(i, tg, nv), 0, 0)),
        pl.BlockSpec((tm, D_MODEL), lambda i, d, tg, nv: (gather_tile(i), 0)),
        pl.BlockSpec((1, N_MOD, D_MODEL), lambda i, d, tg, nv: (mod_row(gather_tile(i)), 0, 0)),
    ]
    args = [h2, w_rows, wg, wu, wd, x1, mod]
    if final:
        in_specs.append(pl.BlockSpec((1, D_MODEL), lambda i, d, tg, nv: (0, 0)))
        args.append(final_g)
    grid_spec = pltpu.PrefetchScalarGridSpec(
        num_scalar_prefetch=3,
        grid=(n_part * per_part,),
        in_specs=in_specs,
        out_specs=pl.BlockSpec((tm, D_MODEL), lambda i, d, tg, nv: (gather_tile(i), 0)),
        scratch_shapes=[
            pltpu.VMEM((n_tiles * MOE_TILE, D_MODEL), F32),
            pltpu.VMEM((n_tiles * MOE_TILE, LANES), F32),
            pltpu.VMEM((tm, D_MODEL), F32),
            pltpu.VMEM((MOE_TILE, D_MODEL), F32),
        ],
    )
    return pl.pallas_call(
        functools.partial(_moe_kernel, final=final, n_scatter=n_scatter, n_expert=n_expert, n_gather=n_gather),
        out_shape=jax.ShapeDtypeStruct((n, D_MODEL), F32),
        grid_spec=grid_spec,
        compiler_params=_params(("arbitrary",)),
    )(dest, tile_group.reshape(-1), n_valid, *args)


def _block_diag(m):
    g, r, c = m.shape
    eye = jnp.eye(g, dtype=m.dtype)
    return (eye[:, None, :, None] * m[:, :, None, :]).reshape(g * r, g * c)


def _s5_direction_params(a_re, a_im, log_dt, b_re, b_im, c_re, c_im, reverse):
    n_g = a_re.shape[0]
    width = S5_T * SSM_GROUP
    hp = lax.Precision.HIGHEST
    step = jnp.exp(log_dt)[:, None]
    mag = jnp.exp(a_re * step)
    ar = mag * jnp.cos(a_im * step)
    ai = mag * jnp.sin(a_im * step)
    den = a_re * a_re + a_im * a_im
    fr = ((ar - 1.0) * a_re + ai * a_im) / den
    fi = (ai * a_re - (ar - 1.0) * a_im) / den
    bbr = (fr[:, :, None] * b_re - fi[:, :, None] * b_im).transpose(0, 2, 1)
    bbi = (fr[:, :, None] * b_im + fi[:, :, None] * b_re).transpose(0, 2, 1)
    pr, pi = [jnp.ones_like(ar)], [jnp.zeros_like(ar)]
    for _ in range(S5_T):
        r, i = pr[-1], pi[-1]
        pr.append(r * ar - i * ai)
        pi.append(r * ai + i * ar)
    pr, pi = jnp.stack(pr)[:, :, None, :], jnp.stack(pi)[:, :, None, :]
    mr = pr * bbr - pi * bbi
    mi = pr * bbi + pi * bbr
    cr = c_re * pr - c_im * pi
    ci = c_re * pi + c_im * pr
    kern = (jnp.einsum('gop,kgip->goki', c_re, mr[:S5_T], precision=hp)
            - jnp.einsum('gop,kgip->goki', c_im, mi[:S5_T], precision=hp))
    pad = jnp.zeros_like(kern[:, :, 1:, :])
    strip = jnp.concatenate([pad, kern] if reverse else [kern[:, :, ::-1, :], pad], axis=2)
    tt = jnp.stack([strip[:, :, S5_T - 1 - t:2 * S5_T - 1 - t, :] for t in range(S5_T)], axis=1)
    tt = tt.reshape(n_g, width, width)
    wx = jnp.concatenate([mr[:S5_T], mi[:S5_T]], axis=3)
    wx = (wx if reverse else wx[::-1]).transpose(1, 0, 2, 3).reshape(n_g, width, S5_GL)
    wy = jnp.concatenate([cr[1:], -ci[1:]], axis=3)
    wy = (wy[::-1] if reverse else wy).transpose(1, 0, 2, 3).reshape(n_g, width, S5_GL)

    def coef_rows(r, i):
        return jnp.stack([jnp.concatenate([r, r], axis=1).reshape(-1), jnp.concatenate([-i, i], axis=1).reshape(-1)])

    sr, si = pr[S5_T, :, 0, :], pi[S5_T, :, 0, :]
    a_t = coef_rows(sr, si)
    for _ in range(int(math.log2(S5_SEG // S5_T))):
        sr, si = sr * sr - si * si, 2.0 * sr * si
    return tt.astype(BF16), wx.astype(BF16), wy.astype(BF16), a_t, coef_rows(sr, si)


def _rope_tables(n_tokens):
    rows = n_tokens // GRID_W
    row = jnp.repeat(jnp.arange(rows, dtype=F32), GRID_W)
    col = jnp.tile(jnp.arange(GRID_W, dtype=F32), rows)
    n_freq = ATT_HEAD_DIM // 4
    inv = ROPE_THETA ** (-jnp.arange(n_freq, dtype=F32) / n_freq)
    ang = jnp.concatenate([row[:, None] * inv, col[:, None] * inv], axis=-1)
    cos = jnp.repeat(jnp.cos(ang), 2, axis=-1)
    sin = jnp.repeat(jnp.sin(ang), 2, axis=-1) * jnp.tile(jnp.array([-1.0, 1.0], F32), ATT_HEAD_DIM // 2)
    return cos, sin


def _swap_pairs(a, axis=-1):
    axis = axis % a.ndim
    pairs = a.reshape(a.shape[:axis] + (a.shape[axis] // 2, 2) + a.shape[axis + 1:])
    return jnp.flip(pairs, axis=axis + 1).reshape(a.shape)


def _layer(x, batch, mod, mod_rows, p, ctx):
    n = x.shape[0]
    seq = n // batch
    latent = ctx is not None
    outs = _inproj(x, mod, mod_rows, p, latent)
    rqkv, rg, gates, q_t, k16, v_t = outs[:6]

    ro, ret_state = _retention(rqkv.reshape(batch, seq, 3 * RET_W), rg.reshape(batch, seq, RET_W), p['ret_lg'],
                               p['ret_gn'], ctx[2] if latent else None, bb=1 if latent else 8)

    if latent:
        k_ctx = ctx[0].reshape(batch * ATT_KC, ATT_KV_W).astype(BF16)
        v_ctx_t = ctx[1].reshape(batch, ATT_KC, ATT_KV_W).transpose(0, 2, 1).reshape(batch * ATT_KV_W, ATT_KC)
        att_t = _attention(q_t, k16, v_t, k_ctx, v_ctx_t.astype(BF16), batch)
    else:
        att_t = _attention(q_t, k16, v_t, None, None, batch)

    u = _s5_in(x, mod, mod_rows, p['g1'], p['w_su_t'])
    if latent:
        s0 = jnp.stack([ctx[3], ctx[4]], axis=3).transpose(1, 0, 2, 3, 4).reshape(2, batch, SSM_GROUPS * S5_GL)
        y = _s5(u, p['s5_tt'], p['s5_wx'], p['s5_wy'], p['s5_a'], p['s5_aseg'], s0)
        ssm_state = None
    else:
        y, ssm_state = _s5(u, p['s5_tt'], p['s5_wx'], p['s5_wy'], p['s5_a'], None, None)
        ssm_state = ssm_state.reshape(2, batch, SSM_GROUPS, 2, SSM_STATE)
    ssm = _s5_out(y, u, p['ssm_d'], p['w_glu'])

    x1, h2, meta, cnt = _merge(x, ro.reshape(n, RET_W), att_t, ssm, gates, mod, mod_rows, p['w_ret_out'],
                               p['w_att_out'], p['w_out'], p['g2'], p['w_router_t'], p['b_router'])
    x2 = _moe(h2, meta, cnt, x1, mod, mod_rows, p['w_gate'], p['w_up'], p['w_down'], p.get('final_g'))
    if latent:
        return x2, None
    return x2, (outs[6], outs[7], ret_state, ssm_state)


def kernel(x_prompt, x_sample, c, cache_attn_k, cache_attn_v, state_ret, state_ssm_re, state_ssm_im, c_ctx, w_ada,
           b_ada, norm1_g, norm2_g, w_in, ret_lg_f, ret_lg_b, ret_norm_g, w_ret_out, att_q_norm_g, att_k_norm_g,
           w_att_out, a_re_f, a_im_f, a_re_b, a_im_b, log_dt_f, log_dt_b, ssm_b_re, ssm_b_im, ssm_c_re, ssm_c_im,
           ssm_d, w_glu, w_out, w_router, b_router, w_gate, w_up, w_down, final_norm_g):
    batch, seq, _ = x_prompt.shape
    dec_batch, dec_seq, _ = x_sample.shape
    assert batch * seq == S5_ROWS * S5_SEG and dec_batch * dec_seq == S5_ROWS * S5_SEG
    assert seq == S5_SEG and dec_seq % S5_SEG == 0
    assert cache_attn_k.shape[2] == ATT_KC and seq % ATT_QB == 0

    c_rows = jnp.zeros((ADA_ROWS, D_MODEL), F32).at[0].set(c_ctx).at[1:1 + dec_batch].set(c)
    mod_all = _ada(c_rows, w_ada, b_ada).reshape(DEPTH, ADA_ROWS, N_MOD, D_MODEL)

    cos, sin = _rope_tables(dec_seq)
    ones = _block_diag(jnp.ones((ATT_KV_HEADS, ATT_HEAD_DIM, ATT_HEAD_DIM), BF16))

    xp = x_prompt.reshape(batch * seq, D_MODEL)
    xs = x_sample.reshape(dec_batch * dec_seq, D_MODEL)
    ks_, vs_, rets_, ssms_ = [], [], [], []
    for l in range(DEPTH):
        w_l = w_in[l]
        gq = att_q_norm_g[l][:, None]
        gk = jnp.tile(att_k_norm_g[l], ATT_KV_HEADS)[None]
        wq_t = w_l[:, C_AQ:C_AK].T
        fwd = _s5_direction_params(a_re_f[l], a_im_f[l], log_dt_f[l], ssm_b_re[l], ssm_b_im[l], ssm_c_re[l],
                                   ssm_c_im[l], reverse=False)
        bwd = _s5_direction_params(a_re_b[l], a_im_b[l], log_dt_b[l], ssm_b_re[l], ssm_b_im[l], ssm_c_re[l],
                                   ssm_c_im[l], reverse=True)
        p = {
            'g1': norm1_g[l][None], 'g2': norm2_g[l][None], 'w_in': w_l.astype(BF16), 'ones': ones,
            'wq_t': wq_t.astype(BF16), 'wv_t': w_l[:, C_AV:C_SU].T.astype(BF16), 'gq': gq, 'gk': gk,
            'wqs_t': _swap_pairs(wq_t, axis=0).astype(BF16), 'wk_sw': _swap_pairs(w_l[:, C_AK:C_AV]).astype(BF16),
            'gqs': _swap_pairs(gq, axis=0), 'gks': _swap_pairs(gk),
            'cos': jnp.tile(cos, (1, ATT_KV_HEADS)), 'sin': jnp.tile(sin, (1, ATT_KV_HEADS)),
            'cos_t': cos.T, 'sin_t': sin.T,
            'ret_lg': jnp.stack([ret_lg_f[l], ret_lg_b[l]]), 'ret_gn': ret_norm_g[l][None],
            'w_ret_out': w_ret_out[l].astype(BF16), 'w_att_out': w_att_out[l].astype(BF16),
            'w_su_t': w_l[:, C_SU:C_G].T.astype(BF16),
            's5_tt': jnp.stack([fwd[0], bwd[0]]), 's5_wx': jnp.stack([fwd[1], bwd[1]]),
            's5_wy': jnp.stack([fwd[2], bwd[2]]), 's5_a': jnp.stack([fwd[3], bwd[3]]),
            's5_aseg': jnp.stack([fwd[4], bwd[4]]),
            'ssm_d': ssm_d[l][:, None], 'w_glu': w_glu[l].astype(BF16), 'w_out': w_out[l].astype(BF16),
            'w_router_t': w_router.T, 'b_router': b_router[:, None],
            'w_gate': w_gate[l].astype(BF16), 'w_up': w_up[l].astype(BF16), 'w_down': w_down[l].astype(BF16),
        }
        if l == DEPTH - 1:
            p['final_g'] = final_norm_g[None]
        mod = mod_all[l]

        xp, (k_c, v_c, st_r, st_s) = _layer(xp, batch, mod, (0, batch * seq), p, None)
        ks_.append(k_c.reshape(batch, seq, ATT_KV_HEADS, ATT_HEAD_DIM))
        vs_.append(v_c.reshape(batch, seq, ATT_KV_HEADS, ATT_HEAD_DIM))
        rets_.append(st_r)
        ssms_.append(st_s)

        ctx = (cache_attn_k[:, l], cache_attn_v[:, l], state_ret[:, l], state_ssm_re[:, l], state_ssm_im[:, l])
        xs, _ = _layer(xs, dec_batch, mod, (1, dec_seq), p, ctx)

    ssm_all = jnp.stack(ssms_, axis=1)
    ssm_all = ssm_all.transpose(2, 1, 0, 3, 4, 5)
    new_re = ssm_all[..., 0, :]
    new_im = ssm_all[..., 1, :]
    return (xp.reshape(batch, seq, D_MODEL), xs.reshape(dec_batch, dec_seq, D_MODEL),
            jnp.stack(ks_, axis=1), jnp.stack(vs_, axis=1), jnp.stack(rets_, axis=1), new_re, new_im)
```

```python
import functools
import math

import jax
import jax.numpy as jnp
from jax import lax
from jax.experimental import pallas as pl
from jax.experimental.pallas import tpu as pltpu

F32 = jnp.float32
BF16 = jnp.bfloat16

D_MODEL = 1024
DEPTH = 2
GRID_W = 64
RET_HEADS = 4
RET_DK = 128
RET_DV = 128
RET_CHUNK = 128
RET_W = RET_HEADS * RET_DV
ATT_HEADS = 8
ATT_KV_HEADS = 2
ATT_GROUP = ATT_HEADS // ATT_KV_HEADS
ATT_HEAD_DIM = 64
ATT_W = ATT_HEADS * ATT_HEAD_DIM
ATT_KV_W = ATT_KV_HEADS * ATT_HEAD_DIM
ROPE_THETA = 10000.0
SSM_CH = 512
SSM_GROUP = 16
SSM_GROUPS = SSM_CH // SSM_GROUP
SSM_STATE = 64
N_EXPERTS = 16
N_EXPERT_GROUPS = 4
EXPERTS_PER_GROUP = N_EXPERTS // N_EXPERT_GROUPS
D_FF = 512
N_MOD = 6
EPS = 1e-6

C_RQ, C_RK, C_RV, C_RG = 0, 512, 1024, 1536
C_AQ, C_AK, C_AV, C_SU, C_G, C_END = 2048, 2560, 2688, 2816, 3328, 6400

VMEM_LIMIT = 52 * 1024 * 1024

ATT_Q_SCALE = ATT_HEAD_DIM ** -0.5 * math.log2(math.e)
ATT_QB = 256
ATT_KC = 256
ATT_V_ROWS = ATT_HEAD_DIM + 16
ATT_RING = 4

TM_PROJ = 512
RET_UNROLL = 4
MOE_TILE = 512
MOE_PART = 4096
META_ROWS = 8
S5_ROWS = 32
S5_SEG = 256
S5_T = 16
S5_GL = 2 * SSM_STATE
S5_GB = 8
S5_TM = 2048
LANES = 128
STAGE_ROWS = 256
ADA_ROWS = 8
ADA_TN = 1536


def _dot(a, b):
    return jnp.dot(a, b, preferred_element_type=F32)


def _dot_nt(a, b):
    return lax.dot_general(a, b, (((1,), (1,)), ((), ())), preferred_element_type=F32)


def _dot_tn(a, b):
    return lax.dot_general(a, b, (((0,), (0,)), ((), ())), preferred_element_type=F32)


def _const_spec(shape):
    n = len(shape)
    return pl.BlockSpec(shape, lambda *_: (0,) * n, pipeline_mode=pl.Buffered(1))


def _params(sem, vmem=VMEM_LIMIT):
    return pltpu.CompilerParams(dimension_semantics=sem, vmem_limit_bytes=vmem)


def _mod_row_fn(mod_rows, tm):
    base, per_row = mod_rows
    return lambda i: base + (i * tm) // per_row


def _modulated_norm(x, g, mod_ref):
    xn = x * lax.rsqrt(jnp.mean(x * x, axis=-1, keepdims=True) + EPS) * g
    return xn * (1.0 + mod_ref[0, 1:2, :]) + mod_ref[0, 0:1, :]


def _ada_kernel(c_ref, w_ref, b_ref, o_ref):
    c = c_ref[...]
    a = (c * jax.nn.sigmoid(c)).astype(BF16)
    o_ref[0] = _dot(a, w_ref[0].astype(BF16)) + b_ref[0]


def _ada(c_rows, w_ada, b_ada):
    n_col = N_MOD * D_MODEL
    return pl.pallas_call(
        _ada_kernel,
        out_shape=jax.ShapeDtypeStruct((DEPTH, ADA_ROWS, n_col), F32),
        grid=(DEPTH, n_col // ADA_TN),
        in_specs=[
            pl.BlockSpec((ADA_ROWS, D_MODEL), lambda l, j: (0, 0)),
            pl.BlockSpec((1, D_MODEL, ADA_TN), lambda l, j: (l, 0, j)),
            pl.BlockSpec((1, 1, ADA_TN), lambda l, j: (l, 0, j)),
        ],
        out_specs=pl.BlockSpec((1, ADA_ROWS, ADA_TN), lambda l, j: (l, 0, j)),
        compiler_params=_params(("arbitrary", "arbitrary")),
    )(c_rows, w_ada, b_ada.reshape(DEPTH, 1, n_col))


def _head_mean_sq(z, ones):
    z2 = z * z
    hi = z2.astype(BF16)
    lo = (z2 - hi.astype(F32)).astype(BF16)
    return (_dot(hi, ones) + _dot(lo, ones)) * (1.0 / ATT_HEAD_DIM)


def _inproj_kernel(*refs, latent):
    if latent:
        (x_ref, mod_ref, g1_ref, w_ref, wqt_ref, wvt_ref, ones_ref, gq_ref, gk_ref,
         wqst_ref, wks_ref, gqs_ref, gks_ref, cos_ref, sin_ref, cost_ref, sint_ref,
         rqkv_ref, rg_ref, gates_ref, qt_ref, k16_ref, vt_ref) = refs
    else:
        (x_ref, mod_ref, g1_ref, w_ref, wqt_ref, wvt_ref, ones_ref, gq_ref, gk_ref,
         rqkv_ref, rg_ref, gates_ref, qt_ref, k16_ref, vt_ref, ak_ref, av_ref) = refs
    tm = x_ref.shape[0]
    h = _modulated_norm(x_ref[...], g1_ref[...], mod_ref).astype(BF16)

    def seg(a, b):
        return _dot(h, w_ref[0, :, a:b])

    rqkv_ref[:, 0:RET_W] = (seg(C_RQ, C_RK) * (RET_DK ** -0.5)).astype(BF16)
    rqkv_ref[:, RET_W:3 * RET_W] = seg(C_RK, C_RG).astype(BF16)
    rg_ref[...] = seg(C_RG, C_AQ).astype(BF16)
    for j in range(3):
        gates_ref[:, j * D_MODEL:(j + 1) * D_MODEL] = seg(C_G + j * D_MODEL, C_G + (j + 1) * D_MODEL).astype(BF16)

    zk = seg(C_AK, C_AV)
    inv_k = lax.rsqrt(_head_mean_sq(zk, ones_ref[...]) + EPS)
    yk = zk * inv_k * gk_ref[...]
    zq = _dot_nt(wqt_ref[...], h).reshape(ATT_HEADS, ATT_HEAD_DIM, tm)
    inv_q = lax.rsqrt(jnp.mean(zq * zq, axis=1, keepdims=True) + EPS)
    yq = zq * inv_q * gq_ref[...]
    if latent:
        yk_sw = _dot(h, wks_ref[...]) * inv_k * gks_ref[...]
        yk = yk * cos_ref[...] + yk_sw * sin_ref[...]
        zq_sw = _dot_nt(wqst_ref[...], h).reshape(ATT_HEADS, ATT_HEAD_DIM, tm)
        yq = yq * cost_ref[...] + (zq_sw * inv_q * gqs_ref[...]) * sint_ref[...]
    qt_ref[...] = (yq * ATT_Q_SCALE).reshape(ATT_W, tm).astype(BF16)
    k16_ref[...] = yk.astype(BF16)
    vt = _dot_nt(wvt_ref[...], h).astype(BF16)
    for c in range(tm // ATT_KC):
        vt_ref[c] = vt[:, c * ATT_KC:(c + 1) * ATT_KC]
    if not latent:
        ak_ref[...] = yk
        av_ref[...] = seg(C_AV, C_SU)


def _inproj(x, mod, mod_rows, p, latent):
    n = x.shape[0]
    tm = TM_PROJ
    mod_row = _mod_row_fn(mod_rows, tm)
    row = lambda i: (i, 0)
    in_specs = [
        pl.BlockSpec((tm, D_MODEL), row),
        pl.BlockSpec((1, N_MOD, D_MODEL), lambda i: (mod_row(i), 0, 0)),
        _const_spec((1, D_MODEL)),
        pl.BlockSpec((1, D_MODEL, C_END), lambda i: (p['layer'], 0, 0), pipeline_mode=pl.Buffered(1)),
        _const_spec((ATT_W, D_MODEL)),
        _const_spec((ATT_KV_W, D_MODEL)),
        _const_spec((ATT_KV_W, ATT_KV_W)),
        _const_spec((ATT_HEAD_DIM, 1)),
        _const_spec((1, ATT_KV_W)),
    ]
    args = [x, mod, p['g1'], p['w_in'], p['wq_t'], p['wv_t'], p['ones'], p['gq'], p['gk']]
    if latent:
        n_pos = p['cos'].shape[0] // tm
        in_specs += [
            _const_spec((ATT_W, D_MODEL)),
            _const_spec((D_MODEL, ATT_KV_W)),
            _const_spec((ATT_HEAD_DIM, 1)),
            _const_spec((1, ATT_KV_W)),
            pl.BlockSpec((tm, ATT_KV_W), lambda i: (i % n_pos, 0)),
            pl.BlockSpec((tm, ATT_KV_W), lambda i: (i % n_pos, 0)),
            pl.BlockSpec((ATT_HEAD_DIM, tm), lambda i: (0, i % n_pos)),
            pl.BlockSpec((ATT_HEAD_DIM, tm), lambda i: (0, i % n_pos)),
        ]
        args += [p['wqs_t'], p['wk_sw'], p['gqs'], p['gks'], p['cos'], p['sin'], p['cos_t'], p['sin_t']]
    out_shape = [
        jax.ShapeDtypeStruct((n, 3 * RET_W), BF16),
        jax.ShapeDtypeStruct((n, RET_W), BF16),
        jax.ShapeDtypeStruct((n, 3 * D_MODEL), BF16),
        jax.ShapeDtypeStruct((ATT_W, n), BF16),
        jax.ShapeDtypeStruct((n, ATT_KV_W), BF16),
        jax.ShapeDtypeStruct((n // ATT_KC, ATT_KV_W, ATT_KC), BF16),
    ]
    out_specs = [
        pl.BlockSpec((tm, 3 * RET_W), row),
        pl.BlockSpec((tm, RET_W), row),
        pl.BlockSpec((tm, 3 * D_MODEL), row),
        pl.BlockSpec((ATT_W, tm), lambda i: (0, i)),
        pl.BlockSpec((tm, ATT_KV_W), row),
        pl.BlockSpec((tm // ATT_KC, ATT_KV_W, ATT_KC), lambda i: (i, 0, 0)),
    ]
    if not latent:
        out_shape += [jax.ShapeDtypeStruct((n, ATT_KV_W), F32)] * 2
        out_specs += [pl.BlockSpec((tm, ATT_KV_W), row)] * 2
    return pl.pallas_call(
        functools.partial(_inproj_kernel, latent=latent),
        out_shape=tuple(out_shape),
        grid=(n // tm,),
        in_specs=in_specs,
        out_specs=tuple(out_specs),
        compiler_params=_params(("parallel",)),
    )(*args)


T_DEC_F, T_DEC_B, T_XI_F, T_ZETA_F, T_XI_B, T_ZETA_B, T_CD_F, T_CD_B, N_TAB = range(9)


def _ret_kernel(*refs, has_s0, bb, seq):
    if has_s0:
        lg_ref, q_ref, k_ref, v_ref, rg_ref, gn_ref, s0_ref, o_ref, st_ref, tab_ref, acc_ref = refs
    else:
        lg_ref, q_ref, k_ref, v_ref, rg_ref, gn_ref, o_ref, st_ref, tab_ref, acc_ref = refs
    c = RET_CHUNK
    head = pl.program_id(1)
    lgf = lg_ref[0, head]
    lgb = lg_ref[1, head]
    t = lax.broadcasted_iota(jnp.int32, (c, c), 0).astype(F32)
    s = lax.broadcasted_iota(jnp.int32, (c, c), 1).astype(F32)
    tab_ref[T_DEC_F] = jnp.where(t >= s, jnp.exp(lgf * jnp.maximum(t - s, 0.0)), 0.0)
    tab_ref[T_DEC_B] = jnp.where(s >= t, jnp.exp(lgb * jnp.maximum(s - t, 0.0)), 0.0)
    tab_ref[T_XI_F] = jnp.exp(lgf * (t + 1.0))
    tab_ref[T_ZETA_F] = jnp.exp(lgf * (c - 1.0 - t))
    tab_ref[T_XI_B] = jnp.exp(lgb * (c - t))
    tab_ref[T_ZETA_B] = jnp.exp(lgb * t)
    tab_ref[T_CD_F] = jnp.exp(lgf * (c + 0.0 * t))
    tab_ref[T_CD_B] = jnp.exp(lgb * (c + 0.0 * t))
    if has_s0:
        st_ref[...] = s0_ref[...]
    else:
        st_ref[...] = jnp.zeros_like(st_ref)
    acc_ref[...] = jnp.zeros_like(acc_ref)
    n_chunk = seq // c

    def one_direction(rb, d, off, t_dec, t_xi, t_zeta, t_cd):
        rows = pl.ds(off, c)
        q = q_ref[rb, rows, :]
        k = k_ref[rb, rows, :]
        v = v_ref[rb, rows, :]
        p = (_dot_nt(q, k) * tab_ref[t_dec]).astype(BF16)
        st = st_ref[rb, d, 0]
        q_xi = (q.astype(F32) * tab_ref[t_xi]).astype(BF16)
        o = _dot(jnp.concatenate([p, q_xi], axis=1), jnp.concatenate([v, st.astype(BF16)], axis=0))
        acc_ref[rb, rows, :] += o
        kz = (k.astype(F32) * tab_ref[t_zeta]).astype(BF16)
        st_ref[rb, d, 0] = tab_ref[t_cd] * st + _dot_tn(kz, v)

    def body(i, carry):
        off_f = pl.multiple_of(i * c, c)
        off_b = pl.multiple_of((n_chunk - 1 - i) * c, c)
        for rb in range(bb):
            one_direction(rb, 0, off_f, T_DEC_F, T_XI_F, T_ZETA_F, T_CD_F)
            one_direction(rb, 1, off_b, T_DEC_B, T_XI_B, T_ZETA_B, T_CD_B)
        return carry

    lax.fori_loop(0, n_chunk, body, 0, unroll=RET_UNROLL if bb == 1 else 1)

    def finish(i, carry):
        rows = pl.ds(pl.multiple_of(i * c, c), c)
        for rb in range(bb):
            o = acc_ref[rb, rows, :]
            ro = o * lax.rsqrt(jnp.mean(o * o, axis=-1, keepdims=True) + EPS) * gn_ref[...]
            g = rg_ref[rb, rows, :].astype(F32)
            o_ref[rb, rows, :] = (ro * (g * jax.nn.sigmoid(g))).astype(BF16)
        return carry

    lax.fori_loop(0, n_chunk, finish, 0)


def _retention(rqkv, rg, lg, gn, s0, bb):
    b, seq, _ = rqkv.shape
    has_s0 = s0 is not None
    blk = (bb, seq, RET_DK)
    st_spec = pl.BlockSpec((bb, 2, 1, RET_DK, RET_DV), lambda i, h: (i, 0, h, 0, 0))
    in_specs = [
        pl.BlockSpec(memory_space=pltpu.SMEM),
        pl.BlockSpec(blk, lambda i, h: (i, 0, h)),
        pl.BlockSpec(blk, lambda i, h: (i, 0, RET_HEADS + h)),
        pl.BlockSpec(blk, lambda i, h: (i, 0, 2 * RET_HEADS + h)),
        pl.BlockSpec(blk, lambda i, h: (i, 0, h)),
        pl.BlockSpec((1, RET_DV), lambda i, h: (0, h)),
    ]
    args = [lg, rqkv, rqkv, rqkv, rg, gn]
    if has_s0:
        in_specs.append(st_spec)
        args.append(s0)
    return pl.pallas_call(
        functools.partial(_ret_kernel, has_s0=has_s0, bb=bb, seq=seq),
        out_shape=(
            jax.ShapeDtypeStruct((b, seq, RET_W), BF16),
            jax.ShapeDtypeStruct((b, 2, RET_HEADS, RET_DK, RET_DV), F32),
        ),
        grid=(b // bb, RET_HEADS),
        in_specs=in_specs,
        out_specs=(pl.BlockSpec(blk, lambda i, h: (i, 0, h)), st_spec),
        scratch_shapes=[
            pltpu.VMEM((N_TAB, RET_CHUNK, RET_CHUNK), F32),
            pltpu.VMEM((bb, seq, RET_DV), F32),
        ],
        compiler_params=_params(("parallel", "parallel")),
    )(*args)


ST_M, ST_ALPHA, N_ST = range(3)


def _attn_kernel(*refs, has_ctx):
    if has_ctx:
        q_ref, k_ref, v_ref, kc_ref, vc_ref, o_ref, s_ref, p_ref, acc_ref, st_ref, mx_ref = refs
    else:
        q_ref, k_ref, v_ref, o_ref, s_ref, p_ref, acc_ref, st_ref, mx_ref = refs
    kv_head = pl.program_id(1)
    qb = q_ref.shape[1]
    kc = s_ref.shape[1]
    n_own = k_ref.shape[0] // kc
    n_chunk = n_own + (1 if has_ctx else 0)
    q_t = jnp.concatenate([q_ref[h * ATT_HEAD_DIM:(h + 1) * ATT_HEAD_DIM, :] for h in range(ATT_GROUP)], axis=1)
    q_both = jnp.concatenate([q_t] * ATT_KV_HEADS, axis=0)
    row = lax.broadcasted_iota(jnp.int32, q_both.shape, 0)
    q_pad = jnp.where(row // ATT_HEAD_DIM == kv_head, q_both, jnp.zeros_like(q_both))
    ones_rows = (lax.broadcasted_iota(jnp.int32, (ATT_V_ROWS - ATT_HEAD_DIM, kc), 0) == 0).astype(BF16)

    def keys(j):
        if has_ctx and isinstance(j, int) and j == n_own:
            return kc_ref[...]
        return k_ref[pl.ds(pl.multiple_of(j * kc, kc), kc), :]

    def values(j):
        v = vc_ref[...] if has_ctx and isinstance(j, int) and j == n_own else v_ref[j]
        return jnp.concatenate([v, ones_rows], axis=0)

    def put_scores(j, slot):
        s = _dot(keys(j), q_pad)
        s_ref[slot] = s
        mx_ref[slot] = jnp.max(s, axis=0, keepdims=True)

    def weighted_values(j, slot):
        return _dot(values(j), p_ref[slot])

    def phase(j, slot, first=False, last=False):
        if not last:
            put_scores(j + 1, (slot + 1) % ATT_RING)
        if not first:
            acc_ref[...] = st_ref[ST_ALPHA] * acc_ref[...] + weighted_values(j - 1, (slot - 1) % ATT_RING)
        m = st_ref[ST_M]
        m_new = jnp.maximum(m, mx_ref[slot])
        st_ref[ST_ALPHA] = jnp.exp2(m - m_new)
        st_ref[ST_M] = m_new
        p_ref[slot] = jnp.exp2((s_ref[slot] - m_new).astype(BF16))

    st_ref[ST_M] = jnp.full(st_ref.shape[1:], -jnp.inf, F32)
    acc_ref[...] = jnp.zeros_like(acc_ref)
    put_scores(0, 0)
    phase(0, 0, first=True, last=n_chunk == 1)
    n_loop = max(n_own - 2, 0) // ATT_RING

    def revolution(i, carry):
        for t in range(1, ATT_RING + 1):
            phase(ATT_RING * i + t, t % ATT_RING)
        return carry

    lax.fori_loop(0, n_loop, revolution, 0)
    for j in range(1 + ATT_RING * n_loop, n_chunk):
        phase(j, j % ATT_RING, last=j == n_chunk - 1)
    acc = st_ref[ST_ALPHA] * acc_ref[...] + weighted_values(n_chunk - 1, (n_chunk - 1) % ATT_RING)
    out = (acc[0:ATT_HEAD_DIM] / acc[ATT_HEAD_DIM:ATT_HEAD_DIM + 1]).astype(BF16)
    for h in range(ATT_GROUP):
        o_ref[h * ATT_HEAD_DIM:(h + 1) * ATT_HEAD_DIM, :] = out[:, h * qb:(h + 1) * qb]


def _attention(q_t, k, v_t, k_ctx, v_ctx_t, batch):
    n = q_t.shape[1]
    seq = n // batch
    qb, kc = ATT_QB, ATT_KC
    n_q = seq // qb
    has_ctx = k_ctx is not None
    q_spec = pl.BlockSpec((ATT_GROUP * ATT_HEAD_DIM, qb), lambda b, g, j: (g, b * n_q + j))
    in_specs = [
        q_spec,
        pl.BlockSpec((seq, ATT_KV_W), lambda b, g, j: (b, 0)),
        pl.BlockSpec((seq // kc, ATT_HEAD_DIM, kc), lambda b, g, j: (b, g, 0)),
    ]
    args = [q_t, k, v_t]
    if has_ctx:
        in_specs += [
            pl.BlockSpec((kc, ATT_KV_W), lambda b, g, j: (b, 0)),
            pl.BlockSpec((ATT_HEAD_DIM, kc), lambda b, g, j: (b * ATT_KV_HEADS + g, 0)),
        ]
        args += [k_ctx, v_ctx_t]
    width = ATT_GROUP * qb
    return pl.pallas_call(
        functools.partial(_attn_kernel, has_ctx=has_ctx),
        out_shape=jax.ShapeDtypeStruct(q_t.shape, BF16),
        grid=(batch, ATT_KV_HEADS, n_q),
        in_specs=in_specs,
        out_specs=q_spec,
        scratch_shapes=[
            pltpu.VMEM((ATT_RING, kc, width), F32),
            pltpu.VMEM((ATT_RING, kc, width), BF16),
            pltpu.VMEM((ATT_V_ROWS, width), F32),
            pltpu.VMEM((N_ST, 1, width), F32),
            pltpu.VMEM((ATT_RING, 1, width), F32),
        ],
        compiler_params=_params(("parallel", "parallel", "parallel")),
    )(*args)


def _s5_in_kernel(x_ref, mod_ref, g1_ref, w_ref, o_ref, h_ref):
    tm = x_ref.shape[0]
    n_chunk = o_ref.shape[2]
    n_col = D_MODEL // LANES

    def stage(i, carry):
        rows = pl.ds(pl.multiple_of(i * STAGE_ROWS, STAGE_ROWS), STAGE_ROWS)
        h = _modulated_norm(x_ref[rows, :], g1_ref[...], mod_ref)
        for j in range(n_col):
            h_ref[j, rows, :] = h[:, j * LANES:(j + 1) * LANES]
        return carry

    lax.fori_loop(0, tm // STAGE_ROWS, stage, 0)
    for s in range(S5_T):
        h = jnp.concatenate([h_ref[j, pl.ds(s, n_chunk, stride=S5_T), :] for j in range(n_col)], axis=1)
        o_ref[s] = _dot_nt(w_ref[...], h.astype(BF16)).astype(BF16)


def _s5_in(x, mod, mod_rows, g1, w_su_t):
    n = x.shape[0]
    tm = S5_TM
    mod_row = _mod_row_fn(mod_rows, tm)
    return pl.pallas_call(
        _s5_in_kernel,
        out_shape=jax.ShapeDtypeStruct((S5_T, SSM_CH, n // S5_T), BF16),
        grid=(n // tm,),
        in_specs=[
            pl.BlockSpec((tm, D_MODEL), lambda i: (i, 0)),
            pl.BlockSpec((1, N_MOD, D_MODEL), lambda i: (mod_row(i), 0, 0)),
            _const_spec((1, D_MODEL)),
            _const_spec((SSM_CH, D_MODEL)),
        ],
        out_specs=pl.BlockSpec((S5_T, SSM_CH, tm // S5_T), lambda i: (0, 0, i)),
        scratch_shapes=[pltpu.VMEM((D_MODEL // LANES, tm, LANES), F32)],
        compiler_params=_params(("parallel",)),
    )(x, mod, g1, w_su_t)


def _s5_kernel(*refs, two_pass):
    if two_pass:
        u_ref, tt_ref, wx_ref, wy_ref, a_ref, aseg_ref, s0_ref, y_ref, v_ref, init_ref = refs
    else:
        u_ref, tt_ref, wx_ref, wy_ref, a_ref, y_ref, fin_ref, v_ref = refs
    gb = tt_ref.shape[1]
    n_chunk = u_ref.shape[2] // S5_ROWS
    grp = SSM_GROUP

    def toeplitz(d, j):
        strip = tt_ref[d, j]
        width = S5_T * grp
        starts = [(S5_T - 1 - t) * grp for t in range(S5_T)]
        return jnp.concatenate([strip[:, a:a + width] for a in starts], axis=0)

    def put_outputs(j, yt, accumulate):
        for t in range(S5_T):
            if accumulate:
                y_ref[t, j * grp:(j + 1) * grp, :] += yt[t * grp:(t + 1) * grp, :]
            else:
                y_ref[t, j * grp:(j + 1) * grp, :] = yt[t * grp:(t + 1) * grp, :]

    for j in range(gb):
        z = jnp.concatenate([u_ref[s, j * grp:(j + 1) * grp, :] for s in range(S5_T)], axis=0)
        put_outputs(j, _dot(toeplitz(0, j), z) + _dot(toeplitz(1, j), z), accumulate=False)
        for d in range(2):
            v_ref[d, j] = _dot_tn(z, wx_ref[d, j])

    def times(coef_ref, d, j, x):
        lanes = slice(j * S5_GL, (j + 1) * S5_GL)
        return coef_ref[d, 0:1, lanes] * x + coef_ref[d, 1:2, lanes] * pltpu.roll(x, SSM_STATE, axis=1)

    def scan(init, keep_states):
        xs = [list(init[0]), list(init[1])]
        for s in range(n_chunk):
            for d in range(2):
                c = s if d == 0 else n_chunk - 1 - s
                rows = pl.ds(c, S5_ROWS, stride=n_chunk)
                for j in range(gb):
                    v = v_ref[d, j, rows, :]
                    if keep_states:
                        v_ref[d, j, rows, :] = xs[d][j]
                    xs[d][j] = times(a_ref, d, j, xs[d][j]) + v
        return xs

    zero = [[jnp.zeros((S5_ROWS, S5_GL), F32)] * gb] * 2
    if two_pass:
        local = scan(zero, keep_states=False)
        n_batch = s0_ref.shape[1]
        n_seg = S5_ROWS // n_batch
        for d in range(2):
            order = range(n_seg) if d == 0 else range(n_seg - 1, -1, -1)
            for j in range(gb):
                for b in range(n_batch):
                    cur = s0_ref[d, b:b + 1, j * S5_GL:(j + 1) * S5_GL]
                    for sgm in order:
                        r = b * n_seg + sgm
                        init_ref[d, j, r:r + 1, :] = cur
                        cur = times(aseg_ref, d, j, cur) + local[d][j][r:r + 1, :]
        scan([[init_ref[d, j] for j in range(gb)] for d in range(2)], keep_states=True)
    else:
        final = scan(zero, keep_states=True)
        for d in range(2):
            for j in range(gb):
                fin_ref[d, :, j * S5_GL:(j + 1) * S5_GL] = final[d][j]
    for j in range(gb):
        put_outputs(j, _dot_nt(wy_ref[0, j], v_ref[0, j].astype(BF16))
                    + _dot_nt(wy_ref[1, j], v_ref[1, j].astype(BF16)), accumulate=True)


def _s5(u, tt, wx, wy, a_t, a_seg, s0):
    two_pass = s0 is not None
    n_rows = u.shape[2]
    gb = S5_GB
    io_spec = pl.BlockSpec((S5_T, gb * SSM_GROUP, n_rows), lambda i: (0, i, 0))
    coef_spec = pl.BlockSpec((2, 2, gb * S5_GL), lambda i: (0, 0, i))
    in_specs = [
        io_spec,
        pl.BlockSpec((2, gb) + tt.shape[2:], lambda i: (0, i, 0, 0)),
        pl.BlockSpec((2, gb) + wx.shape[2:], lambda i: (0, i, 0, 0)),
        pl.BlockSpec((2, gb) + wy.shape[2:], lambda i: (0, i, 0, 0)),
        coef_spec,
    ]
    args = [u, tt, wx, wy, a_t]
    y_shape = jax.ShapeDtypeStruct(u.shape, F32)
    scratch = [pltpu.VMEM((2, gb, n_rows, S5_GL), F32)]
    if two_pass:
        in_specs += [coef_spec, pl.BlockSpec((2, s0.shape[1], gb * S5_GL), lambda i: (0, 0, i))]
        args += [a_seg, s0]
        out_shape, out_specs = y_shape, io_spec
        scratch.append(pltpu.VMEM((2, gb, S5_ROWS, S5_GL), F32))
    else:
        out_shape = (y_shape, jax.ShapeDtypeStruct((2, S5_ROWS, SSM_GROUPS * S5_GL), F32))
        out_specs = (io_spec, pl.BlockSpec((2, S5_ROWS, gb * S5_GL), lambda i: (0, 0, i)))
    return pl.pallas_call(
        functools.partial(_s5_kernel, two_pass=two_pass),
        out_shape=out_shape,
        grid=(SSM_GROUPS // gb,),
        in_specs=in_specs,
        out_specs=out_specs,
        scratch_shapes=scratch,
        compiler_params=_params(("parallel",)),
    )(*args)


def _s5_out_kernel(y_ref, u_ref, d_ref, w_ref, o_ref, buf_ref):
    n_chunk = y_ref.shape[2]
    n_col = D_MODEL // LANES
    for s in range(S5_T):
        y = y_ref[s] + d_ref[...] * u_ref[s].astype(F32)
        glu = _dot_tn(jax.nn.gelu(y).astype(BF16), w_ref[...])
        out = glu[:, 0:D_MODEL] * jax.nn.sigmoid(glu[:, D_MODEL:2 * D_MODEL])
        for j in range(n_col):
            buf_ref[j, pl.ds(s, n_chunk, stride=S5_T), :] = out[:, j * LANES:(j + 1) * LANES]
    for j in range(n_col):
        o_ref[:, j * LANES:(j + 1) * LANES] = buf_ref[j].astype(BF16)


def _s5_out(y, u, d_skip, w_glu):
    n = y.shape[2] * S5_T
    tm = S5_TM
    io_spec = pl.BlockSpec((S5_T, SSM_CH, tm // S5_T), lambda i: (0, 0, i))
    return pl.pallas_call(
        _s5_out_kernel,
        out_shape=jax.ShapeDtypeStruct((n, D_MODEL), BF16),
        grid=(n // tm,),
        in_specs=[io_spec, io_spec, _const_spec((SSM_CH, 1)), _const_spec((SSM_CH, 2 * D_MODEL))],
        out_specs=pl.BlockSpec((tm, D_MODEL), lambda i: (i, 0)),
        scratch_shapes=[pltpu.VMEM((D_MODEL // LANES, tm, LANES), F32)],
        compiler_params=_params(("parallel",)),
    )(y, u, d_skip, w_glu)


def _route(probs):
    epg = EXPERTS_PER_GROUP
    groups = [probs[g * epg:(g + 1) * epg] for g in range(N_EXPERT_GROUPS)]
    scores = []
    for grp in groups:
        best = None
        for a in range(epg):
            for b in range(a + 1, epg):
                pair = grp[a] + grp[b]
                best = pair if best is None else jnp.maximum(best, pair)
        scores.append(best)
    top_score = scores[0]
    top_group = jnp.zeros_like(scores[0], dtype=jnp.int32)
    for g in range(1, N_EXPERT_GROUPS):
        better = scores[g] > top_score
        top_score = jnp.where(better, scores[g], top_score)
        top_group = jnp.where(better, g, top_group)
    sel = []
    for j in range(epg):
        v = groups[0][j]
        for g in range(1, N_EXPERT_GROUPS):
            v = jnp.where(top_group == g, groups[g][j], v)
        sel.append(v)
    v1 = sel[0]
    i1 = jnp.zeros_like(top_group)
    for j in range(1, epg):
        better = sel[j] > v1
        v1 = jnp.where(better, sel[j], v1)
        i1 = jnp.where(better, j, i1)
    v2 = jnp.full_like(v1, -1.0)
    i2 = jnp.zeros_like(top_group)
    for j in range(epg):
        better = (sel[j] > v2) & (i1 != j)
        v2 = jnp.where(better, sel[j], v2)
        i2 = jnp.where(better, j, i2)
    total = v1 + v2
    w1 = v1 / total
    w2 = v2 / total
    return top_group, [jnp.where(i1 == j, w1, jnp.where(i2 == j, w2, 0.0)) for j in range(epg)]


def _merge_kernel(x_ref, ro_ref, att_ref, ssm_ref, gates_ref, mod_ref, wro_ref, wao_ref, wout_ref, g2_ref, wrt_ref,
                  br_ref, x1_ref, h2_ref, meta_ref, cnt_ref):
    ret_branch = _dot(ro_ref[...], wro_ref[...])
    att_branch = _dot_tn(att_ref[...], wao_ref[...])

    def gate(j):
        return jax.nn.sigmoid(gates_ref[:, j * D_MODEL:(j + 1) * D_MODEL].astype(F32))

    merged = gate(0) * ret_branch + gate(1) * att_branch + gate(2) * ssm_ref[...].astype(F32)
    x1 = x_ref[...] + mod_ref[0, 2:3, :] * _dot(merged.astype(BF16), wout_ref[...])
    x1_ref[...] = x1
    xn = x1 * lax.rsqrt(jnp.mean(x1 * x1, axis=-1, keepdims=True) + EPS) * g2_ref[...]
    h2 = xn * (1.0 + mod_ref[0, 4:5, :]) + mod_ref[0, 3:4, :]
    h2_ref[...] = h2.astype(BF16)

    logits = lax.dot_general(wrt_ref[...], h2, (((1,), (1,)), ((), ())), precision=lax.Precision.HIGHEST,
                             preferred_element_type=F32) + br_ref[...]
    e = jnp.exp(logits - jnp.max(logits, axis=0, keepdims=True))
    probs = e / jnp.sum(e, axis=0, keepdims=True)
    top_group, weights = _route([probs[j:j + 1, :] for j in range(N_EXPERTS)])

    tm = top_group.shape[1]

    @pl.when(pl.program_id(0) % (MOE_PART // tm) == 0)
    def _():
        cnt_ref[...] = jnp.zeros_like(cnt_ref)

    zero_row = jnp.zeros((1, tm), F32)
    onehot = jnp.concatenate([(top_group == g).astype(F32) for g in range(N_EXPERT_GROUPS)]
                             + [zero_row] * (META_ROWS - N_EXPERT_GROUPS), axis=0)
    lane = lax.broadcasted_iota(jnp.int32, onehot.shape, 1)
    incl = onehot
    shift = 1
    while shift < tm:
        incl = incl + jnp.where(lane >= shift, pltpu.roll(incl, shift, axis=1), 0.0)
        shift *= 2
    before = cnt_ref[0, :, 0:1]
    rank = jnp.sum(onehot * (incl - onehot + before), axis=0, keepdims=True)
    cnt_ref[0] = jnp.broadcast_to(before + jnp.sum(onehot, axis=1, keepdims=True), cnt_ref.shape[1:])
    meta_ref[...] = jnp.concatenate([top_group.astype(F32), rank] + weights
                                    + [zero_row] * (META_ROWS - 2 - EXPERTS_PER_GROUP), axis=0)


def _merge(x, ro, att_t, ssm, gates, mod, mod_rows, wro, wao, wout, g2, wrt, br):
    n = x.shape[0]
    tm = TM_PROJ
    mod_row = _mod_row_fn(mod_rows, tm)
    row = lambda i: (i, 0)
    in_specs = [
        pl.BlockSpec((tm, D_MODEL), row),
        pl.BlockSpec((tm, RET_W), row),
        pl.BlockSpec((ATT_W, tm), lambda i: (0, i)),
        pl.BlockSpec((tm, D_MODEL), row),
        pl.BlockSpec((tm, 3 * D_MODEL), row),
        pl.BlockSpec((1, N_MOD, D_MODEL), lambda i: (mod_row(i), 0, 0)),
        _const_spec((RET_W, D_MODEL)),
        _const_spec((ATT_W, D_MODEL)),
        _const_spec((D_MODEL, D_MODEL)),
        _const_spec((1, D_MODEL)),
        _const_spec((N_EXPERTS, D_MODEL)),
        _const_spec((N_EXPERTS, 1)),
    ]
    return pl.pallas_call(
        _merge_kernel,
        out_shape=(
            jax.ShapeDtypeStruct((n, D_MODEL), F32),
            jax.ShapeDtypeStruct((n, D_MODEL), BF16),
            jax.ShapeDtypeStruct((META_ROWS, n), F32),
            jax.ShapeDtypeStruct((n // MOE_PART, META_ROWS, LANES), F32),
        ),
        grid=(n // tm,),
        in_specs=in_specs,
        out_specs=(
            pl.BlockSpec((tm, D_MODEL), row),
            pl.BlockSpec((tm, D_MODEL), row),
            pl.BlockSpec((META_ROWS, tm), lambda i: (0, i)),
            pl.BlockSpec((1, META_ROWS, LANES), lambda i: (i // (MOE_PART // tm), 0, 0)),
        ),
        compiler_params=_params(("arbitrary",)),
    )(x, ro, att_t, ssm, gates, mod, wro, wao, wout, g2, wrt, br)


def _moe_kernel(dest_ref, tile_group_ref, n_valid_ref, *refs, final, n_scatter, n_expert, n_gather):
    if final:
        (h2_ref, wrow_ref, wg_ref, wu_ref, wd_ref, x1_ref, mod_ref, fg_ref, o_ref,
         sorted_ref, wsort_ref, stage_ref, acc_ref) = refs
    else:
        (h2_ref, wrow_ref, wg_ref, wu_ref, wd_ref, x1_ref, mod_ref, o_ref,
         sorted_ref, wsort_ref, stage_ref, acc_ref) = refs
    tm = h2_ref.shape[0]
    epg = EXPERTS_PER_GROUP
    per_part = n_scatter + n_expert + n_gather
    part = pl.program_id(0) // per_part
    i = pl.program_id(0) % per_part

    @pl.when(i == 0)
    def _():
        sorted_ref[...] = jnp.zeros_like(sorted_ref)
        wsort_ref[...] = jnp.zeros_like(wsort_ref)

    @pl.when(i < n_scatter)
    def _():
        stage_ref[...] = h2_ref[...].astype(F32)
        base = (part * n_scatter + i) * tm

        def move(t, carry):
            d = dest_ref[base + t]
            sorted_ref[pl.ds(d, 1), :] = stage_ref[pl.ds(t, 1), :]
            wsort_ref[pl.ds(d, 1), :] = wrow_ref[pl.ds(t, 1), :]
            return carry

        lax.fori_loop(0, tm, move, 0, unroll=8)

    step = i - n_scatter
    tile = step // epg
    ein = step % epg

    @pl.when((step >= 0) & (step < n_expert) & (tile < n_valid_ref[part]))
    def _():
        rows = pl.ds(pl.multiple_of(tile * MOE_TILE, MOE_TILE), MOE_TILE)
        x = sorted_ref[rows, :].astype(BF16)
        g = _dot(x, wg_ref[0])
        u = _dot(x, wu_ref[0])
        w = wsort_ref[rows, :]
        lane = lax.broadcasted_iota(jnp.int32, w.shape, 1)
        w_col = jnp.sum(jnp.where(lane == ein, w, 0.0), axis=1, keepdims=True)
        out = _dot((g * jax.nn.sigmoid(g) * u * w_col).astype(BF16), wd_ref[0])

        @pl.when(ein == 0)
        def _():
            acc_ref[...] = out

        @pl.when(ein > 0)
        def _():
            acc_ref[...] += out

        @pl.when(ein == epg - 1)
        def _():
            sorted_ref[rows, :] = acc_ref[...]

    @pl.when(step >= n_expert)
    def _():
        base = (part * n_gather + step - n_expert) * tm

        def move(t, carry):
            stage_ref[pl.ds(t, 1), :] = sorted_ref[pl.ds(dest_ref[base + t], 1), :]
            return carry

        lax.fori_loop(0, tm, move, 0, unroll=8)
        x2 = x1_ref[...] + mod_ref[0, 5:6, :] * stage_ref[...]
        if final:
            x2 = x2 * lax.rsqrt(jnp.mean(x2 * x2, axis=-1, keepdims=True) + EPS) * fg_ref[...]
        o_ref[...] = x2


def _moe_plan(meta, cnt):
    n = meta.shape[1]
    n_part = cnt.shape[0]
    n_tiles = n // n_part // MOE_TILE + N_EXPERT_GROUPS
    counts = cnt[:, 0:N_EXPERT_GROUPS, 0].astype(jnp.int32)
    padded = (counts + MOE_TILE - 1) // MOE_TILE * MOE_TILE
    ends = jnp.cumsum(padded, axis=1)
    starts = ends - padded
    group = meta[0].astype(jnp.int32).reshape(n_part, -1)
    dest = meta[1].astype(jnp.int32).reshape(n_part, -1)
    for g in range(N_EXPERT_GROUPS):
        dest = dest + jnp.where(group == g, starts[:, g:g + 1], 0)
    tile_start = jnp.arange(n_tiles, dtype=jnp.int32) * MOE_TILE
    tile_group = jnp.sum(tile_start[None, :, None] >= ends[:, None, :], axis=2).astype(jnp.int32)
    tile_group = jnp.minimum(tile_group, N_EXPERT_GROUPS - 1)
    n_valid = (ends[:, N_EXPERT_GROUPS - 1] // MOE_TILE).astype(jnp.int32)
    w_rows = jnp.pad(meta[2:2 + EXPERTS_PER_GROUP].T, ((0, 0), (0, LANES - EXPERTS_PER_GROUP)))
    return dest.reshape(-1), tile_group, n_valid, w_rows


def _moe(h2, meta, cnt, x1, mod, mod_rows, layer, wg, wu, wd, final_g):
    n = h2.shape[0]
    tm = MOE_TILE
    n_part = cnt.shape[0]
    dest, tile_group, n_valid, w_rows = _moe_plan(meta, cnt)
    n_tiles = tile_group.shape[1]
    epg = EXPERTS_PER_GROUP
    n_scatter, n_expert, n_gather = n // n_part // tm, n_tiles * epg, n // n_part // tm
    per_part = n_scatter + n_expert + n_gather
    mod_row = _mod_row_fn(mod_rows, tm)
    final = final_g is not None

    def scatter_tile(i):
        return (i // per_part) * n_scatter + jnp.minimum(i % per_part, n_scatter - 1)

    def gather_tile(i):
        return (i // per_part) * n_gather + jnp.clip(i % per_part - n_scatter - n_expert, 0, n_gather - 1)

    def expert(i, tg, nv):
        part = i // per_part
        step = jnp.clip(i % per_part - n_scatter, 0, n_expert - 1)
        tile = step // epg
        in_use = tile < nv[part]
        group = tg[part * n_tiles + jnp.minimum(tile, nv[part] - 1)]
        return layer * N_EXPERTS + group * epg + jnp.where(in_use, step % epg, epg - 1)

    in_specs = [
        pl.BlockSpec((tm, D_MODEL), lambda i, d, tg, nv: (scatter_tile(i), 0)),
        pl.BlockSpec((tm, LANES), lambda i, d, tg, nv: (scatter_tile(i), 0)),
        pl.BlockSpec((1, D_MODEL, D_FF), lambda i, d, tg, nv: (expert(i, tg, nv), 0, 0)),
        pl.BlockSpec((1, D_MODEL, D_FF), lambda i, d, tg, nv: (expert(i, tg, nv), 0, 0)),
        pl.BlockSpec((1, D_FF, D_MODEL), lambda i, d, tg, nv: (expert(i, tg, nv), 0, 0)),
        pl.BlockSpec((tm, D_MODEL), lambda i, d, tg, nv: (gather_tile(i), 0)),
        pl.BlockSpec((1, N_MOD, D_MODEL), lambda i, d, tg, nv: (mod_row(gather_tile(i)), 0, 0)),
    ]
    args = [h2, w_rows, wg, wu, wd, x1, mod]
    if final:
        in_specs.append(pl.BlockSpec((1, D_MODEL), lambda i, d, tg, nv: (0, 0)))
        args.append(final_g)
    grid_spec = pltpu.PrefetchScalarGridSpec(
        num_scalar_prefetch=3,
        grid=(n_part * per_part,),
        in_specs=in_specs,
        out_specs=pl.BlockSpec((tm, D_MODEL), lambda i, d, tg, nv: (gather_tile(i), 0)),
        scratch_shapes=[
            pltpu.VMEM((n_tiles * MOE_TILE, D_MODEL), F32),
            pltpu.VMEM((n_tiles * MOE_TILE, LANES), F32),
            pltpu.VMEM((tm, D_MODEL), F32),
            pltpu.VMEM((MOE_TILE, D_MODEL), F32),
        ],
    )
    return pl.pallas_call(
        functools.partial(_moe_kernel, final=final, n_scatter=n_scatter, n_expert=n_expert, n_gather=n_gather),
        out_shape=jax.ShapeDtypeStruct((n, D_MODEL), F32),
        grid_spec=grid_spec,
        compiler_params=_params(("arbitrary",)),
    )(dest, tile_group.reshape(-1), n_valid, *args)


def _block_diag(m):
    g, r, c = m.shape
    eye = jnp.eye(g, dtype=m.dtype)
    return (eye[:, None, :, None] * m[:, :, None, :]).reshape(g * r, g * c)


def _s5_direction_params(a_re, a_im, log_dt, b_re, b_im, c_re, c_im, reverse):
    n_g = a_re.shape[0]
    width = S5_T * SSM_GROUP
    hp = lax.Precision.HIGHEST
    step = jnp.exp(log_dt)[:, None]
    mag = jnp.exp(a_re * step)
    ar = mag * jnp.cos(a_im * step)
    ai = mag * jnp.sin(a_im * step)
    den = a_re * a_re + a_im * a_im
    fr = ((ar - 1.0) * a_re + ai * a_im) / den
    fi = (ai * a_re - (ar - 1.0) * a_im) / den
    bbr = (fr[:, :, None] * b_re - fi[:, :, None] * b_im).transpose(0, 2, 1)
    bbi = (fr[:, :, None] * b_im + fi[:, :, None] * b_re).transpose(0, 2, 1)
    pr, pi = [jnp.ones_like(ar)], [jnp.zeros_like(ar)]
    for _ in range(S5_T):
        r, i = pr[-1], pi[-1]
        pr.append(r * ar - i * ai)
        pi.append(r * ai + i * ar)
    pr, pi = jnp.stack(pr)[:, :, None, :], jnp.stack(pi)[:, :, None, :]
    mr = pr * bbr - pi * bbi
    mi = pr * bbi + pi * bbr
    cr = c_re * pr - c_im * pi
    ci = c_re * pi + c_im * pr
    kern = (jnp.einsum('gop,kgip->goki', c_re, mr[:S5_T], precision=hp)
            - jnp.einsum('gop,kgip->goki', c_im, mi[:S5_T], precision=hp))
    pad = jnp.zeros_like(kern[:, :, 1:, :])
    strip = jnp.concatenate([pad, kern] if reverse else [kern[:, :, ::-1, :], pad], axis=2)
    tt = jnp.concatenate([strip, pad[:, :, 0:1, :]], axis=2).reshape(n_g, SSM_GROUP, 2 * width)
    wx = jnp.concatenate([mr[:S5_T], mi[:S5_T]], axis=3)
    wx = (wx if reverse else wx[::-1]).transpose(1, 0, 2, 3).reshape(n_g, width, S5_GL)
    wy = jnp.concatenate([cr[1:], -ci[1:]], axis=3)
    wy = (wy[::-1] if reverse else wy).transpose(1, 0, 2, 3).reshape(n_g, width, S5_GL)

    def coef_rows(r, i):
        return jnp.stack([jnp.concatenate([r, r], axis=1).reshape(-1), jnp.concatenate([-i, i], axis=1).reshape(-1)])

    sr, si = pr[S5_T, :, 0, :], pi[S5_T, :, 0, :]
    a_t = coef_rows(sr, si)
    for _ in range(int(math.log2(S5_SEG // S5_T))):
        sr, si = sr * sr - si * si, 2.0 * sr * si
    return tt.astype(BF16), wx.astype(BF16), wy.astype(BF16), a_t, coef_rows(sr, si)


def _rope_tables(n_tokens):
    rows = n_tokens // GRID_W
    row = jnp.repeat(jnp.arange(rows, dtype=F32), GRID_W)
    col = jnp.tile(jnp.arange(GRID_W, dtype=F32), rows)
    n_freq = ATT_HEAD_DIM // 4
    inv = ROPE_THETA ** (-jnp.arange(n_freq, dtype=F32) / n_freq)
    ang = jnp.concatenate([row[:, None] * inv, col[:, None] * inv], axis=-1)
    cos = jnp.repeat(jnp.cos(ang), 2, axis=-1)
    sin = jnp.repeat(jnp.sin(ang), 2, axis=-1) * jnp.tile(jnp.array([-1.0, 1.0], F32), ATT_HEAD_DIM // 2)
    return cos, sin


def _swap_pairs(a, axis=-1):
    axis = axis % a.ndim
    pairs = a.reshape(a.shape[:axis] + (a.shape[axis] // 2, 2) + a.shape[axis + 1:])
    return jnp.flip(pairs, axis=axis + 1).reshape(a.shape)


def _layer(x, batch, mod, mod_rows, p, ctx):
    n = x.shape[0]
    seq = n // batch
    latent = ctx is not None
    outs = _inproj(x, mod, mod_rows, p, latent)
    rqkv, rg, gates, q_t, k16, v_t = outs[:6]

    ro, ret_state = _retention(rqkv.reshape(batch, seq, 3 * RET_W), rg.reshape(batch, seq, RET_W), p['ret_lg'],
                               p['ret_gn'], ctx[2] if latent else None, bb=1 if latent else 8)

    if latent:
        k_ctx = ctx[0].reshape(batch * ATT_KC, ATT_KV_W).astype(BF16)
        v_ctx_t = ctx[1].reshape(batch, ATT_KC, ATT_KV_W).transpose(0, 2, 1).reshape(batch * ATT_KV_W, ATT_KC)
        att_t = _attention(q_t, k16, v_t, k_ctx, v_ctx_t.astype(BF16), batch)
    else:
        att_t = _attention(q_t, k16, v_t, None, None, batch)

    u = _s5_in(x, mod, mod_rows, p['g1'], p['w_su_t'])
    if latent:
        s0 = jnp.stack([ctx[3], ctx[4]], axis=3).transpose(1, 0, 2, 3, 4).reshape(2, batch, SSM_GROUPS * S5_GL)
        y = _s5(u, p['s5_tt'], p['s5_wx'], p['s5_wy'], p['s5_a'], p['s5_aseg'], s0)
        ssm_state = None
    else:
        y, ssm_state = _s5(u, p['s5_tt'], p['s5_wx'], p['s5_wy'], p['s5_a'], None, None)
        ssm_state = ssm_state.reshape(2, batch, SSM_GROUPS, 2, SSM_STATE)
    ssm = _s5_out(y, u, p['ssm_d'], p['w_glu'])

    x1, h2, meta, cnt = _merge(x, ro.reshape(n, RET_W), att_t, ssm, gates, mod, mod_rows, p['w_ret_out'],
                               p['w_att_out'], p['w_out'], p['g2'], p['w_router_t'], p['b_router'])
    x2 = _moe(h2, meta, cnt, x1, mod, mod_rows, p['layer'], p['w_gate'], p['w_up'], p['w_down'], p.get('final_g'))
    if latent:
        return x2, None
    return x2, (outs[6], outs[7], ret_state, ssm_state)


def kernel(x_prompt, x_sample, c, cache_attn_k, cache_attn_v, state_ret, state_ssm_re, state_ssm_im, c_ctx, w_ada,
           b_ada, norm1_g, norm2_g, w_in, ret_lg_f, ret_lg_b, ret_norm_g, w_ret_out, att_q_norm_g, att_k_norm_g,
           w_att_out, a_re_f, a_im_f, a_re_b, a_im_b, log_dt_f, log_dt_b, ssm_b_re, ssm_b_im, ssm_c_re, ssm_c_im,
           ssm_d, w_glu, w_out, w_router, b_router, w_gate, w_up, w_down, final_norm_g):
    batch, seq, _ = x_prompt.shape
    dec_batch, dec_seq, _ = x_sample.shape
    assert batch * seq == S5_ROWS * S5_SEG and dec_batch * dec_seq == S5_ROWS * S5_SEG
    assert seq == S5_SEG and dec_seq % S5_SEG == 0
    assert cache_attn_k.shape[2] == ATT_KC and seq % ATT_QB == 0

    c_rows = jnp.zeros((ADA_ROWS, D_MODEL), F32).at[0].set(c_ctx).at[1:1 + dec_batch].set(c)
    mod_all = _ada(c_rows, w_ada, b_ada).reshape(DEPTH, ADA_ROWS, N_MOD, D_MODEL)

    cos, sin = _rope_tables(dec_seq)
    ones = _block_diag(jnp.ones((ATT_KV_HEADS, ATT_HEAD_DIM, ATT_HEAD_DIM), BF16))
    w_in16 = w_in.astype(BF16)
    w_gate16 = w_gate.astype(BF16).reshape(DEPTH * N_EXPERTS, D_MODEL, D_FF)
    w_up16 = w_up.astype(BF16).reshape(DEPTH * N_EXPERTS, D_MODEL, D_FF)
    w_down16 = w_down.astype(BF16).reshape(DEPTH * N_EXPERTS, D_FF, D_MODEL)
    s5_fwd = jax.vmap(functools.partial(_s5_direction_params, reverse=False))(
        a_re_f, a_im_f, log_dt_f, ssm_b_re, ssm_b_im, ssm_c_re, ssm_c_im)
    s5_bwd = jax.vmap(functools.partial(_s5_direction_params, reverse=True))(
        a_re_b, a_im_b, log_dt_b, ssm_b_re, ssm_b_im, ssm_c_re, ssm_c_im)

    xp = x_prompt.reshape(batch * seq, D_MODEL)
    xs = x_sample.reshape(dec_batch * dec_seq, D_MODEL)
    ks_, vs_, rets_, ssms_ = [], [], [], []
    for l in range(DEPTH):
        w_l = w_in[l]
        gq = att_q_norm_g[l][:, None]
        gk = jnp.tile(att_k_norm_g[l], ATT_KV_HEADS)[None]
        wq_t = w_l[:, C_AQ:C_AK].T
        fwd = [a[l] for a in s5_fwd]
        bwd = [a[l] for a in s5_bwd]
        p = {
            'layer': l, 'g1': norm1_g[l][None], 'g2': norm2_g[l][None], 'w_in': w_in16, 'ones': ones,
            'wq_t': wq_t.astype(BF16), 'wv_t': w_l[:, C_AV:C_SU].T.astype(BF16), 'gq': gq, 'gk': gk,
            'wqs_t': _swap_pairs(wq_t, axis=0).astype(BF16), 'wk_sw': _swap_pairs(w_l[:, C_AK:C_AV]).astype(BF16),
            'gqs': _swap_pairs(gq, axis=0), 'gks': _swap_pairs(gk),
            'cos': jnp.tile(cos, (1, ATT_KV_HEADS)), 'sin': jnp.tile(sin, (1, ATT_KV_HEADS)),
            'cos_t': cos.T, 'sin_t': sin.T,
            'ret_lg': jnp.stack([ret_lg_f[l], ret_lg_b[l]]), 'ret_gn': ret_norm_g[l][None],
            'w_ret_out': w_ret_out[l].astype(BF16), 'w_att_out': w_att_out[l].astype(BF16),
            'w_su_t': w_l[:, C_SU:C_G].T.astype(BF16),
            's5_tt': jnp.stack([fwd[0], bwd[0]]), 's5_wx': jnp.stack([fwd[1], bwd[1]]),
            's5_wy': jnp.stack([fwd[2], bwd[2]]), 's5_a': jnp.stack([fwd[3], bwd[3]]),
            's5_aseg': jnp.stack([fwd[4], bwd[4]]),
            'ssm_d': ssm_d[l][:, None], 'w_glu': w_glu[l].astype(BF16), 'w_out': w_out[l].astype(BF16),
            'w_router_t': w_router.T, 'b_router': b_router[:, None],
            'w_gate': w_gate16, 'w_up': w_up16, 'w_down': w_down16,
        }
        if l == DEPTH - 1:
            p['final_g'] = final_norm_g[None]
        mod = mod_all[l]

        xp, (k_c, v_c, st_r, st_s) = _layer(xp, batch, mod, (0, batch * seq), p, None)
        ks_.append(k_c.reshape(batch, seq, ATT_KV_HEADS, ATT_HEAD_DIM))
        vs_.append(v_c.reshape(batch, seq, ATT_KV_HEADS, ATT_HEAD_DIM))
        rets_.append(st_r)
        ssms_.append(st_s)

        ctx = (cache_attn_k[:, l], cache_attn_v[:, l], state_ret[:, l], state_ssm_re[:, l], state_ssm_im[:, l])
        xs, _ = _layer(xs, dec_batch, mod, (1, dec_seq), p, ctx)

    ssm_all = jnp.stack(ssms_, axis=1)
    ssm_all = ssm_all.transpose(2, 1, 0, 3, 4, 5)
    new_re = ssm_all[..., 0, :]
    new_im = ssm_all[..., 1, :]
    return (xp.reshape(batch, seq, D_MODEL), xs.reshape(dec_batch, dec_seq, D_MODEL),
            jnp.stack(ks_, axis=1), jnp.stack(vs_, axis=1), jnp.stack(rets_, axis=1), new_re, new_im)
```

```python
import functools
import math

import jax
import jax.numpy as jnp
from jax import lax
from jax.experimental import pallas as pl
from jax.experimental.pallas import tpu as pltpu

F32 = jnp.float32
BF16 = jnp.bfloat16

D_MODEL = 1024
DEPTH = 2
GRID_W = 64
RET_HEADS = 4
RET_DK = 128
RET_DV = 128
RET_CHUNK = 128
RET_W = RET_HEADS * RET_DV
ATT_HEADS = 8
ATT_KV_HEADS = 2
ATT_GROUP = ATT_HEADS // ATT_KV_HEADS
ATT_HEAD_DIM = 64
ATT_W = ATT_HEADS * ATT_HEAD_DIM
ATT_KV_W = ATT_KV_HEADS * ATT_HEAD_DIM
ROPE_THETA = 10000.0
SSM_CH = 512
SSM_GROUP = 16
SSM_GROUPS = SSM_CH // SSM_GROUP
SSM_STATE = 64
N_EXPERTS = 16
N_EXPERT_GROUPS = 4
EXPERTS_PER_GROUP = N_EXPERTS // N_EXPERT_GROUPS
D_FF = 512
N_MOD = 6
EPS = 1e-6

C_RQ, C_RK, C_RV, C_RG = 0, 512, 1024, 1536
C_AQ, C_AK, C_AV, C_SU, C_G, C_END = 2048, 2560, 2688, 2816, 3328, 6400

VMEM_LIMIT = 52 * 1024 * 1024

ATT_Q_SCALE = ATT_HEAD_DIM ** -0.5 * math.log2(math.e)
ATT_QB = 512
ATT_KC = 256
ATT_V_ROWS = ATT_HEAD_DIM + 16
ATT_RING = 4

TM_PROJ = 512
RET_UNROLL = 4
MOE_TILE = 512
MOE_PART = 4096
META_ROWS = 8
S5_ROWS = 32
S5_SEG = 256
S5_T = 16
S5_GL = 2 * SSM_STATE
S5_GB = 8
S5_CHUNK_ROWS = 256
LANES = 128
STAGE_ROWS = 256
ADA_ROWS = 8
ADA_TN = 1536


def _dot(a, b):
    return jnp.dot(a, b, preferred_element_type=F32)


def _dot_nt(a, b):
    return lax.dot_general(a, b, (((1,), (1,)), ((), ())), preferred_element_type=F32)


def _dot_tn(a, b):
    return lax.dot_general(a, b, (((0,), (0,)), ((), ())), preferred_element_type=F32)


def _const_spec(shape):
    n = len(shape)
    return pl.BlockSpec(shape, lambda *_: (0,) * n, pipeline_mode=pl.Buffered(1))


def _params(sem, vmem=VMEM_LIMIT):
    return pltpu.CompilerParams(dimension_semantics=sem, vmem_limit_bytes=vmem)


def _mod_row_fn(mod_rows, tm):
    base, per_row = mod_rows
    return lambda i: base + (i * tm) // per_row


def _modulated_norm(x, g, mod_ref):
    xn = x * lax.rsqrt(jnp.mean(x * x, axis=-1, keepdims=True) + EPS) * g
    return xn * (1.0 + mod_ref[0, 1:2, :]) + mod_ref[0, 0:1, :]


def _ada_kernel(c_ref, w_ref, b_ref, o_ref):
    c = c_ref[...]
    a = (c * jax.nn.sigmoid(c)).astype(BF16)
    o_ref[0] = _dot(a, w_ref[0].astype(BF16)) + b_ref[0]


def _ada(c_rows, w_ada, b_ada):
    n_col = N_MOD * D_MODEL
    return pl.pallas_call(
        _ada_kernel,
        out_shape=jax.ShapeDtypeStruct((DEPTH, ADA_ROWS, n_col), F32),
        grid=(DEPTH, n_col // ADA_TN),
        in_specs=[
            pl.BlockSpec((ADA_ROWS, D_MODEL), lambda l, j: (0, 0)),
            pl.BlockSpec((1, D_MODEL, ADA_TN), lambda l, j: (l, 0, j)),
            pl.BlockSpec((1, 1, ADA_TN), lambda l, j: (l, 0, j)),
        ],
        out_specs=pl.BlockSpec((1, ADA_ROWS, ADA_TN), lambda l, j: (l, 0, j)),
        compiler_params=_params(("arbitrary", "arbitrary")),
    )(c_rows, w_ada, b_ada.reshape(DEPTH, 1, n_col))


def _head_mean_sq(z, ones):
    z2 = z * z
    hi = z2.astype(BF16)
    lo = (z2 - hi.astype(F32)).astype(BF16)
    return (_dot(hi, ones) + _dot(lo, ones)) * (1.0 / ATT_HEAD_DIM)


def _inproj_kernel(*refs, latent):
    if latent:
        (x_ref, mod_ref, g1_ref, w_ref, wqt_ref, wvt_ref, ones_ref, gq_ref, gk_ref,
         wqst_ref, wks_ref, gqs_ref, gks_ref, cos_ref, sin_ref, cost_ref, sint_ref,
         rqkv_ref, rg_ref, gates_ref, qt_ref, k16_ref, vt_ref) = refs
    else:
        (x_ref, mod_ref, g1_ref, w_ref, wqt_ref, wvt_ref, ones_ref, gq_ref, gk_ref,
         rqkv_ref, rg_ref, gates_ref, qt_ref, k16_ref, vt_ref, ak_ref, av_ref) = refs
    tm = x_ref.shape[0]
    h = _modulated_norm(x_ref[...], g1_ref[...], mod_ref).astype(BF16)

    def seg(a, b):
        return _dot(h, w_ref[0, :, a:b])

    rqkv_ref[:, 0:RET_W] = (seg(C_RQ, C_RK) * (RET_DK ** -0.5)).astype(BF16)
    rqkv_ref[:, RET_W:3 * RET_W] = seg(C_RK, C_RG).astype(BF16)
    rg_ref[...] = seg(C_RG, C_AQ).astype(BF16)
    for j in range(3):
        gates_ref[:, j * D_MODEL:(j + 1) * D_MODEL] = seg(C_G + j * D_MODEL, C_G + (j + 1) * D_MODEL).astype(BF16)

    zk = seg(C_AK, C_AV)
    inv_k = lax.rsqrt(_head_mean_sq(zk, ones_ref[...]) + EPS)
    yk = zk * inv_k * gk_ref[...]
    zq = _dot_nt(wqt_ref[...], h).reshape(ATT_HEADS, ATT_HEAD_DIM, tm)
    inv_q = lax.rsqrt(jnp.mean(zq * zq, axis=1, keepdims=True) + EPS)
    yq = zq * inv_q * gq_ref[...]
    if latent:
        yk_sw = _dot(h, wks_ref[...]) * inv_k * gks_ref[...]
        yk = yk * cos_ref[...] + yk_sw * sin_ref[...]
        zq_sw = _dot_nt(wqst_ref[...], h).reshape(ATT_HEADS, ATT_HEAD_DIM, tm)
        yq = yq * cost_ref[...] + (zq_sw * inv_q * gqs_ref[...]) * sint_ref[...]
    qt_ref[...] = (yq * ATT_Q_SCALE).reshape(ATT_W, tm).astype(BF16)
    k16_ref[...] = yk.astype(BF16)
    vt = _dot_nt(wvt_ref[...], h).astype(BF16)
    for c in range(tm // ATT_KC):
        vt_ref[c] = vt[:, c * ATT_KC:(c + 1) * ATT_KC]
    if not latent:
        ak_ref[...] = yk
        av_ref[...] = seg(C_AV, C_SU)


def _inproj(x, mod, mod_rows, p, latent):
    n = x.shape[0]
    tm = TM_PROJ
    mod_row = _mod_row_fn(mod_rows, tm)
    row = lambda i: (i, 0)
    in_specs = [
        pl.BlockSpec((tm, D_MODEL), row),
        pl.BlockSpec((1, N_MOD, D_MODEL), lambda i: (mod_row(i), 0, 0)),
        _const_spec((1, D_MODEL)),
        pl.BlockSpec((1, D_MODEL, C_END), lambda i: (p['layer'], 0, 0), pipeline_mode=pl.Buffered(1)),
        _const_spec((ATT_W, D_MODEL)),
        _const_spec((ATT_KV_W, D_MODEL)),
        _const_spec((ATT_KV_W, ATT_KV_W)),
        _const_spec((ATT_HEAD_DIM, 1)),
        _const_spec((1, ATT_KV_W)),
    ]
    args = [x, mod, p['g1'], p['w_in'], p['wq_t'], p['wv_t'], p['ones'], p['gq'], p['gk']]
    if latent:
        n_pos = p['cos'].shape[0] // tm
        in_specs += [
            _const_spec((ATT_W, D_MODEL)),
            _const_spec((D_MODEL, ATT_KV_W)),
            _const_spec((ATT_HEAD_DIM, 1)),
            _const_spec((1, ATT_KV_W)),
            pl.BlockSpec((tm, ATT_KV_W), lambda i: (i % n_pos, 0)),
            pl.BlockSpec((tm, ATT_KV_W), lambda i: (i % n_pos, 0)),
            pl.BlockSpec((ATT_HEAD_DIM, tm), lambda i: (0, i % n_pos)),
            pl.BlockSpec((ATT_HEAD_DIM, tm), lambda i: (0, i % n_pos)),
        ]
        args += [p['wqs_t'], p['wk_sw'], p['gqs'], p['gks'], p['cos'], p['sin'], p['cos_t'], p['sin_t']]
    out_shape = [
        jax.ShapeDtypeStruct((n, 3 * RET_W), BF16),
        jax.ShapeDtypeStruct((n, RET_W), BF16),
        jax.ShapeDtypeStruct((n, 3 * D_MODEL), BF16),
        jax.ShapeDtypeStruct((ATT_W, n), BF16),
        jax.ShapeDtypeStruct((n, ATT_KV_W), BF16),
        jax.ShapeDtypeStruct((n // ATT_KC, ATT_KV_W, ATT_KC), BF16),
    ]
    out_specs = [
        pl.BlockSpec((tm, 3 * RET_W), row),
        pl.BlockSpec((tm, RET_W), row),
        pl.BlockSpec((tm, 3 * D_MODEL), row),
        pl.BlockSpec((ATT_W, tm), lambda i: (0, i)),
        pl.BlockSpec((tm, ATT_KV_W), row),
        pl.BlockSpec((tm // ATT_KC, ATT_KV_W, ATT_KC), lambda i: (i, 0, 0)),
    ]
    if not latent:
        out_shape += [jax.ShapeDtypeStruct((n, ATT_KV_W), F32)] * 2
        out_specs += [pl.BlockSpec((tm, ATT_KV_W), row)] * 2
    return pl.pallas_call(
        functools.partial(_inproj_kernel, latent=latent),
        out_shape=tuple(out_shape),
        grid=(n // tm,),
        in_specs=in_specs,
        out_specs=tuple(out_specs),
        compiler_params=_params(("parallel",)),
    )(*args)


T_DEC_F, T_DEC_B, T_XI_F, T_ZETA_F, T_XI_B, T_ZETA_B, T_CD_F, T_CD_B, N_TAB = range(9)


def _ret_kernel(*refs, has_s0, bb, seq):
    if has_s0:
        lg_ref, q_ref, k_ref, v_ref, rg_ref, gn_ref, s0_ref, o_ref, st_ref, tab_ref, acc_ref = refs
    else:
        lg_ref, q_ref, k_ref, v_ref, rg_ref, gn_ref, o_ref, st_ref, tab_ref, acc_ref = refs
    c = RET_CHUNK
    head = pl.program_id(1)
    lgf = lg_ref[0, head]
    lgb = lg_ref[1, head]
    t = lax.broadcasted_iota(jnp.int32, (c, c), 0).astype(F32)
    s = lax.broadcasted_iota(jnp.int32, (c, c), 1).astype(F32)
    tab_ref[T_DEC_F] = jnp.where(t >= s, jnp.exp(lgf * jnp.maximum(t - s, 0.0)), 0.0)
    tab_ref[T_DEC_B] = jnp.where(s >= t, jnp.exp(lgb * jnp.maximum(s - t, 0.0)), 0.0)
    tab_ref[T_XI_F] = jnp.exp(lgf * (t + 1.0))
    tab_ref[T_ZETA_F] = jnp.exp(lgf * (c - 1.0 - t))
    tab_ref[T_XI_B] = jnp.exp(lgb * (c - t))
    tab_ref[T_ZETA_B] = jnp.exp(lgb * t)
    tab_ref[T_CD_F] = jnp.exp(lgf * (c + 0.0 * t))
    tab_ref[T_CD_B] = jnp.exp(lgb * (c + 0.0 * t))
    if has_s0:
        st_ref[...] = s0_ref[...]
    else:
        st_ref[...] = jnp.zeros_like(st_ref)
    acc_ref[...] = jnp.zeros_like(acc_ref)
    n_chunk = seq // c

    def one_direction(rb, d, off, t_dec, t_xi, t_zeta, t_cd):
        rows = pl.ds(off, c)
        q = q_ref[rb, rows, :]
        k = k_ref[rb, rows, :]
        v = v_ref[rb, rows, :]
        p = (_dot_nt(q, k) * tab_ref[t_dec]).astype(BF16)
        st = st_ref[rb, d, 0]
        q_xi = (q.astype(F32) * tab_ref[t_xi]).astype(BF16)
        o = _dot(jnp.concatenate([p, q_xi], axis=1), jnp.concatenate([v, st.astype(BF16)], axis=0))
        acc_ref[rb, rows, :] += o
        kz = (k.astype(F32) * tab_ref[t_zeta]).astype(BF16)
        st_ref[rb, d, 0] = tab_ref[t_cd] * st + _dot_tn(kz, v)

    def body(i, carry):
        off_f = pl.multiple_of(i * c, c)
        off_b = pl.multiple_of((n_chunk - 1 - i) * c, c)
        for rb in range(bb):
            one_direction(rb, 0, off_f, T_DEC_F, T_XI_F, T_ZETA_F, T_CD_F)
            one_direction(rb, 1, off_b, T_DEC_B, T_XI_B, T_ZETA_B, T_CD_B)
        return carry

    lax.fori_loop(0, n_chunk, body, 0, unroll=RET_UNROLL if bb == 1 else 1)

    def finish(i, carry):
        rows = pl.ds(pl.multiple_of(i * c, c), c)
        for rb in range(bb):
            o = acc_ref[rb, rows, :]
            ro = o * lax.rsqrt(jnp.mean(o * o, axis=-1, keepdims=True) + EPS) * gn_ref[...]
            g = rg_ref[rb, rows, :].astype(F32)
            o_ref[rb, rows, :] = (ro * (g * jax.nn.sigmoid(g))).astype(BF16)
        return carry

    lax.fori_loop(0, n_chunk, finish, 0)


def _retention(rqkv, rg, lg, gn, s0, bb):
    b, seq, _ = rqkv.shape
    has_s0 = s0 is not None
    blk = (bb, seq, RET_DK)
    st_spec = pl.BlockSpec((bb, 2, 1, RET_DK, RET_DV), lambda i, h: (i, 0, h, 0, 0))
    in_specs = [
        pl.BlockSpec(memory_space=pltpu.SMEM),
        pl.BlockSpec(blk, lambda i, h: (i, 0, h)),
        pl.BlockSpec(blk, lambda i, h: (i, 0, RET_HEADS + h)),
        pl.BlockSpec(blk, lambda i, h: (i, 0, 2 * RET_HEADS + h)),
        pl.BlockSpec(blk, lambda i, h: (i, 0, h)),
        pl.BlockSpec((1, RET_DV), lambda i, h: (0, h)),
    ]
    args = [lg, rqkv, rqkv, rqkv, rg, gn]
    if has_s0:
        in_specs.append(st_spec)
        args.append(s0)
    return pl.pallas_call(
        functools.partial(_ret_kernel, has_s0=has_s0, bb=bb, seq=seq),
        out_shape=(
            jax.ShapeDtypeStruct((b, seq, RET_W), BF16),
            jax.ShapeDtypeStruct((b, 2, RET_HEADS, RET_DK, RET_DV), F32),
        ),
        grid=(b // bb, RET_HEADS),
        in_specs=in_specs,
        out_specs=(pl.BlockSpec(blk, lambda i, h: (i, 0, h)), st_spec),
        scratch_shapes=[
            pltpu.VMEM((N_TAB, RET_CHUNK, RET_CHUNK), F32),
            pltpu.VMEM((bb, seq, RET_DV), F32),
        ],
        compiler_params=_params(("parallel", "parallel")),
    )(*args)


ST_M, ST_ALPHA, N_ST = range(3)


def _attn_kernel(*refs, has_ctx):
    if has_ctx:
        q_ref, k_ref, v_ref, kc_ref, vc_ref, o_ref, s_ref, p_ref, acc_ref, st_ref, mx_ref = refs
    else:
        q_ref, k_ref, v_ref, o_ref, s_ref, p_ref, acc_ref, st_ref, mx_ref = refs
    kv_head = pl.program_id(1)
    qb = q_ref.shape[1]
    kc = s_ref.shape[1]
    n_own = k_ref.shape[0] // kc
    n_chunk = n_own + (1 if has_ctx else 0)
    q_t = jnp.concatenate([q_ref[h * ATT_HEAD_DIM:(h + 1) * ATT_HEAD_DIM, :] for h in range(ATT_GROUP)], axis=1)
    q_both = jnp.concatenate([q_t] * ATT_KV_HEADS, axis=0)
    row = lax.broadcasted_iota(jnp.int32, q_both.shape, 0)
    q_pad = jnp.where(row // ATT_HEAD_DIM == kv_head, q_both, jnp.zeros_like(q_both))
    ones_rows = (lax.broadcasted_iota(jnp.int32, (ATT_V_ROWS - ATT_HEAD_DIM, kc), 0) == 0).astype(BF16)

    def keys(j):
        if has_ctx and isinstance(j, int) and j == n_own:
            return kc_ref[...]
        return k_ref[pl.ds(pl.multiple_of(j * kc, kc), kc), :]

    def values(j):
        v = vc_ref[...] if has_ctx and isinstance(j, int) and j == n_own else v_ref[j]
        return jnp.concatenate([v, ones_rows], axis=0)

    def put_scores(j, slot):
        s = _dot(keys(j), q_pad)
        s_ref[slot] = s
        mx_ref[slot] = jnp.max(s, axis=0, keepdims=True)

    def weighted_values(j, slot):
        return _dot(values(j), p_ref[slot])

    def phase(j, slot, first=False, last=False):
        if not last:
            put_scores(j + 1, (slot + 1) % ATT_RING)
        if not first:
            acc_ref[...] = st_ref[ST_ALPHA] * acc_ref[...] + weighted_values(j - 1, (slot - 1) % ATT_RING)
        m = st_ref[ST_M]
        m_new = jnp.maximum(m, mx_ref[slot])
        st_ref[ST_ALPHA] = jnp.exp2(m - m_new)
        st_ref[ST_M] = m_new
        p_ref[slot] = jnp.exp2((s_ref[slot] - m_new).astype(BF16))

    st_ref[ST_M] = jnp.full(st_ref.shape[1:], -jnp.inf, F32)
    acc_ref[...] = jnp.zeros_like(acc_ref)
    put_scores(0, 0)
    phase(0, 0, first=True, last=n_chunk == 1)
    n_loop = max(n_own - 2, 0) // ATT_RING

    def revolution(i, carry):
        for t in range(1, ATT_RING + 1):
            phase(ATT_RING * i + t, t % ATT_RING)
        return carry

    lax.fori_loop(0, n_loop, revolution, 0)
    for j in range(1 + ATT_RING * n_loop, n_chunk):
        phase(j, j % ATT_RING, last=j == n_chunk - 1)
    acc = st_ref[ST_ALPHA] * acc_ref[...] + weighted_values(n_chunk - 1, (n_chunk - 1) % ATT_RING)
    out = (acc[0:ATT_HEAD_DIM] / acc[ATT_HEAD_DIM:ATT_HEAD_DIM + 1]).astype(BF16)
    for h in range(ATT_GROUP):
        o_ref[h * ATT_HEAD_DIM:(h + 1) * ATT_HEAD_DIM, :] = out[:, h * qb:(h + 1) * qb]


def _attention(q_t, k, v_t, k_ctx, v_ctx_t, batch):
    n = q_t.shape[1]
    seq = n // batch
    qb, kc = min(ATT_QB, seq), ATT_KC
    n_q = seq // qb
    has_ctx = k_ctx is not None
    q_spec = pl.BlockSpec((ATT_GROUP * ATT_HEAD_DIM, qb), lambda b, g, j: (g, b * n_q + j))
    in_specs = [
        q_spec,
        pl.BlockSpec((seq, ATT_KV_W), lambda b, g, j: (b, 0)),
        pl.BlockSpec((seq // kc, ATT_HEAD_DIM, kc), lambda b, g, j: (b, g, 0)),
    ]
    args = [q_t, k, v_t]
    if has_ctx:
        in_specs += [
            pl.BlockSpec((kc, ATT_KV_W), lambda b, g, j: (b, 0)),
            pl.BlockSpec((ATT_HEAD_DIM, kc), lambda b, g, j: (b * ATT_KV_HEADS + g, 0)),
        ]
        args += [k_ctx, v_ctx_t]
    width = ATT_GROUP * qb
    return pl.pallas_call(
        functools.partial(_attn_kernel, has_ctx=has_ctx),
        out_shape=jax.ShapeDtypeStruct(q_t.shape, BF16),
        grid=(batch, ATT_KV_HEADS, n_q),
        in_specs=in_specs,
        out_specs=q_spec,
        scratch_shapes=[
            pltpu.VMEM((ATT_RING, kc, width), F32),
            pltpu.VMEM((ATT_RING, kc, width), BF16),
            pltpu.VMEM((ATT_V_ROWS, width), F32),
            pltpu.VMEM((N_ST, 1, width), F32),
            pltpu.VMEM((ATT_RING, 1, width), F32),
        ],
        compiler_params=_params(("parallel", "parallel", "parallel")),
    )(*args)


def _s5_in_kernel(x_ref, mod_ref, g1_ref, w_ref, o_ref):
    h = _modulated_norm(x_ref[...], g1_ref[...], mod_ref).astype(BF16)
    o_ref[0] = _dot_nt(w_ref[...], h).astype(BF16)


def _s5_in(x, mod, mod_rows, g1, w_su_t):
    n = x.shape[0]
    n_chunk = n // S5_T
    rows = S5_CHUNK_ROWS
    mod_row = _mod_row_fn(mod_rows, rows * S5_T)
    return pl.pallas_call(
        _s5_in_kernel,
        out_shape=jax.ShapeDtypeStruct((S5_T, SSM_CH, n_chunk), BF16),
        grid=(n_chunk // rows, S5_T),
        in_specs=[
            pl.BlockSpec((rows, D_MODEL), lambda i, s: (i, s)),
            pl.BlockSpec((1, N_MOD, D_MODEL), lambda i, s: (mod_row(i), 0, 0)),
            _const_spec((1, D_MODEL)),
            _const_spec((SSM_CH, D_MODEL)),
        ],
        out_specs=pl.BlockSpec((1, SSM_CH, rows), lambda i, s: (s, 0, i)),
        compiler_params=_params(("parallel", "parallel")),
    )(x.reshape(n_chunk, S5_T * D_MODEL), mod, g1, w_su_t)


def _s5_kernel(*refs, two_pass):
    if two_pass:
        u_ref, tt_ref, wx_ref, wy_ref, a_ref, aseg_ref, s0_ref, y_ref, v_ref, init_ref = refs
    else:
        u_ref, tt_ref, wx_ref, wy_ref, a_ref, y_ref, fin_ref, v_ref = refs
    gb = tt_ref.shape[1]
    n_chunk = u_ref.shape[2] // S5_ROWS
    grp = SSM_GROUP

    def toeplitz(d, j):
        strip = tt_ref[d, j]
        width = S5_T * grp
        starts = [(S5_T - 1 - t) * grp for t in range(S5_T)]
        return jnp.concatenate([strip[:, a:a + width] for a in starts], axis=0)

    def put_outputs(j, yt, accumulate):
        for t in range(S5_T):
            if accumulate:
                y_ref[t, j * grp:(j + 1) * grp, :] += yt[t * grp:(t + 1) * grp, :]
            else:
                y_ref[t, j * grp:(j + 1) * grp, :] = yt[t * grp:(t + 1) * grp, :]

    for j in range(gb):
        z = jnp.concatenate([u_ref[s, j * grp:(j + 1) * grp, :] for s in range(S5_T)], axis=0)
        put_outputs(j, _dot(toeplitz(0, j), z) + _dot(toeplitz(1, j), z), accumulate=False)
        for d in range(2):
            v_ref[d, j] = _dot_tn(z, wx_ref[d, j])

    def times(coef_ref, d, j, x):
        lanes = slice(j * S5_GL, (j + 1) * S5_GL)
        return coef_ref[d, 0:1, lanes] * x + coef_ref[d, 1:2, lanes] * pltpu.roll(x, SSM_STATE, axis=1)

    def scan(init, keep_states):
        xs = [list(init[0]), list(init[1])]
        for s in range(n_chunk):
            for d in range(2):
                c = s if d == 0 else n_chunk - 1 - s
                rows = pl.ds(c, S5_ROWS, stride=n_chunk)
                for j in range(gb):
                    v = v_ref[d, j, rows, :]
                    if keep_states:
                        v_ref[d, j, rows, :] = xs[d][j]
                    xs[d][j] = times(a_ref, d, j, xs[d][j]) + v
        return xs

    zero = [[jnp.zeros((S5_ROWS, S5_GL), F32)] * gb] * 2
    if two_pass:
        local = scan(zero, keep_states=False)
        n_batch = s0_ref.shape[1]
        n_seg = S5_ROWS // n_batch
        for d in range(2):
            order = range(n_seg) if d == 0 else range(n_seg - 1, -1, -1)
            for j in range(gb):
                for b in range(n_batch):
                    cur = s0_ref[d, b:b + 1, j * S5_GL:(j + 1) * S5_GL]
                    for sgm in order:
                        r = b * n_seg + sgm
                        init_ref[d, j, r:r + 1, :] = cur
                        cur = times(aseg_ref, d, j, cur) + local[d][j][r:r + 1, :]
        scan([[init_ref[d, j] for j in range(gb)] for d in range(2)], keep_states=True)
    else:
        final = scan(zero, keep_states=True)
        for d in range(2):
            for j in range(gb):
                fin_ref[d, :, j * S5_GL:(j + 1) * S5_GL] = final[d][j]
    for j in range(gb):
        put_outputs(j, _dot_nt(wy_ref[0, j], v_ref[0, j].astype(BF16))
                    + _dot_nt(wy_ref[1, j], v_ref[1, j].astype(BF16)), accumulate=True)


def _s5(u, tt, wx, wy, a_t, a_seg, s0):
    two_pass = s0 is not None
    n_rows = u.shape[2]
    gb = S5_GB
    io_spec = pl.BlockSpec((S5_T, gb * SSM_GROUP, n_rows), lambda i: (0, i, 0))
    coef_spec = pl.BlockSpec((2, 2, gb * S5_GL), lambda i: (0, 0, i))
    in_specs = [
        io_spec,
        pl.BlockSpec((2, gb) + tt.shape[2:], lambda i: (0, i, 0, 0)),
        pl.BlockSpec((2, gb) + wx.shape[2:], lambda i: (0, i, 0, 0)),
        pl.BlockSpec((2, gb) + wy.shape[2:], lambda i: (0, i, 0, 0)),
        coef_spec,
    ]
    args = [u, tt, wx, wy, a_t]
    y_shape = jax.ShapeDtypeStruct(u.shape, F32)
    scratch = [pltpu.VMEM((2, gb, n_rows, S5_GL), F32)]
    if two_pass:
        in_specs += [coef_spec, pl.BlockSpec((2, s0.shape[1], gb * S5_GL), lambda i: (0, 0, i))]
        args += [a_seg, s0]
        out_shape, out_specs = y_shape, io_spec
        scratch.append(pltpu.VMEM((2, gb, S5_ROWS, S5_GL), F32))
    else:
        out_shape = (y_shape, jax.ShapeDtypeStruct((2, S5_ROWS, SSM_GROUPS * S5_GL), F32))
        out_specs = (io_spec, pl.BlockSpec((2, S5_ROWS, gb * S5_GL), lambda i: (0, 0, i)))
    return pl.pallas_call(
        functools.partial(_s5_kernel, two_pass=two_pass),
        out_shape=out_shape,
        grid=(SSM_GROUPS // gb,),
        in_specs=in_specs,
        out_specs=out_specs,
        scratch_shapes=scratch,
        compiler_params=_params(("parallel",)),
    )(*args)


def _s5_out_kernel(y_ref, u_ref, d_ref, w_ref, o_ref):
    y = y_ref[0] + d_ref[...] * u_ref[0].astype(F32)
    glu = _dot_tn(jax.nn.gelu(y).astype(BF16), w_ref[...])
    o_ref[...] = (glu[:, 0:D_MODEL] * jax.nn.sigmoid(glu[:, D_MODEL:2 * D_MODEL])).astype(BF16)


def _s5_out(y, u, d_skip, w_glu):
    n_chunk = y.shape[2]
    rows = S5_CHUNK_ROWS
    io_spec = pl.BlockSpec((1, SSM_CH, rows), lambda i, s: (s, 0, i))
    out = pl.pallas_call(
        _s5_out_kernel,
        out_shape=jax.ShapeDtypeStruct((n_chunk, S5_T * D_MODEL), BF16),
        grid=(n_chunk // rows, S5_T),
        in_specs=[io_spec, io_spec, _const_spec((SSM_CH, 1)), _const_spec((SSM_CH, 2 * D_MODEL))],
        out_specs=pl.BlockSpec((rows, D_MODEL), lambda i, s: (i, s)),
        compiler_params=_params(("parallel", "parallel")),
    )(y, u, d_skip, w_glu)
    return out.reshape(n_chunk * S5_T, D_MODEL)


def _route(probs):
    epg = EXPERTS_PER_GROUP
    groups = [probs[g * epg:(g + 1) * epg] for g in range(N_EXPERT_GROUPS)]
    scores = []
    for grp in groups:
        best = None
        for a in range(epg):
            for b in range(a + 1, epg):
                pair = grp[a] + grp[b]
                best = pair if best is None else jnp.maximum(best, pair)
        scores.append(best)
    top_score = scores[0]
    top_group = jnp.zeros_like(scores[0], dtype=jnp.int32)
    for g in range(1, N_EXPERT_GROUPS):
        better = scores[g] > top_score
        top_score = jnp.where(better, scores[g], top_score)
        top_group = jnp.where(better, g, top_group)
    sel = []
    for j in range(epg):
        v = groups[0][j]
        for g in range(1, N_EXPERT_GROUPS):
            v = jnp.where(top_group == g, groups[g][j], v)
        sel.append(v)
    v1 = sel[0]
    i1 = jnp.zeros_like(top_group)
    for j in range(1, epg):
        better = sel[j] > v1
        v1 = jnp.where(better, sel[j], v1)
        i1 = jnp.where(better, j, i1)
    v2 = jnp.full_like(v1, -1.0)
    i2 = jnp.zeros_like(top_group)
    for j in range(epg):
        better = (sel[j] > v2) & (i1 != j)
        v2 = jnp.where(better, sel[j], v2)
        i2 = jnp.where(better, j, i2)
    total = v1 + v2
    w1 = v1 / total
    w2 = v2 / total
    return top_group, [jnp.where(i1 == j, w1, jnp.where(i2 == j, w2, 0.0)) for j in range(epg)]


def _merge_kernel(x_ref, ro_ref, att_ref, ssm_ref, gates_ref, mod_ref, wro_ref, wao_ref, wout_ref, g2_ref, wrt_ref,
                  br_ref, x1_ref, h2_ref, meta_ref, cnt_ref):
    ret_branch = _dot(ro_ref[...], wro_ref[...])
    att_branch = _dot_tn(att_ref[...], wao_ref[...])

    def gate(j):
        return jax.nn.sigmoid(gates_ref[:, j * D_MODEL:(j + 1) * D_MODEL].astype(F32))

    merged = gate(0) * ret_branch + gate(1) * att_branch + gate(2) * ssm_ref[...].astype(F32)
    x1 = x_ref[...] + mod_ref[0, 2:3, :] * _dot(merged.astype(BF16), wout_ref[...])
    x1_ref[...] = x1
    xn = x1 * lax.rsqrt(jnp.mean(x1 * x1, axis=-1, keepdims=True) + EPS) * g2_ref[...]
    h2 = xn * (1.0 + mod_ref[0, 4:5, :]) + mod_ref[0, 3:4, :]
    h2_ref[...] = h2.astype(BF16)

    logits = lax.dot_general(wrt_ref[...], h2, (((1,), (1,)), ((), ())), precision=lax.Precision.HIGHEST,
                             preferred_element_type=F32) + br_ref[...]
    e = jnp.exp(logits - jnp.max(logits, axis=0, keepdims=True))
    probs = e / jnp.sum(e, axis=0, keepdims=True)
    top_group, weights = _route([probs[j:j + 1, :] for j in range(N_EXPERTS)])

    tm = top_group.shape[1]

    @pl.when(pl.program_id(0) % (MOE_PART // tm) == 0)
    def _():
        cnt_ref[...] = jnp.zeros_like(cnt_ref)

    zero_row = jnp.zeros((1, tm), F32)
    onehot = jnp.concatenate([(top_group == g).astype(F32) for g in range(N_EXPERT_GROUPS)]
                             + [zero_row] * (META_ROWS - N_EXPERT_GROUPS), axis=0)
    lane = lax.broadcasted_iota(jnp.int32, onehot.shape, 1)
    incl = onehot
    shift = 1
    while shift < tm:
        incl = incl + jnp.where(lane >= shift, pltpu.roll(incl, shift, axis=1), 0.0)
        shift *= 2
    before = cnt_ref[0, :, 0:1]
    rank = jnp.sum(onehot * (incl - onehot + before), axis=0, keepdims=True)
    cnt_ref[0] = jnp.broadcast_to(before + jnp.sum(onehot, axis=1, keepdims=True), cnt_ref.shape[1:])
    meta_ref[...] = jnp.concatenate([top_group.astype(F32), rank] + weights
                                    + [zero_row] * (META_ROWS - 2 - EXPERTS_PER_GROUP), axis=0)


def _merge(x, ro, att_t, ssm, gates, mod, mod_rows, wro, wao, wout, g2, wrt, br):
    n = x.shape[0]
    tm = TM_PROJ
    mod_row = _mod_row_fn(mod_rows, tm)
    row = lambda i: (i, 0)
    in_specs = [
        pl.BlockSpec((tm, D_MODEL), row),
        pl.BlockSpec((tm, RET_W), row),
        pl.BlockSpec((ATT_W, tm), lambda i: (0, i)),
        pl.BlockSpec((tm, D_MODEL), row),
        pl.BlockSpec((tm, 3 * D_MODEL), row),
        pl.BlockSpec((1, N_MOD, D_MODEL), lambda i: (mod_row(i), 0, 0)),
        _const_spec((RET_W, D_MODEL)),
        _const_spec((ATT_W, D_MODEL)),
        _const_spec((D_MODEL, D_MODEL)),
        _const_spec((1, D_MODEL)),
        _const_spec((N_EXPERTS, D_MODEL)),
        _const_spec((N_EXPERTS, 1)),
    ]
    return pl.pallas_call(
        _merge_kernel,
        out_shape=(
            jax.ShapeDtypeStruct((n, D_MODEL), F32),
            jax.ShapeDtypeStruct((n, D_MODEL), BF16),
            jax.ShapeDtypeStruct((META_ROWS, n), F32),
            jax.ShapeDtypeStruct((n // MOE_PART, META_ROWS, LANES), F32),
        ),
        grid=(n // tm,),
        in_specs=in_specs,
        out_specs=(
            pl.BlockSpec((tm, D_MODEL), row),
            pl.BlockSpec((tm, D_MODEL), row),
            pl.BlockSpec((META_ROWS, tm), lambda i: (0, i)),
            pl.BlockSpec((1, META_ROWS, LANES), lambda i: (i // (MOE_PART // tm), 0, 0)),
        ),
        compiler_params=_params(("arbitrary",)),
    )(x, ro, att_t, ssm, gates, mod, wro, wao, wout, g2, wrt, br)


def _moe_kernel(dest_ref, tile_group_ref, n_valid_ref, *refs, final, n_scatter, n_expert, n_gather):
    if final:
        (h2_ref, wrow_ref, wg_ref, wu_ref, wd_ref, x1_ref, mod_ref, fg_ref, o_ref,
         sorted_ref, wsort_ref, stage_ref, acc_ref) = refs
    else:
        (h2_ref, wrow_ref, wg_ref, wu_ref, wd_ref, x1_ref, mod_ref, o_ref,
         sorted_ref, wsort_ref, stage_ref, acc_ref) = refs
    tm = h2_ref.shape[0]
    epg = EXPERTS_PER_GROUP
    per_part = n_scatter + n_expert + n_gather
    part = pl.program_id(0) // per_part
    i = pl.program_id(0) % per_part

    @pl.when(i == 0)
    def _():
        sorted_ref[...] = jnp.zeros_like(sorted_ref)
        wsort_ref[...] = jnp.zeros_like(wsort_ref)

    @pl.when(i < n_scatter)
    def _():
        stage_ref[...] = h2_ref[...].astype(F32)
        base = (part * n_scatter + i) * tm

        def move(t, carry):
            d = dest_ref[base + t]
            sorted_ref[pl.ds(d, 1), :] = stage_ref[pl.ds(t, 1), :]
            wsort_ref[pl.ds(d, 1), :] = wrow_ref[pl.ds(t, 1), :]
            return carry

        lax.fori_loop(0, tm, move, 0, unroll=8)

    step = i - n_scatter
    tile = step // epg
    ein = step % epg

    @pl.when((step >= 0) & (step < n_expert) & (tile < n_valid_ref[part]))
    def _():
        rows = pl.ds(pl.multiple_of(tile * MOE_TILE, MOE_TILE), MOE_TILE)
        x = sorted_ref[rows, :].astype(BF16)
        g = _dot(x, wg_ref[0])
        u = _dot(x, wu_ref[0])
        w = wsort_ref[rows, :]
        lane = lax.broadcasted_iota(jnp.int32, w.shape, 1)
        w_col = jnp.sum(jnp.where(lane == ein, w, 0.0), axis=1, keepdims=True)
        out = _dot((g * jax.nn.sigmoid(g) * u * w_col).astype(BF16), wd_ref[0])

        @pl.when(ein == 0)
        def _():
            acc_ref[...] = out

        @pl.when(ein > 0)
        def _():
            acc_ref[...] += out

        @pl.when(ein == epg - 1)
        def _():
            sorted_ref[rows, :] = acc_ref[...]

    @pl.when(step >= n_expert)
    def _():
        base = (part * n_gather + step - n_expert) * tm

        def move(t, carry):
            stage_ref[pl.ds(t, 1), :] = sorted_ref[pl.ds(dest_ref[base + t], 1), :]
            return carry

        lax.fori_loop(0, tm, move, 0, unroll=8)
        x2 = x1_ref[...] + mod_ref[0, 5:6, :] * stage_ref[...]
        if final:
            x2 = x2 * lax.rsqrt(jnp.mean(x2 * x2, axis=-1, keepdims=True) + EPS) * fg_ref[...]
        o_ref[...] = x2


def _moe_plan(meta, cnt):
    n = meta.shape[1]
    n_part = cnt.shape[0]
    n_tiles = n // n_part // MOE_TILE + N_EXPERT_GROUPS
    counts = cnt[:, 0:N_EXPERT_GROUPS, 0].astype(jnp.int32)
    padded = (counts + MOE_TILE - 1) // MOE_TILE * MOE_TILE
    ends = jnp.cumsum(padded, axis=1)
    starts = ends - padded
    group = meta[0].astype(jnp.int32).reshape(n_part, -1)
    dest = meta[1].astype(jnp.int32).reshape(n_part, -1)
    for g in range(N_EXPERT_GROUPS):
        dest = dest + jnp.where(group == g, starts[:, g:g + 1], 0)
    tile_start = jnp.arange(n_tiles, dtype=jnp.int32) * MOE_TILE
    tile_group = jnp.sum(tile_start[None, :, None] >= ends[:, None, :], axis=2).astype(jnp.int32)
    tile_group = jnp.minimum(tile_group, N_EXPERT_GROUPS - 1)
    n_valid = (ends[:, N_EXPERT_GROUPS - 1] // MOE_TILE).astype(jnp.int32)
    w_rows = jnp.pad(meta[2:2 + EXPERTS_PER_GROUP].T, ((0, 0), (0, LANES - EXPERTS_PER_GROUP)))
    return dest.reshape(-1), tile_group, n_valid, w_rows


def _moe(h2, meta, cnt, x1, mod, mod_rows, layer, wg, wu, wd, final_g):
    n = h2.shape[0]
    tm = MOE_TILE
    n_part = cnt.shape[0]
    dest, tile_group, n_valid, w_rows = _moe_plan(meta, cnt)
    n_tiles = tile_group.shape[1]
    epg = EXPERTS_PER_GROUP
    n_scatter, n_expert, n_gather = n // n_part // tm, n_tiles * epg, n // n_part // tm
    per_part = n_scatter + n_expert + n_gather
    mod_row = _mod_row_fn(mod_rows, tm)
    final = final_g is not None

    def scatter_tile(i):
        return (i // per_part) * n_scatter + jnp.minimum(i % per_part, n_scatter - 1)

    def gather_tile(i):
        return (i // per_part) * n_gather + jnp.clip(i % per_part - n_scatter - n_expert, 0, n_gather - 1)

    def expert(i, tg, nv):
        part = i // per_part
        step = jnp.clip(i % per_part - n_scatter, 0, n_expert - 1)
        tile = step // epg
        in_use = tile < nv[part]
        group = tg[part * n_tiles + jnp.minimum(tile, nv[part] - 1)]
        return layer * N_EXPERTS + group * epg + jnp.where(in_use, step % epg, epg - 1)

    in_specs = [
        pl.BlockSpec((tm, D_MODEL), lambda i, d, tg, nv: (scatter_tile(i), 0)),
        pl.BlockSpec((tm, LANES), lambda i, d, tg, nv: (scatter_tile(i), 0)),
        pl.BlockSpec((1, D_MODEL, D_FF), lambda i, d, tg, nv: (expert(i, tg, nv), 0, 0)),
        pl.BlockSpec((1, D_MODEL, D_FF), lambda i, d, tg, nv: (expert(i, tg, nv), 0, 0)),
        pl.BlockSpec((1, D_FF, D_MODEL), lambda i, d, tg, nv: (expert(i, tg, nv), 0, 0)),
        pl.BlockSpec((tm, D_MODEL), lambda i, d, tg, nv: (gather_tile(i), 0)),
        pl.BlockSpec((1, N_MOD, D_MODEL), lambda i, d, tg, nv: (mod_row(gather_tile(i)), 0, 0)),
    ]
    args = [h2, w_rows, wg, wu, wd, x1, mod]
    if final:
        in_specs.append(pl.BlockSpec((1, D_MODEL), lambda i, d, tg, nv: (0, 0)))
        args.append(final_g)
    grid_spec = pltpu.PrefetchScalarGridSpec(
        num_scalar_prefetch=3,
        grid=(n_part * per_part,),
        in_specs=in_specs,
        out_specs=pl.BlockSpec((tm, D_MODEL), lambda i, d, tg, nv: (gather_tile(i), 0)),
        scratch_shapes=[
            pltpu.VMEM((n_tiles * MOE_TILE, D_MODEL), F32),
            pltpu.VMEM((n_tiles * MOE_TILE, LANES), F32),
            pltpu.VMEM((tm, D_MODEL), F32),
            pltpu.VMEM((MOE_TILE, D_MODEL), F32),
        ],
    )
    return pl.pallas_call(
        functools.partial(_moe_kernel, final=final, n_scatter=n_scatter, n_expert=n_expert, n_gather=n_gather),
        out_shape=jax.ShapeDtypeStruct((n, D_MODEL), F32),
        grid_spec=grid_spec,
        compiler_params=_params(("arbitrary",)),
    )(dest, tile_group.reshape(-1), n_valid, *args)


def _block_diag(m):
    g, r, c = m.shape
    eye = jnp.eye(g, dtype=m.dtype)
    return (eye[:, None, :, None] * m[:, :, None, :]).reshape(g * r, g * c)


def _s5_direction_params(a_re, a_im, log_dt, b_re, b_im, c_re, c_im, reverse):
    n_g = a_re.shape[0]
    width = S5_T * SSM_GROUP
    hp = lax.Precision.HIGHEST
    step = jnp.exp(log_dt)[:, None]
    mag = jnp.exp(a_re * step)
    ar = mag * jnp.cos(a_im * step)
    ai = mag * jnp.sin(a_im * step)
    den = a_re * a_re + a_im * a_im
    fr = ((ar - 1.0) * a_re + ai * a_im) / den
    fi = (ai * a_re - (ar - 1.0) * a_im) / den
    bbr = (fr[:, :, None] * b_re - fi[:, :, None] * b_im).transpose(0, 2, 1)
    bbi = (fr[:, :, None] * b_im + fi[:, :, None] * b_re).transpose(0, 2, 1)
    pr, pi = [jnp.ones_like(ar)], [jnp.zeros_like(ar)]
    for _ in range(S5_T):
        r, i = pr[-1], pi[-1]
        pr.append(r * ar - i * ai)
        pi.append(r * ai + i * ar)
    pr, pi = jnp.stack(pr)[:, :, None, :], jnp.stack(pi)[:, :, None, :]
    mr = pr * bbr - pi * bbi
    mi = pr * bbi + pi * bbr
    cr = c_re * pr - c_im * pi
    ci = c_re * pi + c_im * pr
    kern = (jnp.einsum('gop,kgip->goki', c_re, mr[:S5_T], precision=hp)
            - jnp.einsum('gop,kgip->goki', c_im, mi[:S5_T], precision=hp))
    pad = jnp.zeros_like(kern[:, :, 1:, :])
    strip = jnp.concatenate([pad, kern] if reverse else [kern[:, :, ::-1, :], pad], axis=2)
    tt = jnp.concatenate([strip, pad[:, :, 0:1, :]], axis=2).reshape(n_g, SSM_GROUP, 2 * width)
    wx = jnp.concatenate([mr[:S5_T], mi[:S5_T]], axis=3)
    wx = (wx if reverse else wx[::-1]).transpose(1, 0, 2, 3).reshape(n_g, width, S5_GL)
    wy = jnp.concatenate([cr[1:], -ci[1:]], axis=3)
    wy = (wy[::-1] if reverse else wy).transpose(1, 0, 2, 3).reshape(n_g, width, S5_GL)

    def coef_rows(r, i):
        return jnp.stack([jnp.concatenate([r, r], axis=1).reshape(-1), jnp.concatenate([-i, i], axis=1).reshape(-1)])

    sr, si = pr[S5_T, :, 0, :], pi[S5_T, :, 0, :]
    a_t = coef_rows(sr, si)
    for _ in range(int(math.log2(S5_SEG // S5_T))):
        sr, si = sr * sr - si * si, 2.0 * sr * si
    return tt.astype(BF16), wx.astype(BF16), wy.astype(BF16), a_t, coef_rows(sr, si)


def _rope_tables(n_tokens):
    rows = n_tokens // GRID_W
    row = jnp.repeat(jnp.arange(rows, dtype=F32), GRID_W)
    col = jnp.tile(jnp.arange(GRID_W, dtype=F32), rows)
    n_freq = ATT_HEAD_DIM // 4
    inv = ROPE_THETA ** (-jnp.arange(n_freq, dtype=F32) / n_freq)
    ang = jnp.concatenate([row[:, None] * inv, col[:, None] * inv], axis=-1)
    cos = jnp.repeat(jnp.cos(ang), 2, axis=-1)
    sin = jnp.repeat(jnp.sin(ang), 2, axis=-1) * jnp.tile(jnp.array([-1.0, 1.0], F32), ATT_HEAD_DIM // 2)
    return cos, sin


def _swap_pairs(a, axis=-1):
    axis = axis % a.ndim
    pairs = a.reshape(a.shape[:axis] + (a.shape[axis] // 2, 2) + a.shape[axis + 1:])
    return jnp.flip(pairs, axis=axis + 1).reshape(a.shape)


def _layer(x, batch, mod, mod_rows, p, ctx):
    n = x.shape[0]
    seq = n // batch
    latent = ctx is not None
    outs = _inproj(x, mod, mod_rows, p, latent)
    rqkv, rg, gates, q_t, k16, v_t = outs[:6]

    ro, ret_state = _retention(rqkv.reshape(batch, seq, 3 * RET_W), rg.reshape(batch, seq, RET_W), p['ret_lg'],
                               p['ret_gn'], ctx[2] if latent else None, bb=1 if latent else 8)

    if latent:
        k_ctx = ctx[0].reshape(batch * ATT_KC, ATT_KV_W).astype(BF16)
        v_ctx_t = ctx[1].reshape(batch, ATT_KC, ATT_KV_W).transpose(0, 2, 1).reshape(batch * ATT_KV_W, ATT_KC)
        att_t = _attention(q_t, k16, v_t, k_ctx, v_ctx_t.astype(BF16), batch)
    else:
        att_t = _attention(q_t, k16, v_t, None, None, batch)

    u = _s5_in(x, mod, mod_rows, p['g1'], p['w_su_t'])
    if latent:
        s0 = jnp.stack([ctx[3], ctx[4]], axis=3).transpose(1, 0, 2, 3, 4).reshape(2, batch, SSM_GROUPS * S5_GL)
        y = _s5(u, p['s5_tt'], p['s5_wx'], p['s5_wy'], p['s5_a'], p['s5_aseg'], s0)
        ssm_state = None
    else:
        y, ssm_state = _s5(u, p['s5_tt'], p['s5_wx'], p['s5_wy'], p['s5_a'], None, None)
        ssm_state = ssm_state.reshape(2, batch, SSM_GROUPS, 2, SSM_STATE)
    ssm = _s5_out(y, u, p['ssm_d'], p['w_glu'])

    x1, h2, meta, cnt = _merge(x, ro.reshape(n, RET_W), att_t, ssm, gates, mod, mod_rows, p['w_ret_out'],
                               p['w_att_out'], p['w_out'], p['g2'], p['w_router_t'], p['b_router'])
    x2 = _moe(h2, meta, cnt, x1, mod, mod_rows, p['layer'], p['w_gate'], p['w_up'], p['w_down'], p.get('final_g'))
    if latent:
        return x2, None
    return x2, (outs[6], outs[7], ret_state, ssm_state)


def kernel(x_prompt, x_sample, c, cache_attn_k, cache_attn_v, state_ret, state_ssm_re, state_ssm_im, c_ctx, w_ada,
           b_ada, norm1_g, norm2_g, w_in, ret_lg_f, ret_lg_b, ret_norm_g, w_ret_out, att_q_norm_g, att_k_norm_g,
           w_att_out, a_re_f, a_im_f, a_re_b, a_im_b, log_dt_f, log_dt_b, ssm_b_re, ssm_b_im, ssm_c_re, ssm_c_im,
           ssm_d, w_glu, w_out, w_router, b_router, w_gate, w_up, w_down, final_norm_g):
    batch, seq, _ = x_prompt.shape
    dec_batch, dec_seq, _ = x_sample.shape
    assert batch * seq == S5_ROWS * S5_SEG and dec_batch * dec_seq == S5_ROWS * S5_SEG
    assert seq == S5_SEG and dec_seq % S5_SEG == 0
    assert cache_attn_k.shape[2] == ATT_KC and seq % ATT_KC == 0 and dec_seq % ATT_QB == 0

    c_rows = jnp.zeros((ADA_ROWS, D_MODEL), F32).at[0].set(c_ctx).at[1:1 + dec_batch].set(c)
    mod_all = _ada(c_rows, w_ada, b_ada).reshape(DEPTH, ADA_ROWS, N_MOD, D_MODEL)

    cos, sin = _rope_tables(dec_seq)
    ones = _block_diag(jnp.ones((ATT_KV_HEADS, ATT_HEAD_DIM, ATT_HEAD_DIM), BF16))
    w_in16 = w_in.astype(BF16)
    w_gate16 = w_gate.astype(BF16).reshape(DEPTH * N_EXPERTS, D_MODEL, D_FF)
    w_up16 = w_up.astype(BF16).reshape(DEPTH * N_EXPERTS, D_MODEL, D_FF)
    w_down16 = w_down.astype(BF16).reshape(DEPTH * N_EXPERTS, D_FF, D_MODEL)
    s5_fwd = jax.vmap(functools.partial(_s5_direction_params, reverse=False))(
        a_re_f, a_im_f, log_dt_f, ssm_b_re, ssm_b_im, ssm_c_re, ssm_c_im)
    s5_bwd = jax.vmap(functools.partial(_s5_direction_params, reverse=True))(
        a_re_b, a_im_b, log_dt_b, ssm_b_re, ssm_b_im, ssm_c_re, ssm_c_im)

    xp = x_prompt.reshape(batch * seq, D_MODEL)
    xs = x_sample.reshape(dec_batch * dec_seq, D_MODEL)
    ks_, vs_, rets_, ssms_ = [], [], [], []
    for l in range(DEPTH):
        w_l = w_in[l]
        gq = att_q_norm_g[l][:, None]
        gk = jnp.tile(att_k_norm_g[l], ATT_KV_HEADS)[None]
        wq_t = w_l[:, C_AQ:C_AK].T
        fwd = [a[l] for a in s5_fwd]
        bwd = [a[l] for a in s5_bwd]
        p = {
            'layer': l, 'g1': norm1_g[l][None], 'g2': norm2_g[l][None], 'w_in': w_in16, 'ones': ones,
            'wq_t': wq_t.astype(BF16), 'wv_t': w_l[:, C_AV:C_SU].T.astype(BF16), 'gq': gq, 'gk': gk,
            'wqs_t': _swap_pairs(wq_t, axis=0).astype(BF16), 'wk_sw': _swap_pairs(w_l[:, C_AK:C_AV]).astype(BF16),
            'gqs': _swap_pairs(gq, axis=0), 'gks': _swap_pairs(gk),
            'cos': jnp.tile(cos, (1, ATT_KV_HEADS)), 'sin': jnp.tile(sin, (1, ATT_KV_HEADS)),
            'cos_t': cos.T, 'sin_t': sin.T,
            'ret_lg': jnp.stack([ret_lg_f[l], ret_lg_b[l]]), 'ret_gn': ret_norm_g[l][None],
            'w_ret_out': w_ret_out[l].astype(BF16), 'w_att_out': w_att_out[l].astype(BF16),
            'w_su_t': w_l[:, C_SU:C_G].T.astype(BF16),
            's5_tt': jnp.stack([fwd[0], bwd[0]]), 's5_wx': jnp.stack([fwd[1], bwd[1]]),
            's5_wy': jnp.stack([fwd[2], bwd[2]]), 's5_a': jnp.stack([fwd[3], bwd[3]]),
            's5_aseg': jnp.stack([fwd[4], bwd[4]]),
            'ssm_d': ssm_d[l][:, None], 'w_glu': w_glu[l].astype(BF16), 'w_out': w_out[l].astype(BF16),
            'w_router_t': w_router.T, 'b_router': b_router[:, None],
            'w_gate': w_gate16, 'w_up': w_up16, 'w_down': w_down16,
        }
        if l == DEPTH - 1:
            p['final_g'] = final_norm_g[None]
        mod = mod_all[l]

        xp, (k_c, v_c, st_r, st_s) = _layer(xp, batch, mod, (0, batch * seq), p, None)
        ks_.append(k_c.reshape(batch, seq, ATT_KV_HEADS, ATT_HEAD_DIM))
        vs_.append(v_c.reshape(batch, seq, ATT_KV_HEADS, ATT_HEAD_DIM))
        rets_.append(st_r)
        ssms_.append(st_s)

        ctx = (cache_attn_k[:, l], cache_attn_v[:, l], state_ret[:, l], state_ssm_re[:, l], state_ssm_im[:, l])
        xs, _ = _layer(xs, dec_batch, mod, (1, dec_seq), p, ctx)

    ssm_all = jnp.stack(ssms_, axis=1)
    ssm_all = ssm_all.transpose(2, 1, 0, 3, 4, 5)
    new_re = ssm_all[..., 0, :]
    new_im = ssm_all[..., 1, :]
    return (xp.reshape(batch, seq, D_MODEL), xs.reshape(dec_batch, dec_seq, D_MODEL),
            jnp.stack(ks_, axis=1), jnp.stack(vs_, axis=1), jnp.stack(rets_, axis=1), new_re, new_im)
```

```python
import functools
import math

import jax
import jax.numpy as jnp
from jax import lax
from jax.experimental import pallas as pl
from jax.experimental.pallas import tpu as pltpu

F32 = jnp.float32
BF16 = jnp.bfloat16

D_MODEL = 1024
DEPTH = 2
GRID_W = 64
RET_HEADS = 4
RET_DK = 128
RET_DV = 128
RET_CHUNK = 256
RET_W = RET_HEADS * RET_DV
ATT_HEADS = 8
ATT_KV_HEADS = 2
ATT_GROUP = ATT_HEADS // ATT_KV_HEADS
ATT_HEAD_DIM = 64
ATT_W = ATT_HEADS * ATT_HEAD_DIM
ATT_KV_W = ATT_KV_HEADS * ATT_HEAD_DIM
ROPE_THETA = 10000.0
SSM_CH = 512
SSM_GROUP = 16
SSM_GROUPS = SSM_CH // SSM_GROUP
SSM_STATE = 64
N_EXPERTS = 16
N_EXPERT_GROUPS = 4
EXPERTS_PER_GROUP = N_EXPERTS // N_EXPERT_GROUPS
D_FF = 512
N_MOD = 6
EPS = 1e-6

C_RQ, C_RK, C_RV, C_RG = 0, 512, 1024, 1536
C_AQ, C_AK, C_AV, C_SU, C_G, C_END = 2048, 2560, 2688, 2816, 3328, 6400

VMEM_LIMIT = 52 * 1024 * 1024

ATT_Q_SCALE = ATT_HEAD_DIM ** -0.5 * math.log2(math.e)
ATT_QB = 512
ATT_KC = 256
ATT_V_ROWS = ATT_HEAD_DIM + 16
ATT_RING = 4

TM_PROJ = 512
RET_UNROLL = 4
MOE_TILE = 512
MOE_PART = 4096
META_ROWS = 8
S5_ROWS = 32
S5_SEG = 256
S5_T = 16
S5_GL = 2 * SSM_STATE
S5_GB = 8
S5_TM = 2048
S5_PITCH = 24
LANES = 128
STAGE_ROWS = 256
ADA_ROWS = 8
ADA_TN = 1536


def _dot(a, b):
    return jnp.dot(a, b, preferred_element_type=F32)


def _dot_nt(a, b):
    return lax.dot_general(a, b, (((1,), (1,)), ((), ())), preferred_element_type=F32)


def _dot_tn(a, b):
    return lax.dot_general(a, b, (((0,), (0,)), ((), ())), preferred_element_type=F32)


def _const_spec(shape):
    n = len(shape)
    return pl.BlockSpec(shape, lambda *_: (0,) * n, pipeline_mode=pl.Buffered(1))


def _params(sem, vmem=VMEM_LIMIT):
    return pltpu.CompilerParams(dimension_semantics=sem, vmem_limit_bytes=vmem)


def _mod_row_fn(mod_rows, tm):
    base, per_row = mod_rows
    return lambda i: base + (i * tm) // per_row


def _modulated_norm(x, g, mod_ref):
    xn = x * lax.rsqrt(jnp.mean(x * x, axis=-1, keepdims=True) + EPS) * g
    return xn * (1.0 + mod_ref[0, 1:2, :]) + mod_ref[0, 0:1, :]


def _ada_kernel(c_ref, w_ref, b_ref, o_ref):
    c = c_ref[...]
    a = (c * jax.nn.sigmoid(c)).astype(BF16)
    o_ref[0] = _dot(a, w_ref[0].astype(BF16)) + b_ref[0]


def _ada(c_rows, w_ada, b_ada):
    n_col = N_MOD * D_MODEL
    return pl.pallas_call(
        _ada_kernel,
        out_shape=jax.ShapeDtypeStruct((DEPTH, ADA_ROWS, n_col), F32),
        grid=(DEPTH, n_col // ADA_TN),
        in_specs=[
            pl.BlockSpec((ADA_ROWS, D_MODEL), lambda l, j: (0, 0)),
            pl.BlockSpec((1, D_MODEL, ADA_TN), lambda l, j: (l, 0, j)),
            pl.BlockSpec((1, 1, ADA_TN), lambda l, j: (l, 0, j)),
        ],
        out_specs=pl.BlockSpec((1, ADA_ROWS, ADA_TN), lambda l, j: (l, 0, j)),
        compiler_params=_params(("arbitrary", "arbitrary")),
    )(c_rows, w_ada, b_ada.reshape(DEPTH, 1, n_col))


def _head_mean_sq(z, ones):
    z2 = z * z
    hi = z2.astype(BF16)
    lo = (z2 - hi.astype(F32)).astype(BF16)
    return (_dot(hi, ones) + _dot(lo, ones)) * (1.0 / ATT_HEAD_DIM)


def _inproj_kernel(*refs, latent):
    if latent:
        (x_ref, mod_ref, g1_ref, w_ref, wqt_ref, wvt_ref, ones_ref, gq_ref, gk_ref,
         wqst_ref, wks_ref, gqs_ref, gks_ref, cos_ref, sin_ref, cost_ref, sint_ref,
         rqkv_ref, rg_ref, gates_ref, qt_ref, k16_ref, vt_ref) = refs
    else:
        (x_ref, mod_ref, g1_ref, w_ref, wqt_ref, wvt_ref, ones_ref, gq_ref, gk_ref,
         rqkv_ref, rg_ref, gates_ref, qt_ref, k16_ref, vt_ref, ak_ref, av_ref) = refs
    tm = x_ref.shape[0]
    h = _modulated_norm(x_ref[...], g1_ref[...], mod_ref).astype(BF16)

    def seg(a, b):
        return _dot(h, w_ref[0, :, a:b])

    rqkv_ref[:, 0:RET_W] = (seg(C_RQ, C_RK) * (RET_DK ** -0.5)).astype(BF16)
    rqkv_ref[:, RET_W:3 * RET_W] = seg(C_RK, C_RG).astype(BF16)
    rg_ref[...] = seg(C_RG, C_AQ).astype(BF16)
    for j in range(3):
        gates_ref[:, j * D_MODEL:(j + 1) * D_MODEL] = seg(C_G + j * D_MODEL, C_G + (j + 1) * D_MODEL).astype(BF16)

    zk = seg(C_AK, C_AV)
    inv_k = lax.rsqrt(_head_mean_sq(zk, ones_ref[...]) + EPS)
    yk = zk * inv_k * gk_ref[...]
    zq = _dot_nt(wqt_ref[...], h).reshape(ATT_HEADS, ATT_HEAD_DIM, tm)
    inv_q = lax.rsqrt(jnp.mean(zq * zq, axis=1, keepdims=True) + EPS)
    yq = zq * inv_q * gq_ref[...]
    if latent:
        yk_sw = _dot(h, wks_ref[...]) * inv_k * gks_ref[...]
        yk = yk * cos_ref[...] + yk_sw * sin_ref[...]
        zq_sw = _dot_nt(wqst_ref[...], h).reshape(ATT_HEADS, ATT_HEAD_DIM, tm)
        yq = yq * cost_ref[...] + (zq_sw * inv_q * gqs_ref[...]) * sint_ref[...]
    qt_ref[...] = (yq * ATT_Q_SCALE).reshape(ATT_W, tm).astype(BF16)
    k16_ref[...] = yk.astype(BF16)
    vt = _dot_nt(wvt_ref[...], h).astype(BF16)
    for c in range(tm // ATT_KC):
        vt_ref[c] = vt[:, c * ATT_KC:(c + 1) * ATT_KC]
    if not latent:
        ak_ref[...] = yk
        av_ref[...] = seg(C_AV, C_SU)


def _inproj(x, mod, mod_rows, p, latent):
    n = x.shape[0]
    tm = TM_PROJ
    mod_row = _mod_row_fn(mod_rows, tm)
    row = lambda i: (i, 0)
    in_specs = [
        pl.BlockSpec((tm, D_MODEL), row),
        pl.BlockSpec((1, N_MOD, D_MODEL), lambda i: (mod_row(i), 0, 0)),
        _const_spec((1, D_MODEL)),
        pl.BlockSpec((1, D_MODEL, C_END), lambda i: (p['layer'], 0, 0), pipeline_mode=pl.Buffered(1)),
        _const_spec((ATT_W, D_MODEL)),
        _const_spec((ATT_KV_W, D_MODEL)),
        _const_spec((ATT_KV_W, ATT_KV_W)),
        _const_spec((ATT_HEAD_DIM, 1)),
        _const_spec((1, ATT_KV_W)),
    ]
    args = [x, mod, p['g1'], p['w_in'], p['wq_t'], p['wv_t'], p['ones'], p['gq'], p['gk']]
    if latent:
        n_pos = p['cos'].shape[0] // tm
        in_specs += [
            _const_spec((ATT_W, D_MODEL)),
            _const_spec((D_MODEL, ATT_KV_W)),
            _const_spec((ATT_HEAD_DIM, 1)),
            _const_spec((1, ATT_KV_W)),
            pl.BlockSpec((tm, ATT_KV_W), lambda i: (i % n_pos, 0)),
            pl.BlockSpec((tm, ATT_KV_W), lambda i: (i % n_pos, 0)),
            pl.BlockSpec((ATT_HEAD_DIM, tm), lambda i: (0, i % n_pos)),
            pl.BlockSpec((ATT_HEAD_DIM, tm), lambda i: (0, i % n_pos)),
        ]
        args += [p['wqs_t'], p['wk_sw'], p['gqs'], p['gks'], p['cos'], p['sin'], p['cos_t'], p['sin_t']]
    out_shape = [
        jax.ShapeDtypeStruct((n, 3 * RET_W), BF16),
        jax.ShapeDtypeStruct((n, RET_W), BF16),
        jax.ShapeDtypeStruct((n, 3 * D_MODEL), BF16),
        jax.ShapeDtypeStruct((ATT_W, n), BF16),
        jax.ShapeDtypeStruct((n, ATT_KV_W), BF16),
        jax.ShapeDtypeStruct((n // ATT_KC, ATT_KV_W, ATT_KC), BF16),
    ]
    out_specs = [
        pl.BlockSpec((tm, 3 * RET_W), row),
        pl.BlockSpec((tm, RET_W), row),
        pl.BlockSpec((tm, 3 * D_MODEL), row),
        pl.BlockSpec((ATT_W, tm), lambda i: (0, i)),
        pl.BlockSpec((tm, ATT_KV_W), row),
        pl.BlockSpec((tm // ATT_KC, ATT_KV_W, ATT_KC), lambda i: (i, 0, 0)),
    ]
    if not latent:
        out_shape += [jax.ShapeDtypeStruct((n, ATT_KV_W), F32)] * 2
        out_specs += [pl.BlockSpec((tm, ATT_KV_W), row)] * 2
    return pl.pallas_call(
        functools.partial(_inproj_kernel, latent=latent),
        out_shape=tuple(out_shape),
        grid=(n // tm,),
        in_specs=in_specs,
        out_specs=tuple(out_specs),
        compiler_params=_params(("parallel",)),
    )(*args)


T_DEC_F, T_DEC_B, T_XI_F, T_ZETA_F, T_XI_B, T_ZETA_B, T_CD_F, T_CD_B, N_TAB = range(9)


def _ret_kernel(*refs, has_s0, bb, seq):
    if has_s0:
        lg_ref, q_ref, k_ref, v_ref, rg_ref, gn_ref, s0_ref, o_ref, st_ref, tab_ref, acc_ref = refs
    else:
        lg_ref, q_ref, k_ref, v_ref, rg_ref, gn_ref, o_ref, st_ref, tab_ref, acc_ref = refs
    c = RET_CHUNK
    head = pl.program_id(1)
    lgf = lg_ref[0, head]
    lgb = lg_ref[1, head]
    t = lax.broadcasted_iota(jnp.int32, (c, c), 0).astype(F32)
    s = lax.broadcasted_iota(jnp.int32, (c, c), 1).astype(F32)
    tab_ref[T_DEC_F] = jnp.where(t >= s, jnp.exp(lgf * jnp.maximum(t - s, 0.0)), 0.0)
    tab_ref[T_DEC_B] = jnp.where(s >= t, jnp.exp(lgb * jnp.maximum(s - t, 0.0)), 0.0)
    tab_ref[T_XI_F] = jnp.exp(lgf * (t + 1.0))
    tab_ref[T_ZETA_F] = jnp.exp(lgf * (c - 1.0 - t))
    tab_ref[T_XI_B] = jnp.exp(lgb * (c - t))
    tab_ref[T_ZETA_B] = jnp.exp(lgb * t)
    tab_ref[T_CD_F] = jnp.exp(lgf * (c + 0.0 * t))
    tab_ref[T_CD_B] = jnp.exp(lgb * (c + 0.0 * t))
    if has_s0:
        st_ref[...] = s0_ref[...]
    else:
        st_ref[...] = jnp.zeros_like(st_ref)
    acc_ref[...] = jnp.zeros_like(acc_ref)
    n_chunk = seq // c

    def one_direction(rb, d, off, t_dec, t_xi, t_zeta, t_cd):
        rows = pl.ds(off, c)
        q = q_ref[rb, rows, :]
        k = k_ref[rb, rows, :]
        v = v_ref[rb, rows, :]
        p = (_dot_nt(q, k) * tab_ref[t_dec]).astype(BF16)
        st = st_ref[rb, d, 0]
        q_xi = (q.astype(F32) * tab_ref[t_xi, :, 0:RET_DK]).astype(BF16)
        o = _dot(jnp.concatenate([p, q_xi], axis=1), jnp.concatenate([v, st.astype(BF16)], axis=0))
        acc_ref[rb, rows, :] += o
        kz = (k.astype(F32) * tab_ref[t_zeta, :, 0:RET_DK]).astype(BF16)
        st_ref[rb, d, 0] = tab_ref[t_cd, 0:RET_DK, 0:RET_DV] * st + _dot_tn(kz, v)

    def body(i, carry):
        off_f = pl.multiple_of(i * c, c)
        off_b = pl.multiple_of((n_chunk - 1 - i) * c, c)
        for rb in range(bb):
            one_direction(rb, 0, off_f, T_DEC_F, T_XI_F, T_ZETA_F, T_CD_F)
            one_direction(rb, 1, off_b, T_DEC_B, T_XI_B, T_ZETA_B, T_CD_B)
        return carry

    lax.fori_loop(0, n_chunk, body, 0, unroll=RET_UNROLL if bb == 1 else 1)

    def finish(i, carry):
        rows = pl.ds(pl.multiple_of(i * c, c), c)
        for rb in range(bb):
            o = acc_ref[rb, rows, :]
            ro = o * lax.rsqrt(jnp.mean(o * o, axis=-1, keepdims=True) + EPS) * gn_ref[...]
            g = rg_ref[rb, rows, :].astype(F32)
            o_ref[rb, rows, :] = (ro * (g * jax.nn.sigmoid(g))).astype(BF16)
        return carry

    lax.fori_loop(0, n_chunk, finish, 0)


def _retention(rqkv, rg, lg, gn, s0, bb):
    b, seq, _ = rqkv.shape
    has_s0 = s0 is not None
    blk = (bb, seq, RET_DK)
    st_spec = pl.BlockSpec((bb, 2, 1, RET_DK, RET_DV), lambda i, h: (i, 0, h, 0, 0))
    in_specs = [
        pl.BlockSpec(memory_space=pltpu.SMEM),
        pl.BlockSpec(blk, lambda i, h: (i, 0, h)),
        pl.BlockSpec(blk, lambda i, h: (i, 0, RET_HEADS + h)),
        pl.BlockSpec(blk, lambda i, h: (i, 0, 2 * RET_HEADS + h)),
        pl.BlockSpec(blk, lambda i, h: (i, 0, h)),
        pl.BlockSpec((1, RET_DV), lambda i, h: (0, h)),
    ]
    args = [lg, rqkv, rqkv, rqkv, rg, gn]
    if has_s0:
        in_specs.append(st_spec)
        args.append(s0)
    return pl.pallas_call(
        functools.partial(_ret_kernel, has_s0=has_s0, bb=bb, seq=seq),
        out_shape=(
            jax.ShapeDtypeStruct((b, seq, RET_W), BF16),
            jax.ShapeDtypeStruct((b, 2, RET_HEADS, RET_DK, RET_DV), F32),
        ),
        grid=(b // bb, RET_HEADS),
        in_specs=in_specs,
        out_specs=(pl.BlockSpec(blk, lambda i, h: (i, 0, h)), st_spec),
        scratch_shapes=[
            pltpu.VMEM((N_TAB, RET_CHUNK, RET_CHUNK), F32),
            pltpu.VMEM((bb, seq, RET_DV), F32),
        ],
        compiler_params=_params(("parallel", "parallel")),
    )(*args)


ST_M, ST_ALPHA, N_ST = range(3)


def _attn_kernel(*refs, has_ctx):
    if has_ctx:
        q_ref, k_ref, v_ref, kc_ref, vc_ref, o_ref, s_ref, p_ref, acc_ref, st_ref, mx_ref = refs
    else:
        q_ref, k_ref, v_ref, o_ref, s_ref, p_ref, acc_ref, st_ref, mx_ref = refs
    kv_head = pl.program_id(1)
    qb = q_ref.shape[1]
    kc = s_ref.shape[1]
    n_own = k_ref.shape[0] // kc
    n_chunk = n_own + (1 if has_ctx else 0)
    q_t = jnp.concatenate([q_ref[h * ATT_HEAD_DIM:(h + 1) * ATT_HEAD_DIM, :] for h in range(ATT_GROUP)], axis=1)
    q_both = jnp.concatenate([q_t] * ATT_KV_HEADS, axis=0)
    row = lax.broadcasted_iota(jnp.int32, q_both.shape, 0)
    q_pad = jnp.where(row // ATT_HEAD_DIM == kv_head, q_both, jnp.zeros_like(q_both))
    ones_rows = (lax.broadcasted_iota(jnp.int32, (ATT_V_ROWS - ATT_HEAD_DIM, kc), 0) == 0).astype(BF16)

    def keys(j):
        if has_ctx and isinstance(j, int) and j == n_own:
            return kc_ref[...]
        return k_ref[pl.ds(pl.multiple_of(j * kc, kc), kc), :]

    def values(j):
        v = vc_ref[...] if has_ctx and isinstance(j, int) and j == n_own else v_ref[j]
        return jnp.concatenate([v, ones_rows], axis=0)

    def put_scores(j, slot):
        s = _dot(keys(j), q_pad)
        s_ref[slot] = s
        mx_ref[slot] = jnp.max(s, axis=0, keepdims=True)

    def weighted_values(j, slot):
        return _dot(values(j), p_ref[slot])

    def phase(j, slot, first=False, last=False):
        if not last:
            put_scores(j + 1, (slot + 1) % ATT_RING)
        if not first:
            acc_ref[...] = st_ref[ST_ALPHA] * acc_ref[...] + weighted_values(j - 1, (slot - 1) % ATT_RING)
        m = st_ref[ST_M]
        m_new = jnp.maximum(m, mx_ref[slot])
        st_ref[ST_ALPHA] = jnp.exp2(m - m_new)
        st_ref[ST_M] = m_new
        p_ref[slot] = jnp.exp2((s_ref[slot] - m_new).astype(BF16))

    st_ref[ST_M] = jnp.full(st_ref.shape[1:], -jnp.inf, F32)
    acc_ref[...] = jnp.zeros_like(acc_ref)
    put_scores(0, 0)
    phase(0, 0, first=True, last=n_chunk == 1)
    n_loop = max(n_own - 2, 0) // ATT_RING

    def revolution(i, carry):
        for t in range(1, ATT_RING + 1):
            phase(ATT_RING * i + t, t % ATT_RING)
        return carry

    lax.fori_loop(0, n_loop, revolution, 0)
    for j in range(1 + ATT_RING * n_loop, n_chunk):
        phase(j, j % ATT_RING, last=j == n_chunk - 1)
    acc = st_ref[ST_ALPHA] * acc_ref[...] + weighted_values(n_chunk - 1, (n_chunk - 1) % ATT_RING)
    out = (acc[0:ATT_HEAD_DIM] / acc[ATT_HEAD_DIM:ATT_HEAD_DIM + 1]).astype(BF16)
    for h in range(ATT_GROUP):
        o_ref[h * ATT_HEAD_DIM:(h + 1) * ATT_HEAD_DIM, :] = out[:, h * qb:(h + 1) * qb]


def _attention(q_t, k, v_t, k_ctx, v_ctx_t, batch):
    n = q_t.shape[1]
    seq = n // batch
    qb, kc = min(ATT_QB, seq), ATT_KC
    n_q = seq // qb
    has_ctx = k_ctx is not None
    q_spec = pl.BlockSpec((ATT_GROUP * ATT_HEAD_DIM, qb), lambda b, g, j: (g, b * n_q + j))
    in_specs = [
        q_spec,
        pl.BlockSpec((seq, ATT_KV_W), lambda b, g, j: (b, 0)),
        pl.BlockSpec((seq // kc, ATT_HEAD_DIM, kc), lambda b, g, j: (b, g, 0)),
    ]
    args = [q_t, k, v_t]
    if has_ctx:
        in_specs += [
            pl.BlockSpec((kc, ATT_KV_W), lambda b, g, j: (b, 0)),
            pl.BlockSpec((ATT_HEAD_DIM, kc), lambda b, g, j: (b * ATT_KV_HEADS + g, 0)),
        ]
        args += [k_ctx, v_ctx_t]
    width = ATT_GROUP * qb
    return pl.pallas_call(
        functools.partial(_attn_kernel, has_ctx=has_ctx),
        out_shape=jax.ShapeDtypeStruct(q_t.shape, BF16),
        grid=(batch, ATT_KV_HEADS, n_q),
        in_specs=in_specs,
        out_specs=q_spec,
        scratch_shapes=[
            pltpu.VMEM((ATT_RING, kc, width), F32),
            pltpu.VMEM((ATT_RING, kc, width), BF16),
            pltpu.VMEM((ATT_V_ROWS, width), F32),
            pltpu.VMEM((N_ST, 1, width), F32),
            pltpu.VMEM((ATT_RING, 1, width), F32),
        ],
        compiler_params=_params(("parallel", "parallel", "parallel")),
    )(*args)


def _s5_in_kernel(x_ref, mod_ref, g1_ref, w_ref, o_ref, h_ref):
    tm = x_ref.shape[0]
    n_chunk = o_ref.shape[2]
    n_col = D_MODEL // LANES
    per_stage = STAGE_ROWS // S5_T

    def stage(i, carry):
        h = _modulated_norm(x_ref[pl.ds(pl.multiple_of(i * STAGE_ROWS, STAGE_ROWS), STAGE_ROWS), :], g1_ref[...],
                            mod_ref)
        for c in range(per_stage):
            rows = pl.ds(pl.multiple_of((i * per_stage + c) * S5_PITCH, S5_PITCH), S5_T)
            for j in range(n_col):
                h_ref[j, rows, :] = h[c * S5_T:(c + 1) * S5_T, j * LANES:(j + 1) * LANES]
        return carry

    lax.fori_loop(0, tm // STAGE_ROWS, stage, 0)
    for s in range(S5_T):
        h = jnp.concatenate([h_ref[j, pl.ds(s, n_chunk, stride=S5_PITCH), :] for j in range(n_col)], axis=1)
        o_ref[s] = _dot_nt(w_ref[...], h.astype(BF16)).astype(BF16)


def _s5_in(x, mod, mod_rows, g1, w_su_t):
    n = x.shape[0]
    tm = S5_TM
    mod_row = _mod_row_fn(mod_rows, tm)
    return pl.pallas_call(
        _s5_in_kernel,
        out_shape=jax.ShapeDtypeStruct((S5_T, SSM_CH, n // S5_T), BF16),
        grid=(n // tm,),
        in_specs=[
            pl.BlockSpec((tm, D_MODEL), lambda i: (i, 0)),
            pl.BlockSpec((1, N_MOD, D_MODEL), lambda i: (mod_row(i), 0, 0)),
            _const_spec((1, D_MODEL)),
            _const_spec((SSM_CH, D_MODEL)),
        ],
        out_specs=pl.BlockSpec((S5_T, SSM_CH, tm // S5_T), lambda i: (0, 0, i)),
        scratch_shapes=[pltpu.VMEM((D_MODEL // LANES, tm // S5_T * S5_PITCH, LANES), F32)],
        compiler_params=_params(("parallel",)),
    )(x, mod, g1, w_su_t)


def _s5_kernel(*refs, two_pass):
    if two_pass:
        u_ref, tt_ref, wx_ref, wy_ref, a_ref, aseg_ref, s0_ref, y_ref, v_ref, init_ref = refs
    else:
        u_ref, tt_ref, wx_ref, wy_ref, a_ref, y_ref, fin_ref, v_ref = refs
    gb = tt_ref.shape[1]
    n_chunk = u_ref.shape[2] // S5_ROWS
    grp = SSM_GROUP

    def toeplitz(d, j):
        strip = tt_ref[d, j]
        width = S5_T * grp
        starts = [(S5_T - 1 - t) * grp for t in range(S5_T)]
        return jnp.concatenate([strip[:, a:a + width] for a in starts], axis=0)

    def put_outputs(j, yt, accumulate):
        for t in range(S5_T):
            if accumulate:
                y_ref[t, j * grp:(j + 1) * grp, :] += yt[t * grp:(t + 1) * grp, :]
            else:
                y_ref[t, j * grp:(j + 1) * grp, :] = yt[t * grp:(t + 1) * grp, :]

    for j in range(gb):
        z = jnp.concatenate([u_ref[s, j * grp:(j + 1) * grp, :] for s in range(S5_T)], axis=0)
        put_outputs(j, _dot(toeplitz(0, j), z) + _dot(toeplitz(1, j), z), accumulate=False)
        for d in range(2):
            v_ref[d, j] = _dot_tn(z, wx_ref[d, j])

    def times(coef_ref, d, j, x):
        lanes = slice(j * S5_GL, (j + 1) * S5_GL)
        return coef_ref[d, 0:1, lanes] * x + coef_ref[d, 1:2, lanes] * pltpu.roll(x, SSM_STATE, axis=1)

    def scan(init, keep_states):
        xs = [list(init[0]), list(init[1])]
        for s in range(n_chunk):
            for d in range(2):
                c = s if d == 0 else n_chunk - 1 - s
                rows = pl.ds(c, S5_ROWS, stride=n_chunk)
                for j in range(gb):
                    v = v_ref[d, j, rows, :]
                    if keep_states:
                        v_ref[d, j, rows, :] = xs[d][j]
                    xs[d][j] = times(a_ref, d, j, xs[d][j]) + v
        return xs

    zero = [[jnp.zeros((S5_ROWS, S5_GL), F32)] * gb] * 2
    if two_pass:
        local = scan(zero, keep_states=False)
        n_batch = s0_ref.shape[1]
        n_seg = S5_ROWS // n_batch
        for d in range(2):
            order = range(n_seg) if d == 0 else range(n_seg - 1, -1, -1)
            for j in range(gb):
                for b in range(n_batch):
                    cur = s0_ref[d, b:b + 1, j * S5_GL:(j + 1) * S5_GL]
                    for sgm in order:
                        r = b * n_seg + sgm
                        init_ref[d, j, r:r + 1, :] = cur
                        cur = times(aseg_ref, d, j, cur) + local[d][j][r:r + 1, :]
        scan([[init_ref[d, j] for j in range(gb)] for d in range(2)], keep_states=True)
    else:
        final = scan(zero, keep_states=True)
        for d in range(2):
            for j in range(gb):
                fin_ref[d, :, j * S5_GL:(j + 1) * S5_GL] = final[d][j]
    for j in range(gb):
        put_outputs(j, _dot_nt(wy_ref[0, j], v_ref[0, j].astype(BF16))
                    + _dot_nt(wy_ref[1, j], v_ref[1, j].astype(BF16)), accumulate=True)


def _s5(u, tt, wx, wy, a_t, a_seg, s0):
    two_pass = s0 is not None
    n_rows = u.shape[2]
    gb = S5_GB
    io_spec = pl.BlockSpec((S5_T, gb * SSM_GROUP, n_rows), lambda i: (0, i, 0))
    coef_spec = pl.BlockSpec((2, 2, gb * S5_GL), lambda i: (0, 0, i))
    in_specs = [
        io_spec,
        pl.BlockSpec((2, gb) + tt.shape[2:], lambda i: (0, i, 0, 0)),
        pl.BlockSpec((2, gb) + wx.shape[2:], lambda i: (0, i, 0, 0)),
        pl.BlockSpec((2, gb) + wy.shape[2:], lambda i: (0, i, 0, 0)),
        coef_spec,
    ]
    args = [u, tt, wx, wy, a_t]
    y_shape = jax.ShapeDtypeStruct(u.shape, F32)
    scratch = [pltpu.VMEM((2, gb, n_rows, S5_GL), F32)]
    if two_pass:
        in_specs += [coef_spec, pl.BlockSpec((2, s0.shape[1], gb * S5_GL), lambda i: (0, 0, i))]
        args += [a_seg, s0]
        out_shape, out_specs = y_shape, io_spec
        scratch.append(pltpu.VMEM((2, gb, S5_ROWS, S5_GL), F32))
    else:
        out_shape = (y_shape, jax.ShapeDtypeStruct((2, S5_ROWS, SSM_GROUPS * S5_GL), F32))
        out_specs = (io_spec, pl.BlockSpec((2, S5_ROWS, gb * S5_GL), lambda i: (0, 0, i)))
    return pl.pallas_call(
        functools.partial(_s5_kernel, two_pass=two_pass),
        out_shape=out_shape,
        grid=(SSM_GROUPS // gb,),
        in_specs=in_specs,
        out_specs=out_specs,
        scratch_shapes=scratch,
        compiler_params=_params(("parallel",)),
    )(*args)


def _s5_out_kernel(y_ref, u_ref, d_ref, w_ref, o_ref, buf_ref):
    n_chunk = y_ref.shape[2]
    n_col = D_MODEL // LANES
    for s in range(S5_T):
        y = y_ref[s] + d_ref[...] * u_ref[s].astype(F32)
        glu = _dot_tn(jax.nn.gelu(y).astype(BF16), w_ref[...])
        out = glu[:, 0:D_MODEL] * jax.nn.sigmoid(glu[:, D_MODEL:2 * D_MODEL])
        for j in range(n_col):
            buf_ref[j, pl.ds(s, n_chunk, stride=S5_PITCH), :] = out[:, j * LANES:(j + 1) * LANES]

    def unstage(c, carry):
        src = pl.ds(pl.multiple_of(c * S5_PITCH, S5_PITCH), S5_T)
        dst = pl.ds(pl.multiple_of(c * S5_T, S5_T), S5_T)
        o_ref[dst, :] = jnp.concatenate([buf_ref[j, src, :] for j in range(n_col)], axis=1).astype(BF16)
        return carry

    lax.fori_loop(0, n_chunk, unstage, 0, unroll=8)


def _s5_out(y, u, d_skip, w_glu):
    n = y.shape[2] * S5_T
    tm = S5_TM
    io_spec = pl.BlockSpec((S5_T, SSM_CH, tm // S5_T), lambda i: (0, 0, i))
    return pl.pallas_call(
        _s5_out_kernel,
        out_shape=jax.ShapeDtypeStruct((n, D_MODEL), BF16),
        grid=(n // tm,),
        in_specs=[io_spec, io_spec, _const_spec((SSM_CH, 1)), _const_spec((SSM_CH, 2 * D_MODEL))],
        out_specs=pl.BlockSpec((tm, D_MODEL), lambda i: (i, 0)),
        scratch_shapes=[pltpu.VMEM((D_MODEL // LANES, tm // S5_T * S5_PITCH, LANES), F32)],
        compiler_params=_params(("parallel",)),
    )(y, u, d_skip, w_glu)


def _route(probs):
    epg = EXPERTS_PER_GROUP
    groups = [probs[g * epg:(g + 1) * epg] for g in range(N_EXPERT_GROUPS)]
    scores = []
    for grp in groups:
        best = None
        for a in range(epg):
            for b in range(a + 1, epg):
                pair = grp[a] + grp[b]
                best = pair if best is None else jnp.maximum(best, pair)
        scores.append(best)
    top_score = scores[0]
    top_group = jnp.zeros_like(scores[0], dtype=jnp.int32)
    for g in range(1, N_EXPERT_GROUPS):
        better = scores[g] > top_score
        top_score = jnp.where(better, scores[g], top_score)
        top_group = jnp.where(better, g, top_group)
    sel = []
    for j in range(epg):
        v = groups[0][j]
        for g in range(1, N_EXPERT_GROUPS):
            v = jnp.where(top_group == g, groups[g][j], v)
        sel.append(v)
    v1 = sel[0]
    i1 = jnp.zeros_like(top_group)
    for j in range(1, epg):
        better = sel[j] > v1
        v1 = jnp.where(better, sel[j], v1)
        i1 = jnp.where(better, j, i1)
    v2 = jnp.full_like(v1, -1.0)
    i2 = jnp.zeros_like(top_group)
    for j in range(epg):
        better = (sel[j] > v2) & (i1 != j)
        v2 = jnp.where(better, sel[j], v2)
        i2 = jnp.where(better, j, i2)
    total = v1 + v2
    w1 = v1 / total
    w2 = v2 / total
    return top_group, [jnp.where(i1 == j, w1, jnp.where(i2 == j, w2, 0.0)) for j in range(epg)]


def _merge_kernel(x_ref, ro_ref, att_ref, ssm_ref, gates_ref, mod_ref, wro_ref, wao_ref, wout_ref, g2_ref, wrt_ref,
                  br_ref, x1_ref, h2_ref, meta_ref, cnt_ref):
    ret_branch = _dot(ro_ref[...], wro_ref[...])
    att_branch = _dot_tn(att_ref[...], wao_ref[...])

    def gate(j):
        return jax.nn.sigmoid(gates_ref[:, j * D_MODEL:(j + 1) * D_MODEL].astype(F32))

    merged = gate(0) * ret_branch + gate(1) * att_branch + gate(2) * ssm_ref[...].astype(F32)
    x1 = x_ref[...] + mod_ref[0, 2:3, :] * _dot(merged.astype(BF16), wout_ref[...])
    x1_ref[...] = x1
    xn = x1 * lax.rsqrt(jnp.mean(x1 * x1, axis=-1, keepdims=True) + EPS) * g2_ref[...]
    h2 = xn * (1.0 + mod_ref[0, 4:5, :]) + mod_ref[0, 3:4, :]
    h2_ref[...] = h2.astype(BF16)

    logits = lax.dot_general(wrt_ref[...], h2, (((1,), (1,)), ((), ())), precision=lax.Precision.HIGHEST,
                             preferred_element_type=F32) + br_ref[...]
    e = jnp.exp(logits - jnp.max(logits, axis=0, keepdims=True))
    probs = e / jnp.sum(e, axis=0, keepdims=True)
    top_group, weights = _route([probs[j:j + 1, :] for j in range(N_EXPERTS)])

    tm = top_group.shape[1]

    @pl.when(pl.program_id(0) % (MOE_PART // tm) == 0)
    def _():
        cnt_ref[...] = jnp.zeros_like(cnt_ref)

    zero_row = jnp.zeros((1, tm), F32)
    onehot = jnp.concatenate([(top_group == g).astype(F32) for g in range(N_EXPERT_GROUPS)]
                             + [zero_row] * (META_ROWS - N_EXPERT_GROUPS), axis=0)
    lane = lax.broadcasted_iota(jnp.int32, onehot.shape, 1)
    incl = onehot
    shift = 1
    while shift < tm:
        incl = incl + jnp.where(lane >= shift, pltpu.roll(incl, shift, axis=1), 0.0)
        shift *= 2
    before = cnt_ref[0, :, 0:1]
    rank = jnp.sum(onehot * (incl - onehot + before), axis=0, keepdims=True)
    cnt_ref[0] = jnp.broadcast_to(before + jnp.sum(onehot, axis=1, keepdims=True), cnt_ref.shape[1:])
    meta_ref[...] = jnp.concatenate([top_group.astype(F32), rank] + weights
                                    + [zero_row] * (META_ROWS - 2 - EXPERTS_PER_GROUP), axis=0)


def _merge(x, ro, att_t, ssm, gates, mod, mod_rows, wro, wao, wout, g2, wrt, br):
    n = x.shape[0]
    tm = TM_PROJ
    mod_row = _mod_row_fn(mod_rows, tm)
    row = lambda i: (i, 0)
    in_specs = [
        pl.BlockSpec((tm, D_MODEL), row),
        pl.BlockSpec((tm, RET_W), row),
        pl.BlockSpec((ATT_W, tm), lambda i: (0, i)),
        pl.BlockSpec((tm, D_MODEL), row),
        pl.BlockSpec((tm, 3 * D_MODEL), row),
        pl.BlockSpec((1, N_MOD, D_MODEL), lambda i: (mod_row(i), 0, 0)),
        _const_spec((RET_W, D_MODEL)),
        _const_spec((ATT_W, D_MODEL)),
        _const_spec((D_MODEL, D_MODEL)),
        _const_spec((1, D_MODEL)),
        _const_spec((N_EXPERTS, D_MODEL)),
        _const_spec((N_EXPERTS, 1)),
    ]
    return pl.pallas_call(
        _merge_kernel,
        out_shape=(
            jax.ShapeDtypeStruct((n, D_MODEL), F32),
            jax.ShapeDtypeStruct((n, D_MODEL), BF16),
            jax.ShapeDtypeStruct((META_ROWS, n), F32),
            jax.ShapeDtypeStruct((n // MOE_PART, META_ROWS, LANES), F32),
        ),
        grid=(n // tm,),
        in_specs=in_specs,
        out_specs=(
            pl.BlockSpec((tm, D_MODEL), row),
            pl.BlockSpec((tm, D_MODEL), row),
            pl.BlockSpec((META_ROWS, tm), lambda i: (0, i)),
            pl.BlockSpec((1, META_ROWS, LANES), lambda i: (i // (MOE_PART // tm), 0, 0)),
        ),
        compiler_params=_params(("arbitrary",)),
    )(x, ro, att_t, ssm, gates, mod, wro, wao, wout, g2, wrt, br)


def _moe_kernel(dest_ref, tile_group_ref, n_valid_ref, *refs, final, n_scatter, n_expert, n_gather):
    if final:
        (h2_ref, wrow_ref, wg_ref, wu_ref, wd_ref, x1_ref, mod_ref, fg_ref, o_ref,
         sorted_ref, wsort_ref, stage_ref, acc_ref) = refs
    else:
        (h2_ref, wrow_ref, wg_ref, wu_ref, wd_ref, x1_ref, mod_ref, o_ref,
         sorted_ref, wsort_ref, stage_ref, acc_ref) = refs
    tm = h2_ref.shape[0]
    epg = EXPERTS_PER_GROUP
    per_part = n_scatter + n_expert + n_gather
    part = pl.program_id(0) // per_part
    i = pl.program_id(0) % per_part

    @pl.when(i == 0)
    def _():
        sorted_ref[...] = jnp.zeros_like(sorted_ref)
        wsort_ref[...] = jnp.zeros_like(wsort_ref)

    @pl.when(i < n_scatter)
    def _():
        stage_ref[...] = h2_ref[...].astype(F32)
        base = (part * n_scatter + i) * tm

        def move(t, carry):
            d = dest_ref[base + t]
            sorted_ref[pl.ds(d, 1), :] = stage_ref[pl.ds(t, 1), :]
            wsort_ref[pl.ds(d, 1), :] = wrow_ref[pl.ds(t, 1), :]
            return carry

        lax.fori_loop(0, tm, move, 0, unroll=8)

    step = i - n_scatter
    tile = step // epg
    ein = step % epg

    @pl.when((step >= 0) & (step < n_expert) & (tile < n_valid_ref[part]))
    def _():
        rows = pl.ds(pl.multiple_of(tile * MOE_TILE, MOE_TILE), MOE_TILE)
        x = sorted_ref[rows, :].astype(BF16)
        g = _dot(x, wg_ref[0])
        u = _dot(x, wu_ref[0])
        w = wsort_ref[rows, :]
        lane = lax.broadcasted_iota(jnp.int32, w.shape, 1)
        w_col = jnp.sum(jnp.where(lane == ein, w, 0.0), axis=1, keepdims=True)
        out = _dot((g * jax.nn.sigmoid(g) * u * w_col).astype(BF16), wd_ref[0])

        @pl.when(ein == 0)
        def _():
            acc_ref[...] = out

        @pl.when(ein > 0)
        def _():
            acc_ref[...] += out

        @pl.when(ein == epg - 1)
        def _():
            sorted_ref[rows, :] = acc_ref[...]

    @pl.when(step >= n_expert)
    def _():
        base = (part * n_gather + step - n_expert) * tm

        def move(t, carry):
            stage_ref[pl.ds(t, 1), :] = sorted_ref[pl.ds(dest_ref[base + t], 1), :]
            return carry

        lax.fori_loop(0, tm, move, 0, unroll=8)
        x2 = x1_ref[...] + mod_ref[0, 5:6, :] * stage_ref[...]
        if final:
            x2 = x2 * lax.rsqrt(jnp.mean(x2 * x2, axis=-1, keepdims=True) + EPS) * fg_ref[...]
        o_ref[...] = x2


def _moe_plan(meta, cnt):
    n = meta.shape[1]
    n_part = cnt.shape[0]
    n_tiles = n // n_part // MOE_TILE + N_EXPERT_GROUPS
    counts = cnt[:, 0:N_EXPERT_GROUPS, 0].astype(jnp.int32)
    padded = (counts + MOE_TILE - 1) // MOE_TILE * MOE_TILE
    ends = jnp.cumsum(padded, axis=1)
    starts = ends - padded
    group = meta[0].astype(jnp.int32).reshape(n_part, -1)
    dest = meta[1].astype(jnp.int32).reshape(n_part, -1)
    for g in range(N_EXPERT_GROUPS):
        dest = dest + jnp.where(group == g, starts[:, g:g + 1], 0)
    tile_start = jnp.arange(n_tiles, dtype=jnp.int32) * MOE_TILE
    tile_group = jnp.sum(tile_start[None, :, None] >= ends[:, None, :], axis=2).astype(jnp.int32)
    tile_group = jnp.minimum(tile_group, N_EXPERT_GROUPS - 1)
    n_valid = (ends[:, N_EXPERT_GROUPS - 1] // MOE_TILE).astype(jnp.int32)
    w_rows = jnp.pad(meta[2:2 + EXPERTS_PER_GROUP].T, ((0, 0), (0, LANES - EXPERTS_PER_GROUP)))
    return dest.reshape(-1), tile_group, n_valid, w_rows


def _moe(h2, meta, cnt, x1, mod, mod_rows, layer, wg, wu, wd, final_g):
    n = h2.shape[0]
    tm = MOE_TILE
    n_part = cnt.shape[0]
    dest, tile_group, n_valid, w_rows = _moe_plan(meta, cnt)
    n_tiles = tile_group.shape[1]
    epg = EXPERTS_PER_GROUP
    n_scatter, n_expert, n_gather = n // n_part // tm, n_tiles * epg, n // n_part // tm
    per_part = n_scatter + n_expert + n_gather
    mod_row = _mod_row_fn(mod_rows, tm)
    final = final_g is not None

    def scatter_tile(i):
        return (i // per_part) * n_scatter + jnp.minimum(i % per_part, n_scatter - 1)

    def gather_tile(i):
        return (i // per_part) * n_gather + jnp.clip(i % per_part - n_scatter - n_expert, 0, n_gather - 1)

    def expert(i, tg, nv):
        part = i // per_part
        step = jnp.clip(i % per_part - n_scatter, 0, n_expert - 1)
        tile = step // epg
        in_use = tile < nv[part]
        group = tg[part * n_tiles + jnp.minimum(tile, nv[part] - 1)]
        return layer * N_EXPERTS + group * epg + jnp.where(in_use, step % epg, epg - 1)

    in_specs = [
        pl.BlockSpec((tm, D_MODEL), lambda i, d, tg, nv: (scatter_tile(i), 0)),
        pl.BlockSpec((tm, LANES), lambda i, d, tg, nv: (scatter_tile(i), 0)),
        pl.BlockSpec((1, D_MODEL, D_FF), lambda i, d, tg, nv: (expert(i, tg, nv), 0, 0)),
        pl.BlockSpec((1, D_MODEL, D_FF), lambda i, d, tg, nv: (expert(i, tg, nv), 0, 0)),
        pl.BlockSpec((1, D_FF, D_MODEL), lambda i, d, tg, nv: (expert(i, tg, nv), 0, 0)),
        pl.BlockSpec((tm, D_MODEL), lambda i, d, tg, nv: (gather_tile(i), 0)),
        pl.BlockSpec((1, N_MOD, D_MODEL), lambda i, d, tg, nv: (mod_row(gather_tile(i)), 0, 0)),
    ]
    args = [h2, w_rows, wg, wu, wd, x1, mod]
    if final:
        in_specs.append(pl.BlockSpec((1, D_MODEL), lambda i, d, tg, nv: (0, 0)))
        args.append(final_g)
    grid_spec = pltpu.PrefetchScalarGridSpec(
        num_scalar_prefetch=3,
        grid=(n_part * per_part,),
        in_specs=in_specs,
        out_specs=pl.BlockSpec((tm, D_MODEL), lambda i, d, tg, nv: (gather_tile(i), 0)),
        scratch_shapes=[
            pltpu.VMEM((n_tiles * MOE_TILE, D_MODEL), F32),
            pltpu.VMEM((n_tiles * MOE_TILE, LANES), F32),
            pltpu.VMEM((tm, D_MODEL), F32),
            pltpu.VMEM((MOE_TILE, D_MODEL), F32),
        ],
    )
    return pl.pallas_call(
        functools.partial(_moe_kernel, final=final, n_scatter=n_scatter, n_expert=n_expert, n_gather=n_gather),
        out_shape=jax.ShapeDtypeStruct((n, D_MODEL), F32),
        grid_spec=grid_spec,
        compiler_params=_params(("arbitrary",)),
    )(dest, tile_group.reshape(-1), n_valid, *args)


def _block_diag(m):
    g, r, c = m.shape
    eye = jnp.eye(g, dtype=m.dtype)
    return (eye[:, None, :, None] * m[:, :, None, :]).reshape(g * r, g * c)


def _s5_direction_params(a_re, a_im, log_dt, b_re, b_im, c_re, c_im, reverse):
    n_g = a_re.shape[0]
    width = S5_T * SSM_GROUP
    hp = lax.Precision.HIGHEST
    step = jnp.exp(log_dt)[:, None]
    mag = jnp.exp(a_re * step)
    ar = mag * jnp.cos(a_im * step)
    ai = mag * jnp.sin(a_im * step)
    den = a_re * a_re + a_im * a_im
    fr = ((ar - 1.0) * a_re + ai * a_im) / den
    fi = (ai * a_re - (ar - 1.0) * a_im) / den
    bbr = (fr[:, :, None] * b_re - fi[:, :, None] * b_im).transpose(0, 2, 1)
    bbi = (fr[:, :, None] * b_im + fi[:, :, None] * b_re).transpose(0, 2, 1)
    pr, pi = [jnp.ones_like(ar)], [jnp.zeros_like(ar)]
    for _ in range(S5_T):
        r, i = pr[-1], pi[-1]
        pr.append(r * ar - i * ai)
        pi.append(r * ai + i * ar)
    pr, pi = jnp.stack(pr)[:, :, None, :], jnp.stack(pi)[:, :, None, :]
    mr = pr * bbr - pi * bbi
    mi = pr * bbi + pi * bbr
    cr = c_re * pr - c_im * pi
    ci = c_re * pi + c_im * pr
    kern = (jnp.einsum('gop,kgip->goki', c_re, mr[:S5_T], precision=hp)
            - jnp.einsum('gop,kgip->goki', c_im, mi[:S5_T], precision=hp))
    pad = jnp.zeros_like(kern[:, :, 1:, :])
    strip = jnp.concatenate([pad, kern] if reverse else [kern[:, :, ::-1, :], pad], axis=2)
    tt = jnp.concatenate([strip, pad[:, :, 0:1, :]], axis=2).reshape(n_g, SSM_GROUP, 2 * width)
    wx = jnp.concatenate([mr[:S5_T], mi[:S5_T]], axis=3)
    wx = (wx if reverse else wx[::-1]).transpose(1, 0, 2, 3).reshape(n_g, width, S5_GL)
    wy = jnp.concatenate([cr[1:], -ci[1:]], axis=3)
    wy = (wy[::-1] if reverse else wy).transpose(1, 0, 2, 3).reshape(n_g, width, S5_GL)

    def coef_rows(r, i):
        return jnp.stack([jnp.concatenate([r, r], axis=1).reshape(-1), jnp.concatenate([-i, i], axis=1).reshape(-1)])

    sr, si = pr[S5_T, :, 0, :], pi[S5_T, :, 0, :]
    a_t = coef_rows(sr, si)
    for _ in range(int(math.log2(S5_SEG // S5_T))):
        sr, si = sr * sr - si * si, 2.0 * sr * si
    return tt.astype(BF16), wx.astype(BF16), wy.astype(BF16), a_t, coef_rows(sr, si)


def _rope_tables(n_tokens):
    rows = n_tokens // GRID_W
    row = jnp.repeat(jnp.arange(rows, dtype=F32), GRID_W)
    col = jnp.tile(jnp.arange(GRID_W, dtype=F32), rows)
    n_freq = ATT_HEAD_DIM // 4
    inv = ROPE_THETA ** (-jnp.arange(n_freq, dtype=F32) / n_freq)
    ang = jnp.concatenate([row[:, None] * inv, col[:, None] * inv], axis=-1)
    cos = jnp.repeat(jnp.cos(ang), 2, axis=-1)
    sin = jnp.repeat(jnp.sin(ang), 2, axis=-1) * jnp.tile(jnp.array([-1.0, 1.0], F32), ATT_HEAD_DIM // 2)
    return cos, sin


def _swap_pairs(a, axis=-1):
    axis = axis % a.ndim
    pairs = a.reshape(a.shape[:axis] + (a.shape[axis] // 2, 2) + a.shape[axis + 1:])
    return jnp.flip(pairs, axis=axis + 1).reshape(a.shape)


def _layer(x, batch, mod, mod_rows, p, ctx):
    n = x.shape[0]
    seq = n // batch
    latent = ctx is not None
    outs = _inproj(x, mod, mod_rows, p, latent)
    rqkv, rg, gates, q_t, k16, v_t = outs[:6]

    ro, ret_state = _retention(rqkv.reshape(batch, seq, 3 * RET_W), rg.reshape(batch, seq, RET_W), p['ret_lg'],
                               p['ret_gn'], ctx[2] if latent else None, bb=1 if latent else 8)

    if latent:
        k_ctx = ctx[0].reshape(batch * ATT_KC, ATT_KV_W).astype(BF16)
        v_ctx_t = ctx[1].reshape(batch, ATT_KC, ATT_KV_W).transpose(0, 2, 1).reshape(batch * ATT_KV_W, ATT_KC)
        att_t = _attention(q_t, k16, v_t, k_ctx, v_ctx_t.astype(BF16), batch)
    else:
        att_t = _attention(q_t, k16, v_t, None, None, batch)

    u = _s5_in(x, mod, mod_rows, p['g1'], p['w_su_t'])
    if latent:
        s0 = jnp.stack([ctx[3], ctx[4]], axis=3).transpose(1, 0, 2, 3, 4).reshape(2, batch, SSM_GROUPS * S5_GL)
        y = _s5(u, p['s5_tt'], p['s5_wx'], p['s5_wy'], p['s5_a'], p['s5_aseg'], s0)
        ssm_state = None
    else:
        y, ssm_state = _s5(u, p['s5_tt'], p['s5_wx'], p['s5_wy'], p['s5_a'], None, None)
        ssm_state = ssm_state.reshape(2, batch, SSM_GROUPS, 2, SSM_STATE)
    ssm = _s5_out(y, u, p['ssm_d'], p['w_glu'])

    x1, h2, meta, cnt = _merge(x, ro.reshape(n, RET_W), att_t, ssm, gates, mod, mod_rows, p['w_ret_out'],
                               p['w_att_out'], p['w_out'], p['g2'], p['w_router_t'], p['b_router'])
    x2 = _moe(h2, meta, cnt, x1, mod, mod_rows, p['layer'], p['w_gate'], p['w_up'], p['w_down'], p.get('final_g'))
    if latent:
        return x2, None
    return x2, (outs[6], outs[7], ret_state, ssm_state)


def kernel(x_prompt, x_sample, c, cache_attn_k, cache_attn_v, state_ret, state_ssm_re, state_ssm_im, c_ctx, w_ada,
           b_ada, norm1_g, norm2_g, w_in, ret_lg_f, ret_lg_b, ret_norm_g, w_ret_out, att_q_norm_g, att_k_norm_g,
           w_att_out, a_re_f, a_im_f, a_re_b, a_im_b, log_dt_f, log_dt_b, ssm_b_re, ssm_b_im, ssm_c_re, ssm_c_im,
           ssm_d, w_glu, w_out, w_router, b_router, w_gate, w_up, w_down, final_norm_g):
    batch, seq, _ = x_prompt.shape
    dec_batch, dec_seq, _ = x_sample.shape
    assert batch * seq == S5_ROWS * S5_SEG and dec_batch * dec_seq == S5_ROWS * S5_SEG
    assert seq == S5_SEG and dec_seq % S5_SEG == 0
    assert cache_attn_k.shape[2] == ATT_KC and seq % ATT_KC == 0 and dec_seq % ATT_QB == 0

    c_rows = jnp.zeros((ADA_ROWS, D_MODEL), F32).at[0].set(c_ctx).at[1:1 + dec_batch].set(c)
    mod_all = _ada(c_rows, w_ada, b_ada).reshape(DEPTH, ADA_ROWS, N_MOD, D_MODEL)

    cos, sin = _rope_tables(dec_seq)
    ones = _block_diag(jnp.ones((ATT_KV_HEADS, ATT_HEAD_DIM, ATT_HEAD_DIM), BF16))
    w_in16 = w_in.astype(BF16)
    w_gate16 = w_gate.astype(BF16).reshape(DEPTH * N_EXPERTS, D_MODEL, D_FF)
    w_up16 = w_up.astype(BF16).reshape(DEPTH * N_EXPERTS, D_MODEL, D_FF)
    w_down16 = w_down.astype(BF16).reshape(DEPTH * N_EXPERTS, D_FF, D_MODEL)
    s5_fwd = jax.vmap(functools.partial(_s5_direction_params, reverse=False))(
        a_re_f, a_im_f, log_dt_f, ssm_b_re, ssm_b_im, ssm_c_re, ssm_c_im)
    s5_bwd = jax.vmap(functools.partial(_s5_direction_params, reverse=True))(
        a_re_b, a_im_b, log_dt_b, ssm_b_re, ssm_b_im, ssm_c_re, ssm_c_im)

    xp = x_prompt.reshape(batch * seq, D_MODEL)
    xs = x_sample.reshape(dec_batch * dec_seq, D_MODEL)
    ks_, vs_, rets_, ssms_ = [], [], [], []
    for l in range(DEPTH):
        w_l = w_in[l]
        gq = att_q_norm_g[l][:, None]
        gk = jnp.tile(att_k_norm_g[l], ATT_KV_HEADS)[None]
        wq_t = w_l[:, C_AQ:C_AK].T
        fwd = [a[l] for a in s5_fwd]
        bwd = [a[l] for a in s5_bwd]
        p = {
            'layer': l, 'g1': norm1_g[l][None], 'g2': norm2_g[l][None], 'w_in': w_in16, 'ones': ones,
            'wq_t': wq_t.astype(BF16), 'wv_t': w_l[:, C_AV:C_SU].T.astype(BF16), 'gq': gq, 'gk': gk,
            'wqs_t': _swap_pairs(wq_t, axis=0).astype(BF16), 'wk_sw': _swap_pairs(w_l[:, C_AK:C_AV]).astype(BF16),
            'gqs': _swap_pairs(gq, axis=0), 'gks': _swap_pairs(gk),
            'cos': jnp.tile(cos, (1, ATT_KV_HEADS)), 'sin': jnp.tile(sin, (1, ATT_KV_HEADS)),
            'cos_t': cos.T, 'sin_t': sin.T,
            'ret_lg': jnp.stack([ret_lg_f[l], ret_lg_b[l]]), 'ret_gn': ret_norm_g[l][None],
            'w_ret_out': w_ret_out[l].astype(BF16), 'w_att_out': w_att_out[l].astype(BF16),
            'w_su_t': w_l[:, C_SU:C_G].T.astype(BF16),
            's5_tt': jnp.stack([fwd[0], bwd[0]]), 's5_wx': jnp.stack([fwd[1], bwd[1]]),
            's5_wy': jnp.stack([fwd[2], bwd[2]]), 's5_a': jnp.stack([fwd[3], bwd[3]]),
            's5_aseg': jnp.stack([fwd[4], bwd[4]]),
            'ssm_d': ssm_d[l][:, None], 'w_glu': w_glu[l].astype(BF16), 'w_out': w_out[l].astype(BF16),
            'w_router_t': w_router.T, 'b_router': b_router[:, None],
            'w_gate': w_gate16, 'w_up': w_up16, 'w_down': w_down16,
        }
        if l == DEPTH - 1:
            p['final_g'] = final_norm_g[None]
        mod = mod_all[l]

        xp, (k_c, v_c, st_r, st_s) = _layer(xp, batch, mod, (0, batch * seq), p, None)
        ks_.append(k_c.reshape(batch, seq, ATT_KV_HEADS, ATT_HEAD_DIM))
        vs_.append(v_c.reshape(batch, seq, ATT_KV_HEADS, ATT_HEAD_DIM))
        rets_.append(st_r)
        ssms_.append(st_s)

        ctx = (cache_attn_k[:, l], cache_attn_v[:, l], state_ret[:, l], state_ssm_re[:, l], state_ssm_im[:, l])
        xs, _ = _layer(xs, dec_batch, mod, (1, dec_seq), p, ctx)

    ssm_all = jnp.stack(ssms_, axis=1)
    ssm_all = ssm_all.transpose(2, 1, 0, 3, 4, 5)
    new_re = ssm_all[..., 0, :]
    new_im = ssm_all[..., 1, :]
    return (xp.reshape(batch, seq, D_MODEL), xs.reshape(dec_batch, dec_seq, D_MODEL),
            jnp.stack(ks_, axis=1), jnp.stack(vs_, axis=1), jnp.stack(rets_, axis=1), new_re, new_im)
```

```python
import functools
import math

import jax
import jax.numpy as jnp
from jax import lax
from jax.experimental import pallas as pl
from jax.experimental.pallas import tpu as pltpu

F32 = jnp.float32
BF16 = jnp.bfloat16

D_MODEL = 1024
DEPTH = 2
GRID_W = 64
RET_HEADS = 4
RET_DK = 128
RET_DV = 128
RET_CHUNK = 256
RET_W = RET_HEADS * RET_DV
ATT_HEADS = 8
ATT_KV_HEADS = 2
ATT_GROUP = ATT_HEADS // ATT_KV_HEADS
ATT_HEAD_DIM = 64
ATT_W = ATT_HEADS * ATT_HEAD_DIM
ATT_KV_W = ATT_KV_HEADS * ATT_HEAD_DIM
ROPE_THETA = 10000.0
SSM_CH = 512
SSM_GROUP = 16
SSM_GROUPS = SSM_CH // SSM_GROUP
SSM_STATE = 64
N_EXPERTS = 16
N_EXPERT_GROUPS = 4
EXPERTS_PER_GROUP = N_EXPERTS // N_EXPERT_GROUPS
D_FF = 512
N_MOD = 6
EPS = 1e-6

C_RQ, C_RK, C_RV, C_RG = 0, 512, 1024, 1536
C_AQ, C_AK, C_AV, C_SU, C_G, C_END = 2048, 2560, 2688, 2816, 3328, 6400

VMEM_LIMIT = 52 * 1024 * 1024

ATT_Q_SCALE = ATT_HEAD_DIM ** -0.5 * math.log2(math.e)
ATT_QB = 512
ATT_KC = 256
ATT_V_ROWS = ATT_HEAD_DIM + 16
ATT_RING = 4

TM_PROJ = 512
RET_UNROLL = 4
MOE_TILE = 512
MOE_PART = 4096
META_ROWS = 8
S5_ROWS = 32
S5_SEG = 256
S5_T = 16
S5_GL = 2 * SSM_STATE
S5_GB = 8
S5_TM = 2048
S5_PITCH = 24
LANES = 128
STAGE_ROWS = 256
ADA_ROWS = 8
ADA_TN = 1536


def _dot(a, b):
    return jnp.dot(a, b, preferred_element_type=F32)


def _dot_nt(a, b):
    return lax.dot_general(a, b, (((1,), (1,)), ((), ())), preferred_element_type=F32)


def _dot_tn(a, b):
    return lax.dot_general(a, b, (((0,), (0,)), ((), ())), preferred_element_type=F32)


def _sigmoid(x):
    return 0.5 * jnp.tanh(0.5 * x) + 0.5


def _const_spec(shape):
    n = len(shape)
    return pl.BlockSpec(shape, lambda *_: (0,) * n, pipeline_mode=pl.Buffered(1))


def _params(sem, vmem=VMEM_LIMIT):
    return pltpu.CompilerParams(dimension_semantics=sem, vmem_limit_bytes=vmem)


def _mod_row_fn(mod_rows, tm):
    base, per_row = mod_rows
    return lambda i: base + (i * tm) // per_row


def _modulated_norm(x, g, mod_ref):
    xn = x * lax.rsqrt(jnp.mean(x * x, axis=-1, keepdims=True) + EPS) * g
    return xn * (1.0 + mod_ref[0, 1:2, :]) + mod_ref[0, 0:1, :]


def _ada_kernel(c_ref, w_ref, b_ref, o_ref):
    c = c_ref[...]
    a = (c * _sigmoid(c)).astype(BF16)
    o_ref[0] = _dot(a, w_ref[0].astype(BF16)) + b_ref[0]


def _ada(c_rows, w_ada, b_ada):
    n_col = N_MOD * D_MODEL
    return pl.pallas_call(
        _ada_kernel,
        out_shape=jax.ShapeDtypeStruct((DEPTH, ADA_ROWS, n_col), F32),
        grid=(DEPTH, n_col // ADA_TN),
        in_specs=[
            pl.BlockSpec((ADA_ROWS, D_MODEL), lambda l, j: (0, 0)),
            pl.BlockSpec((1, D_MODEL, ADA_TN), lambda l, j: (l, 0, j)),
            pl.BlockSpec((1, 1, ADA_TN), lambda l, j: (l, 0, j)),
        ],
        out_specs=pl.BlockSpec((1, ADA_ROWS, ADA_TN), lambda l, j: (l, 0, j)),
        compiler_params=_params(("arbitrary", "arbitrary")),
    )(c_rows, w_ada, b_ada.reshape(DEPTH, 1, n_col))


def _head_mean_sq(z, ones):
    z2 = z * z
    hi = z2.astype(BF16)
    lo = (z2 - hi.astype(F32)).astype(BF16)
    return (_dot(hi, ones) + _dot(lo, ones)) * (1.0 / ATT_HEAD_DIM)


def _inproj_kernel(*refs, latent):
    if latent:
        (x_ref, mod_ref, g1_ref, w_ref, wqt_ref, wvt_ref, ones_ref, gq_ref, gk_ref,
         wqst_ref, wks_ref, gqs_ref, gks_ref, cos_ref, sin_ref, cost_ref, sint_ref,
         rqkv_ref, rg_ref, gates_ref, qt_ref, k16_ref, vt_ref) = refs
    else:
        (x_ref, mod_ref, g1_ref, w_ref, wqt_ref, wvt_ref, ones_ref, gq_ref, gk_ref,
         rqkv_ref, rg_ref, gates_ref, qt_ref, k16_ref, vt_ref, ak_ref, av_ref) = refs
    tm = x_ref.shape[0]
    h = _modulated_norm(x_ref[...], g1_ref[...], mod_ref).astype(BF16)

    def seg(a, b):
        return _dot(h, w_ref[0, :, a:b])

    rqkv_ref[:, 0:RET_W] = (seg(C_RQ, C_RK) * (RET_DK ** -0.5)).astype(BF16)
    rqkv_ref[:, RET_W:3 * RET_W] = seg(C_RK, C_RG).astype(BF16)
    rg_ref[...] = seg(C_RG, C_AQ).astype(BF16)
    for j in range(3):
        gates_ref[:, j * D_MODEL:(j + 1) * D_MODEL] = seg(C_G + j * D_MODEL, C_G + (j + 1) * D_MODEL).astype(BF16)

    zk = seg(C_AK, C_AV)
    inv_k = lax.rsqrt(_head_mean_sq(zk, ones_ref[...]) + EPS)
    yk = zk * inv_k * gk_ref[...]
    zq = _dot_nt(wqt_ref[...], h).reshape(ATT_HEADS, ATT_HEAD_DIM, tm)
    inv_q = lax.rsqrt(jnp.mean(zq * zq, axis=1, keepdims=True) + EPS)
    yq = zq * inv_q * gq_ref[...]
    if latent:
        yk_sw = _dot(h, wks_ref[...]) * inv_k * gks_ref[...]
        yk = yk * cos_ref[...] + yk_sw * sin_ref[...]
        zq_sw = _dot_nt(wqst_ref[...], h).reshape(ATT_HEADS, ATT_HEAD_DIM, tm)
        yq = yq * cost_ref[...] + (zq_sw * inv_q * gqs_ref[...]) * sint_ref[...]
    qt_ref[...] = (yq * ATT_Q_SCALE).reshape(ATT_W, tm).astype(BF16)
    k16_ref[...] = yk.astype(BF16)
    vt = _dot_nt(wvt_ref[...], h).astype(BF16)
    for c in range(tm // ATT_KC):
        vt_ref[c] = vt[:, c * ATT_KC:(c + 1) * ATT_KC]
    if not latent:
        ak_ref[...] = yk
        av_ref[...] = seg(C_AV, C_SU)


def _inproj(x, mod, mod_rows, p, latent):
    n = x.shape[0]
    tm = TM_PROJ
    mod_row = _mod_row_fn(mod_rows, tm)
    row = lambda i: (i, 0)
    in_specs = [
        pl.BlockSpec((tm, D_MODEL), row),
        pl.BlockSpec((1, N_MOD, D_MODEL), lambda i: (mod_row(i), 0, 0)),
        _const_spec((1, D_MODEL)),
        pl.BlockSpec((1, D_MODEL, C_END), lambda i: (p['layer'], 0, 0), pipeline_mode=pl.Buffered(1)),
        _const_spec((ATT_W, D_MODEL)),
        _const_spec((ATT_KV_W, D_MODEL)),
        _const_spec((ATT_KV_W, ATT_KV_W)),
        _const_spec((ATT_HEAD_DIM, 1)),
        _const_spec((1, ATT_KV_W)),
    ]
    args = [x, mod, p['g1'], p['w_in'], p['wq_t'], p['wv_t'], p['ones'], p['gq'], p['gk']]
    if latent:
        n_pos = p['cos'].shape[0] // tm
        in_specs += [
            _const_spec((ATT_W, D_MODEL)),
            _const_spec((D_MODEL, ATT_KV_W)),
            _const_spec((ATT_HEAD_DIM, 1)),
            _const_spec((1, ATT_KV_W)),
            pl.BlockSpec((tm, ATT_KV_W), lambda i: (i % n_pos, 0)),
            pl.BlockSpec((tm, ATT_KV_W), lambda i: (i % n_pos, 0)),
            pl.BlockSpec((ATT_HEAD_DIM, tm), lambda i: (0, i % n_pos)),
            pl.BlockSpec((ATT_HEAD_DIM, tm), lambda i: (0, i % n_pos)),
        ]
        args += [p['wqs_t'], p['wk_sw'], p['gqs'], p['gks'], p['cos'], p['sin'], p['cos_t'], p['sin_t']]
    out_shape = [
        jax.ShapeDtypeStruct((n, 3 * RET_W), BF16),
        jax.ShapeDtypeStruct((n, RET_W), BF16),
        jax.ShapeDtypeStruct((n, 3 * D_MODEL), BF16),
        jax.ShapeDtypeStruct((ATT_W, n), BF16),
        jax.ShapeDtypeStruct((n, ATT_KV_W), BF16),
        jax.ShapeDtypeStruct((n // ATT_KC, ATT_KV_W, ATT_KC), BF16),
    ]
    out_specs = [
        pl.BlockSpec((tm, 3 * RET_W), row),
        pl.BlockSpec((tm, RET_W), row),
        pl.BlockSpec((tm, 3 * D_MODEL), row),
        pl.BlockSpec((ATT_W, tm), lambda i: (0, i)),
        pl.BlockSpec((tm, ATT_KV_W), row),
        pl.BlockSpec((tm // ATT_KC, ATT_KV_W, ATT_KC), lambda i: (i, 0, 0)),
    ]
    if not latent:
        out_shape += [jax.ShapeDtypeStruct((n, ATT_KV_W), F32)] * 2
        out_specs += [pl.BlockSpec((tm, ATT_KV_W), row)] * 2
    return pl.pallas_call(
        functools.partial(_inproj_kernel, latent=latent),
        out_shape=tuple(out_shape),
        grid=(n // tm,),
        in_specs=in_specs,
        out_specs=tuple(out_specs),
        compiler_params=_params(("parallel",)),
    )(*args)


T_DEC_F, T_DEC_B, T_XI_F, T_ZETA_F, T_XI_B, T_ZETA_B, T_CD_F, T_CD_B, N_TAB = range(9)


def _ret_kernel(*refs, has_s0, bb, seq):
    if has_s0:
        lg_ref, q_ref, k_ref, v_ref, rg_ref, gn_ref, s0_ref, o_ref, st_ref, tab_ref, acc_ref = refs
    else:
        lg_ref, q_ref, k_ref, v_ref, rg_ref, gn_ref, o_ref, st_ref, tab_ref, acc_ref = refs
    c = RET_CHUNK
    head = pl.program_id(1)
    lgf = lg_ref[0, head]
    lgb = lg_ref[1, head]
    t = lax.broadcasted_iota(jnp.int32, (c, c), 0).astype(F32)
    s = lax.broadcasted_iota(jnp.int32, (c, c), 1).astype(F32)
    tab_ref[T_DEC_F] = jnp.where(t >= s, jnp.exp(lgf * jnp.maximum(t - s, 0.0)), 0.0)
    tab_ref[T_DEC_B] = jnp.where(s >= t, jnp.exp(lgb * jnp.maximum(s - t, 0.0)), 0.0)
    tab_ref[T_XI_F] = jnp.exp(lgf * (t + 1.0))
    tab_ref[T_ZETA_F] = jnp.exp(lgf * (c - 1.0 - t))
    tab_ref[T_XI_B] = jnp.exp(lgb * (c - t))
    tab_ref[T_ZETA_B] = jnp.exp(lgb * t)
    tab_ref[T_CD_F] = jnp.exp(lgf * (c + 0.0 * t))
    tab_ref[T_CD_B] = jnp.exp(lgb * (c + 0.0 * t))
    if has_s0:
        st_ref[...] = s0_ref[...]
    else:
        st_ref[...] = jnp.zeros_like(st_ref)
    acc_ref[...] = jnp.zeros_like(acc_ref)
    n_chunk = seq // c

    def one_direction(rb, d, off, t_dec, t_xi, t_zeta, t_cd):
        rows = pl.ds(off, c)
        q = q_ref[rb, rows, :]
        k = k_ref[rb, rows, :]
        v = v_ref[rb, rows, :]
        p = (_dot_nt(q, k) * tab_ref[t_dec]).astype(BF16)
        st = st_ref[rb, d, 0]
        q_xi = (q.astype(F32) * tab_ref[t_xi, :, 0:RET_DK]).astype(BF16)
        o = _dot(jnp.concatenate([p, q_xi], axis=1), jnp.concatenate([v, st.astype(BF16)], axis=0))
        acc_ref[rb, rows, :] += o
        kz = (k.astype(F32) * tab_ref[t_zeta, :, 0:RET_DK]).astype(BF16)
        st_ref[rb, d, 0] = tab_ref[t_cd, 0:RET_DK, 0:RET_DV] * st + _dot_tn(kz, v)

    def body(i, carry):
        off_f = pl.multiple_of(i * c, c)
        off_b = pl.multiple_of((n_chunk - 1 - i) * c, c)
        for rb in range(bb):
            one_direction(rb, 0, off_f, T_DEC_F, T_XI_F, T_ZETA_F, T_CD_F)
            one_direction(rb, 1, off_b, T_DEC_B, T_XI_B, T_ZETA_B, T_CD_B)
        return carry

    lax.fori_loop(0, n_chunk, body, 0, unroll=RET_UNROLL if bb == 1 else 1)

    def finish(i, carry):
        rows = pl.ds(pl.multiple_of(i * c, c), c)
        for rb in range(bb):
            o = acc_ref[rb, rows, :]
            ro = o * lax.rsqrt(jnp.mean(o * o, axis=-1, keepdims=True) + EPS) * gn_ref[...]
            g = rg_ref[rb, rows, :].astype(F32)
            o_ref[rb, rows, :] = (ro * (g * _sigmoid(g))).astype(BF16)
        return carry

    lax.fori_loop(0, n_chunk, finish, 0)


def _retention(rqkv, rg, lg, gn, s0, bb):
    b, seq, _ = rqkv.shape
    has_s0 = s0 is not None
    blk = (bb, seq, RET_DK)
    st_spec = pl.BlockSpec((bb, 2, 1, RET_DK, RET_DV), lambda i, h: (i, 0, h, 0, 0))
    in_specs = [
        pl.BlockSpec(memory_space=pltpu.SMEM),
        pl.BlockSpec(blk, lambda i, h: (i, 0, h)),
        pl.BlockSpec(blk, lambda i, h: (i, 0, RET_HEADS + h)),
        pl.BlockSpec(blk, lambda i, h: (i, 0, 2 * RET_HEADS + h)),
        pl.BlockSpec(blk, lambda i, h: (i, 0, h)),
        pl.BlockSpec((1, RET_DV), lambda i, h: (0, h)),
    ]
    args = [lg, rqkv, rqkv, rqkv, rg, gn]
    if has_s0:
        in_specs.append(st_spec)
        args.append(s0)
    return pl.pallas_call(
        functools.partial(_ret_kernel, has_s0=has_s0, bb=bb, seq=seq),
        out_shape=(
            jax.ShapeDtypeStruct((b, seq, RET_W), BF16),
            jax.ShapeDtypeStruct((b, 2, RET_HEADS, RET_DK, RET_DV), F32),
        ),
        grid=(b // bb, RET_HEADS),
        in_specs=in_specs,
        out_specs=(pl.BlockSpec(blk, lambda i, h: (i, 0, h)), st_spec),
        scratch_shapes=[
            pltpu.VMEM((N_TAB, RET_CHUNK, RET_CHUNK), F32),
            pltpu.VMEM((bb, seq, RET_DV), F32),
        ],
        compiler_params=_params(("parallel", "parallel")),
    )(*args)


ST_M, ST_ALPHA, N_ST = range(3)


def _attn_kernel(*refs, has_ctx):
    if has_ctx:
        q_ref, k_ref, v_ref, kc_ref, vc_ref, o_ref, s_ref, p_ref, acc_ref, st_ref, mx_ref = refs
    else:
        q_ref, k_ref, v_ref, o_ref, s_ref, p_ref, acc_ref, st_ref, mx_ref = refs
    kv_head = pl.program_id(1)
    qb = q_ref.shape[1]
    kc = s_ref.shape[1]
    n_own = k_ref.shape[0] // kc
    n_chunk = n_own + (1 if has_ctx else 0)
    q_t = jnp.concatenate([q_ref[h * ATT_HEAD_DIM:(h + 1) * ATT_HEAD_DIM, :] for h in range(ATT_GROUP)], axis=1)
    q_both = jnp.concatenate([q_t] * ATT_KV_HEADS, axis=0)
    row = lax.broadcasted_iota(jnp.int32, q_both.shape, 0)
    q_pad = jnp.where(row // ATT_HEAD_DIM == kv_head, q_both, jnp.zeros_like(q_both))
    ones_rows = (lax.broadcasted_iota(jnp.int32, (ATT_V_ROWS - ATT_HEAD_DIM, kc), 0) == 0).astype(BF16)

    def keys(j):
        if has_ctx and isinstance(j, int) and j == n_own:
            return kc_ref[...]
        return k_ref[pl.ds(pl.multiple_of(j * kc, kc), kc), :]

    def values(j):
        v = vc_ref[...] if has_ctx and isinstance(j, int) and j == n_own else v_ref[j]
        return jnp.concatenate([v, ones_rows], axis=0)

    def put_scores(j, slot):
        s = _dot(keys(j), q_pad)
        s_ref[slot] = s
        mx_ref[slot] = jnp.max(s, axis=0, keepdims=True)

    def weighted_values(j, slot):
        return _dot(values(j), p_ref[slot])

    def phase(j, slot, first=False, last=False):
        if not last:
            put_scores(j + 1, (slot + 1) % ATT_RING)
        if not first:
            acc_ref[...] = st_ref[ST_ALPHA] * acc_ref[...] + weighted_values(j - 1, (slot - 1) % ATT_RING)
        m = st_ref[ST_M]
        m_new = jnp.maximum(m, mx_ref[slot])
        st_ref[ST_ALPHA] = jnp.exp2(m - m_new)
        st_ref[ST_M] = m_new
        p_ref[slot] = jnp.exp2((s_ref[slot] - m_new).astype(BF16))

    st_ref[ST_M] = jnp.full(st_ref.shape[1:], -jnp.inf, F32)
    acc_ref[...] = jnp.zeros_like(acc_ref)
    put_scores(0, 0)
    phase(0, 0, first=True, last=n_chunk == 1)
    n_loop = max(n_own - 2, 0) // ATT_RING

    def revolution(i, carry):
        for t in range(1, ATT_RING + 1):
            phase(ATT_RING * i + t, t % ATT_RING)
        return carry

    lax.fori_loop(0, n_loop, revolution, 0)
    for j in range(1 + ATT_RING * n_loop, n_chunk):
        phase(j, j % ATT_RING, last=j == n_chunk - 1)
    acc = st_ref[ST_ALPHA] * acc_ref[...] + weighted_values(n_chunk - 1, (n_chunk - 1) % ATT_RING)
    out = (acc[0:ATT_HEAD_DIM] / acc[ATT_HEAD_DIM:ATT_HEAD_DIM + 1]).astype(BF16)
    for h in range(ATT_GROUP):
        o_ref[h * ATT_HEAD_DIM:(h + 1) * ATT_HEAD_DIM, :] = out[:, h * qb:(h + 1) * qb]


def _attention(q_t, k, v_t, k_ctx, v_ctx_t, batch):
    n = q_t.shape[1]
    seq = n // batch
    qb, kc = min(ATT_QB, seq), ATT_KC
    n_q = seq // qb
    has_ctx = k_ctx is not None
    q_spec = pl.BlockSpec((ATT_GROUP * ATT_HEAD_DIM, qb), lambda b, g, j: (g, b * n_q + j))
    in_specs = [
        q_spec,
        pl.BlockSpec((seq, ATT_KV_W), lambda b, g, j: (b, 0)),
        pl.BlockSpec((seq // kc, ATT_HEAD_DIM, kc), lambda b, g, j: (b, g, 0)),
    ]
    args = [q_t, k, v_t]
    if has_ctx:
        in_specs += [
            pl.BlockSpec((kc, ATT_KV_W), lambda b, g, j: (b, 0)),
            pl.BlockSpec((ATT_HEAD_DIM, kc), lambda b, g, j: (b * ATT_KV_HEADS + g, 0)),
        ]
        args += [k_ctx, v_ctx_t]
    width = ATT_GROUP * qb
    return pl.pallas_call(
        functools.partial(_attn_kernel, has_ctx=has_ctx),
        out_shape=jax.ShapeDtypeStruct(q_t.shape, BF16),
        grid=(batch, ATT_KV_HEADS, n_q),
        in_specs=in_specs,
        out_specs=q_spec,
        scratch_shapes=[
            pltpu.VMEM((ATT_RING, kc, width), F32),
            pltpu.VMEM((ATT_RING, kc, width), BF16),
            pltpu.VMEM((ATT_V_ROWS, width), F32),
            pltpu.VMEM((N_ST, 1, width), F32),
            pltpu.VMEM((ATT_RING, 1, width), F32),
        ],
        compiler_params=_params(("parallel", "parallel", "parallel")),
    )(*args)


def _s5_in_kernel(x_ref, mod_ref, g1_ref, w_ref, o_ref, h_ref):
    tm = x_ref.shape[0]
    n_chunk = o_ref.shape[2]
    n_col = D_MODEL // LANES
    per_stage = STAGE_ROWS // S5_T

    def stage(i, carry):
        h = _modulated_norm(x_ref[pl.ds(pl.multiple_of(i * STAGE_ROWS, STAGE_ROWS), STAGE_ROWS), :], g1_ref[...],
                            mod_ref)
        for c in range(per_stage):
            rows = pl.ds(pl.multiple_of((i * per_stage + c) * S5_PITCH, S5_PITCH), S5_T)
            for j in range(n_col):
                h_ref[j, rows, :] = h[c * S5_T:(c + 1) * S5_T, j * LANES:(j + 1) * LANES]
        return carry

    lax.fori_loop(0, tm // STAGE_ROWS, stage, 0)
    for s in range(S5_T):
        h = jnp.concatenate([h_ref[j, pl.ds(s, n_chunk, stride=S5_PITCH), :] for j in range(n_col)], axis=1)
        o_ref[s] = _dot_nt(w_ref[...], h.astype(BF16)).astype(BF16)


def _s5_in(x, mod, mod_rows, g1, w_su_t):
    n = x.shape[0]
    tm = S5_TM
    mod_row = _mod_row_fn(mod_rows, tm)
    return pl.pallas_call(
        _s5_in_kernel,
        out_shape=jax.ShapeDtypeStruct((S5_T, SSM_CH, n // S5_T), BF16),
        grid=(n // tm,),
        in_specs=[
            pl.BlockSpec((tm, D_MODEL), lambda i: (i, 0)),
            pl.BlockSpec((1, N_MOD, D_MODEL), lambda i: (mod_row(i), 0, 0)),
            _const_spec((1, D_MODEL)),
            _const_spec((SSM_CH, D_MODEL)),
        ],
        out_specs=pl.BlockSpec((S5_T, SSM_CH, tm // S5_T), lambda i: (0, 0, i)),
        scratch_shapes=[pltpu.VMEM((D_MODEL // LANES, tm // S5_T * S5_PITCH, LANES), F32)],
        compiler_params=_params(("parallel",)),
    )(x, mod, g1, w_su_t)


def _s5_kernel(*refs, two_pass):
    if two_pass:
        u_ref, tt_ref, wx_ref, wy_ref, a_ref, aseg_ref, s0_ref, y_ref, v_ref, init_ref = refs
    else:
        u_ref, tt_ref, wx_ref, wy_ref, a_ref, y_ref, fin_ref, v_ref = refs
    gb = tt_ref.shape[1]
    n_chunk = u_ref.shape[2] // S5_ROWS
    grp = SSM_GROUP

    def toeplitz(d, j):
        strip = tt_ref[d, j]
        width = S5_T * grp
        starts = [(S5_T - 1 - t) * grp for t in range(S5_T)]
        return jnp.concatenate([strip[:, a:a + width] for a in starts], axis=0)

    def put_outputs(j, yt, accumulate):
        for t in range(S5_T):
            if accumulate:
                y_ref[t, j * grp:(j + 1) * grp, :] += yt[t * grp:(t + 1) * grp, :]
            else:
                y_ref[t, j * grp:(j + 1) * grp, :] = yt[t * grp:(t + 1) * grp, :]

    for j in range(gb):
        z = jnp.concatenate([u_ref[s, j * grp:(j + 1) * grp, :] for s in range(S5_T)], axis=0)
        put_outputs(j, _dot(toeplitz(0, j), z) + _dot(toeplitz(1, j), z), accumulate=False)
        for d in range(2):
            v_ref[d, j] = _dot_tn(z, wx_ref[d, j])

    def times(coef_ref, d, j, x):
        lanes = slice(j * S5_GL, (j + 1) * S5_GL)
        return coef_ref[d, 0:1, lanes] * x + coef_ref[d, 1:2, lanes] * pltpu.roll(x, SSM_STATE, axis=1)

    def scan(init, keep_states):
        xs = [list(init[0]), list(init[1])]
        for s in range(n_chunk):
            for d in range(2):
                c = s if d == 0 else n_chunk - 1 - s
                rows = pl.ds(c, S5_ROWS, stride=n_chunk)
                for j in range(gb):
                    v = v_ref[d, j, rows, :]
                    if keep_states:
                        v_ref[d, j, rows, :] = xs[d][j]
                    xs[d][j] = times(a_ref, d, j, xs[d][j]) + v
        return xs

    zero = [[jnp.zeros((S5_ROWS, S5_GL), F32)] * gb] * 2
    if two_pass:
        local = scan(zero, keep_states=False)
        n_batch = s0_ref.shape[1]
        n_seg = S5_ROWS // n_batch
        for d in range(2):
            order = range(n_seg) if d == 0 else range(n_seg - 1, -1, -1)
            for j in range(gb):
                for b in range(n_batch):
                    cur = s0_ref[d, b:b + 1, j * S5_GL:(j + 1) * S5_GL]
                    for sgm in order:
                        r = b * n_seg + sgm
                        init_ref[d, j, r:r + 1, :] = cur
                        cur = times(aseg_ref, d, j, cur) + local[d][j][r:r + 1, :]
        scan([[init_ref[d, j] for j in range(gb)] for d in range(2)], keep_states=True)
    else:
        final = scan(zero, keep_states=True)
        for d in range(2):
            for j in range(gb):
                fin_ref[d, :, j * S5_GL:(j + 1) * S5_GL] = final[d][j]
    for j in range(gb):
        put_outputs(j, _dot_nt(wy_ref[0, j], v_ref[0, j].astype(BF16))
                    + _dot_nt(wy_ref[1, j], v_ref[1, j].astype(BF16)), accumulate=True)


def _s5(u, tt, wx, wy, a_t, a_seg, s0):
    two_pass = s0 is not None
    n_rows = u.shape[2]
    gb = S5_GB
    io_spec = pl.BlockSpec((S5_T, gb * SSM_GROUP, n_rows), lambda i: (0, i, 0))
    coef_spec = pl.BlockSpec((2, 2, gb * S5_GL), lambda i: (0, 0, i))
    in_specs = [
        io_spec,
        pl.BlockSpec((2, gb) + tt.shape[2:], lambda i: (0, i, 0, 0)),
        pl.BlockSpec((2, gb) + wx.shape[2:], lambda i: (0, i, 0, 0)),
        pl.BlockSpec((2, gb) + wy.shape[2:], lambda i: (0, i, 0, 0)),
        coef_spec,
    ]
    args = [u, tt, wx, wy, a_t]
    y_shape = jax.ShapeDtypeStruct(u.shape, F32)
    scratch = [pltpu.VMEM((2, gb, n_rows, S5_GL), F32)]
    if two_pass:
        in_specs += [coef_spec, pl.BlockSpec((2, s0.shape[1], gb * S5_GL), lambda i: (0, 0, i))]
        args += [a_seg, s0]
        out_shape, out_specs = y_shape, io_spec
        scratch.append(pltpu.VMEM((2, gb, S5_ROWS, S5_GL), F32))
    else:
        out_shape = (y_shape, jax.ShapeDtypeStruct((2, S5_ROWS, SSM_GROUPS * S5_GL), F32))
        out_specs = (io_spec, pl.BlockSpec((2, S5_ROWS, gb * S5_GL), lambda i: (0, 0, i)))
    return pl.pallas_call(
        functools.partial(_s5_kernel, two_pass=two_pass),
        out_shape=out_shape,
        grid=(SSM_GROUPS // gb,),
        in_specs=in_specs,
        out_specs=out_specs,
        scratch_shapes=scratch,
        compiler_params=_params(("parallel",)),
    )(*args)


def _s5_out_kernel(y_ref, u_ref, d_ref, w_ref, o_ref, buf_ref):
    n_chunk = y_ref.shape[2]
    n_col = D_MODEL // LANES
    for s in range(S5_T):
        y = y_ref[s] + d_ref[...] * u_ref[s].astype(F32)
        glu = _dot_tn(jax.nn.gelu(y).astype(BF16), w_ref[...])
        out = glu[:, 0:D_MODEL] * _sigmoid(glu[:, D_MODEL:2 * D_MODEL])
        for j in range(n_col):
            buf_ref[j, pl.ds(s, n_chunk, stride=S5_PITCH), :] = out[:, j * LANES:(j + 1) * LANES]

    def unstage(c, carry):
        src = pl.ds(pl.multiple_of(c * S5_PITCH, S5_PITCH), S5_T)
        dst = pl.ds(pl.multiple_of(c * S5_T, S5_T), S5_T)
        o_ref[dst, :] = jnp.concatenate([buf_ref[j, src, :] for j in range(n_col)], axis=1).astype(BF16)
        return carry

    lax.fori_loop(0, n_chunk, unstage, 0, unroll=8)


def _s5_out(y, u, d_skip, w_glu):
    n = y.shape[2] * S5_T
    tm = S5_TM
    io_spec = pl.BlockSpec((S5_T, SSM_CH, tm // S5_T), lambda i: (0, 0, i))
    return pl.pallas_call(
        _s5_out_kernel,
        out_shape=jax.ShapeDtypeStruct((n, D_MODEL), BF16),
        grid=(n // tm,),
        in_specs=[io_spec, io_spec, _const_spec((SSM_CH, 1)), _const_spec((SSM_CH, 2 * D_MODEL))],
        out_specs=pl.BlockSpec((tm, D_MODEL), lambda i: (i, 0)),
        scratch_shapes=[pltpu.VMEM((D_MODEL // LANES, tm // S5_T * S5_PITCH, LANES), F32)],
        compiler_params=_params(("parallel",)),
    )(y, u, d_skip, w_glu)


def _route(probs):
    epg = EXPERTS_PER_GROUP
    groups = [probs[g * epg:(g + 1) * epg] for g in range(N_EXPERT_GROUPS)]
    scores = []
    for grp in groups:
        best = None
        for a in range(epg):
            for b in range(a + 1, epg):
                pair = grp[a] + grp[b]
                best = pair if best is None else jnp.maximum(best, pair)
        scores.append(best)
    top_score = scores[0]
    top_group = jnp.zeros_like(scores[0], dtype=jnp.int32)
    for g in range(1, N_EXPERT_GROUPS):
        better = scores[g] > top_score
        top_score = jnp.where(better, scores[g], top_score)
        top_group = jnp.where(better, g, top_group)
    sel = []
    for j in range(epg):
        v = groups[0][j]
        for g in range(1, N_EXPERT_GROUPS):
            v = jnp.where(top_group == g, groups[g][j], v)
        sel.append(v)
    v1 = sel[0]
    i1 = jnp.zeros_like(top_group)
    for j in range(1, epg):
        better = sel[j] > v1
        v1 = jnp.where(better, sel[j], v1)
        i1 = jnp.where(better, j, i1)
    v2 = jnp.full_like(v1, -1.0)
    i2 = jnp.zeros_like(top_group)
    for j in range(epg):
        better = (sel[j] > v2) & (i1 != j)
        v2 = jnp.where(better, sel[j], v2)
        i2 = jnp.where(better, j, i2)
    total = v1 + v2
    w1 = v1 / total
    w2 = v2 / total
    return top_group, [jnp.where(i1 == j, w1, jnp.where(i2 == j, w2, 0.0)) for j in range(epg)]


def _merge_kernel(x_ref, ro_ref, att_ref, ssm_ref, gates_ref, mod_ref, wro_ref, wao_ref, wout_ref, g2_ref, wrt_ref,
                  br_ref, x1_ref, h2_ref, meta_ref, cnt_ref):
    ret_branch = _dot(ro_ref[...], wro_ref[...])
    att_branch = _dot_tn(att_ref[...], wao_ref[...])

    def gate(j):
        return _sigmoid(gates_ref[:, j * D_MODEL:(j + 1) * D_MODEL].astype(F32))

    merged = gate(0) * ret_branch + gate(1) * att_branch + gate(2) * ssm_ref[...].astype(F32)
    x1 = x_ref[...] + mod_ref[0, 2:3, :] * _dot(merged.astype(BF16), wout_ref[...])
    x1_ref[...] = x1
    xn = x1 * lax.rsqrt(jnp.mean(x1 * x1, axis=-1, keepdims=True) + EPS) * g2_ref[...]
    h2 = xn * (1.0 + mod_ref[0, 4:5, :]) + mod_ref[0, 3:4, :]
    h2_ref[...] = h2.astype(BF16)

    logits = lax.dot_general(wrt_ref[...], h2, (((1,), (1,)), ((), ())), precision=lax.Precision.HIGHEST,
                             preferred_element_type=F32) + br_ref[...]
    e = jnp.exp(logits - jnp.max(logits, axis=0, keepdims=True))
    probs = e / jnp.sum(e, axis=0, keepdims=True)
    top_group, weights = _route([probs[j:j + 1, :] for j in range(N_EXPERTS)])

    tm = top_group.shape[1]

    @pl.when(pl.program_id(0) % (MOE_PART // tm) == 0)
    def _():
        cnt_ref[...] = jnp.zeros_like(cnt_ref)

    zero_row = jnp.zeros((1, tm), F32)
    onehot = jnp.concatenate([(top_group == g).astype(F32) for g in range(N_EXPERT_GROUPS)]
                             + [zero_row] * (META_ROWS - N_EXPERT_GROUPS), axis=0)
    lane = lax.broadcasted_iota(jnp.int32, onehot.shape, 1)
    incl = onehot
    shift = 1
    while shift < tm:
        incl = incl + jnp.where(lane >= shift, pltpu.roll(incl, shift, axis=1), 0.0)
        shift *= 2
    before = cnt_ref[0, :, 0:1]
    rank = jnp.sum(onehot * (incl - onehot + before), axis=0, keepdims=True)
    cnt_ref[0] = jnp.broadcast_to(before + jnp.sum(onehot, axis=1, keepdims=True), cnt_ref.shape[1:])
    meta_ref[...] = jnp.concatenate([top_group.astype(F32), rank] + weights
                                    + [zero_row] * (META_ROWS - 2 - EXPERTS_PER_GROUP), axis=0)


def _merge(x, ro, att_t, ssm, gates, mod, mod_rows, wro, wao, wout, g2, wrt, br):
    n = x.shape[0]
    tm = TM_PROJ
    mod_row = _mod_row_fn(mod_rows, tm)
    row = lambda i: (i, 0)
    in_specs = [
        pl.BlockSpec((tm, D_MODEL), row),
        pl.BlockSpec((tm, RET_W), row),
        pl.BlockSpec((ATT_W, tm), lambda i: (0, i)),
        pl.BlockSpec((tm, D_MODEL), row),
        pl.BlockSpec((tm, 3 * D_MODEL), row),
        pl.BlockSpec((1, N_MOD, D_MODEL), lambda i: (mod_row(i), 0, 0)),
        _const_spec((RET_W, D_MODEL)),
        _const_spec((ATT_W, D_MODEL)),
        _const_spec((D_MODEL, D_MODEL)),
        _const_spec((1, D_MODEL)),
        _const_spec((N_EXPERTS, D_MODEL)),
        _const_spec((N_EXPERTS, 1)),
    ]
    return pl.pallas_call(
        _merge_kernel,
        out_shape=(
            jax.ShapeDtypeStruct((n, D_MODEL), F32),
            jax.ShapeDtypeStruct((n, D_MODEL), BF16),
            jax.ShapeDtypeStruct((META_ROWS, n), F32),
            jax.ShapeDtypeStruct((n // MOE_PART, META_ROWS, LANES), F32),
        ),
        grid=(n // tm,),
        in_specs=in_specs,
        out_specs=(
            pl.BlockSpec((tm, D_MODEL), row),
            pl.BlockSpec((tm, D_MODEL), row),
            pl.BlockSpec((META_ROWS, tm), lambda i: (0, i)),
            pl.BlockSpec((1, META_ROWS, LANES), lambda i: (i // (MOE_PART // tm), 0, 0)),
        ),
        compiler_params=_params(("arbitrary",)),
    )(x, ro, att_t, ssm, gates, mod, wro, wao, wout, g2, wrt, br)


def _moe_kernel(dest_ref, tile_group_ref, n_valid_ref, *refs, final, n_scatter, n_expert, n_gather):
    if final:
        (h2_ref, wrow_ref, wg_ref, wu_ref, wd_ref, x1_ref, mod_ref, fg_ref, o_ref,
         sorted_ref, wsort_ref, stage_ref, acc_ref) = refs
    else:
        (h2_ref, wrow_ref, wg_ref, wu_ref, wd_ref, x1_ref, mod_ref, o_ref,
         sorted_ref, wsort_ref, stage_ref, acc_ref) = refs
    tm = h2_ref.shape[0]
    epg = EXPERTS_PER_GROUP
    per_part = n_scatter + n_expert + n_gather
    part = pl.program_id(0) // per_part
    i = pl.program_id(0) % per_part

    @pl.when(i == 0)
    def _():
        sorted_ref[...] = jnp.zeros_like(sorted_ref)
        wsort_ref[...] = jnp.zeros_like(wsort_ref)

    @pl.when(i < n_scatter)
    def _():
        stage_ref[...] = h2_ref[...].astype(F32)
        base = (part * n_scatter + i) * tm

        def move(t, carry):
            d = dest_ref[base + t]
            sorted_ref[pl.ds(d, 1), :] = stage_ref[pl.ds(t, 1), :]
            wsort_ref[pl.ds(d, 1), :] = wrow_ref[pl.ds(t, 1), :]
            return carry

        lax.fori_loop(0, tm, move, 0, unroll=8)

    step = i - n_scatter
    tile = step // epg
    ein = step % epg

    @pl.when((step >= 0) & (step < n_expert) & (tile < n_valid_ref[part]))
    def _():
        rows = pl.ds(pl.multiple_of(tile * MOE_TILE, MOE_TILE), MOE_TILE)
        x = sorted_ref[rows, :].astype(BF16)
        g = _dot(x, wg_ref[0])
        u = _dot(x, wu_ref[0])
        w = wsort_ref[rows, :]
        lane = lax.broadcasted_iota(jnp.int32, w.shape, 1)
        w_col = jnp.sum(jnp.where(lane == ein, w, 0.0), axis=1, keepdims=True)
        out = _dot((g * _sigmoid(g) * u * w_col).astype(BF16), wd_ref[0])

        @pl.when(ein == 0)
        def _():
            acc_ref[...] = out

        @pl.when(ein > 0)
        def _():
            acc_ref[...] += out

        @pl.when(ein == epg - 1)
        def _():
            sorted_ref[rows, :] = acc_ref[...]

    @pl.when(step >= n_expert)
    def _():
        base = (part * n_gather + step - n_expert) * tm

        def move(t, carry):
            stage_ref[pl.ds(t, 1), :] = sorted_ref[pl.ds(dest_ref[base + t], 1), :]
            return carry

        lax.fori_loop(0, tm, move, 0, unroll=8)
        x2 = x1_ref[...] + mod_ref[0, 5:6, :] * stage_ref[...]
        if final:
            x2 = x2 * lax.rsqrt(jnp.mean(x2 * x2, axis=-1, keepdims=True) + EPS) * fg_ref[...]
        o_ref[...] = x2


def _moe_plan(meta, cnt):
    n = meta.shape[1]
    n_part = cnt.shape[0]
    n_tiles = n // n_part // MOE_TILE + N_EXPERT_GROUPS
    counts = cnt[:, 0:N_EXPERT_GROUPS, 0].astype(jnp.int32)
    padded = (counts + MOE_TILE - 1) // MOE_TILE * MOE_TILE
    ends = jnp.cumsum(padded, axis=1)
    starts = ends - padded
    group = meta[0].astype(jnp.int32).reshape(n_part, -1)
    dest = meta[1].astype(jnp.int32).reshape(n_part, -1)
    for g in range(N_EXPERT_GROUPS):
        dest = dest + jnp.where(group == g, starts[:, g:g + 1], 0)
    tile_start = jnp.arange(n_tiles, dtype=jnp.int32) * MOE_TILE
    tile_group = jnp.sum(tile_start[None, :, None] >= ends[:, None, :], axis=2).astype(jnp.int32)
    tile_group = jnp.minimum(tile_group, N_EXPERT_GROUPS - 1)
    n_valid = (ends[:, N_EXPERT_GROUPS - 1] // MOE_TILE).astype(jnp.int32)
    w_rows = jnp.pad(meta[2:2 + EXPERTS_PER_GROUP].T, ((0, 0), (0, LANES - EXPERTS_PER_GROUP)))
    return dest.reshape(-1), tile_group, n_valid, w_rows


def _moe(h2, meta, cnt, x1, mod, mod_rows, layer, wg, wu, wd, final_g):
    n = h2.shape[0]
    tm = MOE_TILE
    n_part = cnt.shape[0]
    dest, tile_group, n_valid, w_rows = _moe_plan(meta, cnt)
    n_tiles = tile_group.shape[1]
    epg = EXPERTS_PER_GROUP
    n_scatter, n_expert, n_gather = n // n_part // tm, n_tiles * epg, n // n_part // tm
    per_part = n_scatter + n_expert + n_gather
    mod_row = _mod_row_fn(mod_rows, tm)
    final = final_g is not None

    def scatter_tile(i):
        return (i // per_part) * n_scatter + jnp.minimum(i % per_part, n_scatter - 1)

    def gather_tile(i):
        return (i // per_part) * n_gather + jnp.clip(i % per_part - n_scatter - n_expert, 0, n_gather - 1)

    def expert(i, tg, nv):
        part = i // per_part
        step = jnp.clip(i % per_part - n_scatter, 0, n_expert - 1)
        tile = step // epg
        in_use = tile < nv[part]
        group = tg[part * n_tiles + jnp.minimum(tile, nv[part] - 1)]
        return layer * N_EXPERTS + group * epg + jnp.where(in_use, step % epg, epg - 1)

    in_specs = [
        pl.BlockSpec((tm, D_MODEL), lambda i, d, tg, nv: (scatter_tile(i), 0)),
        pl.BlockSpec((tm, LANES), lambda i, d, tg, nv: (scatter_tile(i), 0)),
        pl.BlockSpec((1, D_MODEL, D_FF), lambda i, d, tg, nv: (expert(i, tg, nv), 0, 0)),
        pl.BlockSpec((1, D_MODEL, D_FF), lambda i, d, tg, nv: (expert(i, tg, nv), 0, 0)),
        pl.BlockSpec((1, D_FF, D_MODEL), lambda i, d, tg, nv: (expert(i, tg, nv), 0, 0)),
        pl.BlockSpec((tm, D_MODEL), lambda i, d, tg, nv: (gather_tile(i), 0)),
        pl.BlockSpec((1, N_MOD, D_MODEL), lambda i, d, tg, nv: (mod_row(gather_tile(i)), 0, 0)),
    ]
    args = [h2, w_rows, wg, wu, wd, x1, mod]
    if final:
        in_specs.append(pl.BlockSpec((1, D_MODEL), lambda i, d, tg, nv: (0, 0)))
        args.append(final_g)
    grid_spec = pltpu.PrefetchScalarGridSpec(
        num_scalar_prefetch=3,
        grid=(n_part * per_part,),
        in_specs=in_specs,
        out_specs=pl.BlockSpec((tm, D_MODEL), lambda i, d, tg, nv: (gather_tile(i), 0)),
        scratch_shapes=[
            pltpu.VMEM((n_tiles * MOE_TILE, D_MODEL), F32),
            pltpu.VMEM((n_tiles * MOE_TILE, LANES), F32),
            pltpu.VMEM((tm, D_MODEL), F32),
            pltpu.VMEM((MOE_TILE, D_MODEL), F32),
        ],
    )
    return pl.pallas_call(
        functools.partial(_moe_kernel, final=final, n_scatter=n_scatter, n_expert=n_expert, n_gather=n_gather),
        out_shape=jax.ShapeDtypeStruct((n, D_MODEL), F32),
        grid_spec=grid_spec,
        compiler_params=_params(("arbitrary",)),
    )(dest, tile_group.reshape(-1), n_valid, *args)


def _block_diag(m):
    g, r, c = m.shape
    eye = jnp.eye(g, dtype=m.dtype)
    return (eye[:, None, :, None] * m[:, :, None, :]).reshape(g * r, g * c)


def _s5_direction_params(a_re, a_im, log_dt, b_re, b_im, c_re, c_im, reverse):
    n_g = a_re.shape[0]
    width = S5_T * SSM_GROUP
    hp = lax.Precision.HIGHEST
    step = jnp.exp(log_dt)[:, None]
    mag = jnp.exp(a_re * step)
    ar = mag * jnp.cos(a_im * step)
    ai = mag * jnp.sin(a_im * step)
    den = a_re * a_re + a_im * a_im
    fr = ((ar - 1.0) * a_re + ai * a_im) / den
    fi = (ai * a_re - (ar - 1.0) * a_im) / den
    bbr = (fr[:, :, None] * b_re - fi[:, :, None] * b_im).transpose(0, 2, 1)
    bbi = (fr[:, :, None] * b_im + fi[:, :, None] * b_re).transpose(0, 2, 1)
    pr, pi = [jnp.ones_like(ar)], [jnp.zeros_like(ar)]
    for _ in range(S5_T):
        r, i = pr[-1], pi[-1]
        pr.append(r * ar - i * ai)
        pi.append(r * ai + i * ar)
    pr, pi = jnp.stack(pr)[:, :, None, :], jnp.stack(pi)[:, :, None, :]
    mr = pr * bbr - pi * bbi
    mi = pr * bbi + pi * bbr
    cr = c_re * pr - c_im * pi
    ci = c_re * pi + c_im * pr
    kern = (jnp.einsum('gop,kgip->goki', c_re, mr[:S5_T], precision=hp)
            - jnp.einsum('gop,kgip->goki', c_im, mi[:S5_T], precision=hp))
    pad = jnp.zeros_like(kern[:, :, 1:, :])
    strip = jnp.concatenate([pad, kern] if reverse else [kern[:, :, ::-1, :], pad], axis=2)
    tt = jnp.concatenate([strip, pad[:, :, 0:1, :]], axis=2).reshape(n_g, SSM_GROUP, 2 * width)
    wx = jnp.concatenate([mr[:S5_T], mi[:S5_T]], axis=3)
    wx = (wx if reverse else wx[::-1]).transpose(1, 0, 2, 3).reshape(n_g, width, S5_GL)
    wy = jnp.concatenate([cr[1:], -ci[1:]], axis=3)
    wy = (wy[::-1] if reverse else wy).transpose(1, 0, 2, 3).reshape(n_g, width, S5_GL)

    def coef_rows(r, i):
        return jnp.stack([jnp.concatenate([r, r], axis=1).reshape(-1), jnp.concatenate([-i, i], axis=1).reshape(-1)])

    sr, si = pr[S5_T, :, 0, :], pi[S5_T, :, 0, :]
    a_t = coef_rows(sr, si)
    for _ in range(int(math.log2(S5_SEG // S5_T))):
        sr, si = sr * sr - si * si, 2.0 * sr * si
    return tt.astype(BF16), wx.astype(BF16), wy.astype(BF16), a_t, coef_rows(sr, si)


def _rope_tables(n_tokens):
    rows = n_tokens // GRID_W
    row = jnp.repeat(jnp.arange(rows, dtype=F32), GRID_W)
    col = jnp.tile(jnp.arange(GRID_W, dtype=F32), rows)
    n_freq = ATT_HEAD_DIM // 4
    inv = ROPE_THETA ** (-jnp.arange(n_freq, dtype=F32) / n_freq)
    ang = jnp.concatenate([row[:, None] * inv, col[:, None] * inv], axis=-1)
    cos = jnp.repeat(jnp.cos(ang), 2, axis=-1)
    sin = jnp.repeat(jnp.sin(ang), 2, axis=-1) * jnp.tile(jnp.array([-1.0, 1.0], F32), ATT_HEAD_DIM // 2)
    return cos, sin


def _swap_pairs(a, axis=-1):
    axis = axis % a.ndim
    pairs = a.reshape(a.shape[:axis] + (a.shape[axis] // 2, 2) + a.shape[axis + 1:])
    return jnp.flip(pairs, axis=axis + 1).reshape(a.shape)


def _layer(x, batch, mod, mod_rows, p, ctx):
    n = x.shape[0]
    seq = n // batch
    latent = ctx is not None
    outs = _inproj(x, mod, mod_rows, p, latent)
    rqkv, rg, gates, q_t, k16, v_t = outs[:6]

    ro, ret_state = _retention(rqkv.reshape(batch, seq, 3 * RET_W), rg.reshape(batch, seq, RET_W), p['ret_lg'],
                               p['ret_gn'], ctx[2] if latent else None, bb=1 if latent else 8)

    if latent:
        k_ctx = ctx[0].reshape(batch * ATT_KC, ATT_KV_W).astype(BF16)
        v_ctx_t = ctx[1].reshape(batch, ATT_KC, ATT_KV_W).transpose(0, 2, 1).reshape(batch * ATT_KV_W, ATT_KC)
        att_t = _attention(q_t, k16, v_t, k_ctx, v_ctx_t.astype(BF16), batch)
    else:
        att_t = _attention(q_t, k16, v_t, None, None, batch)

    u = _s5_in(x, mod, mod_rows, p['g1'], p['w_su_t'])
    if latent:
        s0 = jnp.stack([ctx[3], ctx[4]], axis=3).transpose(1, 0, 2, 3, 4).reshape(2, batch, SSM_GROUPS * S5_GL)
        y = _s5(u, p['s5_tt'], p['s5_wx'], p['s5_wy'], p['s5_a'], p['s5_aseg'], s0)
        ssm_state = None
    else:
        y, ssm_state = _s5(u, p['s5_tt'], p['s5_wx'], p['s5_wy'], p['s5_a'], None, None)
        ssm_state = ssm_state.reshape(2, batch, SSM_GROUPS, 2, SSM_STATE)
    ssm = _s5_out(y, u, p['ssm_d'], p['w_glu'])

    x1, h2, meta, cnt = _merge(x, ro.reshape(n, RET_W), att_t, ssm, gates, mod, mod_rows, p['w_ret_out'],
                               p['w_att_out'], p['w_out'], p['g2'], p['w_router_t'], p['b_router'])
    x2 = _moe(h2, meta, cnt, x1, mod, mod_rows, p['layer'], p['w_gate'], p['w_up'], p['w_down'], p.get('final_g'))
    if latent:
        return x2, None
    return x2, (outs[6], outs[7], ret_state, ssm_state)


def kernel(x_prompt, x_sample, c, cache_attn_k, cache_attn_v, state_ret, state_ssm_re, state_ssm_im, c_ctx, w_ada,
           b_ada, norm1_g, norm2_g, w_in, ret_lg_f, ret_lg_b, ret_norm_g, w_ret_out, att_q_norm_g, att_k_norm_g,
           w_att_out, a_re_f, a_im_f, a_re_b, a_im_b, log_dt_f, log_dt_b, ssm_b_re, ssm_b_im, ssm_c_re, ssm_c_im,
           ssm_d, w_glu, w_out, w_router, b_router, w_gate, w_up, w_down, final_norm_g):
    batch, seq, _ = x_prompt.shape
    dec_batch, dec_seq, _ = x_sample.shape
    assert batch * seq == S5_ROWS * S5_SEG and dec_batch * dec_seq == S5_ROWS * S5_SEG
    assert seq == S5_SEG and dec_seq % S5_SEG == 0
    assert cache_attn_k.shape[2] == ATT_KC and seq % ATT_KC == 0 and dec_seq % ATT_QB == 0

    c_rows = jnp.zeros((ADA_ROWS, D_MODEL), F32).at[0].set(c_ctx).at[1:1 + dec_batch].set(c)
    mod_all = _ada(c_rows, w_ada, b_ada).reshape(DEPTH, ADA_ROWS, N_MOD, D_MODEL)

    cos, sin = _rope_tables(dec_seq)
    ones = _block_diag(jnp.ones((ATT_KV_HEADS, ATT_HEAD_DIM, ATT_HEAD_DIM), BF16))
    w_in16 = w_in.astype(BF16)
    w_gate16 = w_gate.astype(BF16).reshape(DEPTH * N_EXPERTS, D_MODEL, D_FF)
    w_up16 = w_up.astype(BF16).reshape(DEPTH * N_EXPERTS, D_MODEL, D_FF)
    w_down16 = w_down.astype(BF16).reshape(DEPTH * N_EXPERTS, D_FF, D_MODEL)
    s5_fwd = jax.vmap(functools.partial(_s5_direction_params, reverse=False))(
        a_re_f, a_im_f, log_dt_f, ssm_b_re, ssm_b_im, ssm_c_re, ssm_c_im)
    s5_bwd = jax.vmap(functools.partial(_s5_direction_params, reverse=True))(
        a_re_b, a_im_b, log_dt_b, ssm_b_re, ssm_b_im, ssm_c_re, ssm_c_im)

    xp = x_prompt.reshape(batch * seq, D_MODEL)
    xs = x_sample.reshape(dec_batch * dec_seq, D_MODEL)
    ks_, vs_, rets_, ssms_ = [], [], [], []
    for l in range(DEPTH):
        w_l = w_in[l]
        gq = att_q_norm_g[l][:, None]
        gk = jnp.tile(att_k_norm_g[l], ATT_KV_HEADS)[None]
        wq_t = w_l[:, C_AQ:C_AK].T
        fwd = [a[l] for a in s5_fwd]
        bwd = [a[l] for a in s5_bwd]
        p = {
            'layer': l, 'g1': norm1_g[l][None], 'g2': norm2_g[l][None], 'w_in': w_in16, 'ones': ones,
            'wq_t': wq_t.astype(BF16), 'wv_t': w_l[:, C_AV:C_SU].T.astype(BF16), 'gq': gq, 'gk': gk,
            'wqs_t': _swap_pairs(wq_t, axis=0).astype(BF16), 'wk_sw': _swap_pairs(w_l[:, C_AK:C_AV]).astype(BF16),
            'gqs': _swap_pairs(gq, axis=0), 'gks': _swap_pairs(gk),
            'cos': jnp.tile(cos, (1, ATT_KV_HEADS)), 'sin': jnp.tile(sin, (1, ATT_KV_HEADS)),
            'cos_t': cos.T, 'sin_t': sin.T,
            'ret_lg': jnp.stack([ret_lg_f[l], ret_lg_b[l]]), 'ret_gn': ret_norm_g[l][None],
            'w_ret_out': w_ret_out[l].astype(BF16), 'w_att_out': w_att_out[l].astype(BF16),
            'w_su_t': w_l[:, C_SU:C_G].T.astype(BF16),
            's5_tt': jnp.stack([fwd[0], bwd[0]]), 's5_wx': jnp.stack([fwd[1], bwd[1]]),
            's5_wy': jnp.stack([fwd[2], bwd[2]]), 's5_a': jnp.stack([fwd[3], bwd[3]]),
            's5_aseg': jnp.stack([fwd[4], bwd[4]]),
            'ssm_d': ssm_d[l][:, None], 'w_glu': w_glu[l].astype(BF16), 'w_out': w_out[l].astype(BF16),
            'w_router_t': w_router.T, 'b_router': b_router[:, None],
            'w_gate': w_gate16, 'w_up': w_up16, 'w_down': w_down16,
        }
        if l == DEPTH - 1:
            p['final_g'] = final_norm_g[None]
        mod = mod_all[l]

        xp, (k_c, v_c, st_r, st_s) = _layer(xp, batch, mod, (0, batch * seq), p, None)
        ks_.append(k_c.reshape(batch, seq, ATT_KV_HEADS, ATT_HEAD_DIM))
        vs_.append(v_c.reshape(batch, seq, ATT_KV_HEADS, ATT_HEAD_DIM))
        rets_.append(st_r)
        ssms_.append(st_s)

        ctx = (cache_attn_k[:, l], cache_attn_v[:, l], state_ret[:, l], state_ssm_re[:, l], state_ssm_im[:, l])
        xs, _ = _layer(xs, dec_batch, mod, (1, dec_seq), p, ctx)

    ssm_all = jnp.stack(ssms_, axis=1)
    ssm_all = ssm_all.transpose(2, 1, 0, 3, 4, 5)
    new_re = ssm_all[..., 0, :]
    new_im = ssm_all[..., 1, :]
    return (xp.reshape(batch, seq, D_MODEL), xs.reshape(dec_batch, dec_seq, D_MODEL),
            jnp.stack(ks_, axis=1), jnp.stack(vs_, axis=1), jnp.stack(rets_, axis=1), new_re, new_im)
```

```python
import functools
import math

import jax
import jax.numpy as jnp
from jax import lax
from jax.experimental import pallas as pl
from jax.experimental.pallas import tpu as pltpu

F32 = jnp.float32
BF16 = jnp.bfloat16

D_MODEL = 1024
DEPTH = 2
GRID_W = 64
RET_HEADS = 4
RET_DK = 128
RET_DV = 128
RET_CHUNK = 256
RET_W = RET_HEADS * RET_DV
ATT_HEADS = 8
ATT_KV_HEADS = 2
ATT_GROUP = ATT_HEADS // ATT_KV_HEADS
ATT_HEAD_DIM = 64
ATT_W = ATT_HEADS * ATT_HEAD_DIM
ATT_KV_W = ATT_KV_HEADS * ATT_HEAD_DIM
ROPE_THETA = 10000.0
SSM_CH = 512
SSM_GROUP = 16
SSM_GROUPS = SSM_CH // SSM_GROUP
SSM_STATE = 64
N_EXPERTS = 16
N_EXPERT_GROUPS = 4
EXPERTS_PER_GROUP = N_EXPERTS // N_EXPERT_GROUPS
D_FF = 512
N_MOD = 6
EPS = 1e-6

C_RQ, C_RK, C_RV, C_RG = 0, 512, 1024, 1536
C_AQ, C_AK, C_AV, C_SU, C_G, C_END = 2048, 2560, 2688, 2816, 3328, 6400

VMEM_LIMIT = 52 * 1024 * 1024

ATT_Q_SCALE = ATT_HEAD_DIM ** -0.5 * math.log2(math.e)
ATT_QB = 512
ATT_KC = 256
ATT_V_ROWS = ATT_HEAD_DIM + 16
ATT_RING = 4

TM_PROJ = 512
RET_UNROLL = 4
MOE_TILE = 512
MOE_PART = 4096
META_ROWS = 8
S5_ROWS = 32
S5_SEG = 256
S5_T = 16
S5_GL = 2 * SSM_STATE
S5_GB = 8
S5_TM = 2048
S5_PITCH = 24
LANES = 128
STAGE_ROWS = 256
ADA_ROWS = 8
ADA_TN = 1536


def _dot(a, b):
    return jnp.dot(a, b, preferred_element_type=F32)


def _dot_nt(a, b):
    return lax.dot_general(a, b, (((1,), (1,)), ((), ())), preferred_element_type=F32)


def _dot_tn(a, b):
    return lax.dot_general(a, b, (((0,), (0,)), ((), ())), preferred_element_type=F32)


def _sigmoid(x):
    return 0.5 * jnp.tanh(0.5 * x) + 0.5


def _const_spec(shape):
    n = len(shape)
    return pl.BlockSpec(shape, lambda *_: (0,) * n, pipeline_mode=pl.Buffered(1))


def _params(sem, vmem=VMEM_LIMIT):
    return pltpu.CompilerParams(dimension_semantics=sem, vmem_limit_bytes=vmem)


def _mod_row_fn(mod_rows, tm):
    base, per_row = mod_rows
    return lambda i: base + (i * tm) // per_row


def _modulated_norm(x, g, mod_ref):
    xn = x * lax.rsqrt(jnp.mean(x * x, axis=-1, keepdims=True) + EPS) * g
    return xn * (1.0 + mod_ref[0, 1:2, :]) + mod_ref[0, 0:1, :]


def _ada_kernel(c_ref, w_ref, b_ref, o_ref):
    c = c_ref[...]
    a = (c * _sigmoid(c)).astype(BF16)
    o_ref[0] = _dot(a, w_ref[0].astype(BF16)) + b_ref[0]


def _ada(c_rows, w_ada, b_ada):
    n_col = N_MOD * D_MODEL
    return pl.pallas_call(
        _ada_kernel,
        out_shape=jax.ShapeDtypeStruct((DEPTH, ADA_ROWS, n_col), F32),
        grid=(DEPTH, n_col // ADA_TN),
        in_specs=[
            pl.BlockSpec((ADA_ROWS, D_MODEL), lambda l, j: (0, 0)),
            pl.BlockSpec((1, D_MODEL, ADA_TN), lambda l, j: (l, 0, j)),
            pl.BlockSpec((1, 1, ADA_TN), lambda l, j: (l, 0, j)),
        ],
        out_specs=pl.BlockSpec((1, ADA_ROWS, ADA_TN), lambda l, j: (l, 0, j)),
        compiler_params=_params(("arbitrary", "arbitrary")),
    )(c_rows, w_ada, b_ada.reshape(DEPTH, 1, n_col))


def _head_mean_sq(z, ones):
    z2 = z * z
    hi = z2.astype(BF16)
    lo = (z2 - hi.astype(F32)).astype(BF16)
    return (_dot(hi, ones) + _dot(lo, ones)) * (1.0 / ATT_HEAD_DIM)


def _inproj_kernel(*refs, latent):
    if latent:
        (x_ref, mod_ref, g1_ref, w_ref, wqt_ref, wvt_ref, ones_ref, gq_ref, gk_ref,
         wqst_ref, wks_ref, gqs_ref, gks_ref, cos_ref, sin_ref, cost_ref, sint_ref,
         rqkv_ref, rg_ref, gates_ref, qt_ref, k16_ref, vt_ref) = refs
    else:
        (x_ref, mod_ref, g1_ref, w_ref, wqt_ref, wvt_ref, ones_ref, gq_ref, gk_ref,
         rqkv_ref, rg_ref, gates_ref, qt_ref, k16_ref, vt_ref, ak_ref, av_ref) = refs
    tm = x_ref.shape[0]
    h = _modulated_norm(x_ref[...], g1_ref[...], mod_ref).astype(BF16)

    def seg(a, b):
        return _dot(h, w_ref[0, :, a:b])

    rqkv_ref[:, 0:RET_W] = (seg(C_RQ, C_RK) * (RET_DK ** -0.5)).astype(BF16)
    rqkv_ref[:, RET_W:3 * RET_W] = seg(C_RK, C_RG).astype(BF16)
    rg_ref[...] = seg(C_RG, C_AQ).astype(BF16)
    for j in range(3):
        gates_ref[:, j * D_MODEL:(j + 1) * D_MODEL] = seg(C_G + j * D_MODEL, C_G + (j + 1) * D_MODEL).astype(BF16)

    zk = seg(C_AK, C_AV)
    inv_k = lax.rsqrt(_head_mean_sq(zk, ones_ref[...]) + EPS)
    yk = zk * inv_k * gk_ref[...]
    zq = _dot_nt(wqt_ref[...], h).reshape(ATT_HEADS, ATT_HEAD_DIM, tm)
    inv_q = lax.rsqrt(jnp.mean(zq * zq, axis=1, keepdims=True) + EPS)
    yq = zq * inv_q * gq_ref[...]
    if latent:
        yk_sw = _dot(h, wks_ref[...]) * inv_k * gks_ref[...]
        yk = yk * cos_ref[...] + yk_sw * sin_ref[...]
        zq_sw = _dot_nt(wqst_ref[...], h).reshape(ATT_HEADS, ATT_HEAD_DIM, tm)
        yq = yq * cost_ref[...] + (zq_sw * inv_q * gqs_ref[...]) * sint_ref[...]
    qt_ref[...] = (yq * ATT_Q_SCALE).reshape(ATT_W, tm).astype(BF16)
    k16_ref[...] = yk.astype(BF16)
    vt = _dot_nt(wvt_ref[...], h).astype(BF16)
    for c in range(tm // ATT_KC):
        vt_ref[c] = vt[:, c * ATT_KC:(c + 1) * ATT_KC]
    if not latent:
        ak_ref[...] = yk
        av_ref[...] = seg(C_AV, C_SU)


def _inproj(x, mod, mod_rows, p, latent):
    n = x.shape[0]
    tm = TM_PROJ
    mod_row = _mod_row_fn(mod_rows, tm)
    row = lambda i: (i, 0)
    in_specs = [
        pl.BlockSpec((tm, D_MODEL), row),
        pl.BlockSpec((1, N_MOD, D_MODEL), lambda i: (mod_row(i), 0, 0)),
        _const_spec((1, D_MODEL)),
        pl.BlockSpec((1, D_MODEL, C_END), lambda i: (p['layer'], 0, 0), pipeline_mode=pl.Buffered(1)),
        _const_spec((ATT_W, D_MODEL)),
        _const_spec((ATT_KV_W, D_MODEL)),
        _const_spec((ATT_KV_W, ATT_KV_W)),
        _const_spec((ATT_HEAD_DIM, 1)),
        _const_spec((1, ATT_KV_W)),
    ]
    args = [x, mod, p['g1'], p['w_in'], p['wq_t'], p['wv_t'], p['ones'], p['gq'], p['gk']]
    if latent:
        n_pos = p['cos'].shape[0] // tm
        in_specs += [
            _const_spec((ATT_W, D_MODEL)),
            _const_spec((D_MODEL, ATT_KV_W)),
            _const_spec((ATT_HEAD_DIM, 1)),
            _const_spec((1, ATT_KV_W)),
            pl.BlockSpec((tm, ATT_KV_W), lambda i: (i % n_pos, 0)),
            pl.BlockSpec((tm, ATT_KV_W), lambda i: (i % n_pos, 0)),
            pl.BlockSpec((ATT_HEAD_DIM, tm), lambda i: (0, i % n_pos)),
            pl.BlockSpec((ATT_HEAD_DIM, tm), lambda i: (0, i % n_pos)),
        ]
        args += [p['wqs_t'], p['wk_sw'], p['gqs'], p['gks'], p['cos'], p['sin'], p['cos_t'], p['sin_t']]
    out_shape = [
        jax.ShapeDtypeStruct((n, 3 * RET_W), BF16),
        jax.ShapeDtypeStruct((n, RET_W), BF16),
        jax.ShapeDtypeStruct((n, 3 * D_MODEL), BF16),
        jax.ShapeDtypeStruct((ATT_W, n), BF16),
        jax.ShapeDtypeStruct((n, ATT_KV_W), BF16),
        jax.ShapeDtypeStruct((n // ATT_KC, ATT_KV_W, ATT_KC), BF16),
    ]
    out_specs = [
        pl.BlockSpec((tm, 3 * RET_W), row),
        pl.BlockSpec((tm, RET_W), row),
        pl.BlockSpec((tm, 3 * D_MODEL), row),
        pl.BlockSpec((ATT_W, tm), lambda i: (0, i)),
        pl.BlockSpec((tm, ATT_KV_W), row),
        pl.BlockSpec((tm // ATT_KC, ATT_KV_W, ATT_KC), lambda i: (i, 0, 0)),
    ]
    if not latent:
        out_shape += [jax.ShapeDtypeStruct((n, ATT_KV_W), F32)] * 2
        out_specs += [pl.BlockSpec((tm, ATT_KV_W), row)] * 2
    return pl.pallas_call(
        functools.partial(_inproj_kernel, latent=latent),
        out_shape=tuple(out_shape),
        grid=(n // tm,),
        in_specs=in_specs,
        out_specs=tuple(out_specs),
        compiler_params=_params(("parallel",)),
    )(*args)


T_DEC_F, T_DEC_B, T_XI_F, T_ZETA_F, T_XI_B, T_ZETA_B, T_CD_F, T_CD_B, N_TAB = range(9)


def _ret_kernel(*refs, has_s0, bb, seq):
    if has_s0:
        lg_ref, q_ref, k_ref, v_ref, rg_ref, gn_ref, s0_ref, o_ref, st_ref, tab_ref, acc_ref = refs
    else:
        lg_ref, q_ref, k_ref, v_ref, rg_ref, gn_ref, o_ref, st_ref, tab_ref, acc_ref = refs
    c = RET_CHUNK
    head = pl.program_id(1)
    lgf = lg_ref[0, head]
    lgb = lg_ref[1, head]
    t = lax.broadcasted_iota(jnp.int32, (c, c), 0).astype(F32)
    s = lax.broadcasted_iota(jnp.int32, (c, c), 1).astype(F32)
    tab_ref[T_DEC_F] = jnp.where(t >= s, jnp.exp(lgf * jnp.maximum(t - s, 0.0)), 0.0)
    tab_ref[T_DEC_B] = jnp.where(s >= t, jnp.exp(lgb * jnp.maximum(s - t, 0.0)), 0.0)
    tab_ref[T_XI_F] = jnp.exp(lgf * (t + 1.0))
    tab_ref[T_ZETA_F] = jnp.exp(lgf * (c - 1.0 - t))
    tab_ref[T_XI_B] = jnp.exp(lgb * (c - t))
    tab_ref[T_ZETA_B] = jnp.exp(lgb * t)
    tab_ref[T_CD_F] = jnp.exp(lgf * (c + 0.0 * t))
    tab_ref[T_CD_B] = jnp.exp(lgb * (c + 0.0 * t))
    if has_s0:
        st_ref[...] = s0_ref[...]
    else:
        st_ref[...] = jnp.zeros_like(st_ref)
    acc_ref[...] = jnp.zeros_like(acc_ref)
    n_chunk = seq // c

    def one_direction(rb, d, off, t_dec, t_xi, t_zeta, t_cd):
        rows = pl.ds(off, c)
        q = q_ref[rb, rows, :]
        k = k_ref[rb, rows, :]
        v = v_ref[rb, rows, :]
        p = (_dot_nt(q, k) * tab_ref[t_dec]).astype(BF16)
        st = st_ref[rb, d, 0]
        q_xi = (q.astype(F32) * tab_ref[t_xi, :, 0:RET_DK]).astype(BF16)
        o = _dot(jnp.concatenate([p, q_xi], axis=1), jnp.concatenate([v, st.astype(BF16)], axis=0))
        acc_ref[rb, rows, :] += o
        kz = (k.astype(F32) * tab_ref[t_zeta, :, 0:RET_DK]).astype(BF16)
        st_ref[rb, d, 0] = tab_ref[t_cd, 0:RET_DK, 0:RET_DV] * st + _dot_tn(kz, v)

    def body(i, carry):
        off_f = pl.multiple_of(i * c, c)
        off_b = pl.multiple_of((n_chunk - 1 - i) * c, c)
        for rb in range(bb):
            one_direction(rb, 0, off_f, T_DEC_F, T_XI_F, T_ZETA_F, T_CD_F)
            one_direction(rb, 1, off_b, T_DEC_B, T_XI_B, T_ZETA_B, T_CD_B)
        return carry

    lax.fori_loop(0, n_chunk, body, 0, unroll=RET_UNROLL if bb == 1 else 1)

    def finish(i, carry):
        rows = pl.ds(pl.multiple_of(i * c, c), c)
        for rb in range(bb):
            o = acc_ref[rb, rows, :]
            ro = o * lax.rsqrt(jnp.mean(o * o, axis=-1, keepdims=True) + EPS) * gn_ref[...]
            g = rg_ref[rb, rows, :].astype(F32)
            o_ref[rb, rows, :] = (ro * (g * _sigmoid(g))).astype(BF16)
        return carry

    lax.fori_loop(0, n_chunk, finish, 0)


def _retention(rqkv, rg, lg, gn, s0, bb):
    b, seq, _ = rqkv.shape
    has_s0 = s0 is not None
    blk = (bb, seq, RET_DK)
    st_spec = pl.BlockSpec((bb, 2, 1, RET_DK, RET_DV), lambda i, h: (i, 0, h, 0, 0))
    in_specs = [
        pl.BlockSpec(memory_space=pltpu.SMEM),
        pl.BlockSpec(blk, lambda i, h: (i, 0, h)),
        pl.BlockSpec(blk, lambda i, h: (i, 0, RET_HEADS + h)),
        pl.BlockSpec(blk, lambda i, h: (i, 0, 2 * RET_HEADS + h)),
        pl.BlockSpec(blk, lambda i, h: (i, 0, h)),
        pl.BlockSpec((1, RET_DV), lambda i, h: (0, h)),
    ]
    args = [lg, rqkv, rqkv, rqkv, rg, gn]
    if has_s0:
        in_specs.append(st_spec)
        args.append(s0)
    return pl.pallas_call(
        functools.partial(_ret_kernel, has_s0=has_s0, bb=bb, seq=seq),
        out_shape=(
            jax.ShapeDtypeStruct((b, seq, RET_W), BF16),
            jax.ShapeDtypeStruct((b, 2, RET_HEADS, RET_DK, RET_DV), F32),
        ),
        grid=(b // bb, RET_HEADS),
        in_specs=in_specs,
        out_specs=(pl.BlockSpec(blk, lambda i, h: (i, 0, h)), st_spec),
        scratch_shapes=[
            pltpu.VMEM((N_TAB, RET_CHUNK, RET_CHUNK), F32),
            pltpu.VMEM((bb, seq, RET_DV), F32),
        ],
        compiler_params=_params(("parallel", "parallel")),
    )(*args)


ST_M, ST_ALPHA, N_ST = range(3)


def _attn_kernel(*refs, has_ctx):
    if has_ctx:
        q_ref, k_ref, v_ref, kc_ref, vc_ref, o_ref, s_ref, p_ref, acc_ref, st_ref, mx_ref = refs
    else:
        q_ref, k_ref, v_ref, o_ref, s_ref, p_ref, acc_ref, st_ref, mx_ref = refs
    kv_head = pl.program_id(1)
    qb = q_ref.shape[1]
    kc = s_ref.shape[1]
    n_own = k_ref.shape[0] // kc
    n_chunk = n_own + (1 if has_ctx else 0)
    q_t = jnp.concatenate([q_ref[h * ATT_HEAD_DIM:(h + 1) * ATT_HEAD_DIM, :] for h in range(ATT_GROUP)], axis=1)
    q_both = jnp.concatenate([q_t] * ATT_KV_HEADS, axis=0)
    row = lax.broadcasted_iota(jnp.int32, q_both.shape, 0)
    q_pad = jnp.where(row // ATT_HEAD_DIM == kv_head, q_both, jnp.zeros_like(q_both))
    ones_rows = (lax.broadcasted_iota(jnp.int32, (ATT_V_ROWS - ATT_HEAD_DIM, kc), 0) == 0).astype(BF16)

    def keys(j):
        if has_ctx and isinstance(j, int) and j == n_own:
            return kc_ref[...]
        return k_ref[pl.ds(pl.multiple_of(j * kc, kc), kc), :]

    def values(j):
        v = vc_ref[...] if has_ctx and isinstance(j, int) and j == n_own else v_ref[j]
        return jnp.concatenate([v, ones_rows], axis=0)

    def put_scores(j, slot):
        s = _dot(keys(j), q_pad)
        s_ref[slot] = s
        mx_ref[slot] = jnp.max(s, axis=0, keepdims=True)

    def weighted_values(j, slot):
        return _dot(values(j), p_ref[slot])

    def phase(j, slot, first=False, last=False):
        if not last:
            put_scores(j + 1, (slot + 1) % ATT_RING)
        if not first:
            acc_ref[...] = st_ref[ST_ALPHA] * acc_ref[...] + weighted_values(j - 1, (slot - 1) % ATT_RING)
        m = st_ref[ST_M]
        m_new = jnp.maximum(m, mx_ref[slot])
        st_ref[ST_ALPHA] = jnp.exp2(m - m_new)
        st_ref[ST_M] = m_new
        p_ref[slot] = jnp.exp2((s_ref[slot] - m_new).astype(BF16))

    st_ref[ST_M] = jnp.full(st_ref.shape[1:], -jnp.inf, F32)
    acc_ref[...] = jnp.zeros_like(acc_ref)
    put_scores(0, 0)
    phase(0, 0, first=True, last=n_chunk == 1)
    n_loop = max(n_own - 2, 0) // ATT_RING

    def revolution(i, carry):
        for t in range(1, ATT_RING + 1):
            phase(ATT_RING * i + t, t % ATT_RING)
        return carry

    lax.fori_loop(0, n_loop, revolution, 0)
    for j in range(1 + ATT_RING * n_loop, n_chunk):
        phase(j, j % ATT_RING, last=j == n_chunk - 1)
    acc = st_ref[ST_ALPHA] * acc_ref[...] + weighted_values(n_chunk - 1, (n_chunk - 1) % ATT_RING)
    out = (acc[0:ATT_HEAD_DIM] / acc[ATT_HEAD_DIM:ATT_HEAD_DIM + 1]).astype(BF16)
    for h in range(ATT_GROUP):
        o_ref[h * ATT_HEAD_DIM:(h + 1) * ATT_HEAD_DIM, :] = out[:, h * qb:(h + 1) * qb]


def _attention(q_t, k, v_t, k_ctx, v_ctx_t, batch):
    n = q_t.shape[1]
    seq = n // batch
    qb, kc = min(ATT_QB, seq), ATT_KC
    n_q = seq // qb
    has_ctx = k_ctx is not None
    q_spec = pl.BlockSpec((ATT_GROUP * ATT_HEAD_DIM, qb), lambda b, g, j: (g, b * n_q + j))
    in_specs = [
        q_spec,
        pl.BlockSpec((seq, ATT_KV_W), lambda b, g, j: (b, 0)),
        pl.BlockSpec((seq // kc, ATT_HEAD_DIM, kc), lambda b, g, j: (b, g, 0)),
    ]
    args = [q_t, k, v_t]
    if has_ctx:
        in_specs += [
            pl.BlockSpec((kc, ATT_KV_W), lambda b, g, j: (b, 0)),
            pl.BlockSpec((ATT_HEAD_DIM, kc), lambda b, g, j: (b * ATT_KV_HEADS + g, 0)),
        ]
        args += [k_ctx, v_ctx_t]
    width = ATT_GROUP * qb
    return pl.pallas_call(
        functools.partial(_attn_kernel, has_ctx=has_ctx),
        out_shape=jax.ShapeDtypeStruct(q_t.shape, BF16),
        grid=(batch, ATT_KV_HEADS, n_q),
        in_specs=in_specs,
        out_specs=q_spec,
        scratch_shapes=[
            pltpu.VMEM((ATT_RING, kc, width), F32),
            pltpu.VMEM((ATT_RING, kc, width), BF16),
            pltpu.VMEM((ATT_V_ROWS, width), F32),
            pltpu.VMEM((N_ST, 1, width), F32),
            pltpu.VMEM((ATT_RING, 1, width), F32),
        ],
        compiler_params=_params(("parallel", "parallel", "parallel")),
    )(*args)


def _s5_in_kernel(x_ref, mod_ref, g1_ref, w_ref, o_ref, h_ref):
    tm = x_ref.shape[0]
    n_chunk = o_ref.shape[2]
    n_col = D_MODEL // LANES
    per_stage = STAGE_ROWS // S5_T

    def stage(i, carry):
        h = _modulated_norm(x_ref[pl.ds(pl.multiple_of(i * STAGE_ROWS, STAGE_ROWS), STAGE_ROWS), :], g1_ref[...],
                            mod_ref)
        for c in range(per_stage):
            rows = pl.ds(pl.multiple_of((i * per_stage + c) * S5_PITCH, S5_PITCH), S5_T)
            for j in range(n_col):
                h_ref[j, rows, :] = h[c * S5_T:(c + 1) * S5_T, j * LANES:(j + 1) * LANES]
        return carry

    lax.fori_loop(0, tm // STAGE_ROWS, stage, 0)
    for s in range(S5_T):
        h = jnp.concatenate([h_ref[j, pl.ds(s, n_chunk, stride=S5_PITCH), :] for j in range(n_col)], axis=1)
        o_ref[s] = _dot_nt(w_ref[...], h.astype(BF16)).astype(BF16)


def _s5_in(x, mod, mod_rows, g1, w_su_t):
    n = x.shape[0]
    tm = S5_TM
    mod_row = _mod_row_fn(mod_rows, tm)
    return pl.pallas_call(
        _s5_in_kernel,
        out_shape=jax.ShapeDtypeStruct((S5_T, SSM_CH, n // S5_T), BF16),
        grid=(n // tm,),
        in_specs=[
            pl.BlockSpec((tm, D_MODEL), lambda i: (i, 0)),
            pl.BlockSpec((1, N_MOD, D_MODEL), lambda i: (mod_row(i), 0, 0)),
            _const_spec((1, D_MODEL)),
            _const_spec((SSM_CH, D_MODEL)),
        ],
        out_specs=pl.BlockSpec((S5_T, SSM_CH, tm // S5_T), lambda i: (0, 0, i)),
        scratch_shapes=[pltpu.VMEM((D_MODEL // LANES, tm // S5_T * S5_PITCH, LANES), F32)],
        compiler_params=_params(("parallel",)),
    )(x, mod, g1, w_su_t)


def _s5_kernel(*refs, two_pass):
    if two_pass:
        u_ref, tt_ref, wx_ref, wy_ref, a_ref, aseg_ref, s0_ref, y_ref, v_ref, init_ref = refs
    else:
        u_ref, tt_ref, wx_ref, wy_ref, a_ref, y_ref, fin_ref, v_ref = refs
    gb = tt_ref.shape[1]
    n_chunk = u_ref.shape[2] // S5_ROWS
    grp = SSM_GROUP

    def toeplitz(d, j):
        strip = tt_ref[d, j]
        width = S5_T * grp
        starts = [(S5_T - 1 - t) * grp for t in range(S5_T)]
        return jnp.concatenate([strip[:, a:a + width] for a in starts], axis=0)

    def put_outputs(j, yt, accumulate):
        for t in range(S5_T):
            if accumulate:
                y_ref[t, j * grp:(j + 1) * grp, :] += yt[t * grp:(t + 1) * grp, :]
            else:
                y_ref[t, j * grp:(j + 1) * grp, :] = yt[t * grp:(t + 1) * grp, :]

    for j in range(gb):
        z = jnp.concatenate([u_ref[s, j * grp:(j + 1) * grp, :] for s in range(S5_T)], axis=0)
        put_outputs(j, _dot(toeplitz(0, j), z) + _dot(toeplitz(1, j), z), accumulate=False)
        for d in range(2):
            v_ref[d, j] = _dot_tn(z, wx_ref[d, j])

    def times(coef_ref, d, j, x):
        lanes = slice(j * S5_GL, (j + 1) * S5_GL)
        return coef_ref[d, 0:1, lanes] * x + coef_ref[d, 1:2, lanes] * pltpu.roll(x, SSM_STATE, axis=1)

    def scan(init, keep_states):
        xs = [list(init[0]), list(init[1])]
        for s in range(n_chunk):
            for d in range(2):
                c = s if d == 0 else n_chunk - 1 - s
                rows = pl.ds(c, S5_ROWS, stride=n_chunk)
                for j in range(gb):
                    v = v_ref[d, j, rows, :]
                    if keep_states:
                        v_ref[d, j, rows, :] = xs[d][j]
                    xs[d][j] = times(a_ref, d, j, xs[d][j]) + v
        return xs

    zero = [[jnp.zeros((S5_ROWS, S5_GL), F32)] * gb] * 2
    if two_pass:
        local = scan(zero, keep_states=False)
        n_batch = s0_ref.shape[1]
        n_seg = S5_ROWS // n_batch
        for d in range(2):
            order = range(n_seg) if d == 0 else range(n_seg - 1, -1, -1)
            for j in range(gb):
                for b in range(n_batch):
                    cur = s0_ref[d, b:b + 1, j * S5_GL:(j + 1) * S5_GL]
                    for sgm in order:
                        r = b * n_seg + sgm
                        init_ref[d, j, r:r + 1, :] = cur
                        cur = times(aseg_ref, d, j, cur) + local[d][j][r:r + 1, :]
        scan([[init_ref[d, j] for j in range(gb)] for d in range(2)], keep_states=True)
    else:
        final = scan(zero, keep_states=True)
        for d in range(2):
            for j in range(gb):
                fin_ref[d, :, j * S5_GL:(j + 1) * S5_GL] = final[d][j]
    for j in range(gb):
        put_outputs(j, _dot_nt(wy_ref[0, j], v_ref[0, j].astype(BF16))
                    + _dot_nt(wy_ref[1, j], v_ref[1, j].astype(BF16)), accumulate=True)


def _s5(u, tt, wx, wy, a_t, a_seg, s0):
    two_pass = s0 is not None
    n_rows = u.shape[2]
    gb = S5_GB
    io_spec = pl.BlockSpec((S5_T, gb * SSM_GROUP, n_rows), lambda i: (0, i, 0))
    coef_spec = pl.BlockSpec((2, 2, gb * S5_GL), lambda i: (0, 0, i))
    in_specs = [
        io_spec,
        pl.BlockSpec((2, gb) + tt.shape[2:], lambda i: (0, i, 0, 0)),
        pl.BlockSpec((2, gb) + wx.shape[2:], lambda i: (0, i, 0, 0)),
        pl.BlockSpec((2, gb) + wy.shape[2:], lambda i: (0, i, 0, 0)),
        coef_spec,
    ]
    args = [u, tt, wx, wy, a_t]
    y_shape = jax.ShapeDtypeStruct(u.shape, F32)
    scratch = [pltpu.VMEM((2, gb, n_rows, S5_GL), F32)]
    if two_pass:
        in_specs += [coef_spec, pl.BlockSpec((2, s0.shape[1], gb * S5_GL), lambda i: (0, 0, i))]
        args += [a_seg, s0]
        out_shape, out_specs = y_shape, io_spec
        scratch.append(pltpu.VMEM((2, gb, S5_ROWS, S5_GL), F32))
    else:
        out_shape = (y_shape, jax.ShapeDtypeStruct((2, S5_ROWS, SSM_GROUPS * S5_GL), F32))
        out_specs = (io_spec, pl.BlockSpec((2, S5_ROWS, gb * S5_GL), lambda i: (0, 0, i)))
    return pl.pallas_call(
        functools.partial(_s5_kernel, two_pass=two_pass),
        out_shape=out_shape,
        grid=(SSM_GROUPS // gb,),
        in_specs=in_specs,
        out_specs=out_specs,
        scratch_shapes=scratch,
        compiler_params=_params(("parallel",)),
    )(*args)


def _s5_out_kernel(y_ref, u_ref, d_ref, w_ref, o_ref, buf_ref):
    n_chunk = y_ref.shape[2]
    n_col = D_MODEL // LANES
    for s in range(S5_T):
        y = y_ref[s] + d_ref[...] * u_ref[s].astype(F32)
        glu = _dot_tn(jax.nn.gelu(y).astype(BF16), w_ref[...])
        out = glu[:, 0:D_MODEL] * _sigmoid(glu[:, D_MODEL:2 * D_MODEL])
        for j in range(n_col):
            buf_ref[j, pl.ds(s, n_chunk, stride=S5_PITCH), :] = out[:, j * LANES:(j + 1) * LANES]

    def unstage(c, carry):
        src = pl.ds(pl.multiple_of(c * S5_PITCH, S5_PITCH), S5_T)
        dst = pl.ds(pl.multiple_of(c * S5_T, S5_T), S5_T)
        o_ref[dst, :] = jnp.concatenate([buf_ref[j, src, :] for j in range(n_col)], axis=1).astype(BF16)
        return carry

    lax.fori_loop(0, n_chunk, unstage, 0, unroll=8)


def _s5_out(y, u, d_skip, w_glu):
    n = y.shape[2] * S5_T
    tm = S5_TM
    io_spec = pl.BlockSpec((S5_T, SSM_CH, tm // S5_T), lambda i: (0, 0, i))
    return pl.pallas_call(
        _s5_out_kernel,
        out_shape=jax.ShapeDtypeStruct((n, D_MODEL), BF16),
        grid=(n // tm,),
        in_specs=[io_spec, io_spec, _const_spec((SSM_CH, 1)), _const_spec((SSM_CH, 2 * D_MODEL))],
        out_specs=pl.BlockSpec((tm, D_MODEL), lambda i: (i, 0)),
        scratch_shapes=[pltpu.VMEM((D_MODEL // LANES, tm // S5_T * S5_PITCH, LANES), F32)],
        compiler_params=_params(("parallel",)),
    )(y, u, d_skip, w_glu)


def _route(probs):
    epg = EXPERTS_PER_GROUP
    groups = [probs[g * epg:(g + 1) * epg] for g in range(N_EXPERT_GROUPS)]
    scores = []
    for grp in groups:
        best = None
        for a in range(epg):
            for b in range(a + 1, epg):
                pair = grp[a] + grp[b]
                best = pair if best is None else jnp.maximum(best, pair)
        scores.append(best)
    top_score = scores[0]
    top_group = jnp.zeros_like(scores[0], dtype=jnp.int32)
    for g in range(1, N_EXPERT_GROUPS):
        better = scores[g] > top_score
        top_score = jnp.where(better, scores[g], top_score)
        top_group = jnp.where(better, g, top_group)
    sel = []
    for j in range(epg):
        v = groups[0][j]
        for g in range(1, N_EXPERT_GROUPS):
            v = jnp.where(top_group == g, groups[g][j], v)
        sel.append(v)
    v1 = sel[0]
    i1 = jnp.zeros_like(top_group)
    for j in range(1, epg):
        better = sel[j] > v1
        v1 = jnp.where(better, sel[j], v1)
        i1 = jnp.where(better, j, i1)
    v2 = jnp.full_like(v1, -1.0)
    i2 = jnp.zeros_like(top_group)
    for j in range(epg):
        better = (sel[j] > v2) & (i1 != j)
        v2 = jnp.where(better, sel[j], v2)
        i2 = jnp.where(better, j, i2)
    total = v1 + v2
    w1 = v1 / total
    w2 = v2 / total
    return top_group, [jnp.where(i1 == j, w1, jnp.where(i2 == j, w2, 0.0)) for j in range(epg)]


def _merge_kernel(x_ref, ro_ref, att_ref, ssm_ref, gates_ref, mod_ref, wro_ref, wao_ref, wout_ref, g2_ref, wrt_ref,
                  br_ref, x1_ref, h2_ref, meta_ref, cnt_ref):
    ret_branch = _dot(ro_ref[...], wro_ref[...])
    att_branch = _dot_tn(att_ref[...], wao_ref[...])

    def gate(j):
        return _sigmoid(gates_ref[:, j * D_MODEL:(j + 1) * D_MODEL].astype(F32))

    merged = gate(0) * ret_branch + gate(1) * att_branch + gate(2) * ssm_ref[...].astype(F32)
    x1 = x_ref[...] + mod_ref[0, 2:3, :] * _dot(merged.astype(BF16), wout_ref[...])
    x1_ref[...] = x1
    xn = x1 * lax.rsqrt(jnp.mean(x1 * x1, axis=-1, keepdims=True) + EPS) * g2_ref[...]
    h2 = xn * (1.0 + mod_ref[0, 4:5, :]) + mod_ref[0, 3:4, :]
    h2_ref[...] = h2.astype(BF16)

    logits = lax.dot_general(wrt_ref[...], h2, (((1,), (1,)), ((), ())), precision=lax.Precision.HIGHEST,
                             preferred_element_type=F32) + br_ref[...]
    e = jnp.exp(logits - jnp.max(logits, axis=0, keepdims=True))
    probs = e / jnp.sum(e, axis=0, keepdims=True)
    top_group, weights = _route([probs[j:j + 1, :] for j in range(N_EXPERTS)])

    tm = top_group.shape[1]

    @pl.when(pl.program_id(0) % (MOE_PART // tm) == 0)
    def _():
        cnt_ref[...] = jnp.zeros_like(cnt_ref)

    zero_row = jnp.zeros((1, tm), F32)
    onehot = jnp.concatenate([(top_group == g).astype(F32) for g in range(N_EXPERT_GROUPS)]
                             + [zero_row] * (META_ROWS - N_EXPERT_GROUPS), axis=0)
    lane = lax.broadcasted_iota(jnp.int32, onehot.shape, 1)
    incl = onehot
    shift = 1
    while shift < tm:
        incl = incl + jnp.where(lane >= shift, pltpu.roll(incl, shift, axis=1), 0.0)
        shift *= 2
    before = cnt_ref[0, :, 0:1]
    rank = jnp.sum(onehot * (incl - onehot + before), axis=0, keepdims=True)
    cnt_ref[0] = jnp.broadcast_to(before + jnp.sum(onehot, axis=1, keepdims=True), cnt_ref.shape[1:])
    meta_ref[...] = jnp.concatenate([top_group.astype(F32), rank] + weights
                                    + [zero_row] * (META_ROWS - 2 - EXPERTS_PER_GROUP), axis=0)


def _merge(x, ro, att_t, ssm, gates, mod, mod_rows, wro, wao, wout, g2, wrt, br):
    n = x.shape[0]
    tm = TM_PROJ
    mod_row = _mod_row_fn(mod_rows, tm)
    row = lambda i: (i, 0)
    in_specs = [
        pl.BlockSpec((tm, D_MODEL), row),
        pl.BlockSpec((tm, RET_W), row),
        pl.BlockSpec((ATT_W, tm), lambda i: (0, i)),
        pl.BlockSpec((tm, D_MODEL), row),
        pl.BlockSpec((tm, 3 * D_MODEL), row),
        pl.BlockSpec((1, N_MOD, D_MODEL), lambda i: (mod_row(i), 0, 0)),
        _const_spec((RET_W, D_MODEL)),
        _const_spec((ATT_W, D_MODEL)),
        _const_spec((D_MODEL, D_MODEL)),
        _const_spec((1, D_MODEL)),
        _const_spec((N_EXPERTS, D_MODEL)),
        _const_spec((N_EXPERTS, 1)),
    ]
    return pl.pallas_call(
        _merge_kernel,
        out_shape=(
            jax.ShapeDtypeStruct((n, D_MODEL), F32),
            jax.ShapeDtypeStruct((n, D_MODEL), BF16),
            jax.ShapeDtypeStruct((META_ROWS, n), F32),
            jax.ShapeDtypeStruct((n // MOE_PART, META_ROWS, LANES), F32),
        ),
        grid=(n // tm,),
        in_specs=in_specs,
        out_specs=(
            pl.BlockSpec((tm, D_MODEL), row),
            pl.BlockSpec((tm, D_MODEL), row),
            pl.BlockSpec((META_ROWS, tm), lambda i: (0, i)),
            pl.BlockSpec((1, META_ROWS, LANES), lambda i: (i // (MOE_PART // tm), 0, 0)),
        ),
        compiler_params=_params(("arbitrary",)),
    )(x, ro, att_t, ssm, gates, mod, wro, wao, wout, g2, wrt, br)


def _moe_kernel(dest_ref, tile_group_ref, n_valid_ref, *refs, final, n_scatter, n_expert, n_gather):
    if final:
        (h2_ref, wrow_ref, wg_ref, wu_ref, wd_ref, x1_ref, mod_ref, fg_ref, o_ref,
         sorted_ref, wsort_ref, stage_ref, acc_ref) = refs
    else:
        (h2_ref, wrow_ref, wg_ref, wu_ref, wd_ref, x1_ref, mod_ref, o_ref,
         sorted_ref, wsort_ref, stage_ref, acc_ref) = refs
    tm = h2_ref.shape[0]
    epg = EXPERTS_PER_GROUP
    per_part = n_scatter + n_expert + n_gather
    part = pl.program_id(0) // per_part
    i = pl.program_id(0) % per_part

    @pl.when(i == 0)
    def _():
        sorted_ref[...] = jnp.zeros_like(sorted_ref)
        wsort_ref[...] = jnp.zeros_like(wsort_ref)

    @pl.when(i < n_scatter)
    def _():
        stage_ref[...] = h2_ref[...].astype(F32)
        base = (part * n_scatter + i) * tm

        def move(t, carry):
            d = dest_ref[base + t]
            sorted_ref[pl.ds(d, 1), :] = stage_ref[pl.ds(t, 1), :]
            wsort_ref[pl.ds(d, 1), :] = wrow_ref[pl.ds(t, 1), :]
            return carry

        lax.fori_loop(0, tm, move, 0, unroll=8)

    step = i - n_scatter
    tile = step // epg
    ein = step % epg

    @pl.when((step >= 0) & (step < n_expert) & (tile < n_valid_ref[part]))
    def _():
        rows = pl.ds(pl.multiple_of(tile * MOE_TILE, MOE_TILE), MOE_TILE)
        x = sorted_ref[rows, :].astype(BF16)
        g = _dot(x, wg_ref[0])
        u = _dot(x, wu_ref[0])
        w = wsort_ref[rows, :]
        lane = lax.broadcasted_iota(jnp.int32, w.shape, 1)
        w_col = jnp.sum(jnp.where(lane == ein, w, 0.0), axis=1, keepdims=True)
        out = _dot((g * _sigmoid(g) * u * w_col).astype(BF16), wd_ref[0])

        @pl.when(ein == 0)
        def _():
            acc_ref[...] = out

        @pl.when(ein > 0)
        def _():
            acc_ref[...] += out

        @pl.when(ein == epg - 1)
        def _():
            sorted_ref[rows, :] = acc_ref[...]

    @pl.when(step >= n_expert)
    def _():
        base = (part * n_gather + step - n_expert) * tm

        def move(t, carry):
            stage_ref[pl.ds(t, 1), :] = sorted_ref[pl.ds(dest_ref[base + t], 1), :]
            return carry

        lax.fori_loop(0, tm, move, 0, unroll=8)
        x2 = x1_ref[...] + mod_ref[0, 5:6, :] * stage_ref[...]
        if final:
            x2 = x2 * lax.rsqrt(jnp.mean(x2 * x2, axis=-1, keepdims=True) + EPS) * fg_ref[...]
        o_ref[...] = x2


def _moe_plan(meta, cnt):
    n = meta.shape[1]
    n_part = cnt.shape[0]
    n_tiles = n // n_part // MOE_TILE + N_EXPERT_GROUPS
    counts = cnt[:, 0:N_EXPERT_GROUPS, 0].astype(jnp.int32)
    padded = (counts + MOE_TILE - 1) // MOE_TILE * MOE_TILE
    ends = jnp.cumsum(padded, axis=1)
    starts = ends - padded
    group = meta[0].astype(jnp.int32).reshape(n_part, -1)
    dest = meta[1].astype(jnp.int32).reshape(n_part, -1)
    for g in range(N_EXPERT_GROUPS):
        dest = dest + jnp.where(group == g, starts[:, g:g + 1], 0)
    tile_start = jnp.arange(n_tiles, dtype=jnp.int32) * MOE_TILE
    tile_group = jnp.sum(tile_start[None, :, None] >= ends[:, None, :], axis=2).astype(jnp.int32)
    tile_group = jnp.minimum(tile_group, N_EXPERT_GROUPS - 1)
    n_valid = (ends[:, N_EXPERT_GROUPS - 1] // MOE_TILE).astype(jnp.int32)
    w_rows = jnp.pad(meta[2:2 + EXPERTS_PER_GROUP].T, ((0, 0), (0, LANES - EXPERTS_PER_GROUP)))
    return dest.reshape(-1), tile_group, n_valid, w_rows


def _moe(h2, meta, cnt, x1, mod, mod_rows, layer, wg, wu, wd, final_g):
    n = h2.shape[0]
    tm = MOE_TILE
    n_part = cnt.shape[0]
    dest, tile_group, n_valid, w_rows = _moe_plan(meta, cnt)
    n_tiles = tile_group.shape[1]
    epg = EXPERTS_PER_GROUP
    n_scatter, n_expert, n_gather = n // n_part // tm, n_tiles * epg, n // n_part // tm
    per_part = n_scatter + n_expert + n_gather
    mod_row = _mod_row_fn(mod_rows, tm)
    final = final_g is not None

    def scatter_tile(i):
        return (i // per_part) * n_scatter + jnp.minimum(i % per_part, n_scatter - 1)

    def gather_tile(i):
        return (i // per_part) * n_gather + jnp.clip(i % per_part - n_scatter - n_expert, 0, n_gather - 1)

    def expert(i, tg, nv):
        part = i // per_part
        step = jnp.clip(i % per_part - n_scatter, 0, n_expert - 1)
        tile = step // epg
        in_use = tile < nv[part]
        group = tg[part * n_tiles + jnp.clip(tile, 0, jnp.maximum(nv[part] - 1, 0))]
        return layer * N_EXPERTS + group * epg + jnp.where(in_use, step % epg, epg - 1)

    in_specs = [
        pl.BlockSpec((tm, D_MODEL), lambda i, d, tg, nv: (scatter_tile(i), 0)),
        pl.BlockSpec((tm, LANES), lambda i, d, tg, nv: (scatter_tile(i), 0)),
        pl.BlockSpec((1, D_MODEL, D_FF), lambda i, d, tg, nv: (expert(i, tg, nv), 0, 0)),
        pl.BlockSpec((1, D_MODEL, D_FF), lambda i, d, tg, nv: (expert(i, tg, nv), 0, 0)),
        pl.BlockSpec((1, D_FF, D_MODEL), lambda i, d, tg, nv: (expert(i, tg, nv), 0, 0)),
        pl.BlockSpec((tm, D_MODEL), lambda i, d, tg, nv: (gather_tile(i), 0)),
        pl.BlockSpec((1, N_MOD, D_MODEL), lambda i, d, tg, nv: (mod_row(gather_tile(i)), 0, 0)),
    ]
    args = [h2, w_rows, wg, wu, wd, x1, mod]
    if final:
        in_specs.append(pl.BlockSpec((1, D_MODEL), lambda i, d, tg, nv: (0, 0)))
        args.append(final_g)
    grid_spec = pltpu.PrefetchScalarGridSpec(
        num_scalar_prefetch=3,
        grid=(n_part * per_part,),
        in_specs=in_specs,
        out_specs=pl.BlockSpec((tm, D_MODEL), lambda i, d, tg, nv: (gather_tile(i), 0)),
        scratch_shapes=[
            pltpu.VMEM((n_tiles * MOE_TILE, D_MODEL), F32),
            pltpu.VMEM((n_tiles * MOE_TILE, LANES), F32),
            pltpu.VMEM((tm, D_MODEL), F32),
            pltpu.VMEM((MOE_TILE, D_MODEL), F32),
        ],
    )
    return pl.pallas_call(
        functools.partial(_moe_kernel, final=final, n_scatter=n_scatter, n_expert=n_expert, n_gather=n_gather),
        out_shape=jax.ShapeDtypeStruct((n, D_MODEL), F32),
        grid_spec=grid_spec,
        compiler_params=_params(("arbitrary",)),
    )(dest, tile_group.reshape(-1), n_valid, *args)


def _block_diag(m):
    g, r, c = m.shape
    eye = jnp.eye(g, dtype=m.dtype)
    return (eye[:, None, :, None] * m[:, :, None, :]).reshape(g * r, g * c)


def _s5_direction_params(a_re, a_im, log_dt, b_re, b_im, c_re, c_im, reverse):
    n_g = a_re.shape[0]
    width = S5_T * SSM_GROUP
    hp = lax.Precision.HIGHEST
    step = jnp.exp(log_dt)[:, None]
    mag = jnp.exp(a_re * step)
    ar = mag * jnp.cos(a_im * step)
    ai = mag * jnp.sin(a_im * step)
    den = a_re * a_re + a_im * a_im
    fr = ((ar - 1.0) * a_re + ai * a_im) / den
    fi = (ai * a_re - (ar - 1.0) * a_im) / den
    bbr = (fr[:, :, None] * b_re - fi[:, :, None] * b_im).transpose(0, 2, 1)
    bbi = (fr[:, :, None] * b_im + fi[:, :, None] * b_re).transpose(0, 2, 1)
    pr, pi = [jnp.ones_like(ar)], [jnp.zeros_like(ar)]
    for _ in range(S5_T):
        r, i = pr[-1], pi[-1]
        pr.append(r * ar - i * ai)
        pi.append(r * ai + i * ar)
    pr, pi = jnp.stack(pr)[:, :, None, :], jnp.stack(pi)[:, :, None, :]
    mr = pr * bbr - pi * bbi
    mi = pr * bbi + pi * bbr
    cr = c_re * pr - c_im * pi
    ci = c_re * pi + c_im * pr
    kern = (jnp.einsum('gop,kgip->goki', c_re, mr[:S5_T], precision=hp)
            - jnp.einsum('gop,kgip->goki', c_im, mi[:S5_T], precision=hp))
    pad = jnp.zeros_like(kern[:, :, 1:, :])
    strip = jnp.concatenate([pad, kern] if reverse else [kern[:, :, ::-1, :], pad], axis=2)
    tt = jnp.concatenate([strip, pad[:, :, 0:1, :]], axis=2).reshape(n_g, SSM_GROUP, 2 * width)
    wx = jnp.concatenate([mr[:S5_T], mi[:S5_T]], axis=3)
    wx = (wx if reverse else wx[::-1]).transpose(1, 0, 2, 3).reshape(n_g, width, S5_GL)
    wy = jnp.concatenate([cr[1:], -ci[1:]], axis=3)
    wy = (wy[::-1] if reverse else wy).transpose(1, 0, 2, 3).reshape(n_g, width, S5_GL)

    def coef_rows(r, i):
        return jnp.stack([jnp.concatenate([r, r], axis=1).reshape(-1), jnp.concatenate([-i, i], axis=1).reshape(-1)])

    sr, si = pr[S5_T, :, 0, :], pi[S5_T, :, 0, :]
    a_t = coef_rows(sr, si)
    for _ in range(int(math.log2(S5_SEG // S5_T))):
        sr, si = sr * sr - si * si, 2.0 * sr * si
    return tt.astype(BF16), wx.astype(BF16), wy.astype(BF16), a_t, coef_rows(sr, si)


def _rope_tables(n_tokens):
    rows = n_tokens // GRID_W
    row = jnp.repeat(jnp.arange(rows, dtype=F32), GRID_W)
    col = jnp.tile(jnp.arange(GRID_W, dtype=F32), rows)
    n_freq = ATT_HEAD_DIM // 4
    inv = ROPE_THETA ** (-jnp.arange(n_freq, dtype=F32) / n_freq)
    ang = jnp.concatenate([row[:, None] * inv, col[:, None] * inv], axis=-1)
    cos = jnp.repeat(jnp.cos(ang), 2, axis=-1)
    sin = jnp.repeat(jnp.sin(ang), 2, axis=-1) * jnp.tile(jnp.array([-1.0, 1.0], F32), ATT_HEAD_DIM // 2)
    return cos, sin


def _swap_pairs(a, axis=-1):
    axis = axis % a.ndim
    pairs = a.reshape(a.shape[:axis] + (a.shape[axis] // 2, 2) + a.shape[axis + 1:])
    return jnp.flip(pairs, axis=axis + 1).reshape(a.shape)


def _layer(x, batch, mod, mod_rows, p, ctx):
    n = x.shape[0]
    seq = n // batch
    latent = ctx is not None
    outs = _inproj(x, mod, mod_rows, p, latent)
    rqkv, rg, gates, q_t, k16, v_t = outs[:6]

    ro, ret_state = _retention(rqkv.reshape(batch, seq, 3 * RET_W), rg.reshape(batch, seq, RET_W), p['ret_lg'],
                               p['ret_gn'], ctx[2] if latent else None, bb=1 if latent else 8)

    if latent:
        k_ctx = ctx[0].reshape(batch * ATT_KC, ATT_KV_W).astype(BF16)
        v_ctx_t = ctx[1].reshape(batch, ATT_KC, ATT_KV_W).transpose(0, 2, 1).reshape(batch * ATT_KV_W, ATT_KC)
        att_t = _attention(q_t, k16, v_t, k_ctx, v_ctx_t.astype(BF16), batch)
    else:
        att_t = _attention(q_t, k16, v_t, None, None, batch)

    u = _s5_in(x, mod, mod_rows, p['g1'], p['w_su_t'])
    if latent:
        s0 = jnp.stack([ctx[3], ctx[4]], axis=3).transpose(1, 0, 2, 3, 4).reshape(2, batch, SSM_GROUPS * S5_GL)
        y = _s5(u, p['s5_tt'], p['s5_wx'], p['s5_wy'], p['s5_a'], p['s5_aseg'], s0)
        ssm_state = None
    else:
        y, ssm_state = _s5(u, p['s5_tt'], p['s5_wx'], p['s5_wy'], p['s5_a'], None, None)
        ssm_state = ssm_state.reshape(2, batch, SSM_GROUPS, 2, SSM_STATE)
    ssm = _s5_out(y, u, p['ssm_d'], p['w_glu'])

    x1, h2, meta, cnt = _merge(x, ro.reshape(n, RET_W), att_t, ssm, gates, mod, mod_rows, p['w_ret_out'],
                               p['w_att_out'], p['w_out'], p['g2'], p['w_router_t'], p['b_router'])
    x2 = _moe(h2, meta, cnt, x1, mod, mod_rows, p['layer'], p['w_gate'], p['w_up'], p['w_down'], p.get('final_g'))
    if latent:
        return x2, None
    return x2, (outs[6], outs[7], ret_state, ssm_state)


def kernel(x_prompt, x_sample, c, cache_attn_k, cache_attn_v, state_ret, state_ssm_re, state_ssm_im, c_ctx, w_ada,
           b_ada, norm1_g, norm2_g, w_in, ret_lg_f, ret_lg_b, ret_norm_g, w_ret_out, att_q_norm_g, att_k_norm_g,
           w_att_out, a_re_f, a_im_f, a_re_b, a_im_b, log_dt_f, log_dt_b, ssm_b_re, ssm_b_im, ssm_c_re, ssm_c_im,
           ssm_d, w_glu, w_out, w_router, b_router, w_gate, w_up, w_down, final_norm_g):
    batch, seq, _ = x_prompt.shape
    dec_batch, dec_seq, _ = x_sample.shape
    assert batch * seq == S5_ROWS * S5_SEG and dec_batch * dec_seq == S5_ROWS * S5_SEG
    assert seq == S5_SEG and dec_seq % S5_SEG == 0
    assert cache_attn_k.shape[2] == ATT_KC and seq % ATT_KC == 0 and dec_seq % ATT_QB == 0

    c_rows = jnp.zeros((ADA_ROWS, D_MODEL), F32).at[0].set(c_ctx).at[1:1 + dec_batch].set(c)
    mod_all = _ada(c_rows, w_ada, b_ada).reshape(DEPTH, ADA_ROWS, N_MOD, D_MODEL)

    cos, sin = _rope_tables(dec_seq)
    ones = _block_diag(jnp.ones((ATT_KV_HEADS, ATT_HEAD_DIM, ATT_HEAD_DIM), BF16))
    w_in16 = w_in.astype(BF16)
    w_gate16 = w_gate.astype(BF16).reshape(DEPTH * N_EXPERTS, D_MODEL, D_FF)
    w_up16 = w_up.astype(BF16).reshape(DEPTH * N_EXPERTS, D_MODEL, D_FF)
    w_down16 = w_down.astype(BF16).reshape(DEPTH * N_EXPERTS, D_FF, D_MODEL)
    s5_fwd = jax.vmap(functools.partial(_s5_direction_params, reverse=False))(
        a_re_f, a_im_f, log_dt_f, ssm_b_re, ssm_b_im, ssm_c_re, ssm_c_im)
    s5_bwd = jax.vmap(functools.partial(_s5_direction_params, reverse=True))(
        a_re_b, a_im_b, log_dt_b, ssm_b_re, ssm_b_im, ssm_c_re, ssm_c_im)

    xp = x_prompt.reshape(batch * seq, D_MODEL)
    xs = x_sample.reshape(dec_batch * dec_seq, D_MODEL)
    ks_, vs_, rets_, ssms_ = [], [], [], []
    for l in range(DEPTH):
        w_l = w_in[l]
        gq = att_q_norm_g[l][:, None]
        gk = jnp.tile(att_k_norm_g[l], ATT_KV_HEADS)[None]
        wq_t = w_l[:, C_AQ:C_AK].T
        fwd = [a[l] for a in s5_fwd]
        bwd = [a[l] for a in s5_bwd]
        p = {
            'layer': l, 'g1': norm1_g[l][None], 'g2': norm2_g[l][None], 'w_in': w_in16, 'ones': ones,
            'wq_t': wq_t.astype(BF16), 'wv_t': w_l[:, C_AV:C_SU].T.astype(BF16), 'gq': gq, 'gk': gk,
            'wqs_t': _swap_pairs(wq_t, axis=0).astype(BF16), 'wk_sw': _swap_pairs(w_l[:, C_AK:C_AV]).astype(BF16),
            'gqs': _swap_pairs(gq, axis=0), 'gks': _swap_pairs(gk),
            'cos': jnp.tile(cos, (1, ATT_KV_HEADS)), 'sin': jnp.tile(sin, (1, ATT_KV_HEADS)),
            'cos_t': cos.T, 'sin_t': sin.T,
            'ret_lg': jnp.stack([ret_lg_f[l], ret_lg_b[l]]), 'ret_gn': ret_norm_g[l][None],
            'w_ret_out': w_ret_out[l].astype(BF16), 'w_att_out': w_att_out[l].astype(BF16),
            'w_su_t': w_l[:, C_SU:C_G].T.astype(BF16),
            's5_tt': jnp.stack([fwd[0], bwd[0]]), 's5_wx': jnp.stack([fwd[1], bwd[1]]),
            's5_wy': jnp.stack([fwd[2], bwd[2]]), 's5_a': jnp.stack([fwd[3], bwd[3]]),
            's5_aseg': jnp.stack([fwd[4], bwd[4]]),
            'ssm_d': ssm_d[l][:, None], 'w_glu': w_glu[l].astype(BF16), 'w_out': w_out[l].astype(BF16),
            'w_router_t': w_router.T, 'b_router': b_router[:, None],
            'w_gate': w_gate16, 'w_up': w_up16, 'w_down': w_down16,
        }
        if l == DEPTH - 1:
            p['final_g'] = final_norm_g[None]
        mod = mod_all[l]

        xp, (k_c, v_c, st_r, st_s) = _layer(xp, batch, mod, (0, batch * seq), p, None)
        ks_.append(k_c.reshape(batch, seq, ATT_KV_HEADS, ATT_HEAD_DIM))
        vs_.append(v_c.reshape(batch, seq, ATT_KV_HEADS, ATT_HEAD_DIM))
        rets_.append(st_r)
        ssms_.append(st_s)

        ctx = (cache_attn_k[:, l], cache_attn_v[:, l], state_ret[:, l], state_ssm_re[:, l], state_ssm_im[:, l])
        xs, _ = _layer(xs, dec_batch, mod, (1, dec_seq), p, ctx)

    ssm_all = jnp.stack(ssms_, axis=1)
    ssm_all = ssm_all.transpose(2, 1, 0, 3, 4, 5)
    new_re = ssm_all[..., 0, :]
    new_im = ssm_all[..., 1, :]
    return (xp.reshape(batch, seq, D_MODEL), xs.reshape(dec_batch, dec_seq, D_MODEL),
            jnp.stack(ks_, axis=1), jnp.stack(vs_, axis=1), jnp.stack(rets_, axis=1), new_re, new_im)
```

```python
import functools
import math

import jax
import jax.numpy as jnp
from jax import lax
from jax.experimental import pallas as pl
from jax.experimental.pallas import tpu as pltpu

F32 = jnp.float32
BF16 = jnp.bfloat16

D_MODEL = 1024
DEPTH = 2
GRID_W = 64
RET_HEADS = 4
RET_DK = 128
RET_DV = 128
RET_CHUNK = 256
RET_W = RET_HEADS * RET_DV
ATT_HEADS = 8
ATT_KV_HEADS = 2
ATT_GROUP = ATT_HEADS // ATT_KV_HEADS
ATT_HEAD_DIM = 64
ATT_W = ATT_HEADS * ATT_HEAD_DIM
ATT_KV_W = ATT_KV_HEADS * ATT_HEAD_DIM
ROPE_THETA = 10000.0
SSM_CH = 512
SSM_GROUP = 16
SSM_GROUPS = SSM_CH // SSM_GROUP
SSM_STATE = 64
N_EXPERTS = 16
N_EXPERT_GROUPS = 4
EXPERTS_PER_GROUP = N_EXPERTS // N_EXPERT_GROUPS
D_FF = 512
N_MOD = 6
EPS = 1e-6

C_RQ, C_RK, C_RV, C_RG = 0, 512, 1024, 1536
C_AQ, C_AK, C_AV, C_SU, C_G, C_END = 2048, 2560, 2688, 2816, 3328, 6400

VMEM_LIMIT = 52 * 1024 * 1024

ATT_Q_SCALE = ATT_HEAD_DIM ** -0.5 * math.log2(math.e)
ATT_QB = 512
ATT_KC = 256
ATT_V_ROWS = ATT_HEAD_DIM + 16
ATT_RING = 4
ATT_SHORT_SEQS = 4

TM_PROJ = 512
RET_UNROLL = 4
MOE_TILE = 512
MOE_PART = 4096
META_ROWS = 8
S5_ROWS = 32
S5_SEG = 256
S5_T = 16
S5_GL = 2 * SSM_STATE
S5_GB = 8
S5_TM = 2048
S5_PITCH = 24
LANES = 128
STAGE_ROWS = 256
ADA_ROWS = 8
ADA_TN = 1536


def _dot(a, b):
    return jnp.dot(a, b, preferred_element_type=F32)


def _dot_nt(a, b):
    return lax.dot_general(a, b, (((1,), (1,)), ((), ())), preferred_element_type=F32)


def _dot_tn(a, b):
    return lax.dot_general(a, b, (((0,), (0,)), ((), ())), preferred_element_type=F32)


def _sigmoid(x):
    return 0.5 * jnp.tanh(0.5 * x) + 0.5


def _const_spec(shape):
    n = len(shape)
    return pl.BlockSpec(shape, lambda *_: (0,) * n, pipeline_mode=pl.Buffered(1))


def _params(sem, vmem=VMEM_LIMIT):
    return pltpu.CompilerParams(dimension_semantics=sem, vmem_limit_bytes=vmem)


def _mod_row_fn(mod_rows, tm):
    base, per_row = mod_rows
    return lambda i: base + (i * tm) // per_row


def _modulated_norm(x, g, mod_ref):
    xn = x * lax.rsqrt(jnp.mean(x * x, axis=-1, keepdims=True) + EPS) * g
    return xn * (1.0 + mod_ref[0, 1:2, :]) + mod_ref[0, 0:1, :]


def _ada_kernel(c_ref, w_ref, b_ref, o_ref):
    c = c_ref[...]
    a = (c * _sigmoid(c)).astype(BF16)
    o_ref[0] = _dot(a, w_ref[0].astype(BF16)) + b_ref[0]


def _ada(c_rows, w_ada, b_ada):
    n_col = N_MOD * D_MODEL
    return pl.pallas_call(
        _ada_kernel,
        out_shape=jax.ShapeDtypeStruct((DEPTH, ADA_ROWS, n_col), F32),
        grid=(DEPTH, n_col // ADA_TN),
        in_specs=[
            pl.BlockSpec((ADA_ROWS, D_MODEL), lambda l, j: (0, 0)),
            pl.BlockSpec((1, D_MODEL, ADA_TN), lambda l, j: (l, 0, j)),
            pl.BlockSpec((1, 1, ADA_TN), lambda l, j: (l, 0, j)),
        ],
        out_specs=pl.BlockSpec((1, ADA_ROWS, ADA_TN), lambda l, j: (l, 0, j)),
        compiler_params=_params(("arbitrary", "arbitrary")),
    )(c_rows, w_ada, b_ada.reshape(DEPTH, 1, n_col))


def _head_mean_sq(z, ones):
    z2 = z * z
    hi = z2.astype(BF16)
    lo = (z2 - hi.astype(F32)).astype(BF16)
    return (_dot(hi, ones) + _dot(lo, ones)) * (1.0 / ATT_HEAD_DIM)


def _inproj_kernel(*refs, latent):
    if latent:
        (x_ref, mod_ref, g1_ref, w_ref, wqt_ref, wvt_ref, ones_ref, gq_ref, gk_ref,
         wqst_ref, wks_ref, gqs_ref, gks_ref, cos_ref, sin_ref, cost_ref, sint_ref,
         rqkv_ref, rg_ref, gates_ref, qt_ref, k16_ref, vt_ref) = refs
    else:
        (x_ref, mod_ref, g1_ref, w_ref, wqt_ref, wvt_ref, ones_ref, gq_ref, gk_ref,
         rqkv_ref, rg_ref, gates_ref, qt_ref, k16_ref, vt_ref, ak_ref, av_ref) = refs
    tm = x_ref.shape[0]
    h = _modulated_norm(x_ref[...], g1_ref[...], mod_ref).astype(BF16)

    def seg(a, b):
        return _dot(h, w_ref[0, :, a:b])

    rqkv_ref[:, 0:RET_W] = (seg(C_RQ, C_RK) * (RET_DK ** -0.5)).astype(BF16)
    rqkv_ref[:, RET_W:3 * RET_W] = seg(C_RK, C_RG).astype(BF16)
    rg_ref[...] = seg(C_RG, C_AQ).astype(BF16)
    for j in range(3):
        gates_ref[:, j * D_MODEL:(j + 1) * D_MODEL] = seg(C_G + j * D_MODEL, C_G + (j + 1) * D_MODEL).astype(BF16)

    zk = seg(C_AK, C_AV)
    inv_k = lax.rsqrt(_head_mean_sq(zk, ones_ref[...]) + EPS)
    yk = zk * inv_k * gk_ref[...]
    zq = _dot_nt(wqt_ref[...], h).reshape(ATT_HEADS, ATT_HEAD_DIM, tm)
    inv_q = lax.rsqrt(jnp.mean(zq * zq, axis=1, keepdims=True) + EPS)
    yq = zq * inv_q * gq_ref[...]
    if latent:
        yk_sw = _dot(h, wks_ref[...]) * inv_k * gks_ref[...]
        yk = yk * cos_ref[...] + yk_sw * sin_ref[...]
        zq_sw = _dot_nt(wqst_ref[...], h).reshape(ATT_HEADS, ATT_HEAD_DIM, tm)
        yq = yq * cost_ref[...] + (zq_sw * inv_q * gqs_ref[...]) * sint_ref[...]
    qt_ref[...] = (yq * ATT_Q_SCALE).reshape(ATT_W, tm).astype(BF16)
    k16_ref[...] = yk.astype(BF16)
    vt = _dot_nt(wvt_ref[...], h).astype(BF16)
    for c in range(tm // ATT_KC):
        vt_ref[c] = vt[:, c * ATT_KC:(c + 1) * ATT_KC]
    if not latent:
        ak_ref[...] = yk
        av_ref[...] = seg(C_AV, C_SU)


def _inproj(x, mod, mod_rows, p, latent):
    n = x.shape[0]
    tm = TM_PROJ
    mod_row = _mod_row_fn(mod_rows, tm)
    row = lambda i: (i, 0)
    in_specs = [
        pl.BlockSpec((tm, D_MODEL), row),
        pl.BlockSpec((1, N_MOD, D_MODEL), lambda i: (mod_row(i), 0, 0)),
        _const_spec((1, D_MODEL)),
        pl.BlockSpec((1, D_MODEL, C_END), lambda i: (p['layer'], 0, 0), pipeline_mode=pl.Buffered(1)),
        _const_spec((ATT_W, D_MODEL)),
        _const_spec((ATT_KV_W, D_MODEL)),
        _const_spec((ATT_KV_W, ATT_KV_W)),
        _const_spec((ATT_HEAD_DIM, 1)),
        _const_spec((1, ATT_KV_W)),
    ]
    args = [x, mod, p['g1'], p['w_in'], p['wq_t'], p['wv_t'], p['ones'], p['gq'], p['gk']]
    if latent:
        n_pos = p['cos'].shape[0] // tm
        in_specs += [
            _const_spec((ATT_W, D_MODEL)),
            _const_spec((D_MODEL, ATT_KV_W)),
            _const_spec((ATT_HEAD_DIM, 1)),
            _const_spec((1, ATT_KV_W)),
            pl.BlockSpec((tm, ATT_KV_W), lambda i: (i % n_pos, 0)),
            pl.BlockSpec((tm, ATT_KV_W), lambda i: (i % n_pos, 0)),
            pl.BlockSpec((ATT_HEAD_DIM, tm), lambda i: (0, i % n_pos)),
            pl.BlockSpec((ATT_HEAD_DIM, tm), lambda i: (0, i % n_pos)),
        ]
        args += [p['wqs_t'], p['wk_sw'], p['gqs'], p['gks'], p['cos'], p['sin'], p['cos_t'], p['sin_t']]
    out_shape = [
        jax.ShapeDtypeStruct((n, 3 * RET_W), BF16),
        jax.ShapeDtypeStruct((n, RET_W), BF16),
        jax.ShapeDtypeStruct((n, 3 * D_MODEL), BF16),
        jax.ShapeDtypeStruct((ATT_W, n), BF16),
        jax.ShapeDtypeStruct((n, ATT_KV_W), BF16),
        jax.ShapeDtypeStruct((n // ATT_KC, ATT_KV_W, ATT_KC), BF16),
    ]
    out_specs = [
        pl.BlockSpec((tm, 3 * RET_W), row),
        pl.BlockSpec((tm, RET_W), row),
        pl.BlockSpec((tm, 3 * D_MODEL), row),
        pl.BlockSpec((ATT_W, tm), lambda i: (0, i)),
        pl.BlockSpec((tm, ATT_KV_W), row),
        pl.BlockSpec((tm // ATT_KC, ATT_KV_W, ATT_KC), lambda i: (i, 0, 0)),
    ]
    if not latent:
        out_shape += [jax.ShapeDtypeStruct((n, ATT_KV_W), F32)] * 2
        out_specs += [pl.BlockSpec((tm, ATT_KV_W), row)] * 2
    return pl.pallas_call(
        functools.partial(_inproj_kernel, latent=latent),
        out_shape=tuple(out_shape),
        grid=(n // tm,),
        in_specs=in_specs,
        out_specs=tuple(out_specs),
        compiler_params=_params(("parallel",)),
    )(*args)


T_DEC_F, T_DEC_B, T_XI_F, T_ZETA_F, T_XI_B, T_ZETA_B, T_CD_F, T_CD_B, N_TAB = range(9)


def _ret_kernel(*refs, has_s0, bb, seq):
    if has_s0:
        lg_ref, q_ref, k_ref, v_ref, rg_ref, gn_ref, s0_ref, o_ref, st_ref, tab_ref, acc_ref = refs
    else:
        lg_ref, q_ref, k_ref, v_ref, rg_ref, gn_ref, o_ref, st_ref, tab_ref, acc_ref = refs
    c = RET_CHUNK
    head = pl.program_id(1)
    lgf = lg_ref[0, head]
    lgb = lg_ref[1, head]
    t = lax.broadcasted_iota(jnp.int32, (c, c), 0).astype(F32)
    s = lax.broadcasted_iota(jnp.int32, (c, c), 1).astype(F32)
    tab_ref[T_DEC_F] = jnp.where(t >= s, jnp.exp(lgf * jnp.maximum(t - s, 0.0)), 0.0)
    tab_ref[T_DEC_B] = jnp.where(s >= t, jnp.exp(lgb * jnp.maximum(s - t, 0.0)), 0.0)
    tab_ref[T_XI_F] = jnp.exp(lgf * (t + 1.0))
    tab_ref[T_ZETA_F] = jnp.exp(lgf * (c - 1.0 - t))
    tab_ref[T_XI_B] = jnp.exp(lgb * (c - t))
    tab_ref[T_ZETA_B] = jnp.exp(lgb * t)
    tab_ref[T_CD_F] = jnp.exp(lgf * (c + 0.0 * t))
    tab_ref[T_CD_B] = jnp.exp(lgb * (c + 0.0 * t))
    if has_s0:
        st_ref[...] = s0_ref[...]
    else:
        st_ref[...] = jnp.zeros_like(st_ref)
    acc_ref[...] = jnp.zeros_like(acc_ref)
    n_chunk = seq // c

    def one_direction(rb, d, off, t_dec, t_xi, t_zeta, t_cd):
        rows = pl.ds(off, c)
        q = q_ref[rb, rows, :]
        k = k_ref[rb, rows, :]
        v = v_ref[rb, rows, :]
        p = (_dot_nt(q, k) * tab_ref[t_dec]).astype(BF16)
        st = st_ref[rb, d, 0]
        q_xi = (q.astype(F32) * tab_ref[t_xi, :, 0:RET_DK]).astype(BF16)
        o = _dot(jnp.concatenate([p, q_xi], axis=1), jnp.concatenate([v, st.astype(BF16)], axis=0))
        acc_ref[rb, rows, :] += o
        kz = (k.astype(F32) * tab_ref[t_zeta, :, 0:RET_DK]).astype(BF16)
        st_ref[rb, d, 0] = tab_ref[t_cd, 0:RET_DK, 0:RET_DV] * st + _dot_tn(kz, v)

    def body(i, carry):
        off_f = pl.multiple_of(i * c, c)
        off_b = pl.multiple_of((n_chunk - 1 - i) * c, c)
        for rb in range(bb):
            one_direction(rb, 0, off_f, T_DEC_F, T_XI_F, T_ZETA_F, T_CD_F)
            one_direction(rb, 1, off_b, T_DEC_B, T_XI_B, T_ZETA_B, T_CD_B)
        return carry

    lax.fori_loop(0, n_chunk, body, 0, unroll=RET_UNROLL if bb == 1 else 1)

    def finish(i, carry):
        rows = pl.ds(pl.multiple_of(i * c, c), c)
        for rb in range(bb):
            o = acc_ref[rb, rows, :]
            ro = o * lax.rsqrt(jnp.mean(o * o, axis=-1, keepdims=True) + EPS) * gn_ref[...]
            g = rg_ref[rb, rows, :].astype(F32)
            o_ref[rb, rows, :] = (ro * (g * _sigmoid(g))).astype(BF16)
        return carry

    lax.fori_loop(0, n_chunk, finish, 0)


def _retention(rqkv, rg, lg, gn, s0, bb):
    b, seq, _ = rqkv.shape
    has_s0 = s0 is not None
    blk = (bb, seq, RET_DK)
    st_spec = pl.BlockSpec((bb, 2, 1, RET_DK, RET_DV), lambda i, h: (i, 0, h, 0, 0))
    in_specs = [
        pl.BlockSpec(memory_space=pltpu.SMEM),
        pl.BlockSpec(blk, lambda i, h: (i, 0, h)),
        pl.BlockSpec(blk, lambda i, h: (i, 0, RET_HEADS + h)),
        pl.BlockSpec(blk, lambda i, h: (i, 0, 2 * RET_HEADS + h)),
        pl.BlockSpec(blk, lambda i, h: (i, 0, h)),
        pl.BlockSpec((1, RET_DV), lambda i, h: (0, h)),
    ]
    args = [lg, rqkv, rqkv, rqkv, rg, gn]
    if has_s0:
        in_specs.append(st_spec)
        args.append(s0)
    return pl.pallas_call(
        functools.partial(_ret_kernel, has_s0=has_s0, bb=bb, seq=seq),
        out_shape=(
            jax.ShapeDtypeStruct((b, seq, RET_W), BF16),
            jax.ShapeDtypeStruct((b, 2, RET_HEADS, RET_DK, RET_DV), F32),
        ),
        grid=(b // bb, RET_HEADS),
        in_specs=in_specs,
        out_specs=(pl.BlockSpec(blk, lambda i, h: (i, 0, h)), st_spec),
        scratch_shapes=[
            pltpu.VMEM((N_TAB, RET_CHUNK, RET_CHUNK), F32),
            pltpu.VMEM((bb, seq, RET_DV), F32),
        ],
        compiler_params=_params(("parallel", "parallel")),
    )(*args)


ST_M, ST_ALPHA, N_ST = range(3)


def _attn_kernel(*refs, has_ctx):
    if has_ctx:
        q_ref, k_ref, v_ref, kc_ref, vc_ref, o_ref, s_ref, p_ref, acc_ref, st_ref, mx_ref = refs
    else:
        q_ref, k_ref, v_ref, o_ref, s_ref, p_ref, acc_ref, st_ref, mx_ref = refs
    kv_head = pl.program_id(1)
    qb = q_ref.shape[1]
    kc = s_ref.shape[1]
    n_own = k_ref.shape[0] // kc
    n_chunk = n_own + (1 if has_ctx else 0)
    q_t = jnp.concatenate([q_ref[h * ATT_HEAD_DIM:(h + 1) * ATT_HEAD_DIM, :] for h in range(ATT_GROUP)], axis=1)
    q_both = jnp.concatenate([q_t] * ATT_KV_HEADS, axis=0)
    row = lax.broadcasted_iota(jnp.int32, q_both.shape, 0)
    q_pad = jnp.where(row // ATT_HEAD_DIM == kv_head, q_both, jnp.zeros_like(q_both))
    ones_rows = (lax.broadcasted_iota(jnp.int32, (ATT_V_ROWS - ATT_HEAD_DIM, kc), 0) == 0).astype(BF16)

    def keys(j):
        if has_ctx and isinstance(j, int) and j == n_own:
            return kc_ref[...]
        return k_ref[pl.ds(pl.multiple_of(j * kc, kc), kc), :]

    def values(j):
        v = vc_ref[...] if has_ctx and isinstance(j, int) and j == n_own else v_ref[j]
        return jnp.concatenate([v, ones_rows], axis=0)

    def put_scores(j, slot):
        s = _dot(keys(j), q_pad)
        s_ref[slot] = s
        mx_ref[slot] = jnp.max(s, axis=0, keepdims=True)

    def weighted_values(j, slot):
        return _dot(values(j), p_ref[slot])

    def phase(j, slot, first=False, last=False):
        if not last:
            put_scores(j + 1, (slot + 1) % ATT_RING)
        if not first:
            acc_ref[...] = st_ref[ST_ALPHA] * acc_ref[...] + weighted_values(j - 1, (slot - 1) % ATT_RING)
        m = st_ref[ST_M]
        m_new = jnp.maximum(m, mx_ref[slot])
        st_ref[ST_ALPHA] = jnp.exp2(m - m_new)
        st_ref[ST_M] = m_new
        p_ref[slot] = jnp.exp2((s_ref[slot] - m_new).astype(BF16))

    st_ref[ST_M] = jnp.full(st_ref.shape[1:], -jnp.inf, F32)
    acc_ref[...] = jnp.zeros_like(acc_ref)
    put_scores(0, 0)
    phase(0, 0, first=True, last=n_chunk == 1)
    n_loop = max(n_own - 2, 0) // ATT_RING

    def revolution(i, carry):
        for t in range(1, ATT_RING + 1):
            phase(ATT_RING * i + t, t % ATT_RING)
        return carry

    lax.fori_loop(0, n_loop, revolution, 0)
    for j in range(1 + ATT_RING * n_loop, n_chunk):
        phase(j, j % ATT_RING, last=j == n_chunk - 1)
    acc = st_ref[ST_ALPHA] * acc_ref[...] + weighted_values(n_chunk - 1, (n_chunk - 1) % ATT_RING)
    out = (acc[0:ATT_HEAD_DIM] / acc[ATT_HEAD_DIM:ATT_HEAD_DIM + 1]).astype(BF16)
    for h in range(ATT_GROUP):
        o_ref[h * ATT_HEAD_DIM:(h + 1) * ATT_HEAD_DIM, :] = out[:, h * qb:(h + 1) * qb]


def _attn_short_kernel(q_ref, k_ref, v_ref, o_ref):
    kv_head = pl.program_id(1)
    n_seq, _, kc = v_ref.shape
    qb = q_ref.shape[1] // n_seq
    ones_rows = (lax.broadcasted_iota(jnp.int32, (ATT_V_ROWS - ATT_HEAD_DIM, kc), 0) == 0).astype(BF16)
    for s in range(n_seq):
        cols = slice(s * qb, (s + 1) * qb)
        q_t = jnp.concatenate([q_ref[h * ATT_HEAD_DIM:(h + 1) * ATT_HEAD_DIM, cols] for h in range(ATT_GROUP)],
                              axis=1)
        q_both = jnp.concatenate([q_t] * ATT_KV_HEADS, axis=0)
        row = lax.broadcasted_iota(jnp.int32, q_both.shape, 0)
        q_pad = jnp.where(row // ATT_HEAD_DIM == kv_head, q_both, jnp.zeros_like(q_both))
        scores = _dot(k_ref[s * kc:(s + 1) * kc, :], q_pad)
        p = jnp.exp2((scores - jnp.max(scores, axis=0, keepdims=True)).astype(BF16))
        acc = _dot(jnp.concatenate([v_ref[s], ones_rows], axis=0), p)
        out = (acc[0:ATT_HEAD_DIM] / acc[ATT_HEAD_DIM:ATT_HEAD_DIM + 1]).astype(BF16)
        for h in range(ATT_GROUP):
            o_ref[h * ATT_HEAD_DIM:(h + 1) * ATT_HEAD_DIM, cols] = out[:, h * qb:(h + 1) * qb]


def _attention_short(q_t, k, v_t, batch):
    n = q_t.shape[1]
    seq = n // batch
    ns = ATT_SHORT_SEQS
    q_spec = pl.BlockSpec((ATT_GROUP * ATT_HEAD_DIM, ns * seq), lambda b, g: (g, b))
    return pl.pallas_call(
        _attn_short_kernel,
        out_shape=jax.ShapeDtypeStruct(q_t.shape, BF16),
        grid=(batch // ns, ATT_KV_HEADS),
        in_specs=[
            q_spec,
            pl.BlockSpec((ns * seq, ATT_KV_W), lambda b, g: (b, 0)),
            pl.BlockSpec((ns, ATT_HEAD_DIM, seq), lambda b, g: (b, g, 0)),
        ],
        out_specs=q_spec,
        compiler_params=_params(("parallel", "parallel")),
    )(q_t, k, v_t)


def _attention(q_t, k, v_t, k_ctx, v_ctx_t, batch):
    n = q_t.shape[1]
    seq = n // batch
    qb, kc = min(ATT_QB, seq), ATT_KC
    n_q = seq // qb
    has_ctx = k_ctx is not None
    q_spec = pl.BlockSpec((ATT_GROUP * ATT_HEAD_DIM, qb), lambda b, g, j: (g, b * n_q + j))
    in_specs = [
        q_spec,
        pl.BlockSpec((seq, ATT_KV_W), lambda b, g, j: (b, 0)),
        pl.BlockSpec((seq // kc, ATT_HEAD_DIM, kc), lambda b, g, j: (b, g, 0)),
    ]
    args = [q_t, k, v_t]
    if has_ctx:
        in_specs += [
            pl.BlockSpec((kc, ATT_KV_W), lambda b, g, j: (b, 0)),
            pl.BlockSpec((ATT_HEAD_DIM, kc), lambda b, g, j: (b * ATT_KV_HEADS + g, 0)),
        ]
        args += [k_ctx, v_ctx_t]
    width = ATT_GROUP * qb
    return pl.pallas_call(
        functools.partial(_attn_kernel, has_ctx=has_ctx),
        out_shape=jax.ShapeDtypeStruct(q_t.shape, BF16),
        grid=(batch, ATT_KV_HEADS, n_q),
        in_specs=in_specs,
        out_specs=q_spec,
        scratch_shapes=[
            pltpu.VMEM((ATT_RING, kc, width), F32),
            pltpu.VMEM((ATT_RING, kc, width), BF16),
            pltpu.VMEM((ATT_V_ROWS, width), F32),
            pltpu.VMEM((N_ST, 1, width), F32),
            pltpu.VMEM((ATT_RING, 1, width), F32),
        ],
        compiler_params=_params(("parallel", "parallel", "parallel")),
    )(*args)


def _s5_in_kernel(x_ref, mod_ref, g1_ref, w_ref, o_ref, h_ref):
    tm = x_ref.shape[0]
    n_chunk = o_ref.shape[2]
    n_col = D_MODEL // LANES
    per_stage = STAGE_ROWS // S5_T

    def stage(i, carry):
        h = _modulated_norm(x_ref[pl.ds(pl.multiple_of(i * STAGE_ROWS, STAGE_ROWS), STAGE_ROWS), :], g1_ref[...],
                            mod_ref)
        for c in range(per_stage):
            rows = pl.ds(pl.multiple_of((i * per_stage + c) * S5_PITCH, S5_PITCH), S5_T)
            for j in range(n_col):
                h_ref[j, rows, :] = h[c * S5_T:(c + 1) * S5_T, j * LANES:(j + 1) * LANES]
        return carry

    lax.fori_loop(0, tm // STAGE_ROWS, stage, 0)
    for s in range(S5_T):
        h = jnp.concatenate([h_ref[j, pl.ds(s, n_chunk, stride=S5_PITCH), :] for j in range(n_col)], axis=1)
        o_ref[s] = _dot_nt(w_ref[...], h.astype(BF16)).astype(BF16)


def _s5_in(x, mod, mod_rows, g1, w_su_t):
    n = x.shape[0]
    tm = S5_TM
    mod_row = _mod_row_fn(mod_rows, tm)
    return pl.pallas_call(
        _s5_in_kernel,
        out_shape=jax.ShapeDtypeStruct((S5_T, SSM_CH, n // S5_T), BF16),
        grid=(n // tm,),
        in_specs=[
            pl.BlockSpec((tm, D_MODEL), lambda i: (i, 0)),
            pl.BlockSpec((1, N_MOD, D_MODEL), lambda i: (mod_row(i), 0, 0)),
            _const_spec((1, D_MODEL)),
            _const_spec((SSM_CH, D_MODEL)),
        ],
        out_specs=pl.BlockSpec((S5_T, SSM_CH, tm // S5_T), lambda i: (0, 0, i)),
        scratch_shapes=[pltpu.VMEM((D_MODEL // LANES, tm // S5_T * S5_PITCH, LANES), F32)],
        compiler_params=_params(("parallel",)),
    )(x, mod, g1, w_su_t)


def _s5_kernel(*refs, two_pass):
    if two_pass:
        u_ref, tt_ref, wx_ref, wy_ref, a_ref, aseg_ref, s0_ref, y_ref, v_ref, init_ref = refs
    else:
        u_ref, tt_ref, wx_ref, wy_ref, a_ref, y_ref, fin_ref, v_ref = refs
    gb = tt_ref.shape[1]
    n_chunk = u_ref.shape[2] // S5_ROWS
    grp = SSM_GROUP

    def toeplitz(d, j):
        strip = tt_ref[d, j]
        width = S5_T * grp
        starts = [(S5_T - 1 - t) * grp for t in range(S5_T)]
        return jnp.concatenate([strip[:, a:a + width] for a in starts], axis=0)

    def put_outputs(j, yt, accumulate):
        for t in range(S5_T):
            if accumulate:
                y_ref[t, j * grp:(j + 1) * grp, :] += yt[t * grp:(t + 1) * grp, :]
            else:
                y_ref[t, j * grp:(j + 1) * grp, :] = yt[t * grp:(t + 1) * grp, :]

    for j in range(gb):
        z = jnp.concatenate([u_ref[s, j * grp:(j + 1) * grp, :] for s in range(S5_T)], axis=0)
        put_outputs(j, _dot(toeplitz(0, j), z) + _dot(toeplitz(1, j), z), accumulate=False)
        for d in range(2):
            v_ref[d, j] = _dot_tn(z, wx_ref[d, j])

    def times(coef_ref, d, j, x):
        lanes = slice(j * S5_GL, (j + 1) * S5_GL)
        return coef_ref[d, 0:1, lanes] * x + coef_ref[d, 1:2, lanes] * pltpu.roll(x, SSM_STATE, axis=1)

    def scan(init, keep_states):
        xs = [list(init[0]), list(init[1])]
        for s in range(n_chunk):
            for d in range(2):
                c = s if d == 0 else n_chunk - 1 - s
                rows = pl.ds(c, S5_ROWS, stride=n_chunk)
                for j in range(gb):
                    v = v_ref[d, j, rows, :]
                    if keep_states:
                        v_ref[d, j, rows, :] = xs[d][j]
                    xs[d][j] = times(a_ref, d, j, xs[d][j]) + v
        return xs

    zero = [[jnp.zeros((S5_ROWS, S5_GL), F32)] * gb] * 2
    if two_pass:
        local = scan(zero, keep_states=False)
        n_batch = s0_ref.shape[1]
        n_seg = S5_ROWS // n_batch
        for d in range(2):
            order = range(n_seg) if d == 0 else range(n_seg - 1, -1, -1)
            for j in range(gb):
                for b in range(n_batch):
                    cur = s0_ref[d, b:b + 1, j * S5_GL:(j + 1) * S5_GL]
                    for sgm in order:
                        r = b * n_seg + sgm
                        init_ref[d, j, r:r + 1, :] = cur
                        cur = times(aseg_ref, d, j, cur) + local[d][j][r:r + 1, :]
        scan([[init_ref[d, j] for j in range(gb)] for d in range(2)], keep_states=True)
    else:
        final = scan(zero, keep_states=True)
        for d in range(2):
            for j in range(gb):
                fin_ref[d, :, j * S5_GL:(j + 1) * S5_GL] = final[d][j]
    for j in range(gb):
        put_outputs(j, _dot_nt(wy_ref[0, j], v_ref[0, j].astype(BF16))
                    + _dot_nt(wy_ref[1, j], v_ref[1, j].astype(BF16)), accumulate=True)


def _s5(u, tt, wx, wy, a_t, a_seg, s0):
    two_pass = s0 is not None
    n_rows = u.shape[2]
    gb = S5_GB
    io_spec = pl.BlockSpec((S5_T, gb * SSM_GROUP, n_rows), lambda i: (0, i, 0))
    coef_spec = pl.BlockSpec((2, 2, gb * S5_GL), lambda i: (0, 0, i))
    in_specs = [
        io_spec,
        pl.BlockSpec((2, gb) + tt.shape[2:], lambda i: (0, i, 0, 0)),
        pl.BlockSpec((2, gb) + wx.shape[2:], lambda i: (0, i, 0, 0)),
        pl.BlockSpec((2, gb) + wy.shape[2:], lambda i: (0, i, 0, 0)),
        coef_spec,
    ]
    args = [u, tt, wx, wy, a_t]
    y_shape = jax.ShapeDtypeStruct(u.shape, F32)
    scratch = [pltpu.VMEM((2, gb, n_rows, S5_GL), F32)]
    if two_pass:
        in_specs += [coef_spec, pl.BlockSpec((2, s0.shape[1], gb * S5_GL), lambda i: (0, 0, i))]
        args += [a_seg, s0]
        out_shape, out_specs = y_shape, io_spec
        scratch.append(pltpu.VMEM((2, gb, S5_ROWS, S5_GL), F32))
    else:
        out_shape = (y_shape, jax.ShapeDtypeStruct((2, S5_ROWS, SSM_GROUPS * S5_GL), F32))
        out_specs = (io_spec, pl.BlockSpec((2, S5_ROWS, gb * S5_GL), lambda i: (0, 0, i)))
    return pl.pallas_call(
        functools.partial(_s5_kernel, two_pass=two_pass),
        out_shape=out_shape,
        grid=(SSM_GROUPS // gb,),
        in_specs=in_specs,
        out_specs=out_specs,
        scratch_shapes=scratch,
        compiler_params=_params(("parallel",)),
    )(*args)


def _s5_out_kernel(y_ref, u_ref, d_ref, w_ref, o_ref, buf_ref):
    n_chunk = y_ref.shape[2]
    n_col = D_MODEL // LANES
    for s in range(S5_T):
        y = y_ref[s] + d_ref[...] * u_ref[s].astype(F32)
        glu = _dot_tn(jax.nn.gelu(y).astype(BF16), w_ref[...])
        out = glu[:, 0:D_MODEL] * _sigmoid(glu[:, D_MODEL:2 * D_MODEL])
        for j in range(n_col):
            buf_ref[j, pl.ds(s, n_chunk, stride=S5_PITCH), :] = out[:, j * LANES:(j + 1) * LANES]

    def unstage(c, carry):
        src = pl.ds(pl.multiple_of(c * S5_PITCH, S5_PITCH), S5_T)
        dst = pl.ds(pl.multiple_of(c * S5_T, S5_T), S5_T)
        o_ref[dst, :] = jnp.concatenate([buf_ref[j, src, :] for j in range(n_col)], axis=1).astype(BF16)
        return carry

    lax.fori_loop(0, n_chunk, unstage, 0, unroll=8)


def _s5_out(y, u, d_skip, w_glu):
    n = y.shape[2] * S5_T
    tm = S5_TM
    io_spec = pl.BlockSpec((S5_T, SSM_CH, tm // S5_T), lambda i: (0, 0, i))
    return pl.pallas_call(
        _s5_out_kernel,
        out_shape=jax.ShapeDtypeStruct((n, D_MODEL), BF16),
        grid=(n // tm,),
        in_specs=[io_spec, io_spec, _const_spec((SSM_CH, 1)), _const_spec((SSM_CH, 2 * D_MODEL))],
        out_specs=pl.BlockSpec((tm, D_MODEL), lambda i: (i, 0)),
        scratch_shapes=[pltpu.VMEM((D_MODEL // LANES, tm // S5_T * S5_PITCH, LANES), F32)],
        compiler_params=_params(("parallel",)),
    )(y, u, d_skip, w_glu)


def _route(probs):
    epg = EXPERTS_PER_GROUP
    groups = [probs[g * epg:(g + 1) * epg] for g in range(N_EXPERT_GROUPS)]
    scores = []
    for grp in groups:
        best = None
        for a in range(epg):
            for b in range(a + 1, epg):
                pair = grp[a] + grp[b]
                best = pair if best is None else jnp.maximum(best, pair)
        scores.append(best)
    top_score = scores[0]
    top_group = jnp.zeros_like(scores[0], dtype=jnp.int32)
    for g in range(1, N_EXPERT_GROUPS):
        better = scores[g] > top_score
        top_score = jnp.where(better, scores[g], top_score)
        top_group = jnp.where(better, g, top_group)
    sel = []
    for j in range(epg):
        v = groups[0][j]
        for g in range(1, N_EXPERT_GROUPS):
            v = jnp.where(top_group == g, groups[g][j], v)
        sel.append(v)
    v1 = sel[0]
    i1 = jnp.zeros_like(top_group)
    for j in range(1, epg):
        better = sel[j] > v1
        v1 = jnp.where(better, sel[j], v1)
        i1 = jnp.where(better, j, i1)
    v2 = jnp.full_like(v1, -1.0)
    i2 = jnp.zeros_like(top_group)
    for j in range(epg):
        better = (sel[j] > v2) & (i1 != j)
        v2 = jnp.where(better, sel[j], v2)
        i2 = jnp.where(better, j, i2)
    total = v1 + v2
    w1 = v1 / total
    w2 = v2 / total
    return top_group, [jnp.where(i1 == j, w1, jnp.where(i2 == j, w2, 0.0)) for j in range(epg)]


def _merge_kernel(x_ref, ro_ref, att_ref, ssm_ref, gates_ref, mod_ref, wro_ref, wao_ref, wout_ref, g2_ref, wrt_ref,
                  br_ref, x1_ref, h2_ref, meta_ref, cnt_ref):
    ret_branch = _dot(ro_ref[...], wro_ref[...])
    att_branch = _dot_tn(att_ref[...], wao_ref[...])

    def gate(j):
        return _sigmoid(gates_ref[:, j * D_MODEL:(j + 1) * D_MODEL].astype(F32))

    merged = gate(0) * ret_branch + gate(1) * att_branch + gate(2) * ssm_ref[...].astype(F32)
    x1 = x_ref[...] + mod_ref[0, 2:3, :] * _dot(merged.astype(BF16), wout_ref[...])
    x1_ref[...] = x1
    xn = x1 * lax.rsqrt(jnp.mean(x1 * x1, axis=-1, keepdims=True) + EPS) * g2_ref[...]
    h2 = xn * (1.0 + mod_ref[0, 4:5, :]) + mod_ref[0, 3:4, :]
    h2_ref[...] = h2.astype(BF16)

    logits = lax.dot_general(wrt_ref[...], h2, (((1,), (1,)), ((), ())), precision=lax.Precision.HIGHEST,
                             preferred_element_type=F32) + br_ref[...]
    e = jnp.exp(logits - jnp.max(logits, axis=0, keepdims=True))
    probs = e / jnp.sum(e, axis=0, keepdims=True)
    top_group, weights = _route([probs[j:j + 1, :] for j in range(N_EXPERTS)])

    tm = top_group.shape[1]

    @pl.when(pl.program_id(0) % (MOE_PART // tm) == 0)
    def _():
        cnt_ref[...] = jnp.zeros_like(cnt_ref)

    zero_row = jnp.zeros((1, tm), F32)
    onehot = jnp.concatenate([(top_group == g).astype(F32) for g in range(N_EXPERT_GROUPS)]
                             + [zero_row] * (META_ROWS - N_EXPERT_GROUPS), axis=0)
    lane = lax.broadcasted_iota(jnp.int32, onehot.shape, 1)
    incl = onehot
    shift = 1
    while shift < tm:
        incl = incl + jnp.where(lane >= shift, pltpu.roll(incl, shift, axis=1), 0.0)
        shift *= 2
    before = cnt_ref[0, :, 0:1]
    rank = jnp.sum(onehot * (incl - onehot + before), axis=0, keepdims=True)
    cnt_ref[0] = jnp.broadcast_to(before + jnp.sum(onehot, axis=1, keepdims=True), cnt_ref.shape[1:])
    meta_ref[...] = jnp.concatenate([top_group.astype(F32), rank] + weights
                                    + [zero_row] * (META_ROWS - 2 - EXPERTS_PER_GROUP), axis=0)


def _merge(x, ro, att_t, ssm, gates, mod, mod_rows, wro, wao, wout, g2, wrt, br):
    n = x.shape[0]
    tm = TM_PROJ
    mod_row = _mod_row_fn(mod_rows, tm)
    row = lambda i: (i, 0)
    in_specs = [
        pl.BlockSpec((tm, D_MODEL), row),
        pl.BlockSpec((tm, RET_W), row),
        pl.BlockSpec((ATT_W, tm), lambda i: (0, i)),
        pl.BlockSpec((tm, D_MODEL), row),
        pl.BlockSpec((tm, 3 * D_MODEL), row),
        pl.BlockSpec((1, N_MOD, D_MODEL), lambda i: (mod_row(i), 0, 0)),
        _const_spec((RET_W, D_MODEL)),
        _const_spec((ATT_W, D_MODEL)),
        _const_spec((D_MODEL, D_MODEL)),
        _const_spec((1, D_MODEL)),
        _const_spec((N_EXPERTS, D_MODEL)),
        _const_spec((N_EXPERTS, 1)),
    ]
    return pl.pallas_call(
        _merge_kernel,
        out_shape=(
            jax.ShapeDtypeStruct((n, D_MODEL), F32),
            jax.ShapeDtypeStruct((n, D_MODEL), BF16),
            jax.ShapeDtypeStruct((META_ROWS, n), F32),
            jax.ShapeDtypeStruct((n // MOE_PART, META_ROWS, LANES), F32),
        ),
        grid=(n // tm,),
        in_specs=in_specs,
        out_specs=(
            pl.BlockSpec((tm, D_MODEL), row),
            pl.BlockSpec((tm, D_MODEL), row),
            pl.BlockSpec((META_ROWS, tm), lambda i: (0, i)),
            pl.BlockSpec((1, META_ROWS, LANES), lambda i: (i // (MOE_PART // tm), 0, 0)),
        ),
        compiler_params=_params(("arbitrary",)),
    )(x, ro, att_t, ssm, gates, mod, wro, wao, wout, g2, wrt, br)


def _moe_kernel(dest_ref, tile_group_ref, n_valid_ref, *refs, final, n_scatter, n_expert, n_gather):
    if final:
        (h2_ref, wrow_ref, wg_ref, wu_ref, wd_ref, x1_ref, mod_ref, fg_ref, o_ref,
         sorted_ref, wsort_ref, stage_ref, acc_ref) = refs
    else:
        (h2_ref, wrow_ref, wg_ref, wu_ref, wd_ref, x1_ref, mod_ref, o_ref,
         sorted_ref, wsort_ref, stage_ref, acc_ref) = refs
    tm = h2_ref.shape[0]
    epg = EXPERTS_PER_GROUP
    per_part = n_scatter + n_expert + n_gather
    part = pl.program_id(0) // per_part
    i = pl.program_id(0) % per_part

    @pl.when(i == 0)
    def _():
        sorted_ref[...] = jnp.zeros_like(sorted_ref)
        wsort_ref[...] = jnp.zeros_like(wsort_ref)

    @pl.when(i < n_scatter)
    def _():
        stage_ref[...] = h2_ref[...].astype(F32)
        base = (part * n_scatter + i) * tm

        def move(t, carry):
            d = dest_ref[base + t]
            sorted_ref[pl.ds(d, 1), :] = stage_ref[pl.ds(t, 1), :]
            wsort_ref[pl.ds(d, 1), :] = wrow_ref[pl.ds(t, 1), :]
            return carry

        lax.fori_loop(0, tm, move, 0, unroll=8)

    step = i - n_scatter
    tile = step // epg
    ein = step % epg

    @pl.when((step >= 0) & (step < n_expert) & (tile < n_valid_ref[part]))
    def _():
        rows = pl.ds(pl.multiple_of(tile * MOE_TILE, MOE_TILE), MOE_TILE)
        x = sorted_ref[rows, :].astype(BF16)
        g = _dot(x, wg_ref[0])
        u = _dot(x, wu_ref[0])
        w = wsort_ref[rows, :]
        lane = lax.broadcasted_iota(jnp.int32, w.shape, 1)
        w_col = jnp.sum(jnp.where(lane == ein, w, 0.0), axis=1, keepdims=True)
        out = _dot((g * _sigmoid(g) * u * w_col).astype(BF16), wd_ref[0])

        @pl.when(ein == 0)
        def _():
            acc_ref[...] = out

        @pl.when(ein > 0)
        def _():
            acc_ref[...] += out

        @pl.when(ein == epg - 1)
        def _():
            sorted_ref[rows, :] = acc_ref[...]

    @pl.when(step >= n_expert)
    def _():
        base = (part * n_gather + step - n_expert) * tm

        def move(t, carry):
            stage_ref[pl.ds(t, 1), :] = sorted_ref[pl.ds(dest_ref[base + t], 1), :]
            return carry

        lax.fori_loop(0, tm, move, 0, unroll=8)
        x2 = x1_ref[...] + mod_ref[0, 5:6, :] * stage_ref[...]
        if final:
            x2 = x2 * lax.rsqrt(jnp.mean(x2 * x2, axis=-1, keepdims=True) + EPS) * fg_ref[...]
        o_ref[...] = x2


def _moe_plan(meta, cnt):
    n = meta.shape[1]
    n_part = cnt.shape[0]
    n_tiles = n // n_part // MOE_TILE + N_EXPERT_GROUPS
    counts = cnt[:, 0:N_EXPERT_GROUPS, 0].astype(jnp.int32)
    padded = (counts + MOE_TILE - 1) // MOE_TILE * MOE_TILE
    ends = jnp.cumsum(padded, axis=1)
    starts = ends - padded
    group = meta[0].astype(jnp.int32).reshape(n_part, -1)
    dest = meta[1].astype(jnp.int32).reshape(n_part, -1)
    for g in range(N_EXPERT_GROUPS):
        dest = dest + jnp.where(group == g, starts[:, g:g + 1], 0)
    tile_start = jnp.arange(n_tiles, dtype=jnp.int32) * MOE_TILE
    tile_group = jnp.sum(tile_start[None, :, None] >= ends[:, None, :], axis=2).astype(jnp.int32)
    tile_group = jnp.minimum(tile_group, N_EXPERT_GROUPS - 1)
    n_valid = (ends[:, N_EXPERT_GROUPS - 1] // MOE_TILE).astype(jnp.int32)
    w_rows = jnp.pad(meta[2:2 + EXPERTS_PER_GROUP].T, ((0, 0), (0, LANES - EXPERTS_PER_GROUP)))
    return dest.reshape(-1), tile_group, n_valid, w_rows


def _moe(h2, meta, cnt, x1, mod, mod_rows, layer, wg, wu, wd, final_g):
    n = h2.shape[0]
    tm = MOE_TILE
    n_part = cnt.shape[0]
    dest, tile_group, n_valid, w_rows = _moe_plan(meta, cnt)
    n_tiles = tile_group.shape[1]
    epg = EXPERTS_PER_GROUP
    n_scatter, n_expert, n_gather = n // n_part // tm, n_tiles * epg, n // n_part // tm
    per_part = n_scatter + n_expert + n_gather
    mod_row = _mod_row_fn(mod_rows, tm)
    final = final_g is not None

    def scatter_tile(i):
        return (i // per_part) * n_scatter + jnp.minimum(i % per_part, n_scatter - 1)

    def gather_tile(i):
        return (i // per_part) * n_gather + jnp.clip(i % per_part - n_scatter - n_expert, 0, n_gather - 1)

    def expert(i, tg, nv):
        part = i // per_part
        step = jnp.clip(i % per_part - n_scatter, 0, n_expert - 1)
        tile = step // epg
        in_use = tile < nv[part]
        group = tg[part * n_tiles + jnp.clip(tile, 0, jnp.maximum(nv[part] - 1, 0))]
        return layer * N_EXPERTS + group * epg + jnp.where(in_use, step % epg, epg - 1)

    in_specs = [
        pl.BlockSpec((tm, D_MODEL), lambda i, d, tg, nv: (scatter_tile(i), 0)),
        pl.BlockSpec((tm, LANES), lambda i, d, tg, nv: (scatter_tile(i), 0)),
        pl.BlockSpec((1, D_MODEL, D_FF), lambda i, d, tg, nv: (expert(i, tg, nv), 0, 0)),
        pl.BlockSpec((1, D_MODEL, D_FF), lambda i, d, tg, nv: (expert(i, tg, nv), 0, 0)),
        pl.BlockSpec((1, D_FF, D_MODEL), lambda i, d, tg, nv: (expert(i, tg, nv), 0, 0)),
        pl.BlockSpec((tm, D_MODEL), lambda i, d, tg, nv: (gather_tile(i), 0)),
        pl.BlockSpec((1, N_MOD, D_MODEL), lambda i, d, tg, nv: (mod_row(gather_tile(i)), 0, 0)),
    ]
    args = [h2, w_rows, wg, wu, wd, x1, mod]
    if final:
        in_specs.append(pl.BlockSpec((1, D_MODEL), lambda i, d, tg, nv: (0, 0)))
        args.append(final_g)
    grid_spec = pltpu.PrefetchScalarGridSpec(
        num_scalar_prefetch=3,
        grid=(n_part * per_part,),
        in_specs=in_specs,
        out_specs=pl.BlockSpec((tm, D_MODEL), lambda i, d, tg, nv: (gather_tile(i), 0)),
        scratch_shapes=[
            pltpu.VMEM((n_tiles * MOE_TILE, D_MODEL), F32),
            pltpu.VMEM((n_tiles * MOE_TILE, LANES), F32),
            pltpu.VMEM((tm, D_MODEL), F32),
            pltpu.VMEM((MOE_TILE, D_MODEL), F32),
        ],
    )
    return pl.pallas_call(
        functools.partial(_moe_kernel, final=final, n_scatter=n_scatter, n_expert=n_expert, n_gather=n_gather),
        out_shape=jax.ShapeDtypeStruct((n, D_MODEL), F32),
        grid_spec=grid_spec,
        compiler_params=_params(("arbitrary",)),
    )(dest, tile_group.reshape(-1), n_valid, *args)


def _block_diag(m):
    g, r, c = m.shape
    eye = jnp.eye(g, dtype=m.dtype)
    return (eye[:, None, :, None] * m[:, :, None, :]).reshape(g * r, g * c)


def _s5_direction_params(a_re, a_im, log_dt, b_re, b_im, c_re, c_im, reverse):
    n_g = a_re.shape[0]
    width = S5_T * SSM_GROUP
    hp = lax.Precision.HIGHEST
    step = jnp.exp(log_dt)[:, None]
    mag = jnp.exp(a_re * step)
    ar = mag * jnp.cos(a_im * step)
    ai = mag * jnp.sin(a_im * step)
    den = a_re * a_re + a_im * a_im
    fr = ((ar - 1.0) * a_re + ai * a_im) / den
    fi = (ai * a_re - (ar - 1.0) * a_im) / den
    bbr = (fr[:, :, None] * b_re - fi[:, :, None] * b_im).transpose(0, 2, 1)
    bbi = (fr[:, :, None] * b_im + fi[:, :, None] * b_re).transpose(0, 2, 1)
    pr, pi = [jnp.ones_like(ar)], [jnp.zeros_like(ar)]
    for _ in range(S5_T):
        r, i = pr[-1], pi[-1]
        pr.append(r * ar - i * ai)
        pi.append(r * ai + i * ar)
    pr, pi = jnp.stack(pr)[:, :, None, :], jnp.stack(pi)[:, :, None, :]
    mr = pr * bbr - pi * bbi
    mi = pr * bbi + pi * bbr
    cr = c_re * pr - c_im * pi
    ci = c_re * pi + c_im * pr
    kern = (jnp.einsum('gop,kgip->goki', c_re, mr[:S5_T], precision=hp)
            - jnp.einsum('gop,kgip->goki', c_im, mi[:S5_T], precision=hp))
    pad = jnp.zeros_like(kern[:, :, 1:, :])
    strip = jnp.concatenate([pad, kern] if reverse else [kern[:, :, ::-1, :], pad], axis=2)
    tt = jnp.concatenate([strip, pad[:, :, 0:1, :]], axis=2).reshape(n_g, SSM_GROUP, 2 * width)
    wx = jnp.concatenate([mr[:S5_T], mi[:S5_T]], axis=3)
    wx = (wx if reverse else wx[::-1]).transpose(1, 0, 2, 3).reshape(n_g, width, S5_GL)
    wy = jnp.concatenate([cr[1:], -ci[1:]], axis=3)
    wy = (wy[::-1] if reverse else wy).transpose(1, 0, 2, 3).reshape(n_g, width, S5_GL)

    def coef_rows(r, i):
        return jnp.stack([jnp.concatenate([r, r], axis=1).reshape(-1), jnp.concatenate([-i, i], axis=1).reshape(-1)])

    sr, si = pr[S5_T, :, 0, :], pi[S5_T, :, 0, :]
    a_t = coef_rows(sr, si)
    for _ in range(int(math.log2(S5_SEG // S5_T))):
        sr, si = sr * sr - si * si, 2.0 * sr * si
    return tt.astype(BF16), wx.astype(BF16), wy.astype(BF16), a_t, coef_rows(sr, si)


def _rope_tables(n_tokens):
    rows = n_tokens // GRID_W
    row = jnp.repeat(jnp.arange(rows, dtype=F32), GRID_W)
    col = jnp.tile(jnp.arange(GRID_W, dtype=F32), rows)
    n_freq = ATT_HEAD_DIM // 4
    inv = ROPE_THETA ** (-jnp.arange(n_freq, dtype=F32) / n_freq)
    ang = jnp.concatenate([row[:, None] * inv, col[:, None] * inv], axis=-1)
    cos = jnp.repeat(jnp.cos(ang), 2, axis=-1)
    sin = jnp.repeat(jnp.sin(ang), 2, axis=-1) * jnp.tile(jnp.array([-1.0, 1.0], F32), ATT_HEAD_DIM // 2)
    return cos, sin


def _swap_pairs(a, axis=-1):
    axis = axis % a.ndim
    pairs = a.reshape(a.shape[:axis] + (a.shape[axis] // 2, 2) + a.shape[axis + 1:])
    return jnp.flip(pairs, axis=axis + 1).reshape(a.shape)


def _layer(x, batch, mod, mod_rows, p, ctx):
    n = x.shape[0]
    seq = n // batch
    latent = ctx is not None
    outs = _inproj(x, mod, mod_rows, p, latent)
    rqkv, rg, gates, q_t, k16, v_t = outs[:6]

    ro, ret_state = _retention(rqkv.reshape(batch, seq, 3 * RET_W), rg.reshape(batch, seq, RET_W), p['ret_lg'],
                               p['ret_gn'], ctx[2] if latent else None, bb=1 if latent else 8)

    if latent:
        k_ctx = ctx[0].reshape(batch * ATT_KC, ATT_KV_W).astype(BF16)
        v_ctx_t = ctx[1].reshape(batch, ATT_KC, ATT_KV_W).transpose(0, 2, 1).reshape(batch * ATT_KV_W, ATT_KC)
        att_t = _attention(q_t, k16, v_t, k_ctx, v_ctx_t.astype(BF16), batch)
    elif seq == ATT_KC and batch % ATT_SHORT_SEQS == 0:
        att_t = _attention_short(q_t, k16, v_t, batch)
    else:
        att_t = _attention(q_t, k16, v_t, None, None, batch)

    u = _s5_in(x, mod, mod_rows, p['g1'], p['w_su_t'])
    if latent:
        s0 = jnp.stack([ctx[3], ctx[4]], axis=3).transpose(1, 0, 2, 3, 4).reshape(2, batch, SSM_GROUPS * S5_GL)
        y = _s5(u, p['s5_tt'], p['s5_wx'], p['s5_wy'], p['s5_a'], p['s5_aseg'], s0)
        ssm_state = None
    else:
        y, ssm_state = _s5(u, p['s5_tt'], p['s5_wx'], p['s5_wy'], p['s5_a'], None, None)
        ssm_state = ssm_state.reshape(2, batch, SSM_GROUPS, 2, SSM_STATE)
    ssm = _s5_out(y, u, p['ssm_d'], p['w_glu'])

    x1, h2, meta, cnt = _merge(x, ro.reshape(n, RET_W), att_t, ssm, gates, mod, mod_rows, p['w_ret_out'],
                               p['w_att_out'], p['w_out'], p['g2'], p['w_router_t'], p['b_router'])
    x2 = _moe(h2, meta, cnt, x1, mod, mod_rows, p['layer'], p['w_gate'], p['w_up'], p['w_down'], p.get('final_g'))
    if latent:
        return x2, None
    return x2, (outs[6], outs[7], ret_state, ssm_state)


def kernel(x_prompt, x_sample, c, cache_attn_k, cache_attn_v, state_ret, state_ssm_re, state_ssm_im, c_ctx, w_ada,
           b_ada, norm1_g, norm2_g, w_in, ret_lg_f, ret_lg_b, ret_norm_g, w_ret_out, att_q_norm_g, att_k_norm_g,
           w_att_out, a_re_f, a_im_f, a_re_b, a_im_b, log_dt_f, log_dt_b, ssm_b_re, ssm_b_im, ssm_c_re, ssm_c_im,
           ssm_d, w_glu, w_out, w_router, b_router, w_gate, w_up, w_down, final_norm_g):
    batch, seq, _ = x_prompt.shape
    dec_batch, dec_seq, _ = x_sample.shape
    assert batch * seq == S5_ROWS * S5_SEG and dec_batch * dec_seq == S5_ROWS * S5_SEG
    assert seq == S5_SEG and dec_seq % S5_SEG == 0
    assert cache_attn_k.shape[2] == ATT_KC and seq % ATT_KC == 0 and dec_seq % ATT_QB == 0

    c_rows = jnp.zeros((ADA_ROWS, D_MODEL), F32).at[0].set(c_ctx).at[1:1 + dec_batch].set(c)
    mod_all = _ada(c_rows, w_ada, b_ada).reshape(DEPTH, ADA_ROWS, N_MOD, D_MODEL)

    cos, sin = _rope_tables(dec_seq)
    ones = _block_diag(jnp.ones((ATT_KV_HEADS, ATT_HEAD_DIM, ATT_HEAD_DIM), BF16))
    w_in16 = w_in.astype(BF16)
    w_gate16 = w_gate.astype(BF16).reshape(DEPTH * N_EXPERTS, D_MODEL, D_FF)
    w_up16 = w_up.astype(BF16).reshape(DEPTH * N_EXPERTS, D_MODEL, D_FF)
    w_down16 = w_down.astype(BF16).reshape(DEPTH * N_EXPERTS, D_FF, D_MODEL)
    s5_fwd = jax.vmap(functools.partial(_s5_direction_params, reverse=False))(
        a_re_f, a_im_f, log_dt_f, ssm_b_re, ssm_b_im, ssm_c_re, ssm_c_im)
    s5_bwd = jax.vmap(functools.partial(_s5_direction_params, reverse=True))(
        a_re_b, a_im_b, log_dt_b, ssm_b_re, ssm_b_im, ssm_c_re, ssm_c_im)

    xp = x_prompt.reshape(batch * seq, D_MODEL)
    xs = x_sample.reshape(dec_batch * dec_seq, D_MODEL)
    ks_, vs_, rets_, ssms_ = [], [], [], []
    for l in range(DEPTH):
        w_l = w_in[l]
        gq = att_q_norm_g[l][:, None]
        gk = jnp.tile(att_k_norm_g[l], ATT_KV_HEADS)[None]
        wq_t = w_l[:, C_AQ:C_AK].T
        fwd = [a[l] for a in s5_fwd]
        bwd = [a[l] for a in s5_bwd]
        p = {
            'layer': l, 'g1': norm1_g[l][None], 'g2': norm2_g[l][None], 'w_in': w_in16, 'ones': ones,
            'wq_t': wq_t.astype(BF16), 'wv_t': w_l[:, C_AV:C_SU].T.astype(BF16), 'gq': gq, 'gk': gk,
            'wqs_t': _swap_pairs(wq_t, axis=0).astype(BF16), 'wk_sw': _swap_pairs(w_l[:, C_AK:C_AV]).astype(BF16),
            'gqs': _swap_pairs(gq, axis=0), 'gks': _swap_pairs(gk),
            'cos': jnp.tile(cos, (1, ATT_KV_HEADS)), 'sin': jnp.tile(sin, (1, ATT_KV_HEADS)),
            'cos_t': cos.T, 'sin_t': sin.T,
            'ret_lg': jnp.stack([ret_lg_f[l], ret_lg_b[l]]), 'ret_gn': ret_norm_g[l][None],
            'w_ret_out': w_ret_out[l].astype(BF16), 'w_att_out': w_att_out[l].astype(BF16),
            'w_su_t': w_l[:, C_SU:C_G].T.astype(BF16),
            's5_tt': jnp.stack([fwd[0], bwd[0]]), 's5_wx': jnp.stack([fwd[1], bwd[1]]),
            's5_wy': jnp.stack([fwd[2], bwd[2]]), 's5_a': jnp.stack([fwd[3], bwd[3]]),
            's5_aseg': jnp.stack([fwd[4], bwd[4]]),
            'ssm_d': ssm_d[l][:, None], 'w_glu': w_glu[l].astype(BF16), 'w_out': w_out[l].astype(BF16),
            'w_router_t': w_router.T, 'b_router': b_router[:, None],
            'w_gate': w_gate16, 'w_up': w_up16, 'w_down': w_down16,
        }
        if l == DEPTH - 1:
            p['final_g'] = final_norm_g[None]
        mod = mod_all[l]

        xp, (k_c, v_c, st_r, st_s) = _layer(xp, batch, mod, (0, batch * seq), p, None)
        ks_.append(k_c.reshape(batch, seq, ATT_KV_HEADS, ATT_HEAD_DIM))
        vs_.append(v_c.reshape(batch, seq, ATT_KV_HEADS, ATT_HEAD_DIM))
        rets_.append(st_r)
        ssms_.append(st_s)

        ctx = (cache_attn_k[:, l], cache_attn_v[:, l], state_ret[:, l], state_ssm_re[:, l], state_ssm_im[:, l])
        xs, _ = _layer(xs, dec_batch, mod, (1, dec_seq), p, ctx)

    ssm_all = jnp.stack(ssms_, axis=1)
    ssm_all = ssm_all.transpose(2, 1, 0, 3, 4, 5)
    new_re = ssm_all[..., 0, :]
    new_im = ssm_all[..., 1, :]
    return (xp.reshape(batch, seq, D_MODEL), xs.reshape(dec_batch, dec_seq, D_MODEL),
            jnp.stack(ks_, axis=1), jnp.stack(vs_, axis=1), jnp.stack(rets_, axis=1), new_re, new_im)
```

```python
import functools
import math

import jax
import jax.numpy as jnp
from jax import lax
from jax.experimental import pallas as pl
from jax.experimental.pallas import tpu as pltpu

F32 = jnp.float32
BF16 = jnp.bfloat16

D_MODEL = 1024
DEPTH = 2
GRID_W = 64
RET_HEADS = 4
RET_DK = 128
RET_DV = 128
RET_CHUNK = 256
RET_W = RET_HEADS * RET_DV
ATT_HEADS = 8
ATT_KV_HEADS = 2
ATT_GROUP = ATT_HEADS // ATT_KV_HEADS
ATT_HEAD_DIM = 64
ATT_W = ATT_HEADS * ATT_HEAD_DIM
ATT_KV_W = ATT_KV_HEADS * ATT_HEAD_DIM
ROPE_THETA = 10000.0
SSM_CH = 512
SSM_GROUP = 16
SSM_GROUPS = SSM_CH // SSM_GROUP
SSM_STATE = 64
N_EXPERTS = 16
N_EXPERT_GROUPS = 4
EXPERTS_PER_GROUP = N_EXPERTS // N_EXPERT_GROUPS
D_FF = 512
N_MOD = 6
EPS = 1e-6

C_RQ, C_RK, C_RV, C_RG = 0, 512, 1024, 1536
C_AQ, C_AK, C_AV, C_SU, C_G, C_END = 2048, 2560, 2688, 2816, 3328, 6400

VMEM_LIMIT = 52 * 1024 * 1024

ATT_Q_SCALE = ATT_HEAD_DIM ** -0.5 * math.log2(math.e)
ATT_QB = 512
ATT_KC = 256
ATT_V_ROWS = ATT_HEAD_DIM + 16
ATT_RING = 4
ATT_SHORT_SEQS = 8

TM_PROJ = 512
RET_UNROLL = 4
MOE_TILE = 512
MOE_PART = 4096
META_ROWS = 8
S5_ROWS = 32
S5_SEG = 256
S5_T = 16
S5_GL = 2 * SSM_STATE
S5_GB = 16
S5_TM = 2048
S5_PITCH = 24
LANES = 128
STAGE_ROWS = 256
ADA_ROWS = 8
ADA_TN = 1536


def _dot(a, b):
    return jnp.dot(a, b, preferred_element_type=F32)


def _dot_nt(a, b):
    return lax.dot_general(a, b, (((1,), (1,)), ((), ())), preferred_element_type=F32)


def _dot_tn(a, b):
    return lax.dot_general(a, b, (((0,), (0,)), ((), ())), preferred_element_type=F32)


def _sigmoid(x):
    return 0.5 * jnp.tanh(0.5 * x) + 0.5


def _const_spec(shape):
    n = len(shape)
    return pl.BlockSpec(shape, lambda *_: (0,) * n, pipeline_mode=pl.Buffered(1))


def _params(sem, vmem=VMEM_LIMIT):
    return pltpu.CompilerParams(dimension_semantics=sem, vmem_limit_bytes=vmem)


def _mod_row_fn(mod_rows, tm):
    base, per_row = mod_rows
    return lambda i: base + (i * tm) // per_row


def _modulated_norm(x, g, mod_ref):
    xn = x * lax.rsqrt(jnp.mean(x * x, axis=-1, keepdims=True) + EPS) * g
    return xn * (1.0 + mod_ref[0, 1:2, :]) + mod_ref[0, 0:1, :]


def _ada_kernel(c_ref, w_ref, b_ref, o_ref):
    c = c_ref[...]
    a = (c * _sigmoid(c)).astype(BF16)
    o_ref[0] = _dot(a, w_ref[0].astype(BF16)) + b_ref[0]


def _ada(c_rows, w_ada, b_ada):
    n_col = N_MOD * D_MODEL
    return pl.pallas_call(
        _ada_kernel,
        out_shape=jax.ShapeDtypeStruct((DEPTH, ADA_ROWS, n_col), F32),
        grid=(DEPTH, n_col // ADA_TN),
        in_specs=[
            pl.BlockSpec((ADA_ROWS, D_MODEL), lambda l, j: (0, 0)),
            pl.BlockSpec((1, D_MODEL, ADA_TN), lambda l, j: (l, 0, j)),
            pl.BlockSpec((1, 1, ADA_TN), lambda l, j: (l, 0, j)),
        ],
        out_specs=pl.BlockSpec((1, ADA_ROWS, ADA_TN), lambda l, j: (l, 0, j)),
        compiler_params=_params(("arbitrary", "arbitrary")),
    )(c_rows, w_ada, b_ada.reshape(DEPTH, 1, n_col))


def _head_mean_sq(z, ones):
    z2 = z * z
    hi = z2.astype(BF16)
    lo = (z2 - hi.astype(F32)).astype(BF16)
    return (_dot(hi, ones) + _dot(lo, ones)) * (1.0 / ATT_HEAD_DIM)


def _inproj_kernel(*refs, latent):
    if latent:
        (x_ref, mod_ref, g1_ref, w_ref, wqt_ref, wvt_ref, ones_ref, gq_ref, gk_ref,
         wqst_ref, wks_ref, gqs_ref, gks_ref, cos_ref, sin_ref, cost_ref, sint_ref,
         rqkv_ref, rg_ref, gates_ref, qt_ref, k16_ref, vt_ref) = refs
    else:
        (x_ref, mod_ref, g1_ref, w_ref, wqt_ref, wvt_ref, ones_ref, gq_ref, gk_ref,
         rqkv_ref, rg_ref, gates_ref, qt_ref, k16_ref, vt_ref, ak_ref, av_ref) = refs
    tm = x_ref.shape[0]
    h = _modulated_norm(x_ref[...], g1_ref[...], mod_ref).astype(BF16)

    def seg(a, b):
        return _dot(h, w_ref[0, :, a:b])

    rqkv_ref[:, 0:RET_W] = (seg(C_RQ, C_RK) * (RET_DK ** -0.5)).astype(BF16)
    rqkv_ref[:, RET_W:3 * RET_W] = seg(C_RK, C_RG).astype(BF16)
    rg_ref[...] = seg(C_RG, C_AQ).astype(BF16)
    for j in range(3):
        gates_ref[:, j * D_MODEL:(j + 1) * D_MODEL] = seg(C_G + j * D_MODEL, C_G + (j + 1) * D_MODEL).astype(BF16)

    zk = seg(C_AK, C_AV)
    inv_k = lax.rsqrt(_head_mean_sq(zk, ones_ref[...]) + EPS)
    yk = zk * inv_k * gk_ref[...]
    zq = _dot_nt(wqt_ref[...], h).reshape(ATT_HEADS, ATT_HEAD_DIM, tm)
    inv_q = lax.rsqrt(jnp.mean(zq * zq, axis=1, keepdims=True) + EPS)
    yq = zq * inv_q * gq_ref[...]
    if latent:
        yk_sw = _dot(h, wks_ref[...]) * inv_k * gks_ref[...]
        yk = yk * cos_ref[...] + yk_sw * sin_ref[...]
        zq_sw = _dot_nt(wqst_ref[...], h).reshape(ATT_HEADS, ATT_HEAD_DIM, tm)
        yq = yq * cost_ref[...] + (zq_sw * inv_q * gqs_ref[...]) * sint_ref[...]
    qt_ref[...] = (yq * ATT_Q_SCALE).reshape(ATT_W, tm).astype(BF16)
    k16_ref[...] = yk.astype(BF16)
    vt = _dot_nt(wvt_ref[...], h).astype(BF16)
    for c in range(tm // ATT_KC):
        vt_ref[c] = vt[:, c * ATT_KC:(c + 1) * ATT_KC]
    if not latent:
        ak_ref[...] = yk
        av_ref[...] = seg(C_AV, C_SU)


def _inproj(x, mod, mod_rows, p, latent):
    n = x.shape[0]
    tm = TM_PROJ
    mod_row = _mod_row_fn(mod_rows, tm)
    row = lambda i: (i, 0)
    in_specs = [
        pl.BlockSpec((tm, D_MODEL), row),
        pl.BlockSpec((1, N_MOD, D_MODEL), lambda i: (mod_row(i), 0, 0)),
        _const_spec((1, D_MODEL)),
        pl.BlockSpec((1, D_MODEL, C_END), lambda i: (p['layer'], 0, 0), pipeline_mode=pl.Buffered(1)),
        _const_spec((ATT_W, D_MODEL)),
        _const_spec((ATT_KV_W, D_MODEL)),
        _const_spec((ATT_KV_W, ATT_KV_W)),
        _const_spec((ATT_HEAD_DIM, 1)),
        _const_spec((1, ATT_KV_W)),
    ]
    args = [x, mod, p['g1'], p['w_in'], p['wq_t'], p['wv_t'], p['ones'], p['gq'], p['gk']]
    if latent:
        n_pos = p['cos'].shape[0] // tm
        in_specs += [
            _const_spec((ATT_W, D_MODEL)),
            _const_spec((D_MODEL, ATT_KV_W)),
            _const_spec((ATT_HEAD_DIM, 1)),
            _const_spec((1, ATT_KV_W)),
            pl.BlockSpec((tm, ATT_KV_W), lambda i: (i % n_pos, 0)),
            pl.BlockSpec((tm, ATT_KV_W), lambda i: (i % n_pos, 0)),
            pl.BlockSpec((ATT_HEAD_DIM, tm), lambda i: (0, i % n_pos)),
            pl.BlockSpec((ATT_HEAD_DIM, tm), lambda i: (0, i % n_pos)),
        ]
        args += [p['wqs_t'], p['wk_sw'], p['gqs'], p['gks'], p['cos'], p['sin'], p['cos_t'], p['sin_t']]
    out_shape = [
        jax.ShapeDtypeStruct((n, 3 * RET_W), BF16),
        jax.ShapeDtypeStruct((n, RET_W), BF16),
        jax.ShapeDtypeStruct((n, 3 * D_MODEL), BF16),
        jax.ShapeDtypeStruct((ATT_W, n), BF16),
        jax.ShapeDtypeStruct((n, ATT_KV_W), BF16),
        jax.ShapeDtypeStruct((n // ATT_KC, ATT_KV_W, ATT_KC), BF16),
    ]
    out_specs = [
        pl.BlockSpec((tm, 3 * RET_W), row),
        pl.BlockSpec((tm, RET_W), row),
        pl.BlockSpec((tm, 3 * D_MODEL), row),
        pl.BlockSpec((ATT_W, tm), lambda i: (0, i)),
        pl.BlockSpec((tm, ATT_KV_W), row),
        pl.BlockSpec((tm // ATT_KC, ATT_KV_W, ATT_KC), lambda i: (i, 0, 0)),
    ]
    if not latent:
        out_shape += [jax.ShapeDtypeStruct((n, ATT_KV_W), F32)] * 2
        out_specs += [pl.BlockSpec((tm, ATT_KV_W), row)] * 2
    return pl.pallas_call(
        functools.partial(_inproj_kernel, latent=latent),
        out_shape=tuple(out_shape),
        grid=(n // tm,),
        in_specs=in_specs,
        out_specs=tuple(out_specs),
        compiler_params=_params(("parallel",)),
    )(*args)


T_DEC_F, T_DEC_B, T_XI_F, T_ZETA_F, T_XI_B, T_ZETA_B, T_CD_F, T_CD_B, N_TAB = range(9)


def _ret_kernel(*refs, has_s0, bb, seq):
    if has_s0:
        lg_ref, q_ref, k_ref, v_ref, rg_ref, gn_ref, s0_ref, o_ref, st_ref, tab_ref, acc_ref = refs
    else:
        lg_ref, q_ref, k_ref, v_ref, rg_ref, gn_ref, o_ref, st_ref, tab_ref, acc_ref = refs
    c = RET_CHUNK
    head = pl.program_id(1)
    lgf = lg_ref[0, head]
    lgb = lg_ref[1, head]
    t = lax.broadcasted_iota(jnp.int32, (c, c), 0).astype(F32)
    s = lax.broadcasted_iota(jnp.int32, (c, c), 1).astype(F32)
    tab_ref[T_DEC_F] = jnp.where(t >= s, jnp.exp(lgf * jnp.maximum(t - s, 0.0)), 0.0)
    tab_ref[T_DEC_B] = jnp.where(s >= t, jnp.exp(lgb * jnp.maximum(s - t, 0.0)), 0.0)
    tab_ref[T_XI_F] = jnp.exp(lgf * (t + 1.0))
    tab_ref[T_ZETA_F] = jnp.exp(lgf * (c - 1.0 - t))
    tab_ref[T_XI_B] = jnp.exp(lgb * (c - t))
    tab_ref[T_ZETA_B] = jnp.exp(lgb * t)
    tab_ref[T_CD_F] = jnp.exp(lgf * (c + 0.0 * t))
    tab_ref[T_CD_B] = jnp.exp(lgb * (c + 0.0 * t))
    if has_s0:
        st_ref[...] = s0_ref[...]
    else:
        st_ref[...] = jnp.zeros_like(st_ref)
    acc_ref[...] = jnp.zeros_like(acc_ref)
    n_chunk = seq // c

    def one_direction(rb, d, off, t_dec, t_xi, t_zeta, t_cd):
        rows = pl.ds(off, c)
        q = q_ref[rb, rows, :]
        k = k_ref[rb, rows, :]
        v = v_ref[rb, rows, :]
        p = (_dot_nt(q, k) * tab_ref[t_dec]).astype(BF16)
        st = st_ref[rb, d, 0]
        q_xi = (q.astype(F32) * tab_ref[t_xi, :, 0:RET_DK]).astype(BF16)
        o = _dot(jnp.concatenate([p, q_xi], axis=1), jnp.concatenate([v, st.astype(BF16)], axis=0))
        acc_ref[rb, rows, :] += o
        kz = (k.astype(F32) * tab_ref[t_zeta, :, 0:RET_DK]).astype(BF16)
        st_ref[rb, d, 0] = tab_ref[t_cd, 0:RET_DK, 0:RET_DV] * st + _dot_tn(kz, v)

    def body(i, carry):
        off_f = pl.multiple_of(i * c, c)
        off_b = pl.multiple_of((n_chunk - 1 - i) * c, c)
        for rb in range(bb):
            one_direction(rb, 0, off_f, T_DEC_F, T_XI_F, T_ZETA_F, T_CD_F)
            one_direction(rb, 1, off_b, T_DEC_B, T_XI_B, T_ZETA_B, T_CD_B)
        return carry

    lax.fori_loop(0, n_chunk, body, 0, unroll=RET_UNROLL if bb == 1 else 1)

    def finish(i, carry):
        rows = pl.ds(pl.multiple_of(i * c, c), c)
        for rb in range(bb):
            o = acc_ref[rb, rows, :]
            ro = o * lax.rsqrt(jnp.mean(o * o, axis=-1, keepdims=True) + EPS) * gn_ref[...]
            g = rg_ref[rb, rows, :].astype(F32)
            o_ref[rb, rows, :] = (ro * (g * _sigmoid(g))).astype(BF16)
        return carry

    lax.fori_loop(0, n_chunk, finish, 0)


def _retention(rqkv, rg, lg, gn, s0, bb):
    b, seq, _ = rqkv.shape
    has_s0 = s0 is not None
    blk = (bb, seq, RET_DK)
    st_spec = pl.BlockSpec((bb, 2, 1, RET_DK, RET_DV), lambda i, h: (i, 0, h, 0, 0))
    in_specs = [
        pl.BlockSpec(memory_space=pltpu.SMEM),
        pl.BlockSpec(blk, lambda i, h: (i, 0, h)),
        pl.BlockSpec(blk, lambda i, h: (i, 0, RET_HEADS + h)),
        pl.BlockSpec(blk, lambda i, h: (i, 0, 2 * RET_HEADS + h)),
        pl.BlockSpec(blk, lambda i, h: (i, 0, h)),
        pl.BlockSpec((1, RET_DV), lambda i, h: (0, h)),
    ]
    args = [lg, rqkv, rqkv, rqkv, rg, gn]
    if has_s0:
        in_specs.append(st_spec)
        args.append(s0)
    return pl.pallas_call(
        functools.partial(_ret_kernel, has_s0=has_s0, bb=bb, seq=seq),
        out_shape=(
            jax.ShapeDtypeStruct((b, seq, RET_W), BF16),
            jax.ShapeDtypeStruct((b, 2, RET_HEADS, RET_DK, RET_DV), F32),
        ),
        grid=(b // bb, RET_HEADS),
        in_specs=in_specs,
        out_specs=(pl.BlockSpec(blk, lambda i, h: (i, 0, h)), st_spec),
        scratch_shapes=[
            pltpu.VMEM((N_TAB, RET_CHUNK, RET_CHUNK), F32),
            pltpu.VMEM((bb, seq, RET_DV), F32),
        ],
        compiler_params=_params(("parallel", "parallel")),
    )(*args)


ST_M, ST_ALPHA, N_ST = range(3)


def _attn_kernel(*refs, has_ctx):
    if has_ctx:
        q_ref, k_ref, v_ref, kc_ref, vc_ref, o_ref, s_ref, p_ref, acc_ref, st_ref, mx_ref = refs
    else:
        q_ref, k_ref, v_ref, o_ref, s_ref, p_ref, acc_ref, st_ref, mx_ref = refs
    kv_head = pl.program_id(1)
    qb = q_ref.shape[1]
    kc = s_ref.shape[1]
    n_own = k_ref.shape[0] // kc
    n_chunk = n_own + (1 if has_ctx else 0)
    q_t = jnp.concatenate([q_ref[h * ATT_HEAD_DIM:(h + 1) * ATT_HEAD_DIM, :] for h in range(ATT_GROUP)], axis=1)
    q_both = jnp.concatenate([q_t] * ATT_KV_HEADS, axis=0)
    row = lax.broadcasted_iota(jnp.int32, q_both.shape, 0)
    q_pad = jnp.where(row // ATT_HEAD_DIM == kv_head, q_both, jnp.zeros_like(q_both))
    ones_rows = (lax.broadcasted_iota(jnp.int32, (ATT_V_ROWS - ATT_HEAD_DIM, kc), 0) == 0).astype(BF16)

    def keys(j):
        if has_ctx and isinstance(j, int) and j == n_own:
            return kc_ref[...]
        return k_ref[pl.ds(pl.multiple_of(j * kc, kc), kc), :]

    def values(j):
        v = vc_ref[...] if has_ctx and isinstance(j, int) and j == n_own else v_ref[j]
        return jnp.concatenate([v, ones_rows], axis=0)

    def put_scores(j, slot):
        s = _dot(keys(j), q_pad)
        s_ref[slot] = s
        mx_ref[slot] = jnp.max(s, axis=0, keepdims=True)

    def weighted_values(j, slot):
        return _dot(values(j), p_ref[slot])

    def phase(j, slot, first=False, last=False):
        if not last:
            put_scores(j + 1, (slot + 1) % ATT_RING)
        if not first:
            acc_ref[...] = st_ref[ST_ALPHA] * acc_ref[...] + weighted_values(j - 1, (slot - 1) % ATT_RING)
        m = st_ref[ST_M]
        m_new = jnp.maximum(m, mx_ref[slot])
        st_ref[ST_ALPHA] = jnp.exp2(m - m_new)
        st_ref[ST_M] = m_new
        p_ref[slot] = jnp.exp2((s_ref[slot] - m_new).astype(BF16))

    st_ref[ST_M] = jnp.full(st_ref.shape[1:], -jnp.inf, F32)
    acc_ref[...] = jnp.zeros_like(acc_ref)
    put_scores(0, 0)
    phase(0, 0, first=True, last=n_chunk == 1)
    n_loop = max(n_own - 2, 0) // ATT_RING

    def revolution(i, carry):
        for t in range(1, ATT_RING + 1):
            phase(ATT_RING * i + t, t % ATT_RING)
        return carry

    lax.fori_loop(0, n_loop, revolution, 0)
    for j in range(1 + ATT_RING * n_loop, n_chunk):
        phase(j, j % ATT_RING, last=j == n_chunk - 1)
    acc = st_ref[ST_ALPHA] * acc_ref[...] + weighted_values(n_chunk - 1, (n_chunk - 1) % ATT_RING)
    out = (acc[0:ATT_HEAD_DIM] / acc[ATT_HEAD_DIM:ATT_HEAD_DIM + 1]).astype(BF16)
    for h in range(ATT_GROUP):
        o_ref[h * ATT_HEAD_DIM:(h + 1) * ATT_HEAD_DIM, :] = out[:, h * qb:(h + 1) * qb]


def _attn_short_kernel(q_ref, k_ref, v_ref, o_ref):
    kv_head = pl.program_id(1)
    n_seq, _, kc = v_ref.shape
    qb = q_ref.shape[1] // n_seq
    ones_rows = (lax.broadcasted_iota(jnp.int32, (ATT_V_ROWS - ATT_HEAD_DIM, kc), 0) == 0).astype(BF16)
    for s in range(n_seq):
        cols = slice(s * qb, (s + 1) * qb)
        q_t = jnp.concatenate([q_ref[h * ATT_HEAD_DIM:(h + 1) * ATT_HEAD_DIM, cols] for h in range(ATT_GROUP)],
                              axis=1)
        q_both = jnp.concatenate([q_t] * ATT_KV_HEADS, axis=0)
        row = lax.broadcasted_iota(jnp.int32, q_both.shape, 0)
        q_pad = jnp.where(row // ATT_HEAD_DIM == kv_head, q_both, jnp.zeros_like(q_both))
        scores = _dot(k_ref[s * kc:(s + 1) * kc, :], q_pad)
        p = jnp.exp2((scores - jnp.max(scores, axis=0, keepdims=True)).astype(BF16))
        acc = _dot(jnp.concatenate([v_ref[s], ones_rows], axis=0), p)
        out = (acc[0:ATT_HEAD_DIM] / acc[ATT_HEAD_DIM:ATT_HEAD_DIM + 1]).astype(BF16)
        for h in range(ATT_GROUP):
            o_ref[h * ATT_HEAD_DIM:(h + 1) * ATT_HEAD_DIM, cols] = out[:, h * qb:(h + 1) * qb]


def _attention_short(q_t, k, v_t, batch):
    n = q_t.shape[1]
    seq = n // batch
    ns = ATT_SHORT_SEQS
    q_spec = pl.BlockSpec((ATT_GROUP * ATT_HEAD_DIM, ns * seq), lambda b, g: (g, b))
    return pl.pallas_call(
        _attn_short_kernel,
        out_shape=jax.ShapeDtypeStruct(q_t.shape, BF16),
        grid=(batch // ns, ATT_KV_HEADS),
        in_specs=[
            q_spec,
            pl.BlockSpec((ns * seq, ATT_KV_W), lambda b, g: (b, 0)),
            pl.BlockSpec((ns, ATT_HEAD_DIM, seq), lambda b, g: (b, g, 0)),
        ],
        out_specs=q_spec,
        compiler_params=_params(("parallel", "parallel")),
    )(q_t, k, v_t)


def _attention(q_t, k, v_t, k_ctx, v_ctx_t, batch):
    n = q_t.shape[1]
    seq = n // batch
    qb, kc = min(ATT_QB, seq), ATT_KC
    n_q = seq // qb
    has_ctx = k_ctx is not None
    q_spec = pl.BlockSpec((ATT_GROUP * ATT_HEAD_DIM, qb), lambda b, g, j: (g, b * n_q + j))
    in_specs = [
        q_spec,
        pl.BlockSpec((seq, ATT_KV_W), lambda b, g, j: (b, 0)),
        pl.BlockSpec((seq // kc, ATT_HEAD_DIM, kc), lambda b, g, j: (b, g, 0)),
    ]
    args = [q_t, k, v_t]
    if has_ctx:
        in_specs += [
            pl.BlockSpec((kc, ATT_KV_W), lambda b, g, j: (b, 0)),
            pl.BlockSpec((ATT_HEAD_DIM, kc), lambda b, g, j: (b * ATT_KV_HEADS + g, 0)),
        ]
        args += [k_ctx, v_ctx_t]
    width = ATT_GROUP * qb
    return pl.pallas_call(
        functools.partial(_attn_kernel, has_ctx=has_ctx),
        out_shape=jax.ShapeDtypeStruct(q_t.shape, BF16),
        grid=(batch, ATT_KV_HEADS, n_q),
        in_specs=in_specs,
        out_specs=q_spec,
        scratch_shapes=[
            pltpu.VMEM((ATT_RING, kc, width), F32),
            pltpu.VMEM((ATT_RING, kc, width), BF16),
            pltpu.VMEM((ATT_V_ROWS, width), F32),
            pltpu.VMEM((N_ST, 1, width), F32),
            pltpu.VMEM((ATT_RING, 1, width), F32),
        ],
        compiler_params=_params(("parallel", "parallel", "parallel")),
    )(*args)


def _s5_in_kernel(x_ref, mod_ref, g1_ref, w_ref, o_ref, h_ref):
    tm = x_ref.shape[0]
    n_chunk = o_ref.shape[2]
    n_col = D_MODEL // LANES
    per_stage = STAGE_ROWS // S5_T

    def stage(i, carry):
        h = _modulated_norm(x_ref[pl.ds(pl.multiple_of(i * STAGE_ROWS, STAGE_ROWS), STAGE_ROWS), :], g1_ref[...],
                            mod_ref)
        for c in range(per_stage):
            rows = pl.ds(pl.multiple_of((i * per_stage + c) * S5_PITCH, S5_PITCH), S5_T)
            for j in range(n_col):
                h_ref[j, rows, :] = h[c * S5_T:(c + 1) * S5_T, j * LANES:(j + 1) * LANES]
        return carry

    lax.fori_loop(0, tm // STAGE_ROWS, stage, 0)
    for s in range(S5_T):
        h = jnp.concatenate([h_ref[j, pl.ds(s, n_chunk, stride=S5_PITCH), :] for j in range(n_col)], axis=1)
        o_ref[s] = _dot_nt(w_ref[...], h.astype(BF16)).astype(BF16)


def _s5_in(x, mod, mod_rows, g1, w_su_t):
    n = x.shape[0]
    tm = S5_TM
    mod_row = _mod_row_fn(mod_rows, tm)
    return pl.pallas_call(
        _s5_in_kernel,
        out_shape=jax.ShapeDtypeStruct((S5_T, SSM_CH, n // S5_T), BF16),
        grid=(n // tm,),
        in_specs=[
            pl.BlockSpec((tm, D_MODEL), lambda i: (i, 0)),
            pl.BlockSpec((1, N_MOD, D_MODEL), lambda i: (mod_row(i), 0, 0)),
            _const_spec((1, D_MODEL)),
            _const_spec((SSM_CH, D_MODEL)),
        ],
        out_specs=pl.BlockSpec((S5_T, SSM_CH, tm // S5_T), lambda i: (0, 0, i)),
        scratch_shapes=[pltpu.VMEM((D_MODEL // LANES, tm // S5_T * S5_PITCH, LANES), F32)],
        compiler_params=_params(("parallel",)),
    )(x, mod, g1, w_su_t)


def _s5_kernel(*refs, two_pass):
    if two_pass:
        u_ref, tt_ref, wx_ref, wy_ref, a_ref, aseg_ref, s0_ref, y_ref, v_ref, init_ref = refs
    else:
        u_ref, tt_ref, wx_ref, wy_ref, a_ref, y_ref, fin_ref, v_ref = refs
    gb = tt_ref.shape[1]
    n_chunk = u_ref.shape[2] // S5_ROWS
    grp = SSM_GROUP

    def toeplitz(d, j):
        strip = tt_ref[d, j]
        width = S5_T * grp
        starts = [(S5_T - 1 - t) * grp for t in range(S5_T)]
        return jnp.concatenate([strip[:, a:a + width] for a in starts], axis=0)

    def put_outputs(j, yt, accumulate):
        for t in range(S5_T):
            if accumulate:
                y_ref[t, j * grp:(j + 1) * grp, :] += yt[t * grp:(t + 1) * grp, :]
            else:
                y_ref[t, j * grp:(j + 1) * grp, :] = yt[t * grp:(t + 1) * grp, :]

    for j in range(gb):
        z = jnp.concatenate([u_ref[s, j * grp:(j + 1) * grp, :] for s in range(S5_T)], axis=0)
        put_outputs(j, _dot(toeplitz(0, j), z) + _dot(toeplitz(1, j), z), accumulate=False)
        for d in range(2):
            v_ref[d, j] = _dot_tn(z, wx_ref[d, j])

    def times(coef_ref, d, j, x):
        lanes = slice(j * S5_GL, (j + 1) * S5_GL)
        return coef_ref[d, 0:1, lanes] * x + coef_ref[d, 1:2, lanes] * pltpu.roll(x, SSM_STATE, axis=1)

    def scan(init, keep_states):
        xs = [list(init[0]), list(init[1])]
        for s in range(n_chunk):
            for d in range(2):
                c = s if d == 0 else n_chunk - 1 - s
                rows = pl.ds(c, S5_ROWS, stride=n_chunk)
                for j in range(gb):
                    v = v_ref[d, j, rows, :]
                    if keep_states:
                        v_ref[d, j, rows, :] = xs[d][j]
                    xs[d][j] = times(a_ref, d, j, xs[d][j]) + v
        return xs

    zero = [[jnp.zeros((S5_ROWS, S5_GL), F32)] * gb] * 2
    if two_pass:
        local = scan(zero, keep_states=False)
        n_batch = s0_ref.shape[1]
        n_seg = S5_ROWS // n_batch
        for d in range(2):
            order = range(n_seg) if d == 0 else range(n_seg - 1, -1, -1)
            for j in range(gb):
                for b in range(n_batch):
                    cur = s0_ref[d, b:b + 1, j * S5_GL:(j + 1) * S5_GL]
                    for sgm in order:
                        r = b * n_seg + sgm
                        init_ref[d, j, r:r + 1, :] = cur
                        cur = times(aseg_ref, d, j, cur) + local[d][j][r:r + 1, :]
        scan([[init_ref[d, j] for j in range(gb)] for d in range(2)], keep_states=True)
    else:
        final = scan(zero, keep_states=True)
        for d in range(2):
            for j in range(gb):
                fin_ref[d, :, j * S5_GL:(j + 1) * S5_GL] = final[d][j]
    for j in range(gb):
        put_outputs(j, _dot_nt(wy_ref[0, j], v_ref[0, j].astype(BF16))
                    + _dot_nt(wy_ref[1, j], v_ref[1, j].astype(BF16)), accumulate=True)


def _s5(u, tt, wx, wy, a_t, a_seg, s0):
    two_pass = s0 is not None
    n_rows = u.shape[2]
    gb = S5_GB
    io_spec = pl.BlockSpec((S5_T, gb * SSM_GROUP, n_rows), lambda i: (0, i, 0))
    coef_spec = pl.BlockSpec((2, 2, gb * S5_GL), lambda i: (0, 0, i))
    in_specs = [
        io_spec,
        pl.BlockSpec((2, gb) + tt.shape[2:], lambda i: (0, i, 0, 0)),
        pl.BlockSpec((2, gb) + wx.shape[2:], lambda i: (0, i, 0, 0)),
        pl.BlockSpec((2, gb) + wy.shape[2:], lambda i: (0, i, 0, 0)),
        coef_spec,
    ]
    args = [u, tt, wx, wy, a_t]
    y_shape = jax.ShapeDtypeStruct(u.shape, F32)
    scratch = [pltpu.VMEM((2, gb, n_rows, S5_GL), F32)]
    if two_pass:
        in_specs += [coef_spec, pl.BlockSpec((2, s0.shape[1], gb * S5_GL), lambda i: (0, 0, i))]
        args += [a_seg, s0]
        out_shape, out_specs = y_shape, io_spec
        scratch.append(pltpu.VMEM((2, gb, S5_ROWS, S5_GL), F32))
    else:
        out_shape = (y_shape, jax.ShapeDtypeStruct((2, S5_ROWS, SSM_GROUPS * S5_GL), F32))
        out_specs = (io_spec, pl.BlockSpec((2, S5_ROWS, gb * S5_GL), lambda i: (0, 0, i)))
    return pl.pallas_call(
        functools.partial(_s5_kernel, two_pass=two_pass),
        out_shape=out_shape,
        grid=(SSM_GROUPS // gb,),
        in_specs=in_specs,
        out_specs=out_specs,
        scratch_shapes=scratch,
        compiler_params=_params(("parallel",)),
    )(*args)


def _s5_out_kernel(y_ref, u_ref, d_ref, w_ref, o_ref, buf_ref):
    n_chunk = y_ref.shape[2]
    n_col = D_MODEL // LANES
    for s in range(S5_T):
        y = y_ref[s] + d_ref[...] * u_ref[s].astype(F32)
        glu = _dot_tn(jax.nn.gelu(y).astype(BF16), w_ref[...])
        out = glu[:, 0:D_MODEL] * _sigmoid(glu[:, D_MODEL:2 * D_MODEL])
        for j in range(n_col):
            buf_ref[j, pl.ds(s, n_chunk, stride=S5_PITCH), :] = out[:, j * LANES:(j + 1) * LANES]

    def unstage(c, carry):
        src = pl.ds(pl.multiple_of(c * S5_PITCH, S5_PITCH), S5_T)
        dst = pl.ds(pl.multiple_of(c * S5_T, S5_T), S5_T)
        o_ref[dst, :] = jnp.concatenate([buf_ref[j, src, :] for j in range(n_col)], axis=1).astype(BF16)
        return carry

    lax.fori_loop(0, n_chunk, unstage, 0, unroll=8)


def _s5_out(y, u, d_skip, w_glu):
    n = y.shape[2] * S5_T
    tm = S5_TM
    io_spec = pl.BlockSpec((S5_T, SSM_CH, tm // S5_T), lambda i: (0, 0, i))
    return pl.pallas_call(
        _s5_out_kernel,
        out_shape=jax.ShapeDtypeStruct((n, D_MODEL), BF16),
        grid=(n // tm,),
        in_specs=[io_spec, io_spec, _const_spec((SSM_CH, 1)), _const_spec((SSM_CH, 2 * D_MODEL))],
        out_specs=pl.BlockSpec((tm, D_MODEL), lambda i: (i, 0)),
        scratch_shapes=[pltpu.VMEM((D_MODEL // LANES, tm // S5_T * S5_PITCH, LANES), F32)],
        compiler_params=_params(("parallel",)),
    )(y, u, d_skip, w_glu)


def _route(probs):
    epg = EXPERTS_PER_GROUP
    groups = [probs[g * epg:(g + 1) * epg] for g in range(N_EXPERT_GROUPS)]
    scores = []
    for grp in groups:
        best = None
        for a in range(epg):
            for b in range(a + 1, epg):
                pair = grp[a] + grp[b]
                best = pair if best is None else jnp.maximum(best, pair)
        scores.append(best)
    top_score = scores[0]
    top_group = jnp.zeros_like(scores[0], dtype=jnp.int32)
    for g in range(1, N_EXPERT_GROUPS):
        better = scores[g] > top_score
        top_score = jnp.where(better, scores[g], top_score)
        top_group = jnp.where(better, g, top_group)
    sel = []
    for j in range(epg):
        v = groups[0][j]
        for g in range(1, N_EXPERT_GROUPS):
            v = jnp.where(top_group == g, groups[g][j], v)
        sel.append(v)
    v1 = sel[0]
    i1 = jnp.zeros_like(top_group)
    for j in range(1, epg):
        better = sel[j] > v1
        v1 = jnp.where(better, sel[j], v1)
        i1 = jnp.where(better, j, i1)
    v2 = jnp.full_like(v1, -1.0)
    i2 = jnp.zeros_like(top_group)
    for j in range(epg):
        better = (sel[j] > v2) & (i1 != j)
        v2 = jnp.where(better, sel[j], v2)
        i2 = jnp.where(better, j, i2)
    total = v1 + v2
    w1 = v1 / total
    w2 = v2 / total
    return top_group, [jnp.where(i1 == j, w1, jnp.where(i2 == j, w2, 0.0)) for j in range(epg)]


def _merge_kernel(x_ref, ro_ref, att_ref, ssm_ref, gates_ref, mod_ref, wro_ref, wao_ref, wout_ref, g2_ref, wrt_ref,
                  br_ref, x1_ref, h2_ref, meta_ref, cnt_ref):
    ret_branch = _dot(ro_ref[...], wro_ref[...])
    att_branch = _dot_tn(att_ref[...], wao_ref[...])

    def gate(j):
        return _sigmoid(gates_ref[:, j * D_MODEL:(j + 1) * D_MODEL].astype(F32))

    merged = gate(0) * ret_branch + gate(1) * att_branch + gate(2) * ssm_ref[...].astype(F32)
    x1 = x_ref[...] + mod_ref[0, 2:3, :] * _dot(merged.astype(BF16), wout_ref[...])
    x1_ref[...] = x1
    xn = x1 * lax.rsqrt(jnp.mean(x1 * x1, axis=-1, keepdims=True) + EPS) * g2_ref[...]
    h2 = xn * (1.0 + mod_ref[0, 4:5, :]) + mod_ref[0, 3:4, :]
    h2_ref[...] = h2.astype(BF16)

    logits = lax.dot_general(wrt_ref[...], h2, (((1,), (1,)), ((), ())), precision=lax.Precision.HIGHEST,
                             preferred_element_type=F32) + br_ref[...]
    e = jnp.exp(logits - jnp.max(logits, axis=0, keepdims=True))
    probs = e / jnp.sum(e, axis=0, keepdims=True)
    top_group, weights = _route([probs[j:j + 1, :] for j in range(N_EXPERTS)])

    tm = top_group.shape[1]

    @pl.when(pl.program_id(0) % (MOE_PART // tm) == 0)
    def _():
        cnt_ref[...] = jnp.zeros_like(cnt_ref)

    zero_row = jnp.zeros((1, tm), F32)
    onehot = jnp.concatenate([(top_group == g).astype(F32) for g in range(N_EXPERT_GROUPS)]
                             + [zero_row] * (META_ROWS - N_EXPERT_GROUPS), axis=0)
    lane = lax.broadcasted_iota(jnp.int32, onehot.shape, 1)
    incl = onehot
    shift = 1
    while shift < tm:
        incl = incl + jnp.where(lane >= shift, pltpu.roll(incl, shift, axis=1), 0.0)
        shift *= 2
    before = cnt_ref[0, :, 0:1]
    rank = jnp.sum(onehot * (incl - onehot + before), axis=0, keepdims=True)
    cnt_ref[0] = jnp.broadcast_to(before + jnp.sum(onehot, axis=1, keepdims=True), cnt_ref.shape[1:])
    meta_ref[...] = jnp.concatenate([top_group.astype(F32), rank] + weights
                                    + [zero_row] * (META_ROWS - 2 - EXPERTS_PER_GROUP), axis=0)


def _merge(x, ro, att_t, ssm, gates, mod, mod_rows, wro, wao, wout, g2, wrt, br):
    n = x.shape[0]
    tm = TM_PROJ
    mod_row = _mod_row_fn(mod_rows, tm)
    row = lambda i: (i, 0)
    in_specs = [
        pl.BlockSpec((tm, D_MODEL), row),
        pl.BlockSpec((tm, RET_W), row),
        pl.BlockSpec((ATT_W, tm), lambda i: (0, i)),
        pl.BlockSpec((tm, D_MODEL), row),
        pl.BlockSpec((tm, 3 * D_MODEL), row),
        pl.BlockSpec((1, N_MOD, D_MODEL), lambda i: (mod_row(i), 0, 0)),
        _const_spec((RET_W, D_MODEL)),
        _const_spec((ATT_W, D_MODEL)),
        _const_spec((D_MODEL, D_MODEL)),
        _const_spec((1, D_MODEL)),
        _const_spec((N_EXPERTS, D_MODEL)),
        _const_spec((N_EXPERTS, 1)),
    ]
    return pl.pallas_call(
        _merge_kernel,
        out_shape=(
            jax.ShapeDtypeStruct((n, D_MODEL), F32),
            jax.ShapeDtypeStruct((n, D_MODEL), BF16),
            jax.ShapeDtypeStruct((META_ROWS, n), F32),
            jax.ShapeDtypeStruct((n // MOE_PART, META_ROWS, LANES), F32),
        ),
        grid=(n // tm,),
        in_specs=in_specs,
        out_specs=(
            pl.BlockSpec((tm, D_MODEL), row),
            pl.BlockSpec((tm, D_MODEL), row),
            pl.BlockSpec((META_ROWS, tm), lambda i: (0, i)),
            pl.BlockSpec((1, META_ROWS, LANES), lambda i: (i // (MOE_PART // tm), 0, 0)),
        ),
        compiler_params=_params(("arbitrary",)),
    )(x, ro, att_t, ssm, gates, mod, wro, wao, wout, g2, wrt, br)


def _moe_kernel(dest_ref, tile_group_ref, n_valid_ref, *refs, final, n_scatter, n_expert, n_gather):
    if final:
        (h2_ref, wrow_ref, wg_ref, wu_ref, wd_ref, x1_ref, mod_ref, fg_ref, o_ref,
         sorted_ref, wsort_ref, stage_ref, acc_ref) = refs
    else:
        (h2_ref, wrow_ref, wg_ref, wu_ref, wd_ref, x1_ref, mod_ref, o_ref,
         sorted_ref, wsort_ref, stage_ref, acc_ref) = refs
    tm = h2_ref.shape[0]
    epg = EXPERTS_PER_GROUP
    per_part = n_scatter + n_expert + n_gather
    part = pl.program_id(0) // per_part
    i = pl.program_id(0) % per_part

    @pl.when(i == 0)
    def _():
        sorted_ref[...] = jnp.zeros_like(sorted_ref)
        wsort_ref[...] = jnp.zeros_like(wsort_ref)

    @pl.when(i < n_scatter)
    def _():
        stage_ref[...] = h2_ref[...].astype(F32)
        base = (part * n_scatter + i) * tm

        def move(t, carry):
            d = dest_ref[base + t]
            sorted_ref[pl.ds(d, 1), :] = stage_ref[pl.ds(t, 1), :]
            wsort_ref[pl.ds(d, 1), :] = wrow_ref[pl.ds(t, 1), :]
            return carry

        lax.fori_loop(0, tm, move, 0, unroll=8)

    step = i - n_scatter
    tile = step // epg
    ein = step % epg

    @pl.when((step >= 0) & (step < n_expert) & (tile < n_valid_ref[part]))
    def _():
        rows = pl.ds(pl.multiple_of(tile * MOE_TILE, MOE_TILE), MOE_TILE)
        x = sorted_ref[rows, :].astype(BF16)
        g = _dot(x, wg_ref[0])
        u = _dot(x, wu_ref[0])
        w = wsort_ref[rows, :]
        lane = lax.broadcasted_iota(jnp.int32, w.shape, 1)
        w_col = jnp.sum(jnp.where(lane == ein, w, 0.0), axis=1, keepdims=True)
        out = _dot((g * _sigmoid(g) * u * w_col).astype(BF16), wd_ref[0])

        @pl.when(ein == 0)
        def _():
            acc_ref[...] = out

        @pl.when(ein > 0)
        def _():
            acc_ref[...] += out

        @pl.when(ein == epg - 1)
        def _():
            sorted_ref[rows, :] = acc_ref[...]

    @pl.when(step >= n_expert)
    def _():
        base = (part * n_gather + step - n_expert) * tm

        def move(t, carry):
            stage_ref[pl.ds(t, 1), :] = sorted_ref[pl.ds(dest_ref[base + t], 1), :]
            return carry

        lax.fori_loop(0, tm, move, 0, unroll=8)
        x2 = x1_ref[...] + mod_ref[0, 5:6, :] * stage_ref[...]
        if final:
            x2 = x2 * lax.rsqrt(jnp.mean(x2 * x2, axis=-1, keepdims=True) + EPS) * fg_ref[...]
        o_ref[...] = x2


def _moe_plan(meta, cnt):
    n = meta.shape[1]
    n_part = cnt.shape[0]
    n_tiles = n // n_part // MOE_TILE + N_EXPERT_GROUPS
    counts = cnt[:, 0:N_EXPERT_GROUPS, 0].astype(jnp.int32)
    padded = (counts + MOE_TILE - 1) // MOE_TILE * MOE_TILE
    ends = jnp.cumsum(padded, axis=1)
    starts = ends - padded
    group = meta[0].astype(jnp.int32).reshape(n_part, -1)
    dest = meta[1].astype(jnp.int32).reshape(n_part, -1)
    for g in range(N_EXPERT_GROUPS):
        dest = dest + jnp.where(group == g, starts[:, g:g + 1], 0)
    tile_start = jnp.arange(n_tiles, dtype=jnp.int32) * MOE_TILE
    tile_group = jnp.sum(tile_start[None, :, None] >= ends[:, None, :], axis=2).astype(jnp.int32)
    tile_group = jnp.minimum(tile_group, N_EXPERT_GROUPS - 1)
    n_valid = (ends[:, N_EXPERT_GROUPS - 1] // MOE_TILE).astype(jnp.int32)
    w_rows = jnp.pad(meta[2:2 + EXPERTS_PER_GROUP].T, ((0, 0), (0, LANES - EXPERTS_PER_GROUP)))
    return dest.reshape(-1), tile_group, n_valid, w_rows


def _moe(h2, meta, cnt, x1, mod, mod_rows, layer, wg, wu, wd, final_g):
    n = h2.shape[0]
    tm = MOE_TILE
    n_part = cnt.shape[0]
    dest, tile_group, n_valid, w_rows = _moe_plan(meta, cnt)
    n_tiles = tile_group.shape[1]
    epg = EXPERTS_PER_GROUP
    n_scatter, n_expert, n_gather = n // n_part // tm, n_tiles * epg, n // n_part // tm
    per_part = n_scatter + n_expert + n_gather
    mod_row = _mod_row_fn(mod_rows, tm)
    final = final_g is not None

    def scatter_tile(i):
        return (i // per_part) * n_scatter + jnp.minimum(i % per_part, n_scatter - 1)

    def gather_tile(i):
        return (i // per_part) * n_gather + jnp.clip(i % per_part - n_scatter - n_expert, 0, n_gather - 1)

    def expert(i, tg, nv):
        part = i // per_part
        step = jnp.clip(i % per_part - n_scatter, 0, n_expert - 1)
        tile = step // epg
        in_use = tile < nv[part]
        group = tg[part * n_tiles + jnp.clip(tile, 0, jnp.maximum(nv[part] - 1, 0))]
        return layer * N_EXPERTS + group * epg + jnp.where(in_use, step % epg, epg - 1)

    in_specs = [
        pl.BlockSpec((tm, D_MODEL), lambda i, d, tg, nv: (scatter_tile(i), 0)),
        pl.BlockSpec((tm, LANES), lambda i, d, tg, nv: (scatter_tile(i), 0)),
        pl.BlockSpec((1, D_MODEL, D_FF), lambda i, d, tg, nv: (expert(i, tg, nv), 0, 0)),
        pl.BlockSpec((1, D_MODEL, D_FF), lambda i, d, tg, nv: (expert(i, tg, nv), 0, 0)),
        pl.BlockSpec((1, D_FF, D_MODEL), lambda i, d, tg, nv: (expert(i, tg, nv), 0, 0)),
        pl.BlockSpec((tm, D_MODEL), lambda i, d, tg, nv: (gather_tile(i), 0)),
        pl.BlockSpec((1, N_MOD, D_MODEL), lambda i, d, tg, nv: (mod_row(gather_tile(i)), 0, 0)),
    ]
    args = [h2, w_rows, wg, wu, wd, x1, mod]
    if final:
        in_specs.append(pl.BlockSpec((1, D_MODEL), lambda i, d, tg, nv: (0, 0)))
        args.append(final_g)
    grid_spec = pltpu.PrefetchScalarGridSpec(
        num_scalar_prefetch=3,
        grid=(n_part * per_part,),
        in_specs=in_specs,
        out_specs=pl.BlockSpec((tm, D_MODEL), lambda i, d, tg, nv: (gather_tile(i), 0)),
        scratch_shapes=[
            pltpu.VMEM((n_tiles * MOE_TILE, D_MODEL), F32),
            pltpu.VMEM((n_tiles * MOE_TILE, LANES), F32),
            pltpu.VMEM((tm, D_MODEL), F32),
            pltpu.VMEM((MOE_TILE, D_MODEL), F32),
        ],
    )
    return pl.pallas_call(
        functools.partial(_moe_kernel, final=final, n_scatter=n_scatter, n_expert=n_expert, n_gather=n_gather),
        out_shape=jax.ShapeDtypeStruct((n, D_MODEL), F32),
        grid_spec=grid_spec,
        compiler_params=_params(("arbitrary",)),
    )(dest, tile_group.reshape(-1), n_valid, *args)


def _block_diag(m):
    g, r, c = m.shape
    eye = jnp.eye(g, dtype=m.dtype)
    return (eye[:, None, :, None] * m[:, :, None, :]).reshape(g * r, g * c)


def _s5_direction_params(a_re, a_im, log_dt, b_re, b_im, c_re, c_im, reverse):
    n_g = a_re.shape[0]
    width = S5_T * SSM_GROUP
    hp = lax.Precision.HIGHEST
    step = jnp.exp(log_dt)[:, None]
    mag = jnp.exp(a_re * step)
    ar = mag * jnp.cos(a_im * step)
    ai = mag * jnp.sin(a_im * step)
    den = a_re * a_re + a_im * a_im
    fr = ((ar - 1.0) * a_re + ai * a_im) / den
    fi = (ai * a_re - (ar - 1.0) * a_im) / den
    bbr = (fr[:, :, None] * b_re - fi[:, :, None] * b_im).transpose(0, 2, 1)
    bbi = (fr[:, :, None] * b_im + fi[:, :, None] * b_re).transpose(0, 2, 1)
    pr, pi = [jnp.ones_like(ar)], [jnp.zeros_like(ar)]
    for _ in range(S5_T):
        r, i = pr[-1], pi[-1]
        pr.append(r * ar - i * ai)
        pi.append(r * ai + i * ar)
    pr, pi = jnp.stack(pr)[:, :, None, :], jnp.stack(pi)[:, :, None, :]
    mr = pr * bbr - pi * bbi
    mi = pr * bbi + pi * bbr
    cr = c_re * pr - c_im * pi
    ci = c_re * pi + c_im * pr
    kern = (jnp.einsum('gop,kgip->goki', c_re, mr[:S5_T], precision=hp)
            - jnp.einsum('gop,kgip->goki', c_im, mi[:S5_T], precision=hp))
    pad = jnp.zeros_like(kern[:, :, 1:, :])
    strip = jnp.concatenate([pad, kern] if reverse else [kern[:, :, ::-1, :], pad], axis=2)
    tt = jnp.concatenate([strip, pad[:, :, 0:1, :]], axis=2).reshape(n_g, SSM_GROUP, 2 * width)
    wx = jnp.concatenate([mr[:S5_T], mi[:S5_T]], axis=3)
    wx = (wx if reverse else wx[::-1]).transpose(1, 0, 2, 3).reshape(n_g, width, S5_GL)
    wy = jnp.concatenate([cr[1:], -ci[1:]], axis=3)
    wy = (wy[::-1] if reverse else wy).transpose(1, 0, 2, 3).reshape(n_g, width, S5_GL)

    def coef_rows(r, i):
        return jnp.stack([jnp.concatenate([r, r], axis=1).reshape(-1), jnp.concatenate([-i, i], axis=1).reshape(-1)])

    sr, si = pr[S5_T, :, 0, :], pi[S5_T, :, 0, :]
    a_t = coef_rows(sr, si)
    for _ in range(int(math.log2(S5_SEG // S5_T))):
        sr, si = sr * sr - si * si, 2.0 * sr * si
    return tt.astype(BF16), wx.astype(BF16), wy.astype(BF16), a_t, coef_rows(sr, si)


def _rope_tables(n_tokens):
    rows = n_tokens // GRID_W
    row = jnp.repeat(jnp.arange(rows, dtype=F32), GRID_W)
    col = jnp.tile(jnp.arange(GRID_W, dtype=F32), rows)
    n_freq = ATT_HEAD_DIM // 4
    inv = ROPE_THETA ** (-jnp.arange(n_freq, dtype=F32) / n_freq)
    ang = jnp.concatenate([row[:, None] * inv, col[:, None] * inv], axis=-1)
    cos = jnp.repeat(jnp.cos(ang), 2, axis=-1)
    sin = jnp.repeat(jnp.sin(ang), 2, axis=-1) * jnp.tile(jnp.array([-1.0, 1.0], F32), ATT_HEAD_DIM // 2)
    return cos, sin


def _swap_pairs(a, axis=-1):
    axis = axis % a.ndim
    pairs = a.reshape(a.shape[:axis] + (a.shape[axis] // 2, 2) + a.shape[axis + 1:])
    return jnp.flip(pairs, axis=axis + 1).reshape(a.shape)


def _layer(x, batch, mod, mod_rows, p, ctx):
    n = x.shape[0]
    seq = n // batch
    latent = ctx is not None
    outs = _inproj(x, mod, mod_rows, p, latent)
    rqkv, rg, gates, q_t, k16, v_t = outs[:6]

    ro, ret_state = _retention(rqkv.reshape(batch, seq, 3 * RET_W), rg.reshape(batch, seq, RET_W), p['ret_lg'],
                               p['ret_gn'], ctx[2] if latent else None, bb=1 if latent else 8)

    if latent:
        k_ctx = ctx[0].reshape(batch * ATT_KC, ATT_KV_W).astype(BF16)
        v_ctx_t = ctx[1].reshape(batch, ATT_KC, ATT_KV_W).transpose(0, 2, 1).reshape(batch * ATT_KV_W, ATT_KC)
        att_t = _attention(q_t, k16, v_t, k_ctx, v_ctx_t.astype(BF16), batch)
    elif seq == ATT_KC and batch % ATT_SHORT_SEQS == 0:
        att_t = _attention_short(q_t, k16, v_t, batch)
    else:
        att_t = _attention(q_t, k16, v_t, None, None, batch)

    u = _s5_in(x, mod, mod_rows, p['g1'], p['w_su_t'])
    if latent:
        s0 = jnp.stack([ctx[3], ctx[4]], axis=3).transpose(1, 0, 2, 3, 4).reshape(2, batch, SSM_GROUPS * S5_GL)
        y = _s5(u, p['s5_tt'], p['s5_wx'], p['s5_wy'], p['s5_a'], p['s5_aseg'], s0)
        ssm_state = None
    else:
        y, ssm_state = _s5(u, p['s5_tt'], p['s5_wx'], p['s5_wy'], p['s5_a'], None, None)
        ssm_state = ssm_state.reshape(2, batch, SSM_GROUPS, 2, SSM_STATE)
    ssm = _s5_out(y, u, p['ssm_d'], p['w_glu'])

    x1, h2, meta, cnt = _merge(x, ro.reshape(n, RET_W), att_t, ssm, gates, mod, mod_rows, p['w_ret_out'],
                               p['w_att_out'], p['w_out'], p['g2'], p['w_router_t'], p['b_router'])
    x2 = _moe(h2, meta, cnt, x1, mod, mod_rows, p['layer'], p['w_gate'], p['w_up'], p['w_down'], p.get('final_g'))
    if latent:
        return x2, None
    return x2, (outs[6], outs[7], ret_state, ssm_state)


def kernel(x_prompt, x_sample, c, cache_attn_k, cache_attn_v, state_ret, state_ssm_re, state_ssm_im, c_ctx, w_ada,
           b_ada, norm1_g, norm2_g, w_in, ret_lg_f, ret_lg_b, ret_norm_g, w_ret_out, att_q_norm_g, att_k_norm_g,
           w_att_out, a_re_f, a_im_f, a_re_b, a_im_b, log_dt_f, log_dt_b, ssm_b_re, ssm_b_im, ssm_c_re, ssm_c_im,
           ssm_d, w_glu, w_out, w_router, b_router, w_gate, w_up, w_down, final_norm_g):
    batch, seq, _ = x_prompt.shape
    dec_batch, dec_seq, _ = x_sample.shape
    assert batch * seq == S5_ROWS * S5_SEG and dec_batch * dec_seq == S5_ROWS * S5_SEG
    assert seq == S5_SEG and dec_seq % S5_SEG == 0
    assert cache_attn_k.shape[2] == ATT_KC and seq % ATT_KC == 0 and dec_seq % ATT_QB == 0

    c_rows = jnp.zeros((ADA_ROWS, D_MODEL), F32).at[0].set(c_ctx).at[1:1 + dec_batch].set(c)
    mod_all = _ada(c_rows, w_ada, b_ada).reshape(DEPTH, ADA_ROWS, N_MOD, D_MODEL)

    cos, sin = _rope_tables(dec_seq)
    ones = _block_diag(jnp.ones((ATT_KV_HEADS, ATT_HEAD_DIM, ATT_HEAD_DIM), BF16))
    w_in16 = w_in.astype(BF16)
    w_gate16 = w_gate.astype(BF16).reshape(DEPTH * N_EXPERTS, D_MODEL, D_FF)
    w_up16 = w_up.astype(BF16).reshape(DEPTH * N_EXPERTS, D_MODEL, D_FF)
    w_down16 = w_down.astype(BF16).reshape(DEPTH * N_EXPERTS, D_FF, D_MODEL)
    s5_fwd = jax.vmap(functools.partial(_s5_direction_params, reverse=False))(
        a_re_f, a_im_f, log_dt_f, ssm_b_re, ssm_b_im, ssm_c_re, ssm_c_im)
    s5_bwd = jax.vmap(functools.partial(_s5_direction_params, reverse=True))(
        a_re_b, a_im_b, log_dt_b, ssm_b_re, ssm_b_im, ssm_c_re, ssm_c_im)

    xp = x_prompt.reshape(batch * seq, D_MODEL)
    xs = x_sample.reshape(dec_batch * dec_seq, D_MODEL)
    ks_, vs_, rets_, ssms_ = [], [], [], []
    for l in range(DEPTH):
        w_l = w_in[l]
        gq = att_q_norm_g[l][:, None]
        gk = jnp.tile(att_k_norm_g[l], ATT_KV_HEADS)[None]
        wq_t = w_l[:, C_AQ:C_AK].T
        fwd = [a[l] for a in s5_fwd]
        bwd = [a[l] for a in s5_bwd]
        p = {
            'layer': l, 'g1': norm1_g[l][None], 'g2': norm2_g[l][None], 'w_in': w_in16, 'ones': ones,
            'wq_t': wq_t.astype(BF16), 'wv_t': w_l[:, C_AV:C_SU].T.astype(BF16), 'gq': gq, 'gk': gk,
            'wqs_t': _swap_pairs(wq_t, axis=0).astype(BF16), 'wk_sw': _swap_pairs(w_l[:, C_AK:C_AV]).astype(BF16),
            'gqs': _swap_pairs(gq, axis=0), 'gks': _swap_pairs(gk),
            'cos': jnp.tile(cos, (1, ATT_KV_HEADS)), 'sin': jnp.tile(sin, (1, ATT_KV_HEADS)),
            'cos_t': cos.T, 'sin_t': sin.T,
            'ret_lg': jnp.stack([ret_lg_f[l], ret_lg_b[l]]), 'ret_gn': ret_norm_g[l][None],
            'w_ret_out': w_ret_out[l].astype(BF16), 'w_att_out': w_att_out[l].astype(BF16),
            'w_su_t': w_l[:, C_SU:C_G].T.astype(BF16),
            's5_tt': jnp.stack([fwd[0], bwd[0]]), 's5_wx': jnp.stack([fwd[1], bwd[1]]),
            's5_wy': jnp.stack([fwd[2], bwd[2]]), 's5_a': jnp.stack([fwd[3], bwd[3]]),
            's5_aseg': jnp.stack([fwd[4], bwd[4]]),
            'ssm_d': ssm_d[l][:, None], 'w_glu': w_glu[l].astype(BF16), 'w_out': w_out[l].astype(BF16),
            'w_router_t': w_router.T, 'b_router': b_router[:, None],
            'w_gate': w_gate16, 'w_up': w_up16, 'w_down': w_down16,
        }
        if l == DEPTH - 1:
            p['final_g'] = final_norm_g[None]
        mod = mod_all[l]

        xp, (k_c, v_c, st_r, st_s) = _layer(xp, batch, mod, (0, batch * seq), p, None)
        ks_.append(k_c.reshape(batch, seq, ATT_KV_HEADS, ATT_HEAD_DIM))
        vs_.append(v_c.reshape(batch, seq, ATT_KV_HEADS, ATT_HEAD_DIM))
        rets_.append(st_r)
        ssms_.append(st_s)

        ctx = (cache_attn_k[:, l], cache_attn_v[:, l], state_ret[:, l], state_ssm_re[:, l], state_ssm_im[:, l])
        xs, _ = _layer(xs, dec_batch, mod, (1, dec_seq), p, ctx)

    ssm_all = jnp.stack(ssms_, axis=1)
    ssm_all = ssm_all.transpose(2, 1, 0, 3, 4, 5)
    new_re = ssm_all[..., 0, :]
    new_im = ssm_all[..., 1, :]
    return (xp.reshape(batch, seq, D_MODEL), xs.reshape(dec_batch, dec_seq, D_MODEL),
            jnp.stack(ks_, axis=1), jnp.stack(vs_, axis=1), jnp.stack(rets_, axis=1), new_re, new_im)
```
